```python
import jax
import jax.numpy as jnp
from jax import lax
import numpy as np

D_MODEL = 2048
BATCH = 8
SEQ = 2048
DEPTH = 1

MOBA_HEAD_DIM = 128
MOBA_HEADS = D_MODEL // (2 * MOBA_HEAD_DIM)
MOBA_BLOCK = 256
MOBA_TOPK = 3
MOBA_Q_CHUNK = 16
MLA_V_DIM = 128
MLA_HEADS = D_MODEL // (2 * MLA_V_DIM)
MLA_NOPE_DIM = 128
MLA_ROPE_DIM = 64
MLA_Q_RANK = D_MODEL // 4
MLA_KV_RANK = D_MODEL // 4
ATTN_Q_BLOCK = 128
ROPE_THETA = 10000.0
MOBA_WIDTH = MOBA_HEADS * MOBA_HEAD_DIM
MLA_WIDTH = MLA_HEADS * MLA_V_DIM
MIX_WIDTH = MOBA_WIDTH + MLA_WIDTH
IN_WIDTH = 3 * MOBA_WIDTH + MLA_Q_RANK + MLA_KV_RANK + MLA_ROPE_DIM
N_EXPERTS = 64
TOP_K = 8
N_GROUPS = 8
TOPK_GROUPS = 4
EXPERTS_PER_GROUP = N_EXPERTS // N_GROUPS
EXPERT_FF = D_MODEL // 4
SHARED_FF = D_MODEL // 4
ROUTED_SCALE = 2.5
MOE_ROW_BLOCK = 256
LN_EPS = 1e-5
RMS_EPS = 1e-6
DEEPNORM_ALPHA = (2 * DEPTH) ** 0.25
DEEPNORM_BETA = (8 * DEPTH) ** -0.25

kernel_name = "hybrid_moba_mla_moe_deepnorm_block"


def layer_norm(x, g, b):
    xf = x.astype(jnp.float32)
    mu = jnp.mean(xf, axis=-1, keepdims=True)
    var = jnp.mean(jnp.square(xf - mu), axis=-1, keepdims=True)
    return ((xf - mu) * lax.rsqrt(var + LN_EPS)).astype(x.dtype) * g + b


def rms_norm(x, g):
    xf = x.astype(jnp.float32)
    return (xf * lax.rsqrt(jnp.mean(xf * xf, axis=-1, keepdims=True) + RMS_EPS)).astype(x.dtype) * g


def rope_tables(seq, dim):
    inv = 1.0 / (ROPE_THETA ** (jnp.arange(0, dim, 2, dtype=jnp.float32) / dim))
    ang = jnp.arange(seq, dtype=jnp.float32)[:, None] * inv[None, :]
    return jnp.cos(ang), jnp.sin(ang)


def apply_rope(x, cos, sin):
    x1, x2 = jnp.split(x.astype(jnp.float32), 2, axis=-1)
    return jnp.concatenate([x1 * cos - x2 * sin, x2 * cos + x1 * sin], axis=-1).astype(x.dtype)


def moba_attention(q, k, v):
    B, H, S, hd = q.shape
    nb = -(-S // MOBA_BLOCK)
    pad = nb * MOBA_BLOCK - S
    kb = jnp.pad(k, ((0, 0), (0, 0), (0, pad), (0, 0))).reshape(B, H, nb, MOBA_BLOCK, hd)
    vb = jnp.pad(v, ((0, 0), (0, 0), (0, pad), (0, 0))).reshape(B, H, nb, MOBA_BLOCK, hd)
    n_sel = min(MOBA_TOPK, nb - 1)
    scale = hd ** -0.5
    n_chunks = S // MOBA_Q_CHUNK

    def to_chunks(t):
        return jnp.moveaxis(t.reshape(B, H, n_chunks, MOBA_Q_CHUNK, *t.shape[3:]), 2, 0)

    starts = jnp.arange(n_chunks, dtype=jnp.int32) * MOBA_Q_CHUNK
    xs = [to_chunks(q), starts]
    if n_sel > 0:
        k_mean = jnp.mean(kb.astype(jnp.float32), axis=3)
        gate = jnp.einsum('bhsd,bhnd->bhsn', q.astype(jnp.float32), k_mean)
        q_blk = jnp.arange(S, dtype=jnp.int32) // MOBA_BLOCK
        fully_past = jnp.arange(nb, dtype=jnp.int32)[None, :] < q_blk[:, None]
        gate = jnp.where(fully_past, gate, -jnp.inf)
        _, sel = lax.top_k(gate, n_sel)
        xs.append(to_chunks(sel))
    b_idx = jnp.arange(B)[:, None, None, None]
    h_idx = jnp.arange(H)[None, :, None, None]

    def chunk(args):
        q_c, start = args[0], args[1]
        own = start // MOBA_BLOCK
        q_pos = start + jnp.arange(MOBA_Q_CHUNK, dtype=jnp.int32)
        k_pos = own * MOBA_BLOCK + jnp.arange(MOBA_BLOCK, dtype=jnp.int32)
        k_own = lax.dynamic_index_in_dim(kb, own, axis=2, keepdims=False)
        v_own = lax.dynamic_index_in_dim(vb, own, axis=2, keepdims=False)
        s_own = jnp.einsum('bhqd,bhkd->bhqk', q_c, k_own, preferred_element_type=jnp.float32) * scale
        s_own = jnp.where(k_pos[None, :] <= q_pos[:, None], s_own, -jnp.inf)
        if n_sel == 0:
            p = jax.nn.softmax(s_own, axis=-1).astype(v.dtype)
            return jnp.einsum('bhqk,bhkd->bhqd', p, v_own)
        sel_c = args[2]
        k_sel = kb[b_idx, h_idx, sel_c]
        v_sel = vb[b_idx, h_idx, sel_c]
        s_sel = jnp.einsum('bhqd,bhqnkd->bhqnk', q_c, k_sel, preferred_element_type=jnp.float32) * scale
        s_sel = jnp.where((sel_c < own)[..., None], s_sel, -jnp.inf)
        n_keys = n_sel * MOBA_BLOCK
        s_all = jnp.concatenate([s_sel.reshape(B, H, MOBA_Q_CHUNK, n_keys), s_own], axis=-1)
        p = jax.nn.softmax(s_all, axis=-1).astype(v.dtype)
        p_sel = p[..., :n_keys].reshape(B, H, MOBA_Q_CHUNK, n_sel, MOBA_BLOCK)
        return (jnp.einsum('bhqnk,bhqnkd->bhqd', p_sel, v_sel)
                + jnp.einsum('bhqk,bhkd->bhqd', p[..., n_keys:], v_own))

    out = lax.map(chunk, tuple(xs))
    return out.transpose(1, 0, 3, 2, 4).reshape(B, S, H * hd)


def causal_attention(q, k, v, scale):
    B, H, S, _ = q.shape
    dv = v.shape[-1]
    nq = S // ATTN_Q_BLOCK
    qb = jnp.moveaxis(q.reshape(B, H, nq, ATTN_Q_BLOCK, q.shape[-1]), 2, 0)
    k_pos = jnp.arange(S, dtype=jnp.int32)

    def block(args):
        q_blk, start = args
        q_pos = start + jnp.arange(ATTN_Q_BLOCK, dtype=jnp.int32)
        s = jnp.einsum('bhqd,bhkd->bhqk', q_blk, k, preferred_element_type=jnp.float32) * scale
        s = jnp.where(k_pos[None, :] <= q_pos[:, None], s, -jnp.inf)
        p = jax.nn.softmax(s, axis=-1).astype(v.dtype)
        return jnp.einsum('bhqk,bhkd->bhqd', p, v)

    out = lax.map(block, (qb, jnp.arange(nq, dtype=jnp.int32) * ATTN_Q_BLOCK))
    return out.transpose(1, 0, 3, 2, 4).reshape(B, S, H * dv)


def token_mixer(h, w_in, g_cq, w_uq, g_ckv, w_ukv, w_o):
    B, S, _ = h.shape
    proj = h @ w_in
    split_at = [MOBA_WIDTH, 2 * MOBA_WIDTH, 3 * MOBA_WIDTH,
                3 * MOBA_WIDTH + MLA_Q_RANK, 3 * MOBA_WIDTH + MLA_Q_RANK + MLA_KV_RANK]
    q_a, k_a, v_a, c_q, c_kv, k_r = jnp.split(proj, split_at, axis=-1)

    def heads(t, n, d):
        return t.reshape(B, S, n, d).transpose(0, 2, 1, 3)
    cos_a, sin_a = rope_tables(S, MOBA_HEAD_DIM)
    q_a = apply_rope(heads(q_a, MOBA_HEADS, MOBA_HEAD_DIM), cos_a, sin_a)
    k_a = apply_rope(heads(k_a, MOBA_HEADS, MOBA_HEAD_DIM), cos_a, sin_a)
    out_a = moba_attention(q_a, k_a, heads(v_a, MOBA_HEADS, MOBA_HEAD_DIM))

    cos_r, sin_r = rope_tables(S, MLA_ROPE_DIM)
    q_m = heads(rms_norm(c_q, g_cq) @ w_uq, MLA_HEADS, MLA_NOPE_DIM + MLA_ROPE_DIM)
    q_m = jnp.concatenate([q_m[..., :MLA_NOPE_DIM],
                           apply_rope(q_m[..., MLA_NOPE_DIM:], cos_r, sin_r)], axis=-1)
    kv = heads(rms_norm(c_kv, g_ckv) @ w_ukv, MLA_HEADS, MLA_NOPE_DIM + MLA_V_DIM)
    k_rope = jnp.broadcast_to(apply_rope(k_r, cos_r, sin_r)[:, None], (B, MLA_HEADS, S, MLA_ROPE_DIM))
    k_m = jnp.concatenate([kv[..., :MLA_NOPE_DIM], k_rope], axis=-1)
    out_m = causal_attention(q_m, k_m, kv[..., MLA_NOPE_DIM:],
                             (MLA_NOPE_DIM + MLA_ROPE_DIM) ** -0.5)

    return jnp.concatenate([out_a, out_m], axis=-1) @ w_o


def routed_experts(t, expert_idx, expert_w, w_gate, w_up, w_down):
    T, D = t.shape
    TK = T * TOP_K
    flat_e = expert_idx.reshape(-1)
    order = jnp.argsort(flat_e)
    sorted_e = flat_e[order]
    sorted_tok = (order // TOP_K).astype(jnp.int32)
    sorted_w = expert_w.reshape(-1)[order]
    sizes = jnp.bincount(flat_e, length=N_EXPERTS)
    padded = (sizes + MOE_ROW_BLOCK - 1) // MOE_ROW_BLOCK * MOE_ROW_BLOCK
    starts = jnp.cumsum(sizes) - sizes
    pends = jnp.cumsum(padded)
    pstarts = pends - padded
    dest = pstarts[sorted_e] + jnp.arange(TK) - starts[sorted_e]
    n_blocks = -(-TK // MOE_ROW_BLOCK) + N_EXPERTS
    n_rows = n_blocks * MOE_ROW_BLOCK
    row_tok = jnp.full((n_rows,), T, jnp.int32).at[dest].set(sorted_tok)
    row_w = jnp.zeros((n_rows,), t.dtype).at[dest].set(sorted_w.astype(t.dtype))
    block_start = jnp.arange(n_blocks) * MOE_ROW_BLOCK
    block_e = jnp.minimum(jnp.searchsorted(pends, block_start, side='right'), N_EXPERTS - 1)
    t_pad = jnp.concatenate([t, jnp.zeros((1, D), t.dtype)], axis=0)

    def block(args):
        toks, e = args
        xb = t_pad[toks]
        hb = jax.nn.silu(xb @ w_gate[e]) * (xb @ w_up[e])
        return hb @ w_down[e]

    y = lax.map(block, (row_tok.reshape(n_blocks, MOE_ROW_BLOCK), block_e))
    y = y.reshape(n_rows, D) * row_w[:, None]
    return jax.ops.segment_sum(y, row_tok, num_segments=T + 1)[:T]


def moe_ffn(h, w_router, router_bias, w_gate, w_up, w_down, ws_gate, ws_up, ws_down):
    B, S, D = h.shape
    t = h.reshape(B * S, D)
    T = t.shape[0]
    scores = jax.nn.sigmoid((t @ w_router).astype(jnp.float32))
    sel_scores = scores + router_bias.astype(jnp.float32)
    grp_score = lax.top_k(sel_scores.reshape(T, N_GROUPS, EXPERTS_PER_GROUP), 2)[0].sum(-1)
    _, grp_idx = lax.top_k(grp_score, TOPK_GROUPS)
    grp_mask = jnp.any(grp_idx[..., None] == jnp.arange(N_GROUPS), axis=-2)
    masked = jnp.where(jnp.repeat(grp_mask, EXPERTS_PER_GROUP, axis=-1), sel_scores, -jnp.inf)
    _, expert_idx = lax.top_k(masked, TOP_K)
    w = jnp.take_along_axis(scores, expert_idx, axis=-1)
    w = w / jnp.sum(w, axis=-1, keepdims=True) * ROUTED_SCALE
    routed = routed_experts(t, expert_idx, w, w_gate, w_up, w_down)
    shared = (jax.nn.silu(t @ ws_gate) * (t @ ws_up)) @ ws_down
    return (routed + shared).reshape(B, S, D)


def hybrid_layer(x, c, w_ada, b_ada, w_in, g_cq, w_uq, g_ckv, w_ukv, w_o, ln1_g, ln1_b,
                 w_router, router_bias, w_gate, w_up, w_down, ws_gate, ws_up, ws_down, ln2_g, ln2_b):
    mod = jax.nn.silu(c) @ w_ada + b_ada
    sh1, sc1, g1, sh2, sc2, g2 = [m[:, None, :] for m in jnp.split(mod, 6, axis=-1)]
    h = x * (1 + sc1) + sh1
    x = layer_norm(DEEPNORM_ALPHA * x + (1 + g1) * token_mixer(h, w_in, g_cq, w_uq, g_ckv, w_ukv, w_o),
                   ln1_g, ln1_b)
    h = x * (1 + sc2) + sh2
    x = layer_norm(DEEPNORM_ALPHA * x + (1 + g2) * moe_ffn(h, w_router, router_bias, w_gate, w_up, w_down,
                                                           ws_gate, ws_up, ws_down),
                   ln2_g, ln2_b)
    return x


def setup_inputs(seed: int = 0) -> dict:
    key = jax.random.key(seed)
    ks = jax.random.split(key, 24)
    L, D = DEPTH, D_MODEL

    def nrm(k, shape, s):
        return jax.random.normal(k, shape, jnp.float32) * s

    x = nrm(ks[0], (BATCH, SEQ, D), 1.0)
    c = nrm(ks[1], (BATCH, D), 1.0)
    w_ada = nrm(ks[2], (L, D, 6 * D), 0.5 * D ** -0.5)
    b_ada = nrm(ks[3], (L, 6 * D), 0.02)
    w_in = nrm(ks[4], (L, D, IN_WIDTH), D ** -0.5)
    w_in = w_in.at[:, :, 2 * MOBA_WIDTH:3 * MOBA_WIDTH].multiply(DEEPNORM_BETA)
    g_cq = 1.0 + nrm(ks[5], (L, MLA_Q_RANK), 0.05)
    w_uq = nrm(ks[6], (L, MLA_Q_RANK, MLA_HEADS * (MLA_NOPE_DIM + MLA_ROPE_DIM)), MLA_Q_RANK ** -0.5)
    g_ckv = 1.0 + nrm(ks[7], (L, MLA_KV_RANK), 0.05)
    kv_col_scale = jnp.concatenate([jnp.ones((MLA_NOPE_DIM,), jnp.float32),
                                    jnp.full((MLA_V_DIM,), DEEPNORM_BETA, jnp.float32)])
    w_ukv = (nrm(ks[8], (L, MLA_KV_RANK, MLA_HEADS, MLA_NOPE_DIM + MLA_V_DIM), MLA_KV_RANK ** -0.5)
             * kv_col_scale).reshape(L, MLA_KV_RANK, MLA_HEADS * (MLA_NOPE_DIM + MLA_V_DIM))
    w_o = nrm(ks[9], (L, MIX_WIDTH, D), MIX_WIDTH ** -0.5 * DEEPNORM_BETA)
    ln1_g = 1.0 + nrm(ks[10], (L, D), 0.05)
    ln1_b = nrm(ks[11], (L, D), 0.02)
    w_router = nrm(ks[12], (L, D, N_EXPERTS), D ** -0.5)
    router_bias = nrm(ks[13], (L, N_EXPERTS), 0.01)
    w_gate = nrm(ks[14], (L, N_EXPERTS, D, EXPERT_FF), D ** -0.5)
    w_up = nrm(ks[15], (L, N_EXPERTS, D, EXPERT_FF), D ** -0.5)
    w_down = nrm(ks[16], (L, N_EXPERTS, EXPERT_FF, D), EXPERT_FF ** -0.5 * DEEPNORM_BETA)
    ws_gate = nrm(ks[17], (L, D, SHARED_FF), D ** -0.5)
    ws_up = nrm(ks[18], (L, D, SHARED_FF), D ** -0.5)
    ws_down = nrm(ks[19], (L, SHARED_FF, D), SHARED_FF ** -0.5 * DEEPNORM_BETA)
    ln2_g = 1.0 + nrm(ks[20], (L, D), 0.05)
    ln2_b = nrm(ks[21], (L, D), 0.02)
    return {"x": x, "c": c, "w_ada": w_ada, "b_ada": b_ada, "w_in": w_in, "g_cq": g_cq,
            "w_uq": w_uq, "g_ckv": g_ckv, "w_ukv": w_ukv, "w_o": w_o, "ln1_g": ln1_g,
            "ln1_b": ln1_b, "w_router": w_router, "router_bias": router_bias, "w_gate": w_gate,
            "w_up": w_up, "w_down": w_down, "ws_gate": ws_gate, "ws_up": ws_up,
            "ws_down": ws_down, "ln2_g": ln2_g, "ln2_b": ln2_b}


def reference(x, c, w_ada, b_ada, w_in, g_cq, w_uq, g_ckv, w_ukv, w_o, ln1_g, ln1_b,
              w_router, router_bias, w_gate, w_up, w_down, ws_gate, ws_up, ws_down, ln2_g, ln2_b):
    for l in range(DEPTH):
        x = hybrid_layer(x, c, w_ada[l], b_ada[l], w_in[l], g_cq[l], w_uq[l], g_ckv[l], w_ukv[l],
                         w_o[l], ln1_g[l], ln1_b[l], w_router[l], router_bias[l], w_gate[l],
                         w_up[l], w_down[l], ws_gate[l], ws_up[l], ws_down[l], ln2_g[l], ln2_b[l])
    return x
```

```python
import functools

import jax
import jax.numpy as jnp
import numpy as np
from jax import lax
from jax.experimental import pallas as pl
from jax.experimental.pallas import tpu as pltpu

F32 = jnp.float32
BF16 = jnp.bfloat16
I32 = jnp.int32

D_MODEL = 2048
HEAD_DIM = 128
N_HEADS = 8
MOBA_BLOCK = 256
MOBA_TOPK = 3
MLA_ROPE_DIM = 64
MLA_RANK = 512
MLA_QK_DIM = HEAD_DIM + MLA_ROPE_DIM
GROUP_WIDTH = N_HEADS * HEAD_DIM
ROPE_THETA = 10000.0
N_EXPERTS = 64
TOP_K = 8
N_GROUPS = 8
TOPK_GROUPS = 4
EXPERT_FF = 512
ROUTED_SCALE = 2.5
MOE_ROW_BLOCK = 256
LN_EPS = 1e-5
RMS_EPS = 1e-6
DEEPNORM_ALPHA = 2.0 ** 0.25

VMEM_LIMIT_BYTES = 56 * 1024 * 1024

NT_DIMS = (((1,), (1,)), ((), ()))


def _params(semantics):
    return pltpu.CompilerParams(dimension_semantics=semantics,
                                vmem_limit_bytes=VMEM_LIMIT_BYTES)


def _silu(x):
    return x * jax.nn.sigmoid(x)


def _ada_kernel(c_ref, w_ref, b_ref, o_ref):
    a = _silu(c_ref[...]).astype(BF16)
    o_ref[...] = jnp.dot(a, w_ref[...].astype(BF16), preferred_element_type=F32) + b_ref[...]


def ada_modulation(c, w_ada, b_ada):
    bsz, d = c.shape
    rows = -(-bsz // 16) * 16
    c_pad = jnp.pad(c, ((0, rows - bsz), (0, 0)))
    n = w_ada.shape[1]
    tn = 1024
    out = pl.pallas_call(
        _ada_kernel,
        grid=(n // tn,),
        in_specs=[pl.BlockSpec((rows, d), lambda j: (0, 0)),
                  pl.BlockSpec((d, tn), lambda j: (0, j)),
                  pl.BlockSpec((1, tn), lambda j: (0, j))],
        out_specs=pl.BlockSpec((rows, tn), lambda j: (0, j)),
        out_shape=jax.ShapeDtypeStruct((rows, n), F32),
        compiler_params=_params(("arbitrary",)),
        name="ada_modulation",
    )(c_pad, w_ada, b_ada.reshape(1, n))
    return out[:bsz]


def _rms(x, g):
    return x * lax.rsqrt(jnp.mean(x * x, axis=-1, keepdims=True) + RMS_EPS) * g


def _rope64(x, cos, sin):
    return x * cos + pltpu.roll(x, 64, axis=1) * sin


def _rope32(x, cos, s1, s2):
    return x * cos + pltpu.roll(x, 96, axis=1) * s1 + pltpu.roll(x, 32, axis=1) * s2


def _inproj_kernel(x_ref, sc_ref, sh_ref, w_ref, wuq_ref, wuk_ref, wuv_ref, gcq_ref, gckv_ref,
                   cosa_ref, sina_ref, cosr_ref, s1_ref, s2_ref,
                   qa_ref, ka_ref, va_ref, qm_ref, kn_ref, vm_ref, kr_ref):
    h = (x_ref[...] * (1.0 + sc_ref[0]) + sh_ref[0]).astype(BF16)
    cosa, sina = cosa_ref[...], sina_ref[...]
    cosr, s1, s2 = cosr_ref[...], s1_ref[...], s2_ref[...]
    gw = GROUP_WIDTH

    def proj(lo, hi):
        return jnp.dot(h, w_ref[:, lo:hi], preferred_element_type=F32)

    for out_ref, base in ((qa_ref, 0), (ka_ref, gw)):
        t = proj(base, base + gw)
        for hd in range(N_HEADS):
            sl = slice(hd * HEAD_DIM, (hd + 1) * HEAD_DIM)
            out_ref[:, sl] = _rope64(t[:, sl], cosa, sina).astype(BF16)
    va_ref[...] = proj(2 * gw, 3 * gw).astype(BF16)

    cq = _rms(proj(3 * gw, 3 * gw + MLA_RANK), gcq_ref[...]).astype(BF16)
    qm = jnp.dot(cq, wuq_ref[...], preferred_element_type=F32)
    for hd in range(N_HEADS):
        b0 = hd * 2 * HEAD_DIM
        qm_ref[:, b0:b0 + HEAD_DIM] = qm[:, b0:b0 + HEAD_DIM].astype(BF16)
        qm_ref[:, b0 + HEAD_DIM:b0 + 2 * HEAD_DIM] = _rope32(
            qm[:, b0 + HEAD_DIM:b0 + 2 * HEAD_DIM], cosr, s1, s2).astype(BF16)

    ckv = _rms(proj(3 * gw + MLA_RANK, 3 * gw + 2 * MLA_RANK), gckv_ref[...]).astype(BF16)
    kn_ref[...] = jnp.dot(ckv, wuk_ref[...], preferred_element_type=F32).astype(BF16)
    vm_ref[...] = jnp.dot(ckv, wuv_ref[...], preferred_element_type=F32).astype(BF16)

    kr = proj(3 * gw + 2 * MLA_RANK, 3 * gw + 2 * MLA_RANK + HEAD_DIM)
    kr_ref[...] = _rope32(kr, cosr, s1, s2).astype(BF16)


def _rope_tables(seq):
    pos = jnp.arange(seq, dtype=F32)[:, None]
    inv_a = 1.0 / (ROPE_THETA ** (jnp.arange(0, HEAD_DIM, 2, dtype=F32) / HEAD_DIM))
    ang_a = pos * inv_a[None, :]
    cosa = jnp.concatenate([jnp.cos(ang_a), jnp.cos(ang_a)], axis=1)
    sina = jnp.concatenate([-jnp.sin(ang_a), jnp.sin(ang_a)], axis=1)
    inv_r = 1.0 / (ROPE_THETA ** (jnp.arange(0, MLA_ROPE_DIM, 2, dtype=F32) / MLA_ROPE_DIM))
    ang_r = pos * inv_r[None, :]
    cr, sr = jnp.cos(ang_r), jnp.sin(ang_r)
    z32 = jnp.zeros_like(cr)
    z64 = jnp.zeros((seq, 64), F32)
    cosr = jnp.concatenate([cr, cr, z64], axis=1)
    s1 = jnp.concatenate([-sr, z32, z64], axis=1)
    s2 = jnp.concatenate([z32, sr, z64], axis=1)
    return cosa, sina, cosr, s1, s2


def input_projection(x2d, sc1, sh1, w_in, g_cq, w_uq, g_ckv, w_ukv, bsz, seq):
    t_tokens, d = x2d.shape
    tm = 256
    tiles_per_seq = seq // tm
    gw = GROUP_WIDTH
    in_w = 3 * gw + 2 * MLA_RANK + HEAD_DIM
    w_cat = jnp.pad(w_in, ((0, 0), (0, in_w - w_in.shape[1]))).astype(BF16)
    wuq = jnp.pad(w_uq.reshape(MLA_RANK, N_HEADS, MLA_QK_DIM),
                  ((0, 0), (0, 0), (0, 2 * HEAD_DIM - MLA_QK_DIM)))
    wuq = wuq.reshape(MLA_RANK, N_HEADS * 2 * HEAD_DIM).astype(BF16)
    wukv = w_ukv.reshape(MLA_RANK, N_HEADS, 2 * HEAD_DIM)
    wuk = wukv[:, :, :HEAD_DIM].reshape(MLA_RANK, gw).astype(BF16)
    wuv = wukv[:, :, HEAD_DIM:].reshape(MLA_RANK, gw).astype(BF16)
    tables = _rope_tables(seq)

    def const(shape):
        return pl.BlockSpec(shape, lambda i: (0,) * len(shape), pipeline_mode=pl.Buffered(1))

    def rows(width):
        return pl.BlockSpec((tm, width), lambda i: (i, 0))

    mod_spec = pl.BlockSpec((1, 1, d), lambda i: (i // tiles_per_seq, 0, 0))
    tab_spec = pl.BlockSpec((tm, HEAD_DIM), lambda i: (i % tiles_per_seq, 0))
    widths = (gw, gw, gw, 2 * gw, gw, gw, HEAD_DIM)
    return pl.pallas_call(
        _inproj_kernel,
        grid=(t_tokens // tm,),
        in_specs=[rows(d), mod_spec, mod_spec, const((d, in_w)),
                  const((MLA_RANK, 2 * gw)), const((MLA_RANK, gw)), const((MLA_RANK, gw)),
                  const((1, MLA_RANK)), const((1, MLA_RANK))] + [tab_spec] * 5,
        out_specs=[rows(w) for w in widths],
        out_shape=[jax.ShapeDtypeStruct((t_tokens, w), BF16) for w in widths],
        compiler_params=_params(("arbitrary",)),
        name="input_projection",
    )(x2d, sc1, sh1, w_cat, wuq, wuk, wuv, g_cq.reshape(1, -1), g_ckv.reshape(1, -1), *tables)


ATTN_BLOCK = 256


def _softmax_init(m_ref, l_ref, acc_ref):
    m_ref[...] = jnp.full(m_ref.shape, -jnp.inf, F32)
    l_ref[...] = jnp.zeros(l_ref.shape, F32)
    acc_ref[...] = jnp.zeros(acc_ref.shape, F32)


def _softmax_step(q, k, v, bias, scale, m_ref, l_ref, acc_ref):
    s = lax.dot_general(q, k, NT_DIMS, preferred_element_type=F32) * scale + bias
    m_prev = m_ref[...]
    m_new = jnp.maximum(m_prev, jnp.max(s, axis=1, keepdims=True))
    alpha = jnp.exp(m_prev - m_new)
    p = jnp.exp(s - m_new)
    l_ref[...] = alpha * l_ref[...] + jnp.sum(p, axis=1, keepdims=True)
    acc_ref[...] = alpha * acc_ref[...] + jnp.dot(p.astype(BF16), v, preferred_element_type=F32)
    m_ref[...] = m_new


def _causal_bias():
    r = lax.broadcasted_iota(I32, (ATTN_BLOCK, ATTN_BLOCK), 0)
    c = lax.broadcasted_iota(I32, (ATTN_BLOCK, ATTN_BLOCK), 1)
    return jnp.where(c <= r, 0.0, -jnp.inf).astype(F32)


def _moba_kernel(q_ref, k_ref, v_ref, o_ref, kmean_ref, bias_ref, m_ref, l_ref, acc_ref, *, n_blocks):
    qi = pl.program_id(2)
    blk = MOBA_BLOCK

    @pl.when(qi == 0)
    def _():
        kmean_ref[...] = jnp.zeros(kmean_ref.shape, F32)
        for n in range(n_blocks):
            kb = k_ref[0, n * blk:(n + 1) * blk, :].astype(F32)
            kmean_ref[n:n + 1, :] = jnp.sum(kb, axis=0, keepdims=True) * (1.0 / blk)

    q = q_ref[0]
    gate = lax.dot_general(q.astype(F32), kmean_ref[...], NT_DIMS, precision=lax.Precision.HIGHEST,
                           preferred_element_type=F32)
    col = lax.broadcasted_iota(I32, gate.shape, 1)
    g = jnp.where(col < qi, gate, -jnp.inf)
    for n in range(n_blocks - 1):
        gn = g[:, n:n + 1]
        beats = (g > gn) | ((g == gn) & (col < n))
        cnt = jnp.sum(jnp.where(beats, 1.0, 0.0), axis=1, keepdims=True)
        keep = cnt < float(MOBA_TOPK)
        bias_ref[n] = jnp.broadcast_to(jnp.where(keep, 0.0, -jnp.inf), (blk, blk)).astype(F32)

    scale = HEAD_DIM ** -0.5
    _softmax_init(m_ref, l_ref, acc_ref)
    own = pl.multiple_of(qi * blk, blk)
    _softmax_step(q, k_ref[0, pl.ds(own, blk), :], v_ref[0, pl.ds(own, blk), :], _causal_bias(),
                  scale, m_ref, l_ref, acc_ref)

    def body(n, carry):
        off = pl.multiple_of(n * blk, blk)
        _softmax_step(q, k_ref[0, pl.ds(off, blk), :], v_ref[0, pl.ds(off, blk), :], bias_ref[n],
                      scale, m_ref, l_ref, acc_ref)
        return carry

    lax.fori_loop(0, qi, body, 0)
    o_ref[0] = (acc_ref[...] / l_ref[...]).astype(o_ref.dtype)


def moba_attention(qa, ka, va, bsz, seq):
    blk = MOBA_BLOCK
    n_blocks = seq // blk
    q3, k3, v3 = (t.reshape(bsz, seq, GROUP_WIDTH) for t in (qa, ka, va))
    q_spec = pl.BlockSpec((1, blk, HEAD_DIM), lambda b, h, i: (b, i, h))
    kv_spec = pl.BlockSpec((1, seq, HEAD_DIM), lambda b, h, i: (b, 0, h))
    out = pl.pallas_call(
        functools.partial(_moba_kernel, n_blocks=n_blocks),
        grid=(bsz, N_HEADS, n_blocks),
        in_specs=[q_spec, kv_spec, kv_spec],
        out_specs=q_spec,
        out_shape=jax.ShapeDtypeStruct((bsz, seq, GROUP_WIDTH), BF16),
        scratch_shapes=[pltpu.VMEM((HEAD_DIM, HEAD_DIM), F32),
                        pltpu.VMEM((n_blocks - 1, blk, blk), F32),
                        pltpu.VMEM((blk, 1), F32), pltpu.VMEM((blk, 1), F32),
                        pltpu.VMEM((blk, HEAD_DIM), F32)],
        compiler_params=_params(("arbitrary", "arbitrary", "arbitrary")),
        name="moba_attention",
    )(q3, k3, v3)
    return out.reshape(bsz * seq, GROUP_WIDTH)


def _mla_kernel(q_ref, kn_ref, kr_ref, v_ref, o_ref, kcat_ref, m_ref, l_ref, acc_ref):
    qi = pl.program_id(2)
    blk = ATTN_BLOCK

    @pl.when(qi == 0)
    def _():
        kcat_ref[:, :HEAD_DIM] = kn_ref[0]
        kcat_ref[:, HEAD_DIM:] = kr_ref[0]

    q = q_ref[0]
    scale = MLA_QK_DIM ** -0.5
    _softmax_init(m_ref, l_ref, acc_ref)
    own = pl.multiple_of(qi * blk, blk)
    _softmax_step(q, kcat_ref[pl.ds(own, blk), :], v_ref[0, pl.ds(own, blk), :], _causal_bias(),
                  scale, m_ref, l_ref, acc_ref)

    def body(n, carry):
        off = pl.multiple_of(n * blk, blk)
        _softmax_step(q, kcat_ref[pl.ds(off, blk), :], v_ref[0, pl.ds(off, blk), :], 0.0,
                      scale, m_ref, l_ref, acc_ref)
        return carry

    lax.fori_loop(0, qi, body, 0)
    o_ref[0] = (acc_ref[...] / l_ref[...]).astype(o_ref.dtype)


def mla_attention(qm, kn, kr, vm, bsz, seq):
    blk = ATTN_BLOCK
    q3 = qm.reshape(bsz, seq, 2 * GROUP_WIDTH)
    kn3, vm3 = kn.reshape(bsz, seq, GROUP_WIDTH), vm.reshape(bsz, seq, GROUP_WIDTH)
    kr3 = kr.reshape(bsz, seq, HEAD_DIM)
    out = pl.pallas_call(
        _mla_kernel,
        grid=(bsz, N_HEADS, seq // blk),
        in_specs=[pl.BlockSpec((1, blk, 2 * HEAD_DIM), lambda b, h, i: (b, i, h)),
                  pl.BlockSpec((1, seq, HEAD_DIM), lambda b, h, i: (b, 0, h)),
                  pl.BlockSpec((1, seq, HEAD_DIM), lambda b, h, i: (b, 0, 0)),
                  pl.BlockSpec((1, seq, HEAD_DIM), lambda b, h, i: (b, 0, h))],
        out_specs=pl.BlockSpec((1, blk, HEAD_DIM), lambda b, h, i: (b, i, h)),
        out_shape=jax.ShapeDtypeStruct((bsz, seq, GROUP_WIDTH), BF16),
        scratch_shapes=[pltpu.VMEM((seq, 2 * HEAD_DIM), BF16),
                        pltpu.VMEM((blk, 1), F32), pltpu.VMEM((blk, 1), F32),
                        pltpu.VMEM((blk, HEAD_DIM), F32)],
        compiler_params=_params(("arbitrary", "arbitrary", "arbitrary")),
        name="mla_attention",
    )(q3, kn3, kr3, vm3)
    return out.reshape(bsz * seq, GROUP_WIDTH)


def _layer_norm(y, g, b):
    mu = jnp.mean(y, axis=-1, keepdims=True)
    yc = y - mu
    var = jnp.mean(yc * yc, axis=-1, keepdims=True)
    return yc * lax.rsqrt(var + LN_EPS) * g + b


def _outproj_kernel(a_ref, m_ref, x_ref, woa_ref, wom_ref, g1_ref, sc2_ref, sh2_ref, g2_ref,
                    lng_ref, lnb_ref, wsg_ref, wsu_ref, wsd_ref, h2_ref, base_ref):
    f = (jnp.dot(a_ref[...], woa_ref[...], preferred_element_type=F32)
         + jnp.dot(m_ref[...], wom_ref[...], preferred_element_type=F32))
    x1 = _layer_norm(DEEPNORM_ALPHA * x_ref[...] + (1.0 + g1_ref[0]) * f, lng_ref[...], lnb_ref[...])
    h2 = x1 * (1.0 + sc2_ref[0]) + sh2_ref[0]
    h2_ref[...] = h2
    hb = h2.astype(BF16)
    act = (_silu(jnp.dot(hb, wsg_ref[...], preferred_element_type=F32))
           * jnp.dot(hb, wsu_ref[...], preferred_element_type=F32)).astype(BF16)
    shared = jnp.dot(act, wsd_ref[...], preferred_element_type=F32)
    base_ref[...] = DEEPNORM_ALPHA * x1 + (1.0 + g2_ref[0]) * shared


def output_projection(attn_a, attn_m, x2d, w_o, g1, sc2, sh2, g2, ln_g, ln_b,
                      ws_gate, ws_up, ws_down, seq):
    t_tokens, d = x2d.shape
    tm = 256
    tiles_per_seq = seq // tm
    gw = GROUP_WIDTH
    ff = ws_gate.shape[1]

    def const(shape):
        return pl.BlockSpec(shape, lambda i: (0,) * len(shape), pipeline_mode=pl.Buffered(1))

    def rows(width):
        return pl.BlockSpec((tm, width), lambda i: (i, 0))

    mod_spec = pl.BlockSpec((1, 1, d), lambda i: (i // tiles_per_seq, 0, 0))
    return pl.pallas_call(
        _outproj_kernel,
        grid=(t_tokens // tm,),
        in_specs=[rows(gw), rows(gw), rows(d), const((gw, d)), const((gw, d)),
                  mod_spec, mod_spec, mod_spec, mod_spec, const((1, d)), const((1, d)),
                  const((d, ff)), const((d, ff)), const((ff, d))],
        out_specs=[rows(d), rows(d)],
        out_shape=[jax.ShapeDtypeStruct((t_tokens, d), F32)] * 2,
        compiler_params=_params(("arbitrary",)),
        name="output_projection",
    )(attn_a, attn_m, x2d, w_o[:gw].astype(BF16), w_o[gw:].astype(BF16), g1, sc2, sh2, g2,
      ln_g.reshape(1, d), ln_b.reshape(1, d),
      ws_gate.astype(BF16), ws_up.astype(BF16), ws_down.astype(BF16))


ROUTER_TILE = 512


def _router_kernel(h_ref, wrt_ref, bias_ref, tri_ref, ltri_ref,
                   e_ref, pos_ref, w_ref, cnt_ref, run_ref, msk_ref):
    tr = ROUTER_TILE
    epg = N_EXPERTS // N_GROUPS

    @pl.when(pl.program_id(0) == 0)
    def _():
        run_ref[...] = jnp.zeros(run_ref.shape, F32)

    logits = lax.dot_general(wrt_ref[...], h_ref[...], NT_DIMS, precision=lax.Precision.HIGHEST,
                             preferred_element_type=F32)
    scores = jax.nn.sigmoid(logits)
    sel_scores = scores + bias_ref[...]

    sub = lax.broadcasted_iota(I32, (epg, tr), 0)
    grp_rows = []
    for g in range(N_GROUPS):
        xg = sel_scores[g * epg:(g + 1) * epg, :]
        m1 = jnp.max(xg, axis=0, keepdims=True)
        first = jnp.min(jnp.where(xg == m1, sub, epg), axis=0, keepdims=True)
        m2 = jnp.max(jnp.where(sub == first, -jnp.inf, xg), axis=0, keepdims=True)
        grp_rows.append(m1 + m2)
    grp = jnp.concatenate(grp_rows, axis=0)

    giota = lax.broadcasted_iota(I32, (N_GROUPS, tr), 0)
    gcnt = jnp.zeros((N_GROUPS, tr), F32)
    for m in range(N_GROUPS):
        rm = grp[m:m + 1, :]
        gcnt = gcnt + jnp.where((rm > grp) | ((rm == grp) & (m < giota)), 1.0, 0.0)
    gkeep = jnp.where(gcnt < float(TOPK_GROUPS), 1.0, 0.0)
    gmask = jnp.concatenate(
        [jnp.broadcast_to(gkeep[g:g + 1, :], (epg, tr)) for g in range(N_GROUPS)], axis=0)
    masked = jnp.where(gmask > 0.0, sel_scores, -jnp.inf)
    msk_ref[...] = masked

    eiota = lax.broadcasted_iota(I32, (N_EXPERTS, tr), 0)

    def rank_body(m, cnt):
        rm = msk_ref[pl.ds(m, 1), :]
        return cnt + jnp.where((rm > masked) | ((rm == masked) & (m < eiota)), 1.0, 0.0)

    ecnt = lax.fori_loop(0, N_EXPERTS, rank_body, jnp.zeros((N_EXPERTS, tr), F32))
    sel = ecnt < float(TOP_K)

    sw = jnp.where(sel, scores, 0.0)
    wn = sw / jnp.sum(sw, axis=0, keepdims=True) * ROUTED_SCALE

    selb = jnp.where(sel, 1.0, 0.0).astype(BF16)
    cum = jnp.dot(selb, tri_ref[...], preferred_element_type=F32)
    pos = run_ref[...] + cum - 1.0
    run_ref[...] = run_ref[...] + cum[:, tr - 1:tr]
    cnt_ref[...] = jnp.broadcast_to(run_ref[...], cnt_ref.shape).astype(I32)

    slot = jnp.dot(ltri_ref[...], selb, preferred_element_type=F32)
    ef = eiota.astype(F32)
    e_rows, p_rows, w_rows = [], [], []
    for j in range(TOP_K):
        mj = sel & (slot == float(j))
        e_rows.append(jnp.sum(jnp.where(mj, ef, 0.0), axis=0, keepdims=True))
        p_rows.append(jnp.sum(jnp.where(mj, pos, 0.0), axis=0, keepdims=True))
        w_rows.append(jnp.sum(jnp.where(mj, wn, 0.0), axis=0, keepdims=True))
    e_ref[...] = jnp.concatenate(e_rows, axis=0).astype(I32)
    pos_ref[...] = jnp.concatenate(p_rows, axis=0).astype(I32)
    w_ref[...] = jnp.concatenate(w_rows, axis=0)


def router(h2, w_router, router_bias):
    t_tokens, d = h2.shape
    tr = ROUTER_TILE
    tri = jnp.asarray(np.triu(np.ones((tr, tr), np.float32)), BF16)
    ltri = jnp.asarray(np.tril(np.ones((N_EXPERTS, N_EXPERTS), np.float32), -1), BF16)

    def const(shape):
        return pl.BlockSpec(shape, lambda i: (0,) * len(shape))

    tok = pl.BlockSpec((TOP_K, tr), lambda i: (0, i))
    e_t, pos_t, w_t, cnt = pl.pallas_call(
        _router_kernel,
        grid=(t_tokens // tr,),
        in_specs=[pl.BlockSpec((tr, d), lambda i: (i, 0)), const((N_EXPERTS, d)),
                  const((N_EXPERTS, 1)), const((tr, tr)), const((N_EXPERTS, N_EXPERTS))],
        out_specs=[tok, tok, tok, const((N_EXPERTS, 128))],
        out_shape=[jax.ShapeDtypeStruct((TOP_K, t_tokens), I32),
                   jax.ShapeDtypeStruct((TOP_K, t_tokens), I32),
                   jax.ShapeDtypeStruct((TOP_K, t_tokens), F32),
                   jax.ShapeDtypeStruct((N_EXPERTS, 128), I32)],
        scratch_shapes=[pltpu.VMEM((N_EXPERTS, 1), F32), pltpu.VMEM((N_EXPERTS, tr), F32)],
        compiler_params=_params(("arbitrary",)),
        name="router",
    )(h2, w_router.T, router_bias.reshape(N_EXPERTS, 1), tri, ltri)
    return e_t, pos_t, w_t, cnt[:, 0]


def _experts_kernel(be_ref, nused_ref, rt_cur_ref, rt_next_ref, h2_ref, wg_ref, wu_ref, wd_ref,
                    y_ref, xbuf, sem, wg_bf, wu_bf, wd_bf):
    i = pl.program_id(0)
    nused = nused_ref[0]
    slot = i % 2
    rb = MOE_ROW_BLOCK

    def row_copy(tok, dst_slot, r):
        return pltpu.make_async_copy(h2_ref.at[pl.ds(tok, 1), :], xbuf.at[dst_slot, pl.ds(r, 1), :],
                                     sem.at[dst_slot])

    def issue(rt_ref, dst_slot):
        def body(r, carry):
            row_copy(rt_ref[r], dst_slot, r).start()
            return carry
        lax.fori_loop(0, rb, body, 0, unroll=8)

    @pl.when(i == 0)
    def _():
        issue(rt_cur_ref, 0)

    @pl.when(i + 1 < nused)
    def _():
        issue(rt_next_ref, 1 - slot)

    @pl.when(i < nused)
    def _():
        @pl.when((i == 0) | (be_ref[i] != be_ref[jnp.maximum(i - 1, 0)]))
        def _():
            wg_bf[...] = wg_ref[0].astype(BF16)
            wu_bf[...] = wu_ref[0].astype(BF16)
            wd_bf[...] = wd_ref[0].astype(BF16)

        pltpu.make_async_copy(h2_ref.at[pl.ds(0, rb), :], xbuf.at[slot], sem.at[slot]).wait()
        xb = xbuf[slot].astype(BF16)
        act = (_silu(jnp.dot(xb, wg_bf[...], preferred_element_type=F32))
               * jnp.dot(xb, wu_bf[...], preferred_element_type=F32)).astype(BF16)
        y_ref[...] = jnp.dot(act, wd_bf[...], preferred_element_type=F32)

    @pl.when(i >= nused)
    def _():
        y_ref[...] = jnp.zeros(y_ref.shape, F32)


def routed_experts(h2, row_tok, block_e, nused, w_gate, w_up, w_down):
    t_tokens, d = h2.shape
    rb = MOE_ROW_BLOCK
    n_rows = row_tok.shape[0]
    n_blocks = n_rows // rb
    ff = w_gate.shape[2]
    grid_spec = pltpu.PrefetchScalarGridSpec(
        num_scalar_prefetch=2,
        grid=(n_blocks,),
        in_specs=[pl.BlockSpec((rb,), lambda i, be, nu: (i,), memory_space=pltpu.SMEM),
                  pl.BlockSpec((rb,), lambda i, be, nu: (jnp.minimum(i + 1, n_blocks - 1),),
                               memory_space=pltpu.SMEM),
                  pl.BlockSpec(memory_space=pl.ANY),
                  pl.BlockSpec((1, d, ff), lambda i, be, nu: (be[i], 0, 0)),
                  pl.BlockSpec((1, d, ff), lambda i, be, nu: (be[i], 0, 0)),
                  pl.BlockSpec((1, ff, d), lambda i, be, nu: (be[i], 0, 0))],
        out_specs=pl.BlockSpec((rb, d), lambda i, be, nu: (i, 0)),
        scratch_shapes=[pltpu.VMEM((2, rb, d), F32), pltpu.SemaphoreType.DMA((2,)),
                        pltpu.VMEM((d, ff), BF16), pltpu.VMEM((d, ff), BF16),
                        pltpu.VMEM((ff, d), BF16)],
    )
    return pl.pallas_call(
        _experts_kernel,
        grid_spec=grid_spec,
        out_shape=jax.ShapeDtypeStruct((n_rows, d), F32),
        compiler_params=_params(("arbitrary",)),
        name="routed_experts",
    )(block_e, nused, row_tok, row_tok, h2, w_gate, w_up, w_down)


COMBINE_TILE = 128


def _combine_kernel(d_cur_ref, d_next_ref, y_ref, w_ref, base_ref, g2_ref, lng_ref, lnb_ref,
                    o_ref, ybuf, sem):
    i = pl.program_id(0)
    n_steps = pl.num_programs(0)
    slot = i % 2
    tc = COMBINE_TILE
    n_copies = TOP_K * tc

    def issue(d_ref, dst_slot):
        def body(r, carry):
            pltpu.make_async_copy(y_ref.at[pl.ds(d_ref[r], 1), :], ybuf.at[dst_slot, pl.ds(r, 1), :],
                                  sem.at[dst_slot]).start()
            return carry
        lax.fori_loop(0, n_copies, body, 0, unroll=8)

    @pl.when(i == 0)
    def _():
        issue(d_cur_ref, 0)

    @pl.when(i + 1 < n_steps)
    def _():
        issue(d_next_ref, 1 - slot)

    pltpu.make_async_copy(y_ref.at[pl.ds(0, n_copies), :], ybuf.at[slot], sem.at[slot]).wait()
    w = w_ref[...]
    routed = w[:, 0:1] * ybuf[slot, 0:tc, :]
    for k in range(1, TOP_K):
        routed = routed + w[:, k:k + 1] * ybuf[slot, k * tc:(k + 1) * tc, :]
    y = base_ref[...] + (1.0 + g2_ref[0]) * routed
    o_ref[...] = _layer_norm(y, lng_ref[...], lnb_ref[...])


def combine(y_sorted, dest_tiles, w_tok, base, g2, ln_g, ln_b, seq):
    t_tokens, d = base.shape
    tc = COMBINE_TILE
    n_steps = t_tokens // tc
    tiles_per_seq = seq // tc
    n_copies = TOP_K * tc

    def const(shape):
        return pl.BlockSpec(shape, lambda i: (0,) * len(shape))

    return pl.pallas_call(
        _combine_kernel,
        grid=(n_steps,),
        in_specs=[pl.BlockSpec((n_copies,), lambda i: (i,), memory_space=pltpu.SMEM),
                  pl.BlockSpec((n_copies,), lambda i: (jnp.minimum(i + 1, n_steps - 1),),
                               memory_space=pltpu.SMEM),
                  pl.BlockSpec(memory_space=pl.ANY),
                  pl.BlockSpec((tc, TOP_K), lambda i: (i, 0)),
                  pl.BlockSpec((tc, d), lambda i: (i, 0)),
                  pl.BlockSpec((1, 1, d), lambda i: (i // tiles_per_seq, 0, 0)),
                  const((1, d)), const((1, d))],
        out_specs=pl.BlockSpec((tc, d), lambda i: (i, 0)),
        out_shape=jax.ShapeDtypeStruct((t_tokens, d), F32),
        scratch_shapes=[pltpu.VMEM((2, n_copies, d), F32), pltpu.SemaphoreType.DMA((2,))],
        compiler_params=_params(("arbitrary",)),
        name="combine",
    )(dest_tiles, dest_tiles, y_sorted, w_tok, base, g2, ln_g.reshape(1, d), ln_b.reshape(1, d))


def _dispatch_tables(e_t, pos_t, counts, t_tokens):
    rb = MOE_ROW_BLOCK
    n_blocks = t_tokens * TOP_K // rb + N_EXPERTS
    n_rows = n_blocks * rb
    padded = (counts + rb - 1) // rb * rb
    pends = jnp.cumsum(padded)
    pstarts = pends - padded
    dest = pstarts[e_t] + pos_t
    tok = jnp.broadcast_to(jnp.arange(t_tokens, dtype=I32)[None, :], dest.shape)
    row_tok = jnp.zeros((n_rows,), I32).at[dest.reshape(-1)].set(tok.reshape(-1))
    nused = (pends[-1] // rb).astype(I32)
    blk = jnp.arange(n_blocks, dtype=I32)
    be = jnp.minimum(jnp.searchsorted(pends, blk * rb, side='right'), N_EXPERTS - 1).astype(I32)
    last_e = be[jnp.maximum(nused - 1, 0)]
    block_e = jnp.where(blk < nused, be, last_e)
    return dest.astype(I32), row_tok, block_e, nused.reshape(1)


def _layer(x, c, w_ada, b_ada, w_in, g_cq, w_uq, g_ckv, w_ukv, w_o, ln1_g, ln1_b,
           w_router, router_bias, w_gate, w_up, w_down, ws_gate, ws_up, ws_down, ln2_g, ln2_b):
    bsz, seq, d = x.shape
    t_tokens = bsz * seq
    x2d = x.reshape(t_tokens, d)

    mod = ada_modulation(c, w_ada, b_ada)
    sh1, sc1, g1, sh2, sc2, g2 = [m.reshape(bsz, 1, d) for m in jnp.split(mod, 6, axis=-1)]

    qa, ka, va, qm, kn, vm, kr = input_projection(x2d, sc1, sh1, w_in, g_cq, w_uq, g_ckv, w_ukv,
                                                  bsz, seq)
    attn_a = moba_attention(qa, ka, va, bsz, seq)
    attn_m = mla_attention(qm, kn, kr, vm, bsz, seq)
    h2, base = output_projection(attn_a, attn_m, x2d, w_o, g1, sc2, sh2, g2, ln1_g, ln1_b,
                                 ws_gate, ws_up, ws_down, seq)

    e_t, pos_t, w_t, counts = router(h2, w_router, router_bias)
    dest, row_tok, block_e, nused = _dispatch_tables(e_t, pos_t, counts, t_tokens)
    y_sorted = routed_experts(h2, row_tok, block_e, nused, w_gate, w_up, w_down)

    tc = COMBINE_TILE
    dest_tiles = dest.reshape(TOP_K, t_tokens // tc, tc).transpose(1, 0, 2).reshape(-1)
    out = combine(y_sorted, dest_tiles, w_t.T, base, g2, ln2_g, ln2_b, seq)
    return out.reshape(bsz, seq, d)


def kernel(x, c, w_ada, b_ada, w_in, g_cq, w_uq, g_ckv, w_ukv, w_o, ln1_g, ln1_b, w_router,
           router_bias, w_gate, w_up, w_down, ws_gate, ws_up, ws_down, ln2_g, ln2_b):
    depth = w_ada.shape[0]
    assert depth == 1, "DeepNorm constants are baked for a single layer"
    return _layer(x, c, w_ada[0], b_ada[0], w_in[0], g_cq[0], w_uq[0], g_ckv[0], w_ukv[0], w_o[0],
                  ln1_g[0], ln1_b[0], w_router[0], router_bias[0], w_gate[0], w_up[0], w_down[0],
                  ws_gate[0], ws_up[0], ws_down[0], ln2_g[0], ln2_b[0])
```

```python
import functools

import jax
import jax.numpy as jnp
import numpy as np
from jax import lax
from jax.experimental import pallas as pl
from jax.experimental.pallas import tpu as pltpu

F32 = jnp.float32
BF16 = jnp.bfloat16
I32 = jnp.int32

D_MODEL = 2048
HEAD_DIM = 128
N_HEADS = 8
MOBA_BLOCK = 256
MOBA_TOPK = 3
MLA_ROPE_DIM = 64
MLA_RANK = 512
MLA_QK_DIM = HEAD_DIM + MLA_ROPE_DIM
GROUP_WIDTH = N_HEADS * HEAD_DIM
ROPE_THETA = 10000.0
N_EXPERTS = 64
TOP_K = 8
N_GROUPS = 8
TOPK_GROUPS = 4
EXPERT_FF = 512
ROUTED_SCALE = 2.5
MOE_ROW_BLOCK = 256
LN_EPS = 1e-5
RMS_EPS = 1e-6
DEEPNORM_ALPHA = 2.0 ** 0.25

VMEM_LIMIT_BYTES = 56 * 1024 * 1024

NT_DIMS = (((1,), (1,)), ((), ()))


def _params(semantics):
    return pltpu.CompilerParams(dimension_semantics=semantics,
                                vmem_limit_bytes=VMEM_LIMIT_BYTES)


def _silu(x):
    return x * jax.nn.sigmoid(x)


def _ada_kernel(c_ref, w_ref, b_ref, o_ref):
    a = _silu(c_ref[...]).astype(BF16)
    o_ref[...] = jnp.dot(a, w_ref[...].astype(BF16), preferred_element_type=F32) + b_ref[...]


def ada_modulation(c, w_ada, b_ada):
    bsz, d = c.shape
    rows = -(-bsz // 16) * 16
    c_pad = jnp.pad(c, ((0, rows - bsz), (0, 0)))
    n = w_ada.shape[1]
    tn = 1024
    out = pl.pallas_call(
        _ada_kernel,
        grid=(n // tn,),
        in_specs=[pl.BlockSpec((rows, d), lambda j: (0, 0)),
                  pl.BlockSpec((d, tn), lambda j: (0, j)),
                  pl.BlockSpec((1, tn), lambda j: (0, j))],
        out_specs=pl.BlockSpec((rows, tn), lambda j: (0, j)),
        out_shape=jax.ShapeDtypeStruct((rows, n), F32),
        compiler_params=_params(("arbitrary",)),
        name="ada_modulation",
    )(c_pad, w_ada, b_ada.reshape(1, n))
    return out[:bsz]


def _rms(x, g):
    return x * lax.rsqrt(jnp.mean(x * x, axis=-1, keepdims=True) + RMS_EPS) * g


def _rope64(x, cos, sin):
    return x * cos + pltpu.roll(x, 64, axis=1) * sin


def _rope32(x, cos, s1, s2):
    return x * cos + pltpu.roll(x, 96, axis=1) * s1 + pltpu.roll(x, 32, axis=1) * s2


def _inproj_kernel(x_ref, sc_ref, sh_ref, w_ref, wuq_ref, wuk_ref, wuv_ref, gcq_ref, gckv_ref,
                   cosa_ref, sina_ref, cosr_ref, s1_ref, s2_ref,
                   qa_ref, ka_ref, va_ref, qm_ref, kn_ref, vm_ref, kr_ref):
    h = (x_ref[...] * (1.0 + sc_ref[0]) + sh_ref[0]).astype(BF16)
    cosa, sina = cosa_ref[...], sina_ref[...]
    cosr, s1, s2 = cosr_ref[...], s1_ref[...], s2_ref[...]
    gw = GROUP_WIDTH

    def proj(lo, hi):
        return jnp.dot(h, w_ref[:, lo:hi], preferred_element_type=F32)

    for out_ref, base in ((qa_ref, 0), (ka_ref, gw)):
        t = proj(base, base + gw)
        for hd in range(N_HEADS):
            sl = slice(hd * HEAD_DIM, (hd + 1) * HEAD_DIM)
            out_ref[:, sl] = _rope64(t[:, sl], cosa, sina).astype(BF16)
    va_ref[...] = proj(2 * gw, 3 * gw).astype(BF16)

    cq = _rms(proj(3 * gw, 3 * gw + MLA_RANK), gcq_ref[...]).astype(BF16)
    qm = jnp.dot(cq, wuq_ref[...], preferred_element_type=F32)
    for hd in range(N_HEADS):
        b0 = hd * 2 * HEAD_DIM
        qm_ref[:, b0:b0 + HEAD_DIM] = qm[:, b0:b0 + HEAD_DIM].astype(BF16)
        qm_ref[:, b0 + HEAD_DIM:b0 + 2 * HEAD_DIM] = _rope32(
            qm[:, b0 + HEAD_DIM:b0 + 2 * HEAD_DIM], cosr, s1, s2).astype(BF16)

    ckv = _rms(proj(3 * gw + MLA_RANK, 3 * gw + 2 * MLA_RANK), gckv_ref[...]).astype(BF16)
    kn_ref[...] = jnp.dot(ckv, wuk_ref[...], preferred_element_type=F32).astype(BF16)
    vm_ref[...] = jnp.dot(ckv, wuv_ref[...], preferred_element_type=F32).astype(BF16)

    kr = proj(3 * gw + 2 * MLA_RANK, 3 * gw + 2 * MLA_RANK + HEAD_DIM)
    kr_ref[...] = _rope32(kr, cosr, s1, s2).astype(BF16)


def _rope_tables(seq):
    pos = jnp.arange(seq, dtype=F32)[:, None]
    inv_a = 1.0 / (ROPE_THETA ** (jnp.arange(0, HEAD_DIM, 2, dtype=F32) / HEAD_DIM))
    ang_a = pos * inv_a[None, :]
    cosa = jnp.concatenate([jnp.cos(ang_a), jnp.cos(ang_a)], axis=1)
    sina = jnp.concatenate([-jnp.sin(ang_a), jnp.sin(ang_a)], axis=1)
    inv_r = 1.0 / (ROPE_THETA ** (jnp.arange(0, MLA_ROPE_DIM, 2, dtype=F32) / MLA_ROPE_DIM))
    ang_r = pos * inv_r[None, :]
    cr, sr = jnp.cos(ang_r), jnp.sin(ang_r)
    z32 = jnp.zeros_like(cr)
    z64 = jnp.zeros((seq, 64), F32)
    cosr = jnp.concatenate([cr, cr, z64], axis=1)
    s1 = jnp.concatenate([-sr, z32, z64], axis=1)
    s2 = jnp.concatenate([z32, sr, z64], axis=1)
    return cosa, sina, cosr, s1, s2


def input_projection(x2d, sc1, sh1, w_in, g_cq, w_uq, g_ckv, w_ukv, bsz, seq):
    t_tokens, d = x2d.shape
    tm = 256
    tiles_per_seq = seq // tm
    gw = GROUP_WIDTH
    in_w = 3 * gw + 2 * MLA_RANK + HEAD_DIM
    w_cat = jnp.pad(w_in, ((0, 0), (0, in_w - w_in.shape[1]))).astype(BF16)
    wuq = jnp.pad(w_uq.reshape(MLA_RANK, N_HEADS, MLA_QK_DIM),
                  ((0, 0), (0, 0), (0, 2 * HEAD_DIM - MLA_QK_DIM)))
    wuq = wuq.reshape(MLA_RANK, N_HEADS * 2 * HEAD_DIM).astype(BF16)
    wukv = w_ukv.reshape(MLA_RANK, N_HEADS, 2 * HEAD_DIM)
    wuk = wukv[:, :, :HEAD_DIM].reshape(MLA_RANK, gw).astype(BF16)
    wuv = wukv[:, :, HEAD_DIM:].reshape(MLA_RANK, gw).astype(BF16)
    tables = _rope_tables(seq)

    def const(shape):
        return pl.BlockSpec(shape, lambda i: (0,) * len(shape), pipeline_mode=pl.Buffered(1))

    def rows(width):
        return pl.BlockSpec((tm, width), lambda i: (i, 0))

    mod_spec = pl.BlockSpec((1, 1, d), lambda i: (i // tiles_per_seq, 0, 0))
    tab_spec = pl.BlockSpec((tm, HEAD_DIM), lambda i: (i % tiles_per_seq, 0))
    widths = (gw, gw, gw, 2 * gw, gw, gw, HEAD_DIM)
    return pl.pallas_call(
        _inproj_kernel,
        grid=(t_tokens // tm,),
        in_specs=[rows(d), mod_spec, mod_spec, const((d, in_w)),
                  const((MLA_RANK, 2 * gw)), const((MLA_RANK, gw)), const((MLA_RANK, gw)),
                  const((1, MLA_RANK)), const((1, MLA_RANK))] + [tab_spec] * 5,
        out_specs=[rows(w) for w in widths],
        out_shape=[jax.ShapeDtypeStruct((t_tokens, w), BF16) for w in widths],
        compiler_params=_params(("arbitrary",)),
        name="input_projection",
    )(x2d, sc1, sh1, w_cat, wuq, wuk, wuv, g_cq.reshape(1, -1), g_ckv.reshape(1, -1), *tables)


ATTN_BLOCK = 256
LOG2_E = 1.4426950408889634


def _causal_mask_t():
    key = lax.broadcasted_iota(I32, (ATTN_BLOCK, ATTN_BLOCK), 0)
    qry = lax.broadcasted_iota(I32, (ATTN_BLOCK, ATTN_BLOCK), 1)
    return key <= qry


def _scores_t(q, k_blk):
    return lax.dot_general(k_blk, q, NT_DIMS, preferred_element_type=F32)


def _first_block(s, vt_blk, c, acc_ref):
    s = jnp.where(_causal_mask_t(), s, -jnp.inf)
    m = jnp.max(s, axis=0, keepdims=True)
    p = jnp.exp2((s - m) * c)
    acc_ref[...] = jnp.dot(vt_blk, p.astype(BF16), preferred_element_type=F32)
    return m, jnp.sum(p, axis=0, keepdims=True)


def _next_block(s, vt_blk, bias, c, m, l, acc_ref):
    m_blk = jnp.max(s, axis=0, keepdims=True)
    if bias is not None:
        m_blk = m_blk + bias
    m_new = jnp.maximum(m, m_blk)
    alpha = jnp.exp2((m - m_new) * c)
    p = jnp.exp2((s - (m_new if bias is None else m_new - bias)) * c)
    acc_ref[...] = alpha * acc_ref[...] + jnp.dot(vt_blk, p.astype(BF16), preferred_element_type=F32)
    return m_new, alpha * l + jnp.sum(p, axis=0, keepdims=True)


HEADS_PER_STEP = 4


def _head_cols(g, width=HEAD_DIM):
    return slice(g * width, (g + 1) * width)


def _moba_block_bias(kmean, q, qi, n_blocks):
    gate = lax.dot_general(kmean, q.astype(F32), NT_DIMS, precision=lax.Precision.HIGHEST,
                           preferred_element_type=F32)
    row = lax.broadcasted_iota(I32, gate.shape, 0)
    gt = jnp.where(row < qi, gate, -jnp.inf)
    cnt = jnp.zeros(gate.shape, F32)
    for m in range(n_blocks - 1):
        gm = gt[m:m + 1, :]
        cnt = cnt + jnp.where((gm > gt) | ((gm == gt) & (m < row)), 1.0, 0.0)
    return jnp.where(cnt < float(MOBA_TOPK), 0.0, -jnp.inf).astype(F32)


def _moba_kernel(q_ref, k_ref, v_ref, o_ref, kmean_ref, vt_ref, acc_ref, *, n_blocks):
    qi = pl.program_id(2)
    blk = MOBA_BLOCK
    heads = HEADS_PER_STEP

    @pl.when(qi == 0)
    def _():
        kmean_ref[...] = jnp.zeros(kmean_ref.shape, F32)
        for g in range(heads):
            vt_ref[g] = v_ref[0, :, _head_cols(g)].astype(F32).T.astype(BF16)
            for n in range(n_blocks):
                kb = k_ref[0, n * blk:(n + 1) * blk, _head_cols(g)].astype(F32)
                kmean_ref[g, n:n + 1, :] = jnp.sum(kb, axis=0, keepdims=True) * (1.0 / blk)

    c = (HEAD_DIM ** -0.5) * LOG2_E
    own = pl.multiple_of(qi * blk, blk)
    qs = [q_ref[0, :, _head_cols(g)] for g in range(heads)]
    scores = [_scores_t(qs[g], k_ref[0, pl.ds(own, blk), _head_cols(g)]) for g in range(heads)]
    biases = [_moba_block_bias(kmean_ref[g], qs[g], qi, n_blocks) for g in range(heads)]
    carry0 = []
    for g in range(heads):
        carry0 += _first_block(scores[g], vt_ref[g, :, pl.ds(own, blk)], c, acc_ref.at[g])

    def body(n, carry):
        off = pl.multiple_of(n * blk, blk)
        scores = [_scores_t(qs[g], k_ref[0, pl.ds(off, blk), _head_cols(g)]) for g in range(heads)]
        row = lax.broadcasted_iota(I32, biases[0].shape, 0)
        out = []
        for g in range(heads):
            bias = jnp.sum(jnp.where(row == n, biases[g], 0.0), axis=0, keepdims=True)
            out += _next_block(scores[g], vt_ref[g, :, pl.ds(off, blk)], bias,
                               c, carry[2 * g], carry[2 * g + 1], acc_ref.at[g])
        return tuple(out)

    final = lax.fori_loop(0, qi, body, tuple(carry0))
    for g in range(heads):
        o_ref[0, :, _head_cols(g)] = (acc_ref[g] / final[2 * g + 1]).T.astype(o_ref.dtype)


def moba_attention(qa, ka, va, bsz, seq):
    blk = MOBA_BLOCK
    n_blocks = seq // blk
    heads = HEADS_PER_STEP
    gate_rows = -(-n_blocks // 8) * 8
    q3, k3, v3 = (t.reshape(bsz, seq, GROUP_WIDTH) for t in (qa, ka, va))
    q_spec = pl.BlockSpec((1, blk, heads * HEAD_DIM), lambda b, h, i: (b, i, h))
    kv_spec = pl.BlockSpec((1, seq, heads * HEAD_DIM), lambda b, h, i: (b, 0, h))
    out = pl.pallas_call(
        functools.partial(_moba_kernel, n_blocks=n_blocks),
        grid=(bsz, N_HEADS // heads, n_blocks),
        in_specs=[q_spec, kv_spec, kv_spec],
        out_specs=q_spec,
        out_shape=jax.ShapeDtypeStruct((bsz, seq, GROUP_WIDTH), BF16),
        scratch_shapes=[pltpu.VMEM((heads, gate_rows, HEAD_DIM), F32),
                        pltpu.VMEM((heads, HEAD_DIM, seq), BF16),
                        pltpu.VMEM((heads, HEAD_DIM, blk), F32)],
        compiler_params=_params(("arbitrary", "arbitrary", "arbitrary")),
        name="moba_attention",
    )(q3, k3, v3)
    return out.reshape(bsz * seq, GROUP_WIDTH)


def _mla_kernel(q_ref, kn_ref, kr_ref, v_ref, o_ref, kcat_ref, vt_ref, acc_ref):
    qi = pl.program_id(2)
    blk = ATTN_BLOCK

    heads = HEADS_PER_STEP

    @pl.when(qi == 0)
    def _():
        for g in range(heads):
            vt_ref[g] = v_ref[0, :, _head_cols(g)].astype(F32).T.astype(BF16)
            kcat_ref[g, :, :HEAD_DIM] = kn_ref[0, :, _head_cols(g)]
            kcat_ref[g, :, HEAD_DIM:] = kr_ref[0]

    c = (MLA_QK_DIM ** -0.5) * LOG2_E
    own = pl.multiple_of(qi * blk, blk)
    qs = [q_ref[0, :, _head_cols(g, 2 * HEAD_DIM)] for g in range(heads)]
    scores = [_scores_t(qs[g], kcat_ref[g, pl.ds(own, blk), :]) for g in range(heads)]
    carry0 = []
    for g in range(heads):
        carry0 += _first_block(scores[g], vt_ref[g, :, pl.ds(own, blk)], c, acc_ref.at[g])

    def body(n, carry):
        off = pl.multiple_of(n * blk, blk)
        scores = [_scores_t(qs[g], kcat_ref[g, pl.ds(off, blk), :]) for g in range(heads)]
        out = []
        for g in range(heads):
            out += _next_block(scores[g], vt_ref[g, :, pl.ds(off, blk)], None, c,
                               carry[2 * g], carry[2 * g + 1], acc_ref.at[g])
        return tuple(out)

    final = lax.fori_loop(0, qi, body, tuple(carry0))
    for g in range(heads):
        o_ref[0, :, _head_cols(g)] = (acc_ref[g] / final[2 * g + 1]).T.astype(o_ref.dtype)


def mla_attention(qm, kn, kr, vm, bsz, seq):
    blk = ATTN_BLOCK
    heads = HEADS_PER_STEP
    q3 = qm.reshape(bsz, seq, 2 * GROUP_WIDTH)
    kn3, vm3 = kn.reshape(bsz, seq, GROUP_WIDTH), vm.reshape(bsz, seq, GROUP_WIDTH)
    kr3 = kr.reshape(bsz, seq, HEAD_DIM)
    out = pl.pallas_call(
        _mla_kernel,
        grid=(bsz, N_HEADS // heads, seq // blk),
        in_specs=[pl.BlockSpec((1, blk, heads * 2 * HEAD_DIM), lambda b, h, i: (b, i, h)),
                  pl.BlockSpec((1, seq, heads * HEAD_DIM), lambda b, h, i: (b, 0, h)),
                  pl.BlockSpec((1, seq, HEAD_DIM), lambda b, h, i: (b, 0, 0)),
                  pl.BlockSpec((1, seq, heads * HEAD_DIM), lambda b, h, i: (b, 0, h))],
        out_specs=pl.BlockSpec((1, blk, heads * HEAD_DIM), lambda b, h, i: (b, i, h)),
        out_shape=jax.ShapeDtypeStruct((bsz, seq, GROUP_WIDTH), BF16),
        scratch_shapes=[pltpu.VMEM((heads, seq, 2 * HEAD_DIM), BF16),
                        pltpu.VMEM((heads, HEAD_DIM, seq), BF16),
                        pltpu.VMEM((heads, HEAD_DIM, blk), F32)],
        compiler_params=_params(("arbitrary", "arbitrary", "arbitrary")),
        name="mla_attention",
    )(q3, kn3, kr3, vm3)
    return out.reshape(bsz * seq, GROUP_WIDTH)


def _layer_norm(y, g, b):
    mu = jnp.mean(y, axis=-1, keepdims=True)
    yc = y - mu
    var = jnp.mean(yc * yc, axis=-1, keepdims=True)
    return yc * lax.rsqrt(var + LN_EPS) * g + b


def _outproj_kernel(a_ref, m_ref, x_ref, woa_ref, wom_ref, g1_ref, sc2_ref, sh2_ref, g2_ref,
                    lng_ref, lnb_ref, wsg_ref, wsu_ref, wsd_ref, h2_ref, base_ref):
    f = (jnp.dot(a_ref[...], woa_ref[...], preferred_element_type=F32)
         + jnp.dot(m_ref[...], wom_ref[...], preferred_element_type=F32))
    x1 = _layer_norm(DEEPNORM_ALPHA * x_ref[...] + (1.0 + g1_ref[0]) * f, lng_ref[...], lnb_ref[...])
    h2 = x1 * (1.0 + sc2_ref[0]) + sh2_ref[0]
    h2_ref[...] = h2
    hb = h2.astype(BF16)
    act = (_silu(jnp.dot(hb, wsg_ref[...], preferred_element_type=F32))
           * jnp.dot(hb, wsu_ref[...], preferred_element_type=F32)).astype(BF16)
    shared = jnp.dot(act, wsd_ref[...], preferred_element_type=F32)
    base_ref[...] = DEEPNORM_ALPHA * x1 + (1.0 + g2_ref[0]) * shared


def output_projection(attn_a, attn_m, x2d, w_o, g1, sc2, sh2, g2, ln_g, ln_b,
                      ws_gate, ws_up, ws_down, seq):
    t_tokens, d = x2d.shape
    tm = 256
    tiles_per_seq = seq // tm
    gw = GROUP_WIDTH
    ff = ws_gate.shape[1]

    def const(shape):
        return pl.BlockSpec(shape, lambda i: (0,) * len(shape), pipeline_mode=pl.Buffered(1))

    def rows(width):
        return pl.BlockSpec((tm, width), lambda i: (i, 0))

    mod_spec = pl.BlockSpec((1, 1, d), lambda i: (i // tiles_per_seq, 0, 0))
    return pl.pallas_call(
        _outproj_kernel,
        grid=(t_tokens // tm,),
        in_specs=[rows(gw), rows(gw), rows(d), const((gw, d)), const((gw, d)),
                  mod_spec, mod_spec, mod_spec, mod_spec, const((1, d)), const((1, d)),
                  const((d, ff)), const((d, ff)), const((ff, d))],
        out_specs=[rows(d), rows(d)],
        out_shape=[jax.ShapeDtypeStruct((t_tokens, d), F32)] * 2,
        compiler_params=_params(("arbitrary",)),
        name="output_projection",
    )(attn_a, attn_m, x2d, w_o[:gw].astype(BF16), w_o[gw:].astype(BF16), g1, sc2, sh2, g2,
      ln_g.reshape(1, d), ln_b.reshape(1, d),
      ws_gate.astype(BF16), ws_up.astype(BF16), ws_down.astype(BF16))


ROUTER_TILE = 512


def _router_kernel(h_ref, wrt_ref, bias_ref, tri_ref, ltri_ref,
                   e_ref, pos_ref, w_ref, cnt_ref, run_ref, msk_ref):
    tr = ROUTER_TILE
    epg = N_EXPERTS // N_GROUPS

    @pl.when(pl.program_id(0) == 0)
    def _():
        run_ref[...] = jnp.zeros(run_ref.shape, F32)

    logits = lax.dot_general(wrt_ref[...], h_ref[...], NT_DIMS, precision=lax.Precision.HIGHEST,
                             preferred_element_type=F32)
    scores = jax.nn.sigmoid(logits)
    sel_scores = scores + bias_ref[...]

    sub = lax.broadcasted_iota(I32, (epg, tr), 0)
    grp_rows = []
    for g in range(N_GROUPS):
        xg = sel_scores[g * epg:(g + 1) * epg, :]
        m1 = jnp.max(xg, axis=0, keepdims=True)
        first = jnp.min(jnp.where(xg == m1, sub, epg), axis=0, keepdims=True)
        m2 = jnp.max(jnp.where(sub == first, -jnp.inf, xg), axis=0, keepdims=True)
        grp_rows.append(m1 + m2)
    grp = jnp.concatenate(grp_rows, axis=0)

    giota = lax.broadcasted_iota(I32, (N_GROUPS, tr), 0)
    gcnt = jnp.zeros((N_GROUPS, tr), F32)
    for m in range(N_GROUPS):
        rm = grp[m:m + 1, :]
        gcnt = gcnt + jnp.where((rm > grp) | ((rm == grp) & (m < giota)), 1.0, 0.0)
    gkeep = jnp.where(gcnt < float(TOPK_GROUPS), 1.0, 0.0)
    gmask = jnp.concatenate(
        [jnp.broadcast_to(gkeep[g:g + 1, :], (epg, tr)) for g in range(N_GROUPS)], axis=0)
    masked = jnp.where(gmask > 0.0, sel_scores, -jnp.inf)
    msk_ref[...] = masked

    eiota = lax.broadcasted_iota(I32, (N_EXPERTS, tr), 0)

    def rank_body(m, cnt):
        rm = msk_ref[pl.ds(m, 1), :]
        return cnt + jnp.where((rm > masked) | ((rm == masked) & (m < eiota)), 1.0, 0.0)

    ecnt = lax.fori_loop(0, N_EXPERTS, rank_body, jnp.zeros((N_EXPERTS, tr), F32))
    sel = ecnt < float(TOP_K)

    sw = jnp.where(sel, scores, 0.0)
    wn = sw / jnp.sum(sw, axis=0, keepdims=True) * ROUTED_SCALE

    selb = jnp.where(sel, 1.0, 0.0).astype(BF16)
    cum = jnp.dot(selb, tri_ref[...], preferred_element_type=F32)
    pos = run_ref[...] + cum - 1.0
    run_ref[...] = run_ref[...] + cum[:, tr - 1:tr]
    cnt_ref[...] = jnp.broadcast_to(run_ref[...], cnt_ref.shape).astype(I32)

    slot = jnp.dot(ltri_ref[...], selb, preferred_element_type=F32)
    ef = eiota.astype(F32)
    e_rows, p_rows, w_rows = [], [], []
    for j in range(TOP_K):
        mj = sel & (slot == float(j))
        e_rows.append(jnp.sum(jnp.where(mj, ef, 0.0), axis=0, keepdims=True))
        p_rows.append(jnp.sum(jnp.where(mj, pos, 0.0), axis=0, keepdims=True))
        w_rows.append(jnp.sum(jnp.where(mj, wn, 0.0), axis=0, keepdims=True))
    e_ref[...] = jnp.concatenate(e_rows, axis=0).astype(I32)
    pos_ref[...] = jnp.concatenate(p_rows, axis=0).astype(I32)
    w_ref[...] = jnp.concatenate(w_rows, axis=0)


def router(h2, w_router, router_bias):
    t_tokens, d = h2.shape
    tr = ROUTER_TILE
    tri = jnp.asarray(np.triu(np.ones((tr, tr), np.float32)), BF16)
    ltri = jnp.asarray(np.tril(np.ones((N_EXPERTS, N_EXPERTS), np.float32), -1), BF16)

    def const(shape):
        return pl.BlockSpec(shape, lambda i: (0,) * len(shape))

    tok = pl.BlockSpec((TOP_K, tr), lambda i: (0, i))
    e_t, pos_t, w_t, cnt = pl.pallas_call(
        _router_kernel,
        grid=(t_tokens // tr,),
        in_specs=[pl.BlockSpec((tr, d), lambda i: (i, 0)), const((N_EXPERTS, d)),
                  const((N_EXPERTS, 1)), const((tr, tr)), const((N_EXPERTS, N_EXPERTS))],
        out_specs=[tok, tok, tok, const((N_EXPERTS, 128))],
        out_shape=[jax.ShapeDtypeStruct((TOP_K, t_tokens), I32),
                   jax.ShapeDtypeStruct((TOP_K, t_tokens), I32),
                   jax.ShapeDtypeStruct((TOP_K, t_tokens), F32),
                   jax.ShapeDtypeStruct((N_EXPERTS, 128), I32)],
        scratch_shapes=[pltpu.VMEM((N_EXPERTS, 1), F32), pltpu.VMEM((N_EXPERTS, tr), F32)],
        compiler_params=_params(("arbitrary",)),
        name="router",
    )(h2, w_router.T, router_bias.reshape(N_EXPERTS, 1), tri, ltri)
    return e_t, pos_t, w_t, cnt[:, 0]


def _experts_kernel(be_ref, nused_ref, rt_cur_ref, rt_next_ref, h2_ref, wg_ref, wu_ref, wd_ref,
                    y_ref, xbuf, sem, wg_bf, wu_bf, wd_bf):
    i = pl.program_id(0)
    nused = nused_ref[0]
    slot = i % 2
    rb = MOE_ROW_BLOCK

    def row_copy(tok, dst_slot, r):
        return pltpu.make_async_copy(h2_ref.at[pl.ds(tok, 1), :], xbuf.at[dst_slot, pl.ds(r, 1), :],
                                     sem.at[dst_slot])

    def issue(rt_ref, dst_slot):
        def body(r, carry):
            row_copy(rt_ref[r], dst_slot, r).start()
            return carry
        lax.fori_loop(0, rb, body, 0, unroll=8)

    @pl.when(i == 0)
    def _():
        issue(rt_cur_ref, 0)

    @pl.when(i + 1 < nused)
    def _():
        issue(rt_next_ref, 1 - slot)

    @pl.when(i < nused)
    def _():
        @pl.when((i == 0) | (be_ref[i] != be_ref[jnp.maximum(i - 1, 0)]))
        def _():
            wg_bf[...] = wg_ref[0].astype(BF16)
            wu_bf[...] = wu_ref[0].astype(BF16)
            wd_bf[...] = wd_ref[0].astype(BF16)

        pltpu.make_async_copy(h2_ref.at[pl.ds(0, rb), :], xbuf.at[slot], sem.at[slot]).wait()
        xb = xbuf[slot].astype(BF16)
        act = (_silu(jnp.dot(xb, wg_bf[...], preferred_element_type=F32))
               * jnp.dot(xb, wu_bf[...], preferred_element_type=F32)).astype(BF16)
        y_ref[...] = jnp.dot(act, wd_bf[...], preferred_element_type=F32)

    @pl.when(i >= nused)
    def _():
        y_ref[...] = jnp.zeros(y_ref.shape, F32)


def routed_experts(h2, row_tok, block_e, nused, w_gate, w_up, w_down):
    t_tokens, d = h2.shape
    rb = MOE_ROW_BLOCK
    n_rows = row_tok.shape[0]
    n_blocks = n_rows // rb
    ff = w_gate.shape[2]
    grid_spec = pltpu.PrefetchScalarGridSpec(
        num_scalar_prefetch=2,
        grid=(n_blocks,),
        in_specs=[pl.BlockSpec((rb,), lambda i, be, nu: (i,), memory_space=pltpu.SMEM),
                  pl.BlockSpec((rb,), lambda i, be, nu: (jnp.minimum(i + 1, n_blocks - 1),),
                               memory_space=pltpu.SMEM),
                  pl.BlockSpec(memory_space=pl.ANY),
                  pl.BlockSpec((1, d, ff), lambda i, be, nu: (be[i], 0, 0)),
                  pl.BlockSpec((1, d, ff), lambda i, be, nu: (be[i], 0, 0)),
                  pl.BlockSpec((1, ff, d), lambda i, be, nu: (be[i], 0, 0))],
        out_specs=pl.BlockSpec((rb, d), lambda i, be, nu: (i, 0)),
        scratch_shapes=[pltpu.VMEM((2, rb, d), F32), pltpu.SemaphoreType.DMA((2,)),
                        pltpu.VMEM((d, ff), BF16), pltpu.VMEM((d, ff), BF16),
                        pltpu.VMEM((ff, d), BF16)],
    )
    return pl.pallas_call(
        _experts_kernel,
        grid_spec=grid_spec,
        out_shape=jax.ShapeDtypeStruct((n_rows, d), F32),
        compiler_params=_params(("arbitrary",)),
        name="routed_experts",
    )(block_e, nused, row_tok, row_tok, h2, w_gate, w_up, w_down)


COMBINE_TILE = 128


def _combine_kernel(d_cur_ref, d_next_ref, y_ref, w_ref, base_ref, g2_ref, lng_ref, lnb_ref,
                    o_ref, ybuf, sem):
    i = pl.program_id(0)
    n_steps = pl.num_programs(0)
    slot = i % 2
    tc = COMBINE_TILE
    n_copies = TOP_K * tc

    def issue(d_ref, dst_slot):
        def body(r, carry):
            pltpu.make_async_copy(y_ref.at[pl.ds(d_ref[r], 1), :], ybuf.at[dst_slot, pl.ds(r, 1), :],
                                  sem.at[dst_slot]).start()
            return carry
        lax.fori_loop(0, n_copies, body, 0, unroll=8)

    @pl.when(i == 0)
    def _():
        issue(d_cur_ref, 0)

    @pl.when(i + 1 < n_steps)
    def _():
        issue(d_next_ref, 1 - slot)

    pltpu.make_async_copy(y_ref.at[pl.ds(0, n_copies), :], ybuf.at[slot], sem.at[slot]).wait()
    w = w_ref[...]
    routed = w[:, 0:1] * ybuf[slot, 0:tc, :]
    for k in range(1, TOP_K):
        routed = routed + w[:, k:k + 1] * ybuf[slot, k * tc:(k + 1) * tc, :]
    y = base_ref[...] + (1.0 + g2_ref[0]) * routed
    o_ref[...] = _layer_norm(y, lng_ref[...], lnb_ref[...])


def combine(y_sorted, dest_tiles, w_tok, base, g2, ln_g, ln_b, seq):
    t_tokens, d = base.shape
    tc = COMBINE_TILE
    n_steps = t_tokens // tc
    tiles_per_seq = seq // tc
    n_copies = TOP_K * tc

    def const(shape):
        return pl.BlockSpec(shape, lambda i: (0,) * len(shape))

    return pl.pallas_call(
        _combine_kernel,
        grid=(n_steps,),
        in_specs=[pl.BlockSpec((n_copies,), lambda i: (i,), memory_space=pltpu.SMEM),
                  pl.BlockSpec((n_copies,), lambda i: (jnp.minimum(i + 1, n_steps - 1),),
                               memory_space=pltpu.SMEM),
                  pl.BlockSpec(memory_space=pl.ANY),
                  pl.BlockSpec((tc, TOP_K), lambda i: (i, 0)),
                  pl.BlockSpec((tc, d), lambda i: (i, 0)),
                  pl.BlockSpec((1, 1, d), lambda i: (i // tiles_per_seq, 0, 0)),
                  const((1, d)), const((1, d))],
        out_specs=pl.BlockSpec((tc, d), lambda i: (i, 0)),
        out_shape=jax.ShapeDtypeStruct((t_tokens, d), F32),
        scratch_shapes=[pltpu.VMEM((2, n_copies, d), F32), pltpu.SemaphoreType.DMA((2,))],
        compiler_params=_params(("arbitrary",)),
        name="combine",
    )(dest_tiles, dest_tiles, y_sorted, w_tok, base, g2, ln_g.reshape(1, d), ln_b.reshape(1, d))


def _dispatch_tables(e_t, pos_t, counts, t_tokens):
    rb = MOE_ROW_BLOCK
    n_blocks = t_tokens * TOP_K // rb + N_EXPERTS
    n_rows = n_blocks * rb
    padded = (counts + rb - 1) // rb * rb
    pends = jnp.cumsum(padded)
    pstarts = pends - padded
    experts = jnp.arange(N_EXPERTS, dtype=I32)
    pstart_of = jnp.sum(jnp.where(e_t[..., None] == experts, pstarts.astype(I32), 0), axis=-1)
    dest = pstart_of + pos_t
    tok = jnp.broadcast_to(jnp.arange(t_tokens, dtype=I32)[None, :], dest.shape)
    row_tok = jnp.zeros((n_rows,), I32).at[dest.reshape(-1)].set(
        tok.reshape(-1), unique_indices=True, mode="promise_in_bounds")
    nused = (pends[-1] // rb).astype(I32)
    blk = jnp.arange(n_blocks, dtype=I32)
    last_blk = jnp.minimum(blk, jnp.maximum(nused - 1, 0))
    block_e = jnp.sum((pends[None, :] <= (last_blk * rb)[:, None]).astype(I32), axis=1)
    block_e = jnp.minimum(block_e, N_EXPERTS - 1).astype(I32)
    return dest.astype(I32), row_tok, block_e, nused.reshape(1)


def _layer(x, c, w_ada, b_ada, w_in, g_cq, w_uq, g_ckv, w_ukv, w_o, ln1_g, ln1_b,
           w_router, router_bias, w_gate, w_up, w_down, ws_gate, ws_up, ws_down, ln2_g, ln2_b):
    bsz, seq, d = x.shape
    t_tokens = bsz * seq
    x2d = x.reshape(t_tokens, d)

    mod = ada_modulation(c, w_ada, b_ada)
    sh1, sc1, g1, sh2, sc2, g2 = [m.reshape(bsz, 1, d) for m in jnp.split(mod, 6, axis=-1)]

    qa, ka, va, qm, kn, vm, kr = input_projection(x2d, sc1, sh1, w_in, g_cq, w_uq, g_ckv, w_ukv,
                                                  bsz, seq)
    attn_a = moba_attention(qa, ka, va, bsz, seq)
    attn_m = mla_attention(qm, kn, kr, vm, bsz, seq)
    h2, base = output_projection(attn_a, attn_m, x2d, w_o, g1, sc2, sh2, g2, ln1_g, ln1_b,
                                 ws_gate, ws_up, ws_down, seq)

    e_t, pos_t, w_t, counts = router(h2, w_router, router_bias)
    dest, row_tok, block_e, nused = _dispatch_tables(e_t, pos_t, counts, t_tokens)
    y_sorted = routed_experts(h2, row_tok, block_e, nused, w_gate, w_up, w_down)

    tc = COMBINE_TILE
    dest_tiles = dest.reshape(TOP_K, t_tokens // tc, tc).transpose(1, 0, 2).reshape(-1)
    out = combine(y_sorted, dest_tiles, w_t.T, base, g2, ln2_g, ln2_b, seq)
    return out.reshape(bsz, seq, d)


def kernel(x, c, w_ada, b_ada, w_in, g_cq, w_uq, g_ckv, w_ukv, w_o, ln1_g, ln1_b, w_router,
           router_bias, w_gate, w_up, w_down, ws_gate, ws_up, ws_down, ln2_g, ln2_b):
    depth = w_ada.shape[0]
    assert depth == 1, "DeepNorm constants are baked for a single layer"
    return _layer(x, c, w_ada[0], b_ada[0], w_in[0], g_cq[0], w_uq[0], g_ckv[0], w_ukv[0], w_o[0],
                  ln1_g[0], ln1_b[0], w_router[0], router_bias[0], w_gate[0], w_up[0], w_down[0],
                  ws_gate[0], ws_up[0], ws_down[0], ln2_g[0], ln2_b[0])
```

```python
import functools

import jax
import jax.numpy as jnp
import numpy as np
from jax import lax
from jax.experimental import pallas as pl
from jax.experimental.pallas import tpu as pltpu

F32 = jnp.float32
BF16 = jnp.bfloat16
I32 = jnp.int32

D_MODEL = 2048
HEAD_DIM = 128
N_HEADS = 8
MOBA_BLOCK = 256
MOBA_TOPK = 3
MLA_ROPE_DIM = 64
MLA_RANK = 512
MLA_QK_DIM = HEAD_DIM + MLA_ROPE_DIM
GROUP_WIDTH = N_HEADS * HEAD_DIM
ROPE_THETA = 10000.0
N_EXPERTS = 64
TOP_K = 8
N_GROUPS = 8
TOPK_GROUPS = 4
EXPERT_FF = 512
ROUTED_SCALE = 2.5
MOE_ROW_BLOCK = 256
LN_EPS = 1e-5
RMS_EPS = 1e-6
DEEPNORM_ALPHA = 2.0 ** 0.25

VMEM_LIMIT_BYTES = 56 * 1024 * 1024

NT_DIMS = (((1,), (1,)), ((), ()))


def _params(semantics):
    return pltpu.CompilerParams(dimension_semantics=semantics,
                                vmem_limit_bytes=VMEM_LIMIT_BYTES)


def _silu(x):
    return x * jax.nn.sigmoid(x)


def _ada_kernel(c_ref, w_ref, b_ref, o_ref):
    a = _silu(c_ref[...]).astype(BF16)
    o_ref[...] = jnp.dot(a, w_ref[...].astype(BF16), preferred_element_type=F32) + b_ref[...]


def ada_modulation(c, w_ada, b_ada):
    bsz, d = c.shape
    rows = -(-bsz // 16) * 16
    c_pad = jnp.pad(c, ((0, rows - bsz), (0, 0)))
    n = w_ada.shape[1]
    tn = 1024
    out = pl.pallas_call(
        _ada_kernel,
        grid=(n // tn,),
        in_specs=[pl.BlockSpec((rows, d), lambda j: (0, 0)),
                  pl.BlockSpec((d, tn), lambda j: (0, j)),
                  pl.BlockSpec((1, tn), lambda j: (0, j))],
        out_specs=pl.BlockSpec((rows, tn), lambda j: (0, j)),
        out_shape=jax.ShapeDtypeStruct((rows, n), F32),
        compiler_params=_params(("arbitrary",)),
        name="ada_modulation",
    )(c_pad, w_ada, b_ada.reshape(1, n))
    return out[:bsz]


def _rms(x, g):
    return x * lax.rsqrt(jnp.mean(x * x, axis=-1, keepdims=True) + RMS_EPS) * g


def _rope64(x, cos, sin):
    return x * cos + pltpu.roll(x, 64, axis=1) * sin


def _rope32(x, cos, s1, s2):
    return x * cos + pltpu.roll(x, 96, axis=1) * s1 + pltpu.roll(x, 32, axis=1) * s2


def _inproj_kernel(x_ref, sc_ref, sh_ref, w_ref, wuq_ref, wuk_ref, wuv_ref, gcq_ref, gckv_ref,
                   cosa_ref, sina_ref, cosr_ref, s1_ref, s2_ref,
                   qa_ref, ka_ref, va_ref, qm_ref, kn_ref, vm_ref, kr_ref):
    h = (x_ref[...] * (1.0 + sc_ref[0]) + sh_ref[0]).astype(BF16)
    cosa, sina = cosa_ref[...], sina_ref[...]
    cosr, s1, s2 = cosr_ref[...], s1_ref[...], s2_ref[...]
    gw = GROUP_WIDTH

    def proj(lo, hi):
        return jnp.dot(h, w_ref[:, lo:hi], preferred_element_type=F32)

    for out_ref, base in ((qa_ref, 0), (ka_ref, gw)):
        t = proj(base, base + gw)
        for hd in range(N_HEADS):
            sl = slice(hd * HEAD_DIM, (hd + 1) * HEAD_DIM)
            out_ref[:, sl] = _rope64(t[:, sl], cosa, sina).astype(BF16)
    va_ref[...] = proj(2 * gw, 3 * gw).astype(BF16)

    cq = _rms(proj(3 * gw, 3 * gw + MLA_RANK), gcq_ref[...]).astype(BF16)
    qm = jnp.dot(cq, wuq_ref[...], preferred_element_type=F32)
    for hd in range(N_HEADS):
        b0 = hd * 2 * HEAD_DIM
        qm_ref[:, b0:b0 + HEAD_DIM] = qm[:, b0:b0 + HEAD_DIM].astype(BF16)
        qm_ref[:, b0 + HEAD_DIM:b0 + 2 * HEAD_DIM] = _rope32(
            qm[:, b0 + HEAD_DIM:b0 + 2 * HEAD_DIM], cosr, s1, s2).astype(BF16)

    ckv = _rms(proj(3 * gw + MLA_RANK, 3 * gw + 2 * MLA_RANK), gckv_ref[...]).astype(BF16)
    kn_ref[...] = jnp.dot(ckv, wuk_ref[...], preferred_element_type=F32).astype(BF16)
    vm_ref[...] = jnp.dot(ckv, wuv_ref[...], preferred_element_type=F32).astype(BF16)

    kr = proj(3 * gw + 2 * MLA_RANK, 3 * gw + 2 * MLA_RANK + HEAD_DIM)
    kr_ref[...] = _rope32(kr, cosr, s1, s2).astype(BF16)


def _rope_tables(seq):
    pos = jnp.arange(seq, dtype=F32)[:, None]
    inv_a = 1.0 / (ROPE_THETA ** (jnp.arange(0, HEAD_DIM, 2, dtype=F32) / HEAD_DIM))
    ang_a = pos * inv_a[None, :]
    cosa = jnp.concatenate([jnp.cos(ang_a), jnp.cos(ang_a)], axis=1)
    sina = jnp.concatenate([-jnp.sin(ang_a), jnp.sin(ang_a)], axis=1)
    inv_r = 1.0 / (ROPE_THETA ** (jnp.arange(0, MLA_ROPE_DIM, 2, dtype=F32) / MLA_ROPE_DIM))
    ang_r = pos * inv_r[None, :]
    cr, sr = jnp.cos(ang_r), jnp.sin(ang_r)
    z32 = jnp.zeros_like(cr)
    z64 = jnp.zeros((seq, 64), F32)
    cosr = jnp.concatenate([cr, cr, z64], axis=1)
    s1 = jnp.concatenate([-sr, z32, z64], axis=1)
    s2 = jnp.concatenate([z32, sr, z64], axis=1)
    return cosa, sina, cosr, s1, s2


def input_projection(x2d, sc1, sh1, w_in, g_cq, w_uq, g_ckv, w_ukv, bsz, seq):
    t_tokens, d = x2d.shape
    tm = 256
    tiles_per_seq = seq // tm
    gw = GROUP_WIDTH
    in_w = 3 * gw + 2 * MLA_RANK + HEAD_DIM
    w_cat = jnp.pad(w_in, ((0, 0), (0, in_w - w_in.shape[1]))).astype(BF16)
    wuq = jnp.pad(w_uq.reshape(MLA_RANK, N_HEADS, MLA_QK_DIM),
                  ((0, 0), (0, 0), (0, 2 * HEAD_DIM - MLA_QK_DIM)))
    wuq = wuq.reshape(MLA_RANK, N_HEADS * 2 * HEAD_DIM).astype(BF16)
    wukv = w_ukv.reshape(MLA_RANK, N_HEADS, 2 * HEAD_DIM)
    wuk = wukv[:, :, :HEAD_DIM].reshape(MLA_RANK, gw).astype(BF16)
    wuv = wukv[:, :, HEAD_DIM:].reshape(MLA_RANK, gw).astype(BF16)
    tables = _rope_tables(seq)

    def const(shape):
        return pl.BlockSpec(shape, lambda i: (0,) * len(shape), pipeline_mode=pl.Buffered(1))

    def rows(width):
        return pl.BlockSpec((tm, width), lambda i: (i, 0))

    mod_spec = pl.BlockSpec((1, 1, d), lambda i: (i // tiles_per_seq, 0, 0))
    tab_spec = pl.BlockSpec((tm, HEAD_DIM), lambda i: (i % tiles_per_seq, 0))
    widths = (gw, gw, gw, 2 * gw, gw, gw, HEAD_DIM)
    return pl.pallas_call(
        _inproj_kernel,
        grid=(t_tokens // tm,),
        in_specs=[rows(d), mod_spec, mod_spec, const((d, in_w)),
                  const((MLA_RANK, 2 * gw)), const((MLA_RANK, gw)), const((MLA_RANK, gw)),
                  const((1, MLA_RANK)), const((1, MLA_RANK))] + [tab_spec] * 5,
        out_specs=[rows(w) for w in widths],
        out_shape=[jax.ShapeDtypeStruct((t_tokens, w), BF16) for w in widths],
        compiler_params=_params(("arbitrary",)),
        name="input_projection",
    )(x2d, sc1, sh1, w_cat, wuq, wuk, wuv, g_cq.reshape(1, -1), g_ckv.reshape(1, -1), *tables)


ATTN_BLOCK = 256
LOG2_E = 1.4426950408889634


def _causal_mask_t():
    key = lax.broadcasted_iota(I32, (ATTN_BLOCK, ATTN_BLOCK), 0)
    qry = lax.broadcasted_iota(I32, (ATTN_BLOCK, ATTN_BLOCK), 1)
    return key <= qry


def _scores_t(q, k_blk):
    return lax.dot_general(k_blk, q, NT_DIMS, preferred_element_type=F32)


def _first_block(s, vt_blk, c, acc_ref):
    s = jnp.where(_causal_mask_t(), s, -jnp.inf)
    m = jnp.max(s, axis=0, keepdims=True)
    p = jnp.exp2((s - m) * c)
    acc_ref[...] = jnp.dot(vt_blk, p.astype(BF16), preferred_element_type=F32)
    return m, jnp.sum(p, axis=0, keepdims=True)


def _next_block(s, vt_blk, bias, c, m, l, acc_ref):
    m_blk = jnp.max(s, axis=0, keepdims=True)
    if bias is not None:
        m_blk = m_blk + bias
    m_new = jnp.maximum(m, m_blk)
    alpha = jnp.exp2((m - m_new) * c)
    p = jnp.exp2((s - (m_new if bias is None else m_new - bias)) * c)
    acc_ref[...] = alpha * acc_ref[...] + jnp.dot(vt_blk, p.astype(BF16), preferred_element_type=F32)
    return m_new, alpha * l + jnp.sum(p, axis=0, keepdims=True)


HEADS_PER_STEP = 4


def _head_cols(g, width=HEAD_DIM):
    return slice(g * width, (g + 1) * width)


def _moba_block_bias(kmean, q, qi, n_blocks):
    gate = lax.dot_general(kmean, q.astype(F32), NT_DIMS, precision=lax.Precision.HIGHEST,
                           preferred_element_type=F32)
    row = lax.broadcasted_iota(I32, gate.shape, 0)
    gt = jnp.where(row < qi, gate, -jnp.inf)
    cnt = jnp.zeros(gate.shape, F32)
    for m in range(n_blocks - 1):
        gm = gt[m:m + 1, :]
        cnt = cnt + jnp.where((gm > gt) | ((gm == gt) & (m < row)), 1.0, 0.0)
    return jnp.where(cnt < float(MOBA_TOPK), 0.0, -jnp.inf).astype(F32)


def _moba_kernel(q_ref, k_ref, v_ref, o_ref, kmean_ref, vt_ref, acc_ref, *, n_blocks):
    qi = pl.program_id(2)
    blk = MOBA_BLOCK
    heads = HEADS_PER_STEP

    @pl.when(qi == 0)
    def _():
        kmean_ref[...] = jnp.zeros(kmean_ref.shape, F32)
        for g in range(heads):
            vt_ref[g] = v_ref[0, :, _head_cols(g)].astype(F32).T.astype(BF16)
            for n in range(n_blocks):
                kb = k_ref[0, n * blk:(n + 1) * blk, _head_cols(g)].astype(F32)
                kmean_ref[g, n:n + 1, :] = jnp.sum(kb, axis=0, keepdims=True) * (1.0 / blk)

    c = (HEAD_DIM ** -0.5) * LOG2_E
    own = pl.multiple_of(qi * blk, blk)
    qs = [q_ref[0, :, _head_cols(g)] for g in range(heads)]
    scores = [_scores_t(qs[g], k_ref[0, pl.ds(own, blk), _head_cols(g)]) for g in range(heads)]
    biases = [_moba_block_bias(kmean_ref[g], qs[g], qi, n_blocks) for g in range(heads)]
    carry0 = []
    for g in range(heads):
        carry0 += _first_block(scores[g], vt_ref[g, :, pl.ds(own, blk)], c, acc_ref.at[g])

    def body(n, carry):
        off = pl.multiple_of(n * blk, blk)
        scores = [_scores_t(qs[g], k_ref[0, pl.ds(off, blk), _head_cols(g)]) for g in range(heads)]
        row = lax.broadcasted_iota(I32, biases[0].shape, 0)
        out = []
        for g in range(heads):
            bias = jnp.sum(jnp.where(row == n, biases[g], 0.0), axis=0, keepdims=True)
            out += _next_block(scores[g], vt_ref[g, :, pl.ds(off, blk)], bias,
                               c, carry[2 * g], carry[2 * g + 1], acc_ref.at[g])
        return tuple(out)

    final = lax.fori_loop(0, qi, body, tuple(carry0))
    for g in range(heads):
        o_ref[0, :, _head_cols(g)] = (acc_ref[g] / final[2 * g + 1]).T.astype(o_ref.dtype)


def moba_attention(qa, ka, va, bsz, seq):
    blk = MOBA_BLOCK
    n_blocks = seq // blk
    heads = HEADS_PER_STEP
    gate_rows = -(-n_blocks // 8) * 8
    q3, k3, v3 = (t.reshape(bsz, seq, GROUP_WIDTH) for t in (qa, ka, va))
    q_spec = pl.BlockSpec((1, blk, heads * HEAD_DIM), lambda b, h, i: (b, i, h))
    kv_spec = pl.BlockSpec((1, seq, heads * HEAD_DIM), lambda b, h, i: (b, 0, h))
    out = pl.pallas_call(
        functools.partial(_moba_kernel, n_blocks=n_blocks),
        grid=(bsz, N_HEADS // heads, n_blocks),
        in_specs=[q_spec, kv_spec, kv_spec],
        out_specs=q_spec,
        out_shape=jax.ShapeDtypeStruct((bsz, seq, GROUP_WIDTH), BF16),
        scratch_shapes=[pltpu.VMEM((heads, gate_rows, HEAD_DIM), F32),
                        pltpu.VMEM((heads, HEAD_DIM, seq), BF16),
                        pltpu.VMEM((heads, HEAD_DIM, blk), F32)],
        compiler_params=_params(("arbitrary", "arbitrary", "arbitrary")),
        name="moba_attention",
    )(q3, k3, v3)
    return out.reshape(bsz * seq, GROUP_WIDTH)


def _mla_kernel(q_ref, kn_ref, kr_ref, v_ref, o_ref, kcat_ref, vt_ref, acc_ref):
    qi = pl.program_id(2)
    blk = ATTN_BLOCK

    heads = HEADS_PER_STEP

    @pl.when(qi == 0)
    def _():
        for g in range(heads):
            vt_ref[g] = v_ref[0, :, _head_cols(g)].astype(F32).T.astype(BF16)
            kcat_ref[g, :, :HEAD_DIM] = kn_ref[0, :, _head_cols(g)]
            kcat_ref[g, :, HEAD_DIM:] = kr_ref[0]

    c = (MLA_QK_DIM ** -0.5) * LOG2_E
    own = pl.multiple_of(qi * blk, blk)
    qs = [q_ref[0, :, _head_cols(g, 2 * HEAD_DIM)] for g in range(heads)]
    scores = [_scores_t(qs[g], kcat_ref[g, pl.ds(own, blk), :]) for g in range(heads)]
    carry0 = []
    for g in range(heads):
        carry0 += _first_block(scores[g], vt_ref[g, :, pl.ds(own, blk)], c, acc_ref.at[g])

    def body(n, carry):
        off = pl.multiple_of(n * blk, blk)
        scores = [_scores_t(qs[g], kcat_ref[g, pl.ds(off, blk), :]) for g in range(heads)]
        out = []
        for g in range(heads):
            out += _next_block(scores[g], vt_ref[g, :, pl.ds(off, blk)], None, c,
                               carry[2 * g], carry[2 * g + 1], acc_ref.at[g])
        return tuple(out)

    final = lax.fori_loop(0, qi, body, tuple(carry0))
    for g in range(heads):
        o_ref[0, :, _head_cols(g)] = (acc_ref[g] / final[2 * g + 1]).T.astype(o_ref.dtype)


def mla_attention(qm, kn, kr, vm, bsz, seq):
    blk = ATTN_BLOCK
    heads = HEADS_PER_STEP
    q3 = qm.reshape(bsz, seq, 2 * GROUP_WIDTH)
    kn3, vm3 = kn.reshape(bsz, seq, GROUP_WIDTH), vm.reshape(bsz, seq, GROUP_WIDTH)
    kr3 = kr.reshape(bsz, seq, HEAD_DIM)
    out = pl.pallas_call(
        _mla_kernel,
        grid=(bsz, N_HEADS // heads, seq // blk),
        in_specs=[pl.BlockSpec((1, blk, heads * 2 * HEAD_DIM), lambda b, h, i: (b, i, h)),
                  pl.BlockSpec((1, seq, heads * HEAD_DIM), lambda b, h, i: (b, 0, h)),
                  pl.BlockSpec((1, seq, HEAD_DIM), lambda b, h, i: (b, 0, 0)),
                  pl.BlockSpec((1, seq, heads * HEAD_DIM), lambda b, h, i: (b, 0, h))],
        out_specs=pl.BlockSpec((1, blk, heads * HEAD_DIM), lambda b, h, i: (b, i, h)),
        out_shape=jax.ShapeDtypeStruct((bsz, seq, GROUP_WIDTH), BF16),
        scratch_shapes=[pltpu.VMEM((heads, seq, 2 * HEAD_DIM), BF16),
                        pltpu.VMEM((heads, HEAD_DIM, seq), BF16),
                        pltpu.VMEM((heads, HEAD_DIM, blk), F32)],
        compiler_params=_params(("arbitrary", "arbitrary", "arbitrary")),
        name="mla_attention",
    )(q3, kn3, kr3, vm3)
    return out.reshape(bsz * seq, GROUP_WIDTH)


def _layer_norm(y, g, b):
    mu = jnp.mean(y, axis=-1, keepdims=True)
    yc = y - mu
    var = jnp.mean(yc * yc, axis=-1, keepdims=True)
    return yc * lax.rsqrt(var + LN_EPS) * g + b


def _outproj_kernel(a_ref, m_ref, x_ref, woa_ref, wom_ref, g1_ref, sc2_ref, sh2_ref, g2_ref,
                    lng_ref, lnb_ref, wsg_ref, wsu_ref, wsd_ref, h2_ref, base_ref):
    f = (jnp.dot(a_ref[...], woa_ref[...], preferred_element_type=F32)
         + jnp.dot(m_ref[...], wom_ref[...], preferred_element_type=F32))
    x1 = _layer_norm(DEEPNORM_ALPHA * x_ref[...] + (1.0 + g1_ref[0]) * f, lng_ref[...], lnb_ref[...])
    h2 = x1 * (1.0 + sc2_ref[0]) + sh2_ref[0]
    h2_ref[...] = h2
    hb = h2.astype(BF16)
    act = (_silu(jnp.dot(hb, wsg_ref[...], preferred_element_type=F32))
           * jnp.dot(hb, wsu_ref[...], preferred_element_type=F32)).astype(BF16)
    shared = jnp.dot(act, wsd_ref[...], preferred_element_type=F32)
    base_ref[...] = DEEPNORM_ALPHA * x1 + (1.0 + g2_ref[0]) * shared


def output_projection(attn_a, attn_m, x2d, w_o, g1, sc2, sh2, g2, ln_g, ln_b,
                      ws_gate, ws_up, ws_down, seq):
    t_tokens, d = x2d.shape
    tm = 256
    tiles_per_seq = seq // tm
    gw = GROUP_WIDTH
    ff = ws_gate.shape[1]

    def const(shape):
        return pl.BlockSpec(shape, lambda i: (0,) * len(shape), pipeline_mode=pl.Buffered(1))

    def rows(width):
        return pl.BlockSpec((tm, width), lambda i: (i, 0))

    mod_spec = pl.BlockSpec((1, 1, d), lambda i: (i // tiles_per_seq, 0, 0))
    return pl.pallas_call(
        _outproj_kernel,
        grid=(t_tokens // tm,),
        in_specs=[rows(gw), rows(gw), rows(d), const((gw, d)), const((gw, d)),
                  mod_spec, mod_spec, mod_spec, mod_spec, const((1, d)), const((1, d)),
                  const((d, ff)), const((d, ff)), const((ff, d))],
        out_specs=[rows(d), rows(d)],
        out_shape=[jax.ShapeDtypeStruct((t_tokens, d), F32)] * 2,
        compiler_params=_params(("arbitrary",)),
        name="output_projection",
    )(attn_a, attn_m, x2d, w_o[:gw].astype(BF16), w_o[gw:].astype(BF16), g1, sc2, sh2, g2,
      ln_g.reshape(1, d), ln_b.reshape(1, d),
      ws_gate.astype(BF16), ws_up.astype(BF16), ws_down.astype(BF16))


ROUTER_TILE = 512


def _router_kernel(h_ref, wrt_ref, bias_ref, tri_ref, ltri_ref,
                   e_ref, pos_ref, w_ref, cnt_ref, run_ref, msk_ref):
    tr = ROUTER_TILE
    epg = N_EXPERTS // N_GROUPS

    @pl.when(pl.program_id(0) == 0)
    def _():
        run_ref[...] = jnp.zeros(run_ref.shape, F32)

    logits = lax.dot_general(wrt_ref[...], h_ref[...], NT_DIMS, precision=lax.Precision.HIGHEST,
                             preferred_element_type=F32)
    scores = jax.nn.sigmoid(logits)
    sel_scores = scores + bias_ref[...]

    sub = lax.broadcasted_iota(I32, (epg, tr), 0)
    grp_rows = []
    for g in range(N_GROUPS):
        xg = sel_scores[g * epg:(g + 1) * epg, :]
        m1 = jnp.max(xg, axis=0, keepdims=True)
        first = jnp.min(jnp.where(xg == m1, sub, epg), axis=0, keepdims=True)
        m2 = jnp.max(jnp.where(sub == first, -jnp.inf, xg), axis=0, keepdims=True)
        grp_rows.append(m1 + m2)
    grp = jnp.concatenate(grp_rows, axis=0)

    giota = lax.broadcasted_iota(I32, (N_GROUPS, tr), 0)
    gcnt = jnp.zeros((N_GROUPS, tr), F32)
    for m in range(N_GROUPS):
        rm = grp[m:m + 1, :]
        gcnt = gcnt + jnp.where((rm > grp) | ((rm == grp) & (m < giota)), 1.0, 0.0)
    gkeep = jnp.where(gcnt < float(TOPK_GROUPS), 1.0, 0.0)
    gmask = jnp.concatenate(
        [jnp.broadcast_to(gkeep[g:g + 1, :], (epg, tr)) for g in range(N_GROUPS)], axis=0)
    masked = jnp.where(gmask > 0.0, sel_scores, -jnp.inf)
    msk_ref[...] = masked

    eiota = lax.broadcasted_iota(I32, (N_EXPERTS, tr), 0)

    def rank_body(m, cnt):
        rm = msk_ref[pl.ds(m, 1), :]
        return cnt + jnp.where((rm > masked) | ((rm == masked) & (m < eiota)), 1.0, 0.0)

    ecnt = lax.fori_loop(0, N_EXPERTS, rank_body, jnp.zeros((N_EXPERTS, tr), F32))
    sel = ecnt < float(TOP_K)

    sw = jnp.where(sel, scores, 0.0)
    wn = sw / jnp.sum(sw, axis=0, keepdims=True) * ROUTED_SCALE

    selb = jnp.where(sel, 1.0, 0.0).astype(BF16)
    cum = jnp.dot(selb, tri_ref[...], preferred_element_type=F32)
    pos = run_ref[...] + cum - 1.0
    run_ref[...] = run_ref[...] + cum[:, tr - 1:tr]
    cnt_ref[...] = jnp.broadcast_to(run_ref[...], cnt_ref.shape).astype(I32)

    slot = jnp.dot(ltri_ref[...], selb, preferred_element_type=F32)
    ef = eiota.astype(F32)
    e_rows, p_rows, w_rows = [], [], []
    for j in range(TOP_K):
        mj = sel & (slot == float(j))
        e_rows.append(jnp.sum(jnp.where(mj, ef, 0.0), axis=0, keepdims=True))
        p_rows.append(jnp.sum(jnp.where(mj, pos, 0.0), axis=0, keepdims=True))
        w_rows.append(jnp.sum(jnp.where(mj, wn, 0.0), axis=0, keepdims=True))
    e_ref[...] = jnp.concatenate(e_rows, axis=0).astype(I32)
    pos_ref[...] = jnp.concatenate(p_rows, axis=0).astype(I32)
    w_ref[...] = jnp.concatenate(w_rows, axis=0)


def router(h2, w_router, router_bias):
    t_tokens, d = h2.shape
    tr = ROUTER_TILE
    tri = jnp.asarray(np.triu(np.ones((tr, tr), np.float32)), BF16)
    ltri = jnp.asarray(np.tril(np.ones((N_EXPERTS, N_EXPERTS), np.float32), -1), BF16)

    def const(shape):
        return pl.BlockSpec(shape, lambda i: (0,) * len(shape))

    tok = pl.BlockSpec((TOP_K, tr), lambda i: (0, i))
    e_t, pos_t, w_t, cnt = pl.pallas_call(
        _router_kernel,
        grid=(t_tokens // tr,),
        in_specs=[pl.BlockSpec((tr, d), lambda i: (i, 0)), const((N_EXPERTS, d)),
                  const((N_EXPERTS, 1)), const((tr, tr)), const((N_EXPERTS, N_EXPERTS))],
        out_specs=[tok, tok, tok, const((N_EXPERTS, 128))],
        out_shape=[jax.ShapeDtypeStruct((TOP_K, t_tokens), I32),
                   jax.ShapeDtypeStruct((TOP_K, t_tokens), I32),
                   jax.ShapeDtypeStruct((TOP_K, t_tokens), F32),
                   jax.ShapeDtypeStruct((N_EXPERTS, 128), I32)],
        scratch_shapes=[pltpu.VMEM((N_EXPERTS, 1), F32), pltpu.VMEM((N_EXPERTS, tr), F32)],
        compiler_params=_params(("arbitrary",)),
        name="router",
    )(h2, w_router.T, router_bias.reshape(N_EXPERTS, 1), tri, ltri)
    return e_t, pos_t, w_t, cnt[:, 0]


def _experts_kernel(be_ref, nused_ref, rt_cur_ref, rt_next_ref, h2_ref, wg_ref, wu_ref, wd_ref,
                    y_ref, xbuf_a, xbuf_b, sem, wg_bf, wu_bf, wd_bf):
    i = pl.program_id(0)
    nused = nused_ref[0]
    rb = MOE_ROW_BLOCK

    def row_copy(tok, buf, r, s):
        return pltpu.make_async_copy(h2_ref.at[pl.ds(tok, 1), :], buf.at[pl.ds(r, 1), :], sem.at[s])

    def wait_rows(buf, s):
        pltpu.make_async_copy(h2_ref.at[pl.ds(0, rb), :], buf, sem.at[s]).wait()

    @pl.when(i == 0)
    def _():
        def body(r, carry):
            row_copy(rt_cur_ref[r], xbuf_a, r, 0).start()
            return carry
        lax.fori_loop(0, rb, body, 0, unroll=8)

    def step(cur, cur_s, nxt, nxt_s):
        wait_rows(cur, cur_s)
        for r in range(rb):
            row_copy(rt_next_ref[r], nxt, r, nxt_s).start()
        xb = cur[...].astype(BF16)
        act = (_silu(jnp.dot(xb, wg_bf[...], preferred_element_type=F32))
               * jnp.dot(xb, wu_bf[...], preferred_element_type=F32)).astype(BF16)
        y_ref[...] = jnp.dot(act, wd_bf[...], preferred_element_type=F32)

    @pl.when(i < nused)
    def _():
        @pl.when((i == 0) | (be_ref[i] != be_ref[jnp.maximum(i - 1, 0)]))
        def _():
            wg_bf[...] = wg_ref[0].astype(BF16)
            wu_bf[...] = wu_ref[0].astype(BF16)
            wd_bf[...] = wd_ref[0].astype(BF16)

        @pl.when(i % 2 == 0)
        def _():
            step(xbuf_a, 0, xbuf_b, 1)

        @pl.when(i % 2 == 1)
        def _():
            step(xbuf_b, 1, xbuf_a, 0)

    @pl.when((i == nused - 1) & (i % 2 == 0))
    def _():
        wait_rows(xbuf_b, 1)

    @pl.when((i == nused - 1) & (i % 2 == 1))
    def _():
        wait_rows(xbuf_a, 0)

    @pl.when(i >= nused)
    def _():
        y_ref[...] = jnp.zeros(y_ref.shape, F32)


def routed_experts(h2, row_tok, block_e, nused, w_gate, w_up, w_down):
    t_tokens, d = h2.shape
    rb = MOE_ROW_BLOCK
    n_rows = row_tok.shape[0]
    n_blocks = n_rows // rb
    ff = w_gate.shape[2]
    grid_spec = pltpu.PrefetchScalarGridSpec(
        num_scalar_prefetch=2,
        grid=(n_blocks,),
        in_specs=[pl.BlockSpec((rb,), lambda i, be, nu: (i,), memory_space=pltpu.SMEM),
                  pl.BlockSpec((rb,), lambda i, be, nu: (jnp.minimum(i + 1, n_blocks - 1),),
                               memory_space=pltpu.SMEM),
                  pl.BlockSpec(memory_space=pl.ANY),
                  pl.BlockSpec((1, d, ff), lambda i, be, nu: (be[i], 0, 0)),
                  pl.BlockSpec((1, d, ff), lambda i, be, nu: (be[i], 0, 0)),
                  pl.BlockSpec((1, ff, d), lambda i, be, nu: (be[i], 0, 0))],
        out_specs=pl.BlockSpec((rb, d), lambda i, be, nu: (i, 0)),
        scratch_shapes=[pltpu.VMEM((rb, d), F32), pltpu.VMEM((rb, d), F32),
                        pltpu.SemaphoreType.DMA((2,)),
                        pltpu.VMEM((d, ff), BF16), pltpu.VMEM((d, ff), BF16),
                        pltpu.VMEM((ff, d), BF16)],
    )
    return pl.pallas_call(
        _experts_kernel,
        grid_spec=grid_spec,
        out_shape=jax.ShapeDtypeStruct((n_rows, d), F32),
        compiler_params=_params(("arbitrary",)),
        name="routed_experts",
    )(block_e, nused, row_tok, row_tok, h2, w_gate, w_up, w_down)


COMBINE_TILE = 128


def _combine_kernel(d_cur_ref, d_next_ref, y_ref, w_ref, base_ref, g2_ref, lng_ref, lnb_ref,
                    o_ref, ybuf_a, ybuf_b, sem):
    i = pl.program_id(0)
    n_steps = pl.num_programs(0)
    tc = COMBINE_TILE
    n_copies = TOP_K * tc

    def row_copy(row, buf, r, s):
        return pltpu.make_async_copy(y_ref.at[pl.ds(row, 1), :], buf.at[pl.ds(r, 1), :], sem.at[s])

    def wait_rows(buf, s):
        pltpu.make_async_copy(y_ref.at[pl.ds(0, n_copies), :], buf, sem.at[s]).wait()

    @pl.when(i == 0)
    def _():
        def body(r, carry):
            row_copy(d_cur_ref[r], ybuf_a, r, 0).start()
            return carry
        lax.fori_loop(0, n_copies, body, 0, unroll=8)

    def step(cur, cur_s, nxt, nxt_s):
        wait_rows(cur, cur_s)
        for r in range(n_copies):
            row_copy(d_next_ref[r], nxt, r, nxt_s).start(priority=r % 2)
        w = w_ref[...]
        routed = w[:, 0:1] * cur[0:tc, :]
        for k in range(1, TOP_K):
            routed = routed + w[:, k:k + 1] * cur[k * tc:(k + 1) * tc, :]
        y = base_ref[...] + (1.0 + g2_ref[0]) * routed
        o_ref[...] = _layer_norm(y, lng_ref[...], lnb_ref[...])

    @pl.when(i % 2 == 0)
    def _():
        step(ybuf_a, 0, ybuf_b, 1)

    @pl.when(i % 2 == 1)
    def _():
        step(ybuf_b, 1, ybuf_a, 0)

    @pl.when((i == n_steps - 1) & (i % 2 == 0))
    def _():
        wait_rows(ybuf_b, 1)

    @pl.when((i == n_steps - 1) & (i % 2 == 1))
    def _():
        wait_rows(ybuf_a, 0)


def combine(y_sorted, dest_tiles, w_tok, base, g2, ln_g, ln_b, seq):
    t_tokens, d = base.shape
    tc = COMBINE_TILE
    n_steps = t_tokens // tc
    tiles_per_seq = seq // tc
    n_copies = TOP_K * tc

    def const(shape):
        return pl.BlockSpec(shape, lambda i: (0,) * len(shape))

    return pl.pallas_call(
        _combine_kernel,
        grid=(n_steps,),
        in_specs=[pl.BlockSpec((n_copies,), lambda i: (i,), memory_space=pltpu.SMEM),
                  pl.BlockSpec((n_copies,), lambda i: (jnp.minimum(i + 1, n_steps - 1),),
                               memory_space=pltpu.SMEM),
                  pl.BlockSpec(memory_space=pl.ANY),
                  pl.BlockSpec((tc, TOP_K), lambda i: (i, 0)),
                  pl.BlockSpec((tc, d), lambda i: (i, 0)),
                  pl.BlockSpec((1, 1, d), lambda i: (i // tiles_per_seq, 0, 0)),
                  const((1, d)), const((1, d))],
        out_specs=pl.BlockSpec((tc, d), lambda i: (i, 0)),
        out_shape=jax.ShapeDtypeStruct((t_tokens, d), F32),
        scratch_shapes=[pltpu.VMEM((n_copies, d), F32), pltpu.VMEM((n_copies, d), F32),
                        pltpu.SemaphoreType.DMA((2,))],
        compiler_params=_params(("arbitrary",)),
        name="combine",
    )(dest_tiles, dest_tiles, y_sorted, w_tok, base, g2, ln_g.reshape(1, d), ln_b.reshape(1, d))


def _dispatch_tables(e_t, pos_t, counts, t_tokens):
    rb = MOE_ROW_BLOCK
    n_blocks = t_tokens * TOP_K // rb + N_EXPERTS
    n_rows = n_blocks * rb
    padded = (counts + rb - 1) // rb * rb
    pends = jnp.cumsum(padded)
    pstarts = pends - padded
    experts = jnp.arange(N_EXPERTS, dtype=I32)
    pstart_of = jnp.sum(jnp.where(e_t[..., None] == experts, pstarts.astype(I32), 0), axis=-1)
    dest = pstart_of + pos_t
    tok = jnp.broadcast_to(jnp.arange(t_tokens, dtype=I32)[None, :], dest.shape)
    row_tok = jnp.zeros((n_rows,), I32).at[dest.reshape(-1)].set(
        tok.reshape(-1), unique_indices=True, mode="promise_in_bounds")
    nused = (pends[-1] // rb).astype(I32)
    blk = jnp.arange(n_blocks, dtype=I32)
    last_blk = jnp.minimum(blk, jnp.maximum(nused - 1, 0))
    block_e = jnp.sum((pends[None, :] <= (last_blk * rb)[:, None]).astype(I32), axis=1)
    block_e = jnp.minimum(block_e, N_EXPERTS - 1).astype(I32)
    return dest.astype(I32), row_tok, block_e, nused.reshape(1)


def _layer(x, c, w_ada, b_ada, w_in, g_cq, w_uq, g_ckv, w_ukv, w_o, ln1_g, ln1_b,
           w_router, router_bias, w_gate, w_up, w_down, ws_gate, ws_up, ws_down, ln2_g, ln2_b):
    bsz, seq, d = x.shape
    t_tokens = bsz * seq
    x2d = x.reshape(t_tokens, d)

    mod = ada_modulation(c, w_ada, b_ada)
    sh1, sc1, g1, sh2, sc2, g2 = [m.reshape(bsz, 1, d) for m in jnp.split(mod, 6, axis=-1)]

    qa, ka, va, qm, kn, vm, kr = input_projection(x2d, sc1, sh1, w_in, g_cq, w_uq, g_ckv, w_ukv,
                                                  bsz, seq)
    attn_a = moba_attention(qa, ka, va, bsz, seq)
    attn_m = mla_attention(qm, kn, kr, vm, bsz, seq)
    h2, base = output_projection(attn_a, attn_m, x2d, w_o, g1, sc2, sh2, g2, ln1_g, ln1_b,
                                 ws_gate, ws_up, ws_down, seq)

    e_t, pos_t, w_t, counts = router(h2, w_router, router_bias)
    dest, row_tok, block_e, nused = _dispatch_tables(e_t, pos_t, counts, t_tokens)
    y_sorted = routed_experts(h2, row_tok, block_e, nused, w_gate, w_up, w_down)

    tc = COMBINE_TILE
    dest_tiles = dest.reshape(TOP_K, t_tokens // tc, tc).transpose(1, 0, 2).reshape(-1)
    out = combine(y_sorted, dest_tiles, w_t.T, base, g2, ln2_g, ln2_b, seq)
    return out.reshape(bsz, seq, d)


def kernel(x, c, w_ada, b_ada, w_in, g_cq, w_uq, g_ckv, w_ukv, w_o, ln1_g, ln1_b, w_router,
           router_bias, w_gate, w_up, w_down, ws_gate, ws_up, ws_down, ln2_g, ln2_b):
    depth = w_ada.shape[0]
    assert depth == 1, "DeepNorm constants are baked for a single layer"
    return _layer(x, c, w_ada[0], b_ada[0], w_in[0], g_cq[0], w_uq[0], g_ckv[0], w_ukv[0], w_o[0],
                  ln1_g[0], ln1_b[0], w_router[0], router_bias[0], w_gate[0], w_up[0], w_down[0],
                  ws_gate[0], ws_up[0], ws_down[0], ln2_g[0], ln2_b[0])
```

```python
import functools

import jax
import jax.numpy as jnp
import numpy as np
from jax import lax
from jax.experimental import pallas as pl
from jax.experimental.pallas import tpu as pltpu

F32 = jnp.float32
BF16 = jnp.bfloat16
I32 = jnp.int32

D_MODEL = 2048
HEAD_DIM = 128
N_HEADS = 8
MOBA_BLOCK = 256
MOBA_TOPK = 3
MLA_ROPE_DIM = 64
MLA_RANK = 512
MLA_QK_DIM = HEAD_DIM + MLA_ROPE_DIM
GROUP_WIDTH = N_HEADS * HEAD_DIM
ROPE_THETA = 10000.0
N_EXPERTS = 64
TOP_K = 8
N_GROUPS = 8
TOPK_GROUPS = 4
EXPERT_FF = 512
ROUTED_SCALE = 2.5
MOE_ROW_BLOCK = 256
LN_EPS = 1e-5
RMS_EPS = 1e-6
DEEPNORM_ALPHA = 2.0 ** 0.25

VMEM_LIMIT_BYTES = 56 * 1024 * 1024

NT_DIMS = (((1,), (1,)), ((), ()))


def _params(semantics):
    return pltpu.CompilerParams(dimension_semantics=semantics,
                                vmem_limit_bytes=VMEM_LIMIT_BYTES)


def _silu(x):
    return x * jax.nn.sigmoid(x)


PACK_CHUNKS = 8
PACK_LANES = 128
U32 = jnp.uint32
HIGH_HALF = np.uint32(0xFFFF0000)


def _store_packed_rows(ref, x):
    n_rows, half = x.shape[0], x.shape[1] // 2
    lo = lax.bitcast_convert_type(x[:, :half].astype(BF16).astype(F32), U32) >> 16
    hi = lax.bitcast_convert_type(x[:, half:].astype(BF16).astype(F32), U32) & HIGH_HALF
    packed = lo | hi
    for j in range(PACK_CHUNKS):
        ref[pl.ds(j, n_rows, stride=PACK_CHUNKS), :] = packed[:, j * PACK_LANES:(j + 1) * PACK_LANES]


def _load_packed_rows(ref, first_row, n_rows):
    lo, hi = [], []
    for j in range(PACK_CHUNKS):
        u = ref[pl.ds(first_row * PACK_CHUNKS + j, n_rows, stride=PACK_CHUNKS), :]
        lo.append(lax.bitcast_convert_type(u << 16, F32))
        hi.append(lax.bitcast_convert_type(u & HIGH_HALF, F32))
    return jnp.concatenate(lo, axis=1), jnp.concatenate(hi, axis=1)


def _ada_kernel(c_ref, w_ref, b_ref, o_ref):
    a = _silu(c_ref[...]).astype(BF16)
    o_ref[...] = jnp.dot(a, w_ref[...].astype(BF16), preferred_element_type=F32) + b_ref[...]


def ada_modulation(c, w_ada, b_ada):
    bsz, d = c.shape
    rows = -(-bsz // 16) * 16
    c_pad = jnp.pad(c, ((0, rows - bsz), (0, 0)))
    n = w_ada.shape[1]
    tn = 1024
    out = pl.pallas_call(
        _ada_kernel,
        grid=(n // tn,),
        in_specs=[pl.BlockSpec((rows, d), lambda j: (0, 0)),
                  pl.BlockSpec((d, tn), lambda j: (0, j)),
                  pl.BlockSpec((1, tn), lambda j: (0, j))],
        out_specs=pl.BlockSpec((rows, tn), lambda j: (0, j)),
        out_shape=jax.ShapeDtypeStruct((rows, n), F32),
        compiler_params=_params(("arbitrary",)),
        name="ada_modulation",
    )(c_pad, w_ada, b_ada.reshape(1, n))
    return out[:bsz]


def _rms(x, g):
    return x * lax.rsqrt(jnp.mean(x * x, axis=-1, keepdims=True) + RMS_EPS) * g


def _rope64(x, cos, sin):
    return x * cos + pltpu.roll(x, 64, axis=1) * sin


def _rope32(x, cos, s1, s2):
    return x * cos + pltpu.roll(x, 96, axis=1) * s1 + pltpu.roll(x, 32, axis=1) * s2


def _inproj_kernel(x_ref, sc_ref, sh_ref, w_ref, wuq_ref, wuk_ref, wuv_ref, gcq_ref, gckv_ref,
                   cosa_ref, sina_ref, cosr_ref, s1_ref, s2_ref,
                   qa_ref, ka_ref, va_ref, qm_ref, kn_ref, vm_ref, kr_ref):
    h = (x_ref[...] * (1.0 + sc_ref[0]) + sh_ref[0]).astype(BF16)
    cosa, sina = cosa_ref[...], sina_ref[...]
    cosr, s1, s2 = cosr_ref[...], s1_ref[...], s2_ref[...]
    gw = GROUP_WIDTH

    def proj(lo, hi):
        return jnp.dot(h, w_ref[:, lo:hi], preferred_element_type=F32)

    for out_ref, base in ((qa_ref, 0), (ka_ref, gw)):
        t = proj(base, base + gw)
        for hd in range(N_HEADS):
            sl = slice(hd * HEAD_DIM, (hd + 1) * HEAD_DIM)
            out_ref[:, sl] = _rope64(t[:, sl], cosa, sina).astype(BF16)
    va_ref[...] = proj(2 * gw, 3 * gw).astype(BF16)

    cq = _rms(proj(3 * gw, 3 * gw + MLA_RANK), gcq_ref[...]).astype(BF16)
    qm = jnp.dot(cq, wuq_ref[...], preferred_element_type=F32)
    for hd in range(N_HEADS):
        b0 = hd * 2 * HEAD_DIM
        qm_ref[:, b0:b0 + HEAD_DIM] = qm[:, b0:b0 + HEAD_DIM].astype(BF16)
        qm_ref[:, b0 + HEAD_DIM:b0 + 2 * HEAD_DIM] = _rope32(
            qm[:, b0 + HEAD_DIM:b0 + 2 * HEAD_DIM], cosr, s1, s2).astype(BF16)

    ckv = _rms(proj(3 * gw + MLA_RANK, 3 * gw + 2 * MLA_RANK), gckv_ref[...]).astype(BF16)
    kn_ref[...] = jnp.dot(ckv, wuk_ref[...], preferred_element_type=F32).astype(BF16)
    vm_ref[...] = jnp.dot(ckv, wuv_ref[...], preferred_element_type=F32).astype(BF16)

    kr = proj(3 * gw + 2 * MLA_RANK, 3 * gw + 2 * MLA_RANK + HEAD_DIM)
    kr_ref[...] = _rope32(kr, cosr, s1, s2).astype(BF16)


def _rope_tables(seq):
    pos = jnp.arange(seq, dtype=F32)[:, None]
    inv_a = 1.0 / (ROPE_THETA ** (jnp.arange(0, HEAD_DIM, 2, dtype=F32) / HEAD_DIM))
    ang_a = pos * inv_a[None, :]
    cosa = jnp.concatenate([jnp.cos(ang_a), jnp.cos(ang_a)], axis=1)
    sina = jnp.concatenate([-jnp.sin(ang_a), jnp.sin(ang_a)], axis=1)
    inv_r = 1.0 / (ROPE_THETA ** (jnp.arange(0, MLA_ROPE_DIM, 2, dtype=F32) / MLA_ROPE_DIM))
    ang_r = pos * inv_r[None, :]
    cr, sr = jnp.cos(ang_r), jnp.sin(ang_r)
    z32 = jnp.zeros_like(cr)
    z64 = jnp.zeros((seq, 64), F32)
    cosr = jnp.concatenate([cr, cr, z64], axis=1)
    s1 = jnp.concatenate([-sr, z32, z64], axis=1)
    s2 = jnp.concatenate([z32, sr, z64], axis=1)
    return cosa, sina, cosr, s1, s2


def input_projection(x2d, sc1, sh1, w_in, g_cq, w_uq, g_ckv, w_ukv, bsz, seq):
    t_tokens, d = x2d.shape
    tm = 256
    tiles_per_seq = seq // tm
    gw = GROUP_WIDTH
    in_w = 3 * gw + 2 * MLA_RANK + HEAD_DIM
    w_cat = jnp.pad(w_in, ((0, 0), (0, in_w - w_in.shape[1]))).astype(BF16)
    wuq = jnp.pad(w_uq.reshape(MLA_RANK, N_HEADS, MLA_QK_DIM),
                  ((0, 0), (0, 0), (0, 2 * HEAD_DIM - MLA_QK_DIM)))
    wuq = wuq.reshape(MLA_RANK, N_HEADS * 2 * HEAD_DIM).astype(BF16)
    wukv = w_ukv.reshape(MLA_RANK, N_HEADS, 2 * HEAD_DIM)
    wuk = wukv[:, :, :HEAD_DIM].reshape(MLA_RANK, gw).astype(BF16)
    wuv = wukv[:, :, HEAD_DIM:].reshape(MLA_RANK, gw).astype(BF16)
    tables = _rope_tables(seq)

    def const(shape):
        return pl.BlockSpec(shape, lambda i: (0,) * len(shape), pipeline_mode=pl.Buffered(1))

    def rows(width):
        return pl.BlockSpec((tm, width), lambda i: (i, 0))

    mod_spec = pl.BlockSpec((1, 1, d), lambda i: (i // tiles_per_seq, 0, 0))
    tab_spec = pl.BlockSpec((tm, HEAD_DIM), lambda i: (i % tiles_per_seq, 0))
    widths = (gw, gw, gw, 2 * gw, gw, gw, HEAD_DIM)
    return pl.pallas_call(
        _inproj_kernel,
        grid=(t_tokens // tm,),
        in_specs=[rows(d), mod_spec, mod_spec, const((d, in_w)),
                  const((MLA_RANK, 2 * gw)), const((MLA_RANK, gw)), const((MLA_RANK, gw)),
                  const((1, MLA_RANK)), const((1, MLA_RANK))] + [tab_spec] * 5,
        out_specs=[rows(w) for w in widths],
        out_shape=[jax.ShapeDtypeStruct((t_tokens, w), BF16) for w in widths],
        compiler_params=_params(("arbitrary",)),
        name="input_projection",
    )(x2d, sc1, sh1, w_cat, wuq, wuk, wuv, g_cq.reshape(1, -1), g_ckv.reshape(1, -1), *tables)


ATTN_BLOCK = 256
LOG2_E = 1.4426950408889634


def _causal_mask_t():
    key = lax.broadcasted_iota(I32, (ATTN_BLOCK, ATTN_BLOCK), 0)
    qry = lax.broadcasted_iota(I32, (ATTN_BLOCK, ATTN_BLOCK), 1)
    return key <= qry


def _scores_t(q, k_blk):
    return lax.dot_general(k_blk, q, NT_DIMS, preferred_element_type=F32)


def _first_block(s, vt_blk, c, acc_ref):
    s = jnp.where(_causal_mask_t(), s, -jnp.inf)
    m = jnp.max(s, axis=0, keepdims=True)
    p = jnp.exp2((s - m) * c)
    acc_ref[...] = jnp.dot(vt_blk, p.astype(BF16), preferred_element_type=F32)
    return m, jnp.sum(p, axis=0, keepdims=True)


def _next_block(s, vt_blk, bias, c, m, l, acc_ref):
    m_blk = jnp.max(s, axis=0, keepdims=True)
    if bias is not None:
        m_blk = m_blk + bias
    m_new = jnp.maximum(m, m_blk)
    alpha = jnp.exp2((m - m_new) * c)
    p = jnp.exp2((s - (m_new if bias is None else m_new - bias)) * c)
    acc_ref[...] = alpha * acc_ref[...] + jnp.dot(vt_blk, p.astype(BF16), preferred_element_type=F32)
    return m_new, alpha * l + jnp.sum(p, axis=0, keepdims=True)


HEADS_PER_STEP = 4


def _head_cols(g, width=HEAD_DIM):
    return slice(g * width, (g + 1) * width)


def _moba_block_bias(kmean, q, qi, n_blocks):
    gate = lax.dot_general(kmean, q.astype(F32), NT_DIMS, precision=lax.Precision.HIGHEST,
                           preferred_element_type=F32)
    row = lax.broadcasted_iota(I32, gate.shape, 0)
    gt = jnp.where(row < qi, gate, -jnp.inf)
    cnt = jnp.zeros(gate.shape, F32)
    for m in range(n_blocks - 1):
        gm = gt[m:m + 1, :]
        cnt = cnt + jnp.where((gm > gt) | ((gm == gt) & (m < row)), 1.0, 0.0)
    return jnp.where(cnt < float(MOBA_TOPK), 0.0, -jnp.inf).astype(F32)


def _moba_kernel(q_ref, k_ref, v_ref, o_ref, kmean_ref, vt_ref, acc_ref, *, n_blocks):
    qi = pl.program_id(2)
    blk = MOBA_BLOCK
    heads = HEADS_PER_STEP

    @pl.when(qi == 0)
    def _():
        kmean_ref[...] = jnp.zeros(kmean_ref.shape, F32)
        for g in range(heads):
            vt_ref[g] = v_ref[0, :, _head_cols(g)].astype(F32).T.astype(BF16)
            for n in range(n_blocks):
                kb = k_ref[0, n * blk:(n + 1) * blk, _head_cols(g)].astype(F32)
                kmean_ref[g, n:n + 1, :] = jnp.sum(kb, axis=0, keepdims=True) * (1.0 / blk)

    c = (HEAD_DIM ** -0.5) * LOG2_E
    own = pl.multiple_of(qi * blk, blk)
    qs = [q_ref[0, :, _head_cols(g)] for g in range(heads)]
    scores = [_scores_t(qs[g], k_ref[0, pl.ds(own, blk), _head_cols(g)]) for g in range(heads)]
    biases = [_moba_block_bias(kmean_ref[g], qs[g], qi, n_blocks) for g in range(heads)]
    carry0 = []
    for g in range(heads):
        carry0 += _first_block(scores[g], vt_ref[g, :, pl.ds(own, blk)], c, acc_ref.at[g])

    def body(n, carry):
        off = pl.multiple_of(n * blk, blk)
        scores = [_scores_t(qs[g], k_ref[0, pl.ds(off, blk), _head_cols(g)]) for g in range(heads)]
        row = lax.broadcasted_iota(I32, biases[0].shape, 0)
        out = []
        for g in range(heads):
            bias = jnp.sum(jnp.where(row == n, biases[g], 0.0), axis=0, keepdims=True)
            out += _next_block(scores[g], vt_ref[g, :, pl.ds(off, blk)], bias,
                               c, carry[2 * g], carry[2 * g + 1], acc_ref.at[g])
        return tuple(out)

    final = lax.fori_loop(0, qi, body, tuple(carry0))
    for g in range(heads):
        o_ref[0, :, _head_cols(g)] = (acc_ref[g] / final[2 * g + 1]).T.astype(o_ref.dtype)


def moba_attention(qa, ka, va, bsz, seq):
    blk = MOBA_BLOCK
    n_blocks = seq // blk
    heads = HEADS_PER_STEP
    gate_rows = -(-n_blocks // 8) * 8
    q3, k3, v3 = (t.reshape(bsz, seq, GROUP_WIDTH) for t in (qa, ka, va))
    q_spec = pl.BlockSpec((1, blk, heads * HEAD_DIM), lambda b, h, i: (b, i, h))
    kv_spec = pl.BlockSpec((1, seq, heads * HEAD_DIM), lambda b, h, i: (b, 0, h))
    out = pl.pallas_call(
        functools.partial(_moba_kernel, n_blocks=n_blocks),
        grid=(bsz, N_HEADS // heads, n_blocks),
        in_specs=[q_spec, kv_spec, kv_spec],
        out_specs=q_spec,
        out_shape=jax.ShapeDtypeStruct((bsz, seq, GROUP_WIDTH), BF16),
        scratch_shapes=[pltpu.VMEM((heads, gate_rows, HEAD_DIM), F32),
                        pltpu.VMEM((heads, HEAD_DIM, seq), BF16),
                        pltpu.VMEM((heads, HEAD_DIM, blk), F32)],
        compiler_params=_params(("arbitrary", "arbitrary", "arbitrary")),
        name="moba_attention",
    )(q3, k3, v3)
    return out.reshape(bsz * seq, GROUP_WIDTH)


def _mla_kernel(q_ref, kn_ref, kr_ref, v_ref, o_ref, kcat_ref, vt_ref, acc_ref):
    qi = pl.program_id(2)
    blk = ATTN_BLOCK

    heads = HEADS_PER_STEP

    @pl.when(qi == 0)
    def _():
        for g in range(heads):
            vt_ref[g] = v_ref[0, :, _head_cols(g)].astype(F32).T.astype(BF16)
            kcat_ref[g, :, :HEAD_DIM] = kn_ref[0, :, _head_cols(g)]
            kcat_ref[g, :, HEAD_DIM:] = kr_ref[0]

    c = (MLA_QK_DIM ** -0.5) * LOG2_E
    own = pl.multiple_of(qi * blk, blk)
    qs = [q_ref[0, :, _head_cols(g, 2 * HEAD_DIM)] for g in range(heads)]
    scores = [_scores_t(qs[g], kcat_ref[g, pl.ds(own, blk), :]) for g in range(heads)]
    carry0 = []
    for g in range(heads):
        carry0 += _first_block(scores[g], vt_ref[g, :, pl.ds(own, blk)], c, acc_ref.at[g])

    def body(n, carry):
        off = pl.multiple_of(n * blk, blk)
        scores = [_scores_t(qs[g], kcat_ref[g, pl.ds(off, blk), :]) for g in range(heads)]
        out = []
        for g in range(heads):
            out += _next_block(scores[g], vt_ref[g, :, pl.ds(off, blk)], None, c,
                               carry[2 * g], carry[2 * g + 1], acc_ref.at[g])
        return tuple(out)

    final = lax.fori_loop(0, qi, body, tuple(carry0))
    for g in range(heads):
        o_ref[0, :, _head_cols(g)] = (acc_ref[g] / final[2 * g + 1]).T.astype(o_ref.dtype)


def mla_attention(qm, kn, kr, vm, bsz, seq):
    blk = ATTN_BLOCK
    heads = HEADS_PER_STEP
    q3 = qm.reshape(bsz, seq, 2 * GROUP_WIDTH)
    kn3, vm3 = kn.reshape(bsz, seq, GROUP_WIDTH), vm.reshape(bsz, seq, GROUP_WIDTH)
    kr3 = kr.reshape(bsz, seq, HEAD_DIM)
    out = pl.pallas_call(
        _mla_kernel,
        grid=(bsz, N_HEADS // heads, seq // blk),
        in_specs=[pl.BlockSpec((1, blk, heads * 2 * HEAD_DIM), lambda b, h, i: (b, i, h)),
                  pl.BlockSpec((1, seq, heads * HEAD_DIM), lambda b, h, i: (b, 0, h)),
                  pl.BlockSpec((1, seq, HEAD_DIM), lambda b, h, i: (b, 0, 0)),
                  pl.BlockSpec((1, seq, heads * HEAD_DIM), lambda b, h, i: (b, 0, h))],
        out_specs=pl.BlockSpec((1, blk, heads * HEAD_DIM), lambda b, h, i: (b, i, h)),
        out_shape=jax.ShapeDtypeStruct((bsz, seq, GROUP_WIDTH), BF16),
        scratch_shapes=[pltpu.VMEM((heads, seq, 2 * HEAD_DIM), BF16),
                        pltpu.VMEM((heads, HEAD_DIM, seq), BF16),
                        pltpu.VMEM((heads, HEAD_DIM, blk), F32)],
        compiler_params=_params(("arbitrary", "arbitrary", "arbitrary")),
        name="mla_attention",
    )(q3, kn3, kr3, vm3)
    return out.reshape(bsz * seq, GROUP_WIDTH)


def _layer_norm(y, g, b):
    mu = jnp.mean(y, axis=-1, keepdims=True)
    yc = y - mu
    var = jnp.mean(yc * yc, axis=-1, keepdims=True)
    return yc * lax.rsqrt(var + LN_EPS) * g + b


def _outproj_kernel(a_ref, m_ref, x_ref, woa_ref, wom_ref, g1_ref, sc2_ref, sh2_ref, g2_ref,
                    lng_ref, lnb_ref, wsg_ref, wsu_ref, wsd_ref, h2_ref, h2p_ref, base_ref):
    f = (jnp.dot(a_ref[...], woa_ref[...], preferred_element_type=F32)
         + jnp.dot(m_ref[...], wom_ref[...], preferred_element_type=F32))
    x1 = _layer_norm(DEEPNORM_ALPHA * x_ref[...] + (1.0 + g1_ref[0]) * f, lng_ref[...], lnb_ref[...])
    h2 = x1 * (1.0 + sc2_ref[0]) + sh2_ref[0]
    h2_ref[...] = h2
    _store_packed_rows(h2p_ref, h2)
    hb = h2.astype(BF16)
    act = (_silu(jnp.dot(hb, wsg_ref[...], preferred_element_type=F32))
           * jnp.dot(hb, wsu_ref[...], preferred_element_type=F32)).astype(BF16)
    shared = jnp.dot(act, wsd_ref[...], preferred_element_type=F32)
    base_ref[...] = DEEPNORM_ALPHA * x1 + (1.0 + g2_ref[0]) * shared


def output_projection(attn_a, attn_m, x2d, w_o, g1, sc2, sh2, g2, ln_g, ln_b,
                      ws_gate, ws_up, ws_down, seq):
    t_tokens, d = x2d.shape
    tm = 256
    tiles_per_seq = seq // tm
    gw = GROUP_WIDTH
    ff = ws_gate.shape[1]

    def const(shape):
        return pl.BlockSpec(shape, lambda i: (0,) * len(shape), pipeline_mode=pl.Buffered(1))

    def rows(width):
        return pl.BlockSpec((tm, width), lambda i: (i, 0))

    mod_spec = pl.BlockSpec((1, 1, d), lambda i: (i // tiles_per_seq, 0, 0))
    return pl.pallas_call(
        _outproj_kernel,
        grid=(t_tokens // tm,),
        in_specs=[rows(gw), rows(gw), rows(d), const((gw, d)), const((gw, d)),
                  mod_spec, mod_spec, mod_spec, mod_spec, const((1, d)), const((1, d)),
                  const((d, ff)), const((d, ff)), const((ff, d))],
        out_specs=[rows(d), pl.BlockSpec((tm * PACK_CHUNKS, PACK_LANES), lambda i: (i, 0)), rows(d)],
        out_shape=[jax.ShapeDtypeStruct((t_tokens, d), F32),
                   jax.ShapeDtypeStruct((t_tokens * PACK_CHUNKS, PACK_LANES), U32),
                   jax.ShapeDtypeStruct((t_tokens, d), F32)],
        compiler_params=_params(("arbitrary",)),
        name="output_projection",
    )(attn_a, attn_m, x2d, w_o[:gw].astype(BF16), w_o[gw:].astype(BF16), g1, sc2, sh2, g2,
      ln_g.reshape(1, d), ln_b.reshape(1, d),
      ws_gate.astype(BF16), ws_up.astype(BF16), ws_down.astype(BF16))


ROUTER_TILE = 512


def _router_kernel(h_ref, wrt_ref, bias_ref, tri_ref, ltri_ref,
                   e_ref, pos_ref, w_ref, cnt_ref, run_ref, msk_ref):
    tr = ROUTER_TILE
    epg = N_EXPERTS // N_GROUPS

    @pl.when(pl.program_id(0) == 0)
    def _():
        run_ref[...] = jnp.zeros(run_ref.shape, F32)

    logits = lax.dot_general(wrt_ref[...], h_ref[...], NT_DIMS, precision=lax.Precision.HIGHEST,
                             preferred_element_type=F32)
    scores = jax.nn.sigmoid(logits)
    sel_scores = scores + bias_ref[...]

    sub = lax.broadcasted_iota(I32, (epg, tr), 0)
    grp_rows = []
    for g in range(N_GROUPS):
        xg = sel_scores[g * epg:(g + 1) * epg, :]
        m1 = jnp.max(xg, axis=0, keepdims=True)
        first = jnp.min(jnp.where(xg == m1, sub, epg), axis=0, keepdims=True)
        m2 = jnp.max(jnp.where(sub == first, -jnp.inf, xg), axis=0, keepdims=True)
        grp_rows.append(m1 + m2)
    grp = jnp.concatenate(grp_rows, axis=0)

    giota = lax.broadcasted_iota(I32, (N_GROUPS, tr), 0)
    gcnt = jnp.zeros((N_GROUPS, tr), F32)
    for m in range(N_GROUPS):
        rm = grp[m:m + 1, :]
        gcnt = gcnt + jnp.where((rm > grp) | ((rm == grp) & (m < giota)), 1.0, 0.0)
    gkeep = jnp.where(gcnt < float(TOPK_GROUPS), 1.0, 0.0)
    gmask = jnp.concatenate(
        [jnp.broadcast_to(gkeep[g:g + 1, :], (epg, tr)) for g in range(N_GROUPS)], axis=0)
    masked = jnp.where(gmask > 0.0, sel_scores, -jnp.inf)
    msk_ref[...] = masked

    eiota = lax.broadcasted_iota(I32, (N_EXPERTS, tr), 0)

    def rank_body(m, cnt):
        rm = msk_ref[pl.ds(m, 1), :]
        return cnt + jnp.where((rm > masked) | ((rm == masked) & (m < eiota)), 1.0, 0.0)

    ecnt = lax.fori_loop(0, N_EXPERTS, rank_body, jnp.zeros((N_EXPERTS, tr), F32))
    sel = ecnt < float(TOP_K)

    sw = jnp.where(sel, scores, 0.0)
    wn = sw / jnp.sum(sw, axis=0, keepdims=True) * ROUTED_SCALE

    selb = jnp.where(sel, 1.0, 0.0).astype(BF16)
    cum = jnp.dot(selb, tri_ref[...], preferred_element_type=F32)
    pos = run_ref[...] + cum - 1.0
    run_ref[...] = run_ref[...] + cum[:, tr - 1:tr]
    cnt_ref[...] = jnp.broadcast_to(run_ref[...], cnt_ref.shape).astype(I32)

    slot = jnp.dot(ltri_ref[...], selb, preferred_element_type=F32)
    ef = eiota.astype(F32)
    e_rows, p_rows, w_rows = [], [], []
    for j in range(TOP_K):
        mj = sel & (slot == float(j))
        e_rows.append(jnp.sum(jnp.where(mj, ef, 0.0), axis=0, keepdims=True))
        p_rows.append(jnp.sum(jnp.where(mj, pos, 0.0), axis=0, keepdims=True))
        w_rows.append(jnp.sum(jnp.where(mj, wn, 0.0), axis=0, keepdims=True))
    e_ref[...] = jnp.concatenate(e_rows, axis=0).astype(I32)
    pos_ref[...] = jnp.concatenate(p_rows, axis=0).astype(I32)
    w_ref[...] = jnp.concatenate(w_rows, axis=0)


def router(h2, w_router, router_bias):
    t_tokens, d = h2.shape
    tr = ROUTER_TILE
    tri = jnp.asarray(np.triu(np.ones((tr, tr), np.float32)), BF16)
    ltri = jnp.asarray(np.tril(np.ones((N_EXPERTS, N_EXPERTS), np.float32), -1), BF16)

    def const(shape):
        return pl.BlockSpec(shape, lambda i: (0,) * len(shape))

    tok = pl.BlockSpec((TOP_K, tr), lambda i: (0, i))
    e_t, pos_t, w_t, cnt = pl.pallas_call(
        _router_kernel,
        grid=(t_tokens // tr,),
        in_specs=[pl.BlockSpec((tr, d), lambda i: (i, 0)), const((N_EXPERTS, d)),
                  const((N_EXPERTS, 1)), const((tr, tr)), const((N_EXPERTS, N_EXPERTS))],
        out_specs=[tok, tok, tok, const((N_EXPERTS, 128))],
        out_shape=[jax.ShapeDtypeStruct((TOP_K, t_tokens), I32),
                   jax.ShapeDtypeStruct((TOP_K, t_tokens), I32),
                   jax.ShapeDtypeStruct((TOP_K, t_tokens), F32),
                   jax.ShapeDtypeStruct((N_EXPERTS, 128), I32)],
        scratch_shapes=[pltpu.VMEM((N_EXPERTS, 1), F32), pltpu.VMEM((N_EXPERTS, tr), F32)],
        compiler_params=_params(("arbitrary",)),
        name="router",
    )(h2, w_router.T, router_bias.reshape(N_EXPERTS, 1), tri, ltri)
    return e_t, pos_t, w_t, cnt[:, 0]


def _experts_kernel(be_ref, nused_ref, rt_cur_ref, rt_next_ref, h2p_ref, wg_ref, wu_ref, wd_ref,
                    y_ref, xbuf_a, xbuf_b, sem, wg_bf, wu_bf, wd_bf):
    i = pl.program_id(0)
    nused = nused_ref[0]
    rb = MOE_ROW_BLOCK
    pc = PACK_CHUNKS

    def row_copy(tok, buf, r, s):
        src = h2p_ref.at[pl.ds(pl.multiple_of(tok * pc, pc), pc), :]
        return pltpu.make_async_copy(src, buf.at[pl.ds(r * pc, pc), :], sem.at[s])

    def wait_rows(buf, s):
        pltpu.make_async_copy(h2p_ref.at[pl.ds(0, rb * pc), :], buf, sem.at[s]).wait()

    @pl.when(i == 0)
    def _():
        def body(r, carry):
            src = h2p_ref.at[pl.ds(pl.multiple_of(rt_cur_ref[r] * pc, pc), pc), :]
            dst = xbuf_a.at[pl.ds(pl.multiple_of(r * pc, pc), pc), :]
            pltpu.make_async_copy(src, dst, sem.at[0]).start()
            return carry
        lax.fori_loop(0, rb, body, 0, unroll=8)

    def step(cur, cur_s, nxt, nxt_s):
        wait_rows(cur, cur_s)
        for r in range(rb):
            row_copy(rt_next_ref[r], nxt, r, nxt_s).start()
        lo, hi = _load_packed_rows(cur, 0, rb)
        xb = jnp.concatenate([lo.astype(BF16), hi.astype(BF16)], axis=1)
        act = (_silu(jnp.dot(xb, wg_bf[...], preferred_element_type=F32))
               * jnp.dot(xb, wu_bf[...], preferred_element_type=F32)).astype(BF16)
        _store_packed_rows(y_ref, jnp.dot(act, wd_bf[...], preferred_element_type=F32))

    @pl.when(i < nused)
    def _():
        @pl.when((i == 0) | (be_ref[i] != be_ref[jnp.maximum(i - 1, 0)]))
        def _():
            wg_bf[...] = wg_ref[0].astype(BF16)
            wu_bf[...] = wu_ref[0].astype(BF16)
            wd_bf[...] = wd_ref[0].astype(BF16)

        @pl.when(i % 2 == 0)
        def _():
            step(xbuf_a, 0, xbuf_b, 1)

        @pl.when(i % 2 == 1)
        def _():
            step(xbuf_b, 1, xbuf_a, 0)

    @pl.when((i == nused - 1) & (i % 2 == 0))
    def _():
        wait_rows(xbuf_b, 1)

    @pl.when((i == nused - 1) & (i % 2 == 1))
    def _():
        wait_rows(xbuf_a, 0)

    @pl.when(i >= nused)
    def _():
        y_ref[...] = jnp.zeros(y_ref.shape, U32)


def routed_experts(h2p, row_tok, block_e, nused, w_gate, w_up, w_down):
    rb = MOE_ROW_BLOCK
    pc = PACK_CHUNKS
    n_rows = row_tok.shape[0]
    n_blocks = n_rows // rb
    d, ff = w_gate.shape[1], w_gate.shape[2]
    grid_spec = pltpu.PrefetchScalarGridSpec(
        num_scalar_prefetch=2,
        grid=(n_blocks,),
        in_specs=[pl.BlockSpec((rb,), lambda i, be, nu: (i,), memory_space=pltpu.SMEM),
                  pl.BlockSpec((rb,), lambda i, be, nu: (jnp.minimum(i + 1, n_blocks - 1),),
                               memory_space=pltpu.SMEM),
                  pl.BlockSpec(memory_space=pl.ANY),
                  pl.BlockSpec((1, d, ff), lambda i, be, nu: (be[i], 0, 0)),
                  pl.BlockSpec((1, d, ff), lambda i, be, nu: (be[i], 0, 0)),
                  pl.BlockSpec((1, ff, d), lambda i, be, nu: (be[i], 0, 0))],
        out_specs=pl.BlockSpec((rb * pc, PACK_LANES), lambda i, be, nu: (i, 0)),
        scratch_shapes=[pltpu.VMEM((rb * pc, PACK_LANES), U32), pltpu.VMEM((rb * pc, PACK_LANES), U32),
                        pltpu.SemaphoreType.DMA((2,)),
                        pltpu.VMEM((d, ff), BF16), pltpu.VMEM((d, ff), BF16),
                        pltpu.VMEM((ff, d), BF16)],
    )
    return pl.pallas_call(
        _experts_kernel,
        grid_spec=grid_spec,
        out_shape=jax.ShapeDtypeStruct((n_rows * pc, PACK_LANES), U32),
        compiler_params=_params(("arbitrary",)),
        name="routed_experts",
    )(block_e, nused, row_tok, row_tok, h2p, w_gate, w_up, w_down)


COMBINE_TILE = 128


def _combine_kernel(d_cur_ref, d_next_ref, y_ref, w_ref, base_ref, g2_ref, lng_ref, lnb_ref,
                    o_ref, ybuf_a, ybuf_b, sem):
    i = pl.program_id(0)
    n_steps = pl.num_programs(0)
    tc = COMBINE_TILE
    n_copies = TOP_K * tc
    pc = PACK_CHUNKS

    def row_copy(row, buf, r, s):
        src = y_ref.at[pl.ds(pl.multiple_of(row * pc, pc), pc), :]
        return pltpu.make_async_copy(src, buf.at[pl.ds(r * pc, pc), :], sem.at[s])

    def wait_rows(buf, s):
        pltpu.make_async_copy(y_ref.at[pl.ds(0, n_copies * pc), :], buf, sem.at[s]).wait()

    @pl.when(i == 0)
    def _():
        def body(r, carry):
            src = y_ref.at[pl.ds(pl.multiple_of(d_cur_ref[r] * pc, pc), pc), :]
            dst = ybuf_a.at[pl.ds(pl.multiple_of(r * pc, pc), pc), :]
            pltpu.make_async_copy(src, dst, sem.at[0]).start()
            return carry
        lax.fori_loop(0, n_copies, body, 0, unroll=8)

    def step(cur, cur_s, nxt, nxt_s):
        wait_rows(cur, cur_s)
        for r in range(n_copies):
            row_copy(d_next_ref[r], nxt, r, nxt_s).start(priority=r % 2)
        w = w_ref[...]
        lo, hi = _load_packed_rows(cur, 0, tc)
        routed_lo, routed_hi = w[:, 0:1] * lo, w[:, 0:1] * hi
        for k in range(1, TOP_K):
            lo, hi = _load_packed_rows(cur, k * tc, tc)
            routed_lo = routed_lo + w[:, k:k + 1] * lo
            routed_hi = routed_hi + w[:, k:k + 1] * hi
        routed = jnp.concatenate([routed_lo, routed_hi], axis=1)
        y = base_ref[...] + (1.0 + g2_ref[0]) * routed
        o_ref[...] = _layer_norm(y, lng_ref[...], lnb_ref[...])

    @pl.when(i % 2 == 0)
    def _():
        step(ybuf_a, 0, ybuf_b, 1)

    @pl.when(i % 2 == 1)
    def _():
        step(ybuf_b, 1, ybuf_a, 0)

    @pl.when((i == n_steps - 1) & (i % 2 == 0))
    def _():
        wait_rows(ybuf_b, 1)

    @pl.when((i == n_steps - 1) & (i % 2 == 1))
    def _():
        wait_rows(ybuf_a, 0)


def combine(y_sorted, dest_tiles, w_tok, base, g2, ln_g, ln_b, seq):
    t_tokens, d = base.shape
    tc = COMBINE_TILE
    n_steps = t_tokens // tc
    tiles_per_seq = seq // tc
    n_copies = TOP_K * tc

    def const(shape):
        return pl.BlockSpec(shape, lambda i: (0,) * len(shape))

    return pl.pallas_call(
        _combine_kernel,
        grid=(n_steps,),
        in_specs=[pl.BlockSpec((n_copies,), lambda i: (i,), memory_space=pltpu.SMEM),
                  pl.BlockSpec((n_copies,), lambda i: (jnp.minimum(i + 1, n_steps - 1),),
                               memory_space=pltpu.SMEM),
                  pl.BlockSpec(memory_space=pl.ANY),
                  pl.BlockSpec((tc, TOP_K), lambda i: (i, 0)),
                  pl.BlockSpec((tc, d), lambda i: (i, 0)),
                  pl.BlockSpec((1, 1, d), lambda i: (i // tiles_per_seq, 0, 0)),
                  const((1, d)), const((1, d))],
        out_specs=pl.BlockSpec((tc, d), lambda i: (i, 0)),
        out_shape=jax.ShapeDtypeStruct((t_tokens, d), F32),
        scratch_shapes=[pltpu.VMEM((n_copies * PACK_CHUNKS, PACK_LANES), U32),
                        pltpu.VMEM((n_copies * PACK_CHUNKS, PACK_LANES), U32),
                        pltpu.SemaphoreType.DMA((2,))],
        compiler_params=_params(("arbitrary",)),
        name="combine",
    )(dest_tiles, dest_tiles, y_sorted, w_tok, base, g2, ln_g.reshape(1, d), ln_b.reshape(1, d))


def _dispatch_tables(e_t, pos_t, counts, t_tokens):
    rb = MOE_ROW_BLOCK
    n_blocks = t_tokens * TOP_K // rb + N_EXPERTS
    n_rows = n_blocks * rb
    padded = (counts + rb - 1) // rb * rb
    pends = jnp.cumsum(padded)
    pstarts = pends - padded
    experts = jnp.arange(N_EXPERTS, dtype=I32)
    pstart_of = jnp.sum(jnp.where(e_t[..., None] == experts, pstarts.astype(I32), 0), axis=-1)
    dest = pstart_of + pos_t
    tok = jnp.broadcast_to(jnp.arange(t_tokens, dtype=I32)[None, :], dest.shape)
    row_tok = jnp.zeros((n_rows,), I32).at[dest.reshape(-1)].set(
        tok.reshape(-1), unique_indices=True, mode="promise_in_bounds")
    nused = (pends[-1] // rb).astype(I32)
    blk = jnp.arange(n_blocks, dtype=I32)
    last_blk = jnp.minimum(blk, jnp.maximum(nused - 1, 0))
    block_e = jnp.sum((pends[None, :] <= (last_blk * rb)[:, None]).astype(I32), axis=1)
    block_e = jnp.minimum(block_e, N_EXPERTS - 1).astype(I32)
    return dest.astype(I32), row_tok, block_e, nused.reshape(1)


def _layer(x, c, w_ada, b_ada, w_in, g_cq, w_uq, g_ckv, w_ukv, w_o, ln1_g, ln1_b,
           w_router, router_bias, w_gate, w_up, w_down, ws_gate, ws_up, ws_down, ln2_g, ln2_b):
    bsz, seq, d = x.shape
    t_tokens = bsz * seq
    x2d = x.reshape(t_tokens, d)

    mod = ada_modulation(c, w_ada, b_ada)
    sh1, sc1, g1, sh2, sc2, g2 = [m.reshape(bsz, 1, d) for m in jnp.split(mod, 6, axis=-1)]

    qa, ka, va, qm, kn, vm, kr = input_projection(x2d, sc1, sh1, w_in, g_cq, w_uq, g_ckv, w_ukv,
                                                  bsz, seq)
    attn_a = moba_attention(qa, ka, va, bsz, seq)
    attn_m = mla_attention(qm, kn, kr, vm, bsz, seq)
    h2, h2p, base = output_projection(attn_a, attn_m, x2d, w_o, g1, sc2, sh2, g2, ln1_g, ln1_b,
                                 ws_gate, ws_up, ws_down, seq)

    e_t, pos_t, w_t, counts = router(h2, w_router, router_bias)
    dest, row_tok, block_e, nused = _dispatch_tables(e_t, pos_t, counts, t_tokens)
    y_sorted = routed_experts(h2p, row_tok, block_e, nused, w_gate, w_up, w_down)

    tc = COMBINE_TILE
    dest_tiles = dest.reshape(TOP_K, t_tokens // tc, tc).transpose(1, 0, 2).reshape(-1)
    out = combine(y_sorted, dest_tiles, w_t.T, base, g2, ln2_g, ln2_b, seq)
    return out.reshape(bsz, seq, d)


def kernel(x, c, w_ada, b_ada, w_in, g_cq, w_uq, g_ckv, w_ukv, w_o, ln1_g, ln1_b, w_router,
           router_bias, w_gate, w_up, w_down, ws_gate, ws_up, ws_down, ln2_g, ln2_b):
    depth = w_ada.shape[0]
    assert depth == 1, "DeepNorm constants are baked for a single layer"
    return _layer(x, c, w_ada[0], b_ada[0], w_in[0], g_cq[0], w_uq[0], g_ckv[0], w_ukv[0], w_o[0],
                  ln1_g[0], ln1_b[0], w_router[0], router_bias[0], w_gate[0], w_up[0], w_down[0],
                  ws_gate[0], ws_up[0], ws_down[0], ln2_g[0], ln2_b[0])
```

```python
import functools

import jax
import jax.numpy as jnp
import numpy as np
from jax import lax
from jax.experimental import pallas as pl
from jax.experimental.pallas import tpu as pltpu

F32 = jnp.float32
BF16 = jnp.bfloat16
I32 = jnp.int32

D_MODEL = 2048
HEAD_DIM = 128
N_HEADS = 8
MOBA_BLOCK = 256
MOBA_TOPK = 3
MLA_ROPE_DIM = 64
MLA_RANK = 512
MLA_QK_DIM = HEAD_DIM + MLA_ROPE_DIM
GROUP_WIDTH = N_HEADS * HEAD_DIM
ROPE_THETA = 10000.0
N_EXPERTS = 64
TOP_K = 8
N_GROUPS = 8
TOPK_GROUPS = 4
EXPERT_FF = 512
ROUTED_SCALE = 2.5
MOE_ROW_BLOCK = 256
LN_EPS = 1e-5
RMS_EPS = 1e-6
DEEPNORM_ALPHA = 2.0 ** 0.25

VMEM_LIMIT_BYTES = 56 * 1024 * 1024

NT_DIMS = (((1,), (1,)), ((), ()))


def _params(semantics):
    return pltpu.CompilerParams(dimension_semantics=semantics,
                                vmem_limit_bytes=VMEM_LIMIT_BYTES)


def _silu(x):
    return x * jax.nn.sigmoid(x)


PACK_CHUNKS = 8
PACK_LANES = 128
U32 = jnp.uint32
HIGH_HALF = np.uint32(0xFFFF0000)


def _store_packed_rows(ref, x):
    n_rows, half = x.shape[0], x.shape[1] // 2
    lo = lax.bitcast_convert_type(x[:, :half].astype(BF16).astype(F32), U32) >> 16
    hi = lax.bitcast_convert_type(x[:, half:].astype(BF16).astype(F32), U32) & HIGH_HALF
    packed = lo | hi
    for j in range(PACK_CHUNKS):
        ref[pl.ds(j, n_rows, stride=PACK_CHUNKS), :] = packed[:, j * PACK_LANES:(j + 1) * PACK_LANES]


def _load_packed_rows(ref, first_row, n_rows):
    lo, hi = [], []
    for j in range(PACK_CHUNKS):
        u = ref[pl.ds(first_row * PACK_CHUNKS + j, n_rows, stride=PACK_CHUNKS), :]
        lo.append(lax.bitcast_convert_type(u << 16, F32))
        hi.append(lax.bitcast_convert_type(u & HIGH_HALF, F32))
    return jnp.concatenate(lo, axis=1), jnp.concatenate(hi, axis=1)


def _ada_kernel(c_ref, w_ref, b_ref, o_ref):
    a = _silu(c_ref[...]).astype(BF16)
    o_ref[...] = jnp.dot(a, w_ref[...].astype(BF16), preferred_element_type=F32) + b_ref[...]


def ada_modulation(c, w_ada, b_ada):
    bsz, d = c.shape
    rows = -(-bsz // 16) * 16
    c_pad = jnp.pad(c, ((0, rows - bsz), (0, 0)))
    n = w_ada.shape[1]
    tn = 1024
    out = pl.pallas_call(
        _ada_kernel,
        grid=(n // tn,),
        in_specs=[pl.BlockSpec((rows, d), lambda j: (0, 0)),
                  pl.BlockSpec((d, tn), lambda j: (0, j)),
                  pl.BlockSpec((1, tn), lambda j: (0, j))],
        out_specs=pl.BlockSpec((rows, tn), lambda j: (0, j)),
        out_shape=jax.ShapeDtypeStruct((rows, n), F32),
        compiler_params=_params(("arbitrary",)),
        name="ada_modulation",
    )(c_pad, w_ada, b_ada.reshape(1, n))
    return out[:bsz]


def _rms(x, g):
    return x * lax.rsqrt(jnp.mean(x * x, axis=-1, keepdims=True) + RMS_EPS) * g


def _rope64(x, cos, sin):
    return x * cos + pltpu.roll(x, 64, axis=1) * sin


def _rope32(x, cos, s1, s2):
    return x * cos + pltpu.roll(x, 96, axis=1) * s1 + pltpu.roll(x, 32, axis=1) * s2


def _inproj_kernel(x_ref, sc_ref, sh_ref, w_ref, wuq_ref, wuk_ref, wuv_ref, gcq_ref, gckv_ref,
                   cosa_ref, sina_ref, cosr_ref, s1_ref, s2_ref,
                   qa_ref, ka_ref, va_ref, qm_ref, kn_ref, vm_ref, kr_ref):
    h = (x_ref[...] * (1.0 + sc_ref[0]) + sh_ref[0]).astype(BF16)
    cosa, sina = cosa_ref[...], sina_ref[...]
    cosr, s1, s2 = cosr_ref[...], s1_ref[...], s2_ref[...]
    gw = GROUP_WIDTH

    def proj(lo, hi):
        return jnp.dot(h, w_ref[:, lo:hi], preferred_element_type=F32)

    for out_ref, base in ((qa_ref, 0), (ka_ref, gw)):
        t = proj(base, base + gw)
        for hd in range(N_HEADS):
            sl = slice(hd * HEAD_DIM, (hd + 1) * HEAD_DIM)
            out_ref[:, sl] = _rope64(t[:, sl], cosa, sina).astype(BF16)
    va_ref[...] = proj(2 * gw, 3 * gw).astype(BF16)

    cq = _rms(proj(3 * gw, 3 * gw + MLA_RANK), gcq_ref[...]).astype(BF16)
    qm = jnp.dot(cq, wuq_ref[...], preferred_element_type=F32)
    for hd in range(N_HEADS):
        b0 = hd * 2 * HEAD_DIM
        qm_ref[:, b0:b0 + HEAD_DIM] = qm[:, b0:b0 + HEAD_DIM].astype(BF16)
        qm_ref[:, b0 + HEAD_DIM:b0 + 2 * HEAD_DIM] = _rope32(
            qm[:, b0 + HEAD_DIM:b0 + 2 * HEAD_DIM], cosr, s1, s2).astype(BF16)

    ckv = _rms(proj(3 * gw + MLA_RANK, 3 * gw + 2 * MLA_RANK), gckv_ref[...]).astype(BF16)
    kn_ref[...] = jnp.dot(ckv, wuk_ref[...], preferred_element_type=F32).astype(BF16)
    vm_ref[...] = jnp.dot(ckv, wuv_ref[...], preferred_element_type=F32).astype(BF16)

    kr = proj(3 * gw + 2 * MLA_RANK, 3 * gw + 2 * MLA_RANK + HEAD_DIM)
    kr_ref[...] = _rope32(kr, cosr, s1, s2).astype(BF16)


def _rope_tables(seq):
    pos = jnp.arange(seq, dtype=F32)[:, None]
    inv_a = 1.0 / (ROPE_THETA ** (jnp.arange(0, HEAD_DIM, 2, dtype=F32) / HEAD_DIM))
    ang_a = pos * inv_a[None, :]
    cosa = jnp.concatenate([jnp.cos(ang_a), jnp.cos(ang_a)], axis=1)
    sina = jnp.concatenate([-jnp.sin(ang_a), jnp.sin(ang_a)], axis=1)
    inv_r = 1.0 / (ROPE_THETA ** (jnp.arange(0, MLA_ROPE_DIM, 2, dtype=F32) / MLA_ROPE_DIM))
    ang_r = pos * inv_r[None, :]
    cr, sr = jnp.cos(ang_r), jnp.sin(ang_r)
    z32 = jnp.zeros_like(cr)
    z64 = jnp.zeros((seq, 64), F32)
    cosr = jnp.concatenate([cr, cr, z64], axis=1)
    s1 = jnp.concatenate([-sr, z32, z64], axis=1)
    s2 = jnp.concatenate([z32, sr, z64], axis=1)
    return cosa, sina, cosr, s1, s2


def input_projection(x2d, sc1, sh1, w_in, g_cq, w_uq, g_ckv, w_ukv, bsz, seq):
    t_tokens, d = x2d.shape
    tm = 256
    tiles_per_seq = seq // tm
    gw = GROUP_WIDTH
    in_w = 3 * gw + 2 * MLA_RANK + HEAD_DIM
    w_cat = jnp.pad(w_in, ((0, 0), (0, in_w - w_in.shape[1]))).astype(BF16)
    wuq = jnp.pad(w_uq.reshape(MLA_RANK, N_HEADS, MLA_QK_DIM),
                  ((0, 0), (0, 0), (0, 2 * HEAD_DIM - MLA_QK_DIM)))
    wuq = wuq.reshape(MLA_RANK, N_HEADS * 2 * HEAD_DIM).astype(BF16)
    wukv = w_ukv.reshape(MLA_RANK, N_HEADS, 2 * HEAD_DIM)
    wuk = wukv[:, :, :HEAD_DIM].reshape(MLA_RANK, gw).astype(BF16)
    wuv = wukv[:, :, HEAD_DIM:].reshape(MLA_RANK, gw).astype(BF16)
    tables = _rope_tables(seq)

    def const(shape):
        return pl.BlockSpec(shape, lambda i: (0,) * len(shape), pipeline_mode=pl.Buffered(1))

    def rows(width):
        return pl.BlockSpec((tm, width), lambda i: (i, 0))

    mod_spec = pl.BlockSpec((1, 1, d), lambda i: (i // tiles_per_seq, 0, 0))
    tab_spec = pl.BlockSpec((tm, HEAD_DIM), lambda i: (i % tiles_per_seq, 0))
    widths = (gw, gw, gw, 2 * gw, gw, gw, HEAD_DIM)
    return pl.pallas_call(
        _inproj_kernel,
        grid=(t_tokens // tm,),
        in_specs=[rows(d), mod_spec, mod_spec, const((d, in_w)),
                  const((MLA_RANK, 2 * gw)), const((MLA_RANK, gw)), const((MLA_RANK, gw)),
                  const((1, MLA_RANK)), const((1, MLA_RANK))] + [tab_spec] * 5,
        out_specs=[rows(w) for w in widths],
        out_shape=[jax.ShapeDtypeStruct((t_tokens, w), BF16) for w in widths],
        compiler_params=_params(("arbitrary",)),
        name="input_projection",
    )(x2d, sc1, sh1, w_cat, wuq, wuk, wuv, g_cq.reshape(1, -1), g_ckv.reshape(1, -1), *tables)


ATTN_BLOCK = 256
LOG2_E = 1.4426950408889634


def _causal_mask_t():
    key = lax.broadcasted_iota(I32, (ATTN_BLOCK, ATTN_BLOCK), 0)
    qry = lax.broadcasted_iota(I32, (ATTN_BLOCK, ATTN_BLOCK), 1)
    return key <= qry


def _scores_t(q, k_blk):
    return lax.dot_general(k_blk, q, NT_DIMS, preferred_element_type=F32)


def _first_block(s, vt_blk, c, acc_ref):
    s = jnp.where(_causal_mask_t(), s, -jnp.inf)
    m = jnp.max(s, axis=0, keepdims=True)
    p = jnp.exp2((s - m) * c)
    acc_ref[...] = jnp.dot(vt_blk, p.astype(BF16), preferred_element_type=F32)
    return m, jnp.sum(p, axis=0, keepdims=True)


def _next_block(s, vt_blk, bias, c, m, l, acc_ref):
    m_blk = jnp.max(s, axis=0, keepdims=True)
    if bias is not None:
        m_blk = m_blk + bias
    m_new = jnp.maximum(m, m_blk)
    alpha = jnp.exp2((m - m_new) * c)
    p = jnp.exp2((s - (m_new if bias is None else m_new - bias)) * c)
    acc_ref[...] = alpha * acc_ref[...] + jnp.dot(vt_blk, p.astype(BF16), preferred_element_type=F32)
    return m_new, alpha * l + jnp.sum(p, axis=0, keepdims=True)


HEADS_PER_STEP = 8


def _head_cols(g, width=HEAD_DIM):
    return slice(g * width, (g + 1) * width)


def _moba_block_bias(kmean, q, qi, n_blocks):
    gate = lax.dot_general(kmean, q.astype(F32), NT_DIMS, precision=lax.Precision.HIGHEST,
                           preferred_element_type=F32)
    row = lax.broadcasted_iota(I32, gate.shape, 0)
    gt = jnp.where(row < qi, gate, -jnp.inf)
    cnt = jnp.zeros(gate.shape, F32)
    for m in range(n_blocks - 1):
        gm = gt[m:m + 1, :]
        cnt = cnt + jnp.where((gm > gt) | ((gm == gt) & (m < row)), 1.0, 0.0)
    return jnp.where(cnt < float(MOBA_TOPK), 0.0, -jnp.inf).astype(F32)


def _moba_kernel(q_ref, k_ref, v_ref, o_ref, kmean_ref, vt_ref, acc_ref, *, n_blocks):
    qi = pl.program_id(2)
    blk = MOBA_BLOCK
    heads = HEADS_PER_STEP

    @pl.when(qi == 0)
    def _():
        kmean_ref[...] = jnp.zeros(kmean_ref.shape, F32)
        for g in range(heads):
            vt_ref[g] = v_ref[0, :, _head_cols(g)].astype(F32).T.astype(BF16)
            for n in range(n_blocks):
                kb = k_ref[0, n * blk:(n + 1) * blk, _head_cols(g)].astype(F32)
                kmean_ref[g, n:n + 1, :] = jnp.sum(kb, axis=0, keepdims=True) * (1.0 / blk)

    c = (HEAD_DIM ** -0.5) * LOG2_E
    own = pl.multiple_of(qi * blk, blk)
    qs = [q_ref[0, :, _head_cols(g)] for g in range(heads)]
    scores = [_scores_t(qs[g], k_ref[0, pl.ds(own, blk), _head_cols(g)]) for g in range(heads)]
    biases = [_moba_block_bias(kmean_ref[g], qs[g], qi, n_blocks) for g in range(heads)]
    carry0 = []
    for g in range(heads):
        carry0 += _first_block(scores[g], vt_ref[g, :, pl.ds(own, blk)], c, acc_ref.at[g])

    def body(n, carry):
        off = pl.multiple_of(n * blk, blk)
        scores = [_scores_t(qs[g], k_ref[0, pl.ds(off, blk), _head_cols(g)]) for g in range(heads)]
        row = lax.broadcasted_iota(I32, biases[0].shape, 0)
        out = []
        for g in range(heads):
            bias = jnp.sum(jnp.where(row == n, biases[g], 0.0), axis=0, keepdims=True)
            out += _next_block(scores[g], vt_ref[g, :, pl.ds(off, blk)], bias,
                               c, carry[2 * g], carry[2 * g + 1], acc_ref.at[g])
        return tuple(out)

    final = lax.fori_loop(0, qi, body, tuple(carry0))
    for g in range(heads):
        o_ref[0, :, _head_cols(g)] = (acc_ref[g] / final[2 * g + 1]).T.astype(o_ref.dtype)


def moba_attention(qa, ka, va, bsz, seq):
    blk = MOBA_BLOCK
    n_blocks = seq // blk
    heads = HEADS_PER_STEP
    gate_rows = -(-n_blocks // 8) * 8
    q3, k3, v3 = (t.reshape(bsz, seq, GROUP_WIDTH) for t in (qa, ka, va))
    q_spec = pl.BlockSpec((1, blk, heads * HEAD_DIM), lambda b, h, i: (b, i, h))
    kv_spec = pl.BlockSpec((1, seq, heads * HEAD_DIM), lambda b, h, i: (b, 0, h))
    out = pl.pallas_call(
        functools.partial(_moba_kernel, n_blocks=n_blocks),
        grid=(bsz, N_HEADS // heads, n_blocks),
        in_specs=[q_spec, kv_spec, kv_spec],
        out_specs=q_spec,
        out_shape=jax.ShapeDtypeStruct((bsz, seq, GROUP_WIDTH), BF16),
        scratch_shapes=[pltpu.VMEM((heads, gate_rows, HEAD_DIM), F32),
                        pltpu.VMEM((heads, HEAD_DIM, seq), BF16),
                        pltpu.VMEM((heads, HEAD_DIM, blk), F32)],
        compiler_params=_params(("arbitrary", "arbitrary", "arbitrary")),
        name="moba_attention",
    )(q3, k3, v3)
    return out.reshape(bsz * seq, GROUP_WIDTH)


def _mla_kernel(q_ref, kn_ref, kr_ref, v_ref, o_ref, kcat_ref, vt_ref, acc_ref):
    qi = pl.program_id(2)
    blk = ATTN_BLOCK

    heads = HEADS_PER_STEP

    @pl.when(qi == 0)
    def _():
        for g in range(heads):
            vt_ref[g] = v_ref[0, :, _head_cols(g)].astype(F32).T.astype(BF16)
            kcat_ref[g, :, :HEAD_DIM] = kn_ref[0, :, _head_cols(g)]
            kcat_ref[g, :, HEAD_DIM:] = kr_ref[0]

    c = (MLA_QK_DIM ** -0.5) * LOG2_E
    own = pl.multiple_of(qi * blk, blk)
    qs = [q_ref[0, :, _head_cols(g, 2 * HEAD_DIM)] for g in range(heads)]
    scores = [_scores_t(qs[g], kcat_ref[g, pl.ds(own, blk), :]) for g in range(heads)]
    carry0 = []
    for g in range(heads):
        carry0 += _first_block(scores[g], vt_ref[g, :, pl.ds(own, blk)], c, acc_ref.at[g])

    def body(n, carry):
        off = pl.multiple_of(n * blk, blk)
        scores = [_scores_t(qs[g], kcat_ref[g, pl.ds(off, blk), :]) for g in range(heads)]
        out = []
        for g in range(heads):
            out += _next_block(scores[g], vt_ref[g, :, pl.ds(off, blk)], None, c,
                               carry[2 * g], carry[2 * g + 1], acc_ref.at[g])
        return tuple(out)

    final = lax.fori_loop(0, qi, body, tuple(carry0))
    for g in range(heads):
        o_ref[0, :, _head_cols(g)] = (acc_ref[g] / final[2 * g + 1]).T.astype(o_ref.dtype)


def mla_attention(qm, kn, kr, vm, bsz, seq):
    blk = ATTN_BLOCK
    heads = HEADS_PER_STEP
    q3 = qm.reshape(bsz, seq, 2 * GROUP_WIDTH)
    kn3, vm3 = kn.reshape(bsz, seq, GROUP_WIDTH), vm.reshape(bsz, seq, GROUP_WIDTH)
    kr3 = kr.reshape(bsz, seq, HEAD_DIM)
    out = pl.pallas_call(
        _mla_kernel,
        grid=(bsz, N_HEADS // heads, seq // blk),
        in_specs=[pl.BlockSpec((1, blk, heads * 2 * HEAD_DIM), lambda b, h, i: (b, i, h)),
                  pl.BlockSpec((1, seq, heads * HEAD_DIM), lambda b, h, i: (b, 0, h)),
                  pl.BlockSpec((1, seq, HEAD_DIM), lambda b, h, i: (b, 0, 0)),
                  pl.BlockSpec((1, seq, heads * HEAD_DIM), lambda b, h, i: (b, 0, h))],
        out_specs=pl.BlockSpec((1, blk, heads * HEAD_DIM), lambda b, h, i: (b, i, h)),
        out_shape=jax.ShapeDtypeStruct((bsz, seq, GROUP_WIDTH), BF16),
        scratch_shapes=[pltpu.VMEM((heads, seq, 2 * HEAD_DIM), BF16),
                        pltpu.VMEM((heads, HEAD_DIM, seq), BF16),
                        pltpu.VMEM((heads, HEAD_DIM, blk), F32)],
        compiler_params=_params(("arbitrary", "arbitrary", "arbitrary")),
        name="mla_attention",
    )(q3, kn3, kr3, vm3)
    return out.reshape(bsz * seq, GROUP_WIDTH)


def _layer_norm(y, g, b):
    mu = jnp.mean(y, axis=-1, keepdims=True)
    yc = y - mu
    var = jnp.mean(yc * yc, axis=-1, keepdims=True)
    return yc * lax.rsqrt(var + LN_EPS) * g + b


def _outproj_kernel(a_ref, m_ref, x_ref, woa_ref, wom_ref, g1_ref, sc2_ref, sh2_ref, g2_ref,
                    lng_ref, lnb_ref, wsg_ref, wsu_ref, wsd_ref, h2_ref, h2p_ref, base_ref):
    f = (jnp.dot(a_ref[...], woa_ref[...], preferred_element_type=F32)
         + jnp.dot(m_ref[...], wom_ref[...], preferred_element_type=F32))
    x1 = _layer_norm(DEEPNORM_ALPHA * x_ref[...] + (1.0 + g1_ref[0]) * f, lng_ref[...], lnb_ref[...])
    h2 = x1 * (1.0 + sc2_ref[0]) + sh2_ref[0]
    h2_ref[...] = h2
    _store_packed_rows(h2p_ref, h2)
    hb = h2.astype(BF16)
    act = (_silu(jnp.dot(hb, wsg_ref[...], preferred_element_type=F32))
           * jnp.dot(hb, wsu_ref[...], preferred_element_type=F32)).astype(BF16)
    shared = jnp.dot(act, wsd_ref[...], preferred_element_type=F32)
    base_ref[...] = DEEPNORM_ALPHA * x1 + (1.0 + g2_ref[0]) * shared


def output_projection(attn_a, attn_m, x2d, w_o, g1, sc2, sh2, g2, ln_g, ln_b,
                      ws_gate, ws_up, ws_down, seq):
    t_tokens, d = x2d.shape
    tm = 256
    tiles_per_seq = seq // tm
    gw = GROUP_WIDTH
    ff = ws_gate.shape[1]

    def const(shape):
        return pl.BlockSpec(shape, lambda i: (0,) * len(shape), pipeline_mode=pl.Buffered(1))

    def rows(width):
        return pl.BlockSpec((tm, width), lambda i: (i, 0))

    mod_spec = pl.BlockSpec((1, 1, d), lambda i: (i // tiles_per_seq, 0, 0))
    return pl.pallas_call(
        _outproj_kernel,
        grid=(t_tokens // tm,),
        in_specs=[rows(gw), rows(gw), rows(d), const((gw, d)), const((gw, d)),
                  mod_spec, mod_spec, mod_spec, mod_spec, const((1, d)), const((1, d)),
                  const((d, ff)), const((d, ff)), const((ff, d))],
        out_specs=[rows(d), pl.BlockSpec((tm * PACK_CHUNKS, PACK_LANES), lambda i: (i, 0)), rows(d)],
        out_shape=[jax.ShapeDtypeStruct((t_tokens, d), F32),
                   jax.ShapeDtypeStruct((t_tokens * PACK_CHUNKS, PACK_LANES), U32),
                   jax.ShapeDtypeStruct((t_tokens, d), F32)],
        compiler_params=_params(("arbitrary",)),
        name="output_projection",
    )(attn_a, attn_m, x2d, w_o[:gw].astype(BF16), w_o[gw:].astype(BF16), g1, sc2, sh2, g2,
      ln_g.reshape(1, d), ln_b.reshape(1, d),
      ws_gate.astype(BF16), ws_up.astype(BF16), ws_down.astype(BF16))


ROUTER_TILE = 512


def _router_kernel(h_ref, wrt_ref, bias_ref, tri_ref, ltri_ref,
                   e_ref, pos_ref, w_ref, cnt_ref, run_ref, msk_ref):
    tr = ROUTER_TILE
    epg = N_EXPERTS // N_GROUPS

    @pl.when(pl.program_id(0) == 0)
    def _():
        run_ref[...] = jnp.zeros(run_ref.shape, F32)

    logits = lax.dot_general(wrt_ref[...], h_ref[...], NT_DIMS, precision=lax.Precision.HIGHEST,
                             preferred_element_type=F32)
    scores = jax.nn.sigmoid(logits)
    sel_scores = scores + bias_ref[...]

    sub = lax.broadcasted_iota(I32, (epg, tr), 0)
    grp_rows = []
    for g in range(N_GROUPS):
        xg = sel_scores[g * epg:(g + 1) * epg, :]
        m1 = jnp.max(xg, axis=0, keepdims=True)
        first = jnp.min(jnp.where(xg == m1, sub, epg), axis=0, keepdims=True)
        m2 = jnp.max(jnp.where(sub == first, -jnp.inf, xg), axis=0, keepdims=True)
        grp_rows.append(m1 + m2)
    grp = jnp.concatenate(grp_rows, axis=0)

    giota = lax.broadcasted_iota(I32, (N_GROUPS, tr), 0)
    gcnt = jnp.zeros((N_GROUPS, tr), F32)
    for m in range(N_GROUPS):
        rm = grp[m:m + 1, :]
        gcnt = gcnt + jnp.where((rm > grp) | ((rm == grp) & (m < giota)), 1.0, 0.0)
    gkeep = jnp.where(gcnt < float(TOPK_GROUPS), 1.0, 0.0)
    gmask = jnp.concatenate(
        [jnp.broadcast_to(gkeep[g:g + 1, :], (epg, tr)) for g in range(N_GROUPS)], axis=0)
    masked = jnp.where(gmask > 0.0, sel_scores, -jnp.inf)
    msk_ref[...] = masked

    eiota = lax.broadcasted_iota(I32, (N_EXPERTS, tr), 0)

    def rank_body(m, cnt):
        rm = msk_ref[pl.ds(m, 1), :]
        return cnt + jnp.where((rm > masked) | ((rm == masked) & (m < eiota)), 1.0, 0.0)

    ecnt = lax.fori_loop(0, N_EXPERTS, rank_body, jnp.zeros((N_EXPERTS, tr), F32))
    sel = ecnt < float(TOP_K)

    sw = jnp.where(sel, scores, 0.0)
    wn = sw / jnp.sum(sw, axis=0, keepdims=True) * ROUTED_SCALE

    selb = jnp.where(sel, 1.0, 0.0).astype(BF16)
    cum = jnp.dot(selb, tri_ref[...], preferred_element_type=F32)
    pos = run_ref[...] + cum - 1.0
    run_ref[...] = run_ref[...] + cum[:, tr - 1:tr]
    cnt_ref[...] = jnp.broadcast_to(run_ref[...], cnt_ref.shape).astype(I32)

    slot = jnp.dot(ltri_ref[...], selb, preferred_element_type=F32)
    ef = eiota.astype(F32)
    e_rows, p_rows, w_rows = [], [], []
    for j in range(TOP_K):
        mj = sel & (slot == float(j))
        e_rows.append(jnp.sum(jnp.where(mj, ef, 0.0), axis=0, keepdims=True))
        p_rows.append(jnp.sum(jnp.where(mj, pos, 0.0), axis=0, keepdims=True))
        w_rows.append(jnp.sum(jnp.where(mj, wn, 0.0), axis=0, keepdims=True))
    e_ref[...] = jnp.concatenate(e_rows, axis=0).astype(I32)
    pos_ref[...] = jnp.concatenate(p_rows, axis=0).astype(I32)
    w_ref[...] = jnp.concatenate(w_rows, axis=0)


def router(h2, w_router, router_bias):
    t_tokens, d = h2.shape
    tr = ROUTER_TILE
    tri = jnp.asarray(np.triu(np.ones((tr, tr), np.float32)), BF16)
    ltri = jnp.asarray(np.tril(np.ones((N_EXPERTS, N_EXPERTS), np.float32), -1), BF16)

    def const(shape):
        return pl.BlockSpec(shape, lambda i: (0,) * len(shape))

    tok = pl.BlockSpec((TOP_K, tr), lambda i: (0, i))
    e_t, pos_t, w_t, cnt = pl.pallas_call(
        _router_kernel,
        grid=(t_tokens // tr,),
        in_specs=[pl.BlockSpec((tr, d), lambda i: (i, 0)), const((N_EXPERTS, d)),
                  const((N_EXPERTS, 1)), const((tr, tr)), const((N_EXPERTS, N_EXPERTS))],
        out_specs=[tok, tok, tok, const((N_EXPERTS, 128))],
        out_shape=[jax.ShapeDtypeStruct((TOP_K, t_tokens), I32),
                   jax.ShapeDtypeStruct((TOP_K, t_tokens), I32),
                   jax.ShapeDtypeStruct((TOP_K, t_tokens), F32),
                   jax.ShapeDtypeStruct((N_EXPERTS, 128), I32)],
        scratch_shapes=[pltpu.VMEM((N_EXPERTS, 1), F32), pltpu.VMEM((N_EXPERTS, tr), F32)],
        compiler_params=_params(("arbitrary",)),
        name="router",
    )(h2, w_router.T, router_bias.reshape(N_EXPERTS, 1), tri, ltri)
    return e_t, pos_t, w_t, cnt[:, 0]


def _experts_kernel(be_ref, nused_ref, rt_cur_ref, rt_next_ref, h2p_ref, wg_ref, wu_ref, wd_ref,
                    y_ref, xbuf_a, xbuf_b, sem, wg_bf, wu_bf, wd_bf):
    i = pl.program_id(0)
    nused = nused_ref[0]
    rb = MOE_ROW_BLOCK
    pc = PACK_CHUNKS

    def row_copy(tok, buf, r, s):
        src = h2p_ref.at[pl.ds(pl.multiple_of(tok * pc, pc), pc), :]
        return pltpu.make_async_copy(src, buf.at[pl.ds(r * pc, pc), :], sem.at[s])

    def wait_rows(buf, s):
        pltpu.make_async_copy(h2p_ref.at[pl.ds(0, rb * pc), :], buf, sem.at[s]).wait()

    @pl.when(i == 0)
    def _():
        def body(r, carry):
            src = h2p_ref.at[pl.ds(pl.multiple_of(rt_cur_ref[r] * pc, pc), pc), :]
            dst = xbuf_a.at[pl.ds(pl.multiple_of(r * pc, pc), pc), :]
            pltpu.make_async_copy(src, dst, sem.at[0]).start()
            return carry
        lax.fori_loop(0, rb, body, 0, unroll=8)

    def step(cur, cur_s, nxt, nxt_s):
        wait_rows(cur, cur_s)
        for r in range(rb):
            row_copy(rt_next_ref[r], nxt, r, nxt_s).start(priority=r % 2)
        lo, hi = _load_packed_rows(cur, 0, rb)
        xb = jnp.concatenate([lo.astype(BF16), hi.astype(BF16)], axis=1)
        act = (_silu(jnp.dot(xb, wg_bf[...], preferred_element_type=F32))
               * jnp.dot(xb, wu_bf[...], preferred_element_type=F32)).astype(BF16)
        _store_packed_rows(y_ref, jnp.dot(act, wd_bf[...], preferred_element_type=F32))

    @pl.when(i < nused)
    def _():
        @pl.when((i == 0) | (be_ref[i] != be_ref[jnp.maximum(i - 1, 0)]))
        def _():
            wg_bf[...] = wg_ref[0].astype(BF16)
            wu_bf[...] = wu_ref[0].astype(BF16)
            wd_bf[...] = wd_ref[0].astype(BF16)

        @pl.when(i % 2 == 0)
        def _():
            step(xbuf_a, 0, xbuf_b, 1)

        @pl.when(i % 2 == 1)
        def _():
            step(xbuf_b, 1, xbuf_a, 0)

    @pl.when((i == nused - 1) & (i % 2 == 0))
    def _():
        wait_rows(xbuf_b, 1)

    @pl.when((i == nused - 1) & (i % 2 == 1))
    def _():
        wait_rows(xbuf_a, 0)

    @pl.when(i >= nused)
    def _():
        y_ref[...] = jnp.zeros(y_ref.shape, U32)


def routed_experts(h2p, row_tok, block_e, nused, w_gate, w_up, w_down):
    rb = MOE_ROW_BLOCK
    pc = PACK_CHUNKS
    n_rows = row_tok.shape[0]
    n_blocks = n_rows // rb
    d, ff = w_gate.shape[1], w_gate.shape[2]
    grid_spec = pltpu.PrefetchScalarGridSpec(
        num_scalar_prefetch=2,
        grid=(n_blocks,),
        in_specs=[pl.BlockSpec((rb,), lambda i, be, nu: (i,), memory_space=pltpu.SMEM),
                  pl.BlockSpec((rb,), lambda i, be, nu: (jnp.minimum(i + 1, n_blocks - 1),),
                               memory_space=pltpu.SMEM),
                  pl.BlockSpec(memory_space=pl.ANY),
                  pl.BlockSpec((1, d, ff), lambda i, be, nu: (be[i], 0, 0)),
                  pl.BlockSpec((1, d, ff), lambda i, be, nu: (be[i], 0, 0)),
                  pl.BlockSpec((1, ff, d), lambda i, be, nu: (be[i], 0, 0))],
        out_specs=pl.BlockSpec((rb * pc, PACK_LANES), lambda i, be, nu: (i, 0)),
        scratch_shapes=[pltpu.VMEM((rb * pc, PACK_LANES), U32), pltpu.VMEM((rb * pc, PACK_LANES), U32),
                        pltpu.SemaphoreType.DMA((2,)),
                        pltpu.VMEM((d, ff), BF16), pltpu.VMEM((d, ff), BF16),
                        pltpu.VMEM((ff, d), BF16)],
    )
    return pl.pallas_call(
        _experts_kernel,
        grid_spec=grid_spec,
        out_shape=jax.ShapeDtypeStruct((n_rows * pc, PACK_LANES), U32),
        compiler_params=_params(("arbitrary",)),
        name="routed_experts",
    )(block_e, nused, row_tok, row_tok, h2p, w_gate, w_up, w_down)


COMBINE_TILE = 128


def _combine_kernel(d_cur_ref, d_next_ref, y_ref, w_ref, base_ref, g2_ref, lng_ref, lnb_ref,
                    o_ref, ybuf_a, ybuf_b, sem):
    i = pl.program_id(0)
    n_steps = pl.num_programs(0)
    tc = COMBINE_TILE
    n_copies = TOP_K * tc
    pc = PACK_CHUNKS

    def row_copy(row, buf, r, s):
        src = y_ref.at[pl.ds(pl.multiple_of(row * pc, pc), pc), :]
        return pltpu.make_async_copy(src, buf.at[pl.ds(r * pc, pc), :], sem.at[s])

    def wait_rows(buf, s):
        pltpu.make_async_copy(y_ref.at[pl.ds(0, n_copies * pc), :], buf, sem.at[s]).wait()

    @pl.when(i == 0)
    def _():
        def body(r, carry):
            src = y_ref.at[pl.ds(pl.multiple_of(d_cur_ref[r] * pc, pc), pc), :]
            dst = ybuf_a.at[pl.ds(pl.multiple_of(r * pc, pc), pc), :]
            pltpu.make_async_copy(src, dst, sem.at[0]).start()
            return carry
        lax.fori_loop(0, n_copies, body, 0, unroll=8)

    def step(cur, cur_s, nxt, nxt_s):
        wait_rows(cur, cur_s)
        for r in range(n_copies):
            row_copy(d_next_ref[r], nxt, r, nxt_s).start(priority=r % 2)
        w = w_ref[...]
        lo, hi = _load_packed_rows(cur, 0, tc)
        routed_lo, routed_hi = w[:, 0:1] * lo, w[:, 0:1] * hi
        for k in range(1, TOP_K):
            lo, hi = _load_packed_rows(cur, k * tc, tc)
            routed_lo = routed_lo + w[:, k:k + 1] * lo
            routed_hi = routed_hi + w[:, k:k + 1] * hi
        routed = jnp.concatenate([routed_lo, routed_hi], axis=1)
        y = base_ref[...] + (1.0 + g2_ref[0]) * routed
        o_ref[...] = _layer_norm(y, lng_ref[...], lnb_ref[...])

    @pl.when(i % 2 == 0)
    def _():
        step(ybuf_a, 0, ybuf_b, 1)

    @pl.when(i % 2 == 1)
    def _():
        step(ybuf_b, 1, ybuf_a, 0)

    @pl.when((i == n_steps - 1) & (i % 2 == 0))
    def _():
        wait_rows(ybuf_b, 1)

    @pl.when((i == n_steps - 1) & (i % 2 == 1))
    def _():
        wait_rows(ybuf_a, 0)


def combine(y_sorted, dest_tiles, w_tok, base, g2, ln_g, ln_b, seq):
    t_tokens, d = base.shape
    tc = COMBINE_TILE
    n_steps = t_tokens // tc
    tiles_per_seq = seq // tc
    n_copies = TOP_K * tc

    def const(shape):
        return pl.BlockSpec(shape, lambda i: (0,) * len(shape))

    return pl.pallas_call(
        _combine_kernel,
        grid=(n_steps,),
        in_specs=[pl.BlockSpec((n_copies,), lambda i: (i,), memory_space=pltpu.SMEM),
                  pl.BlockSpec((n_copies,), lambda i: (jnp.minimum(i + 1, n_steps - 1),),
                               memory_space=pltpu.SMEM),
                  pl.BlockSpec(memory_space=pl.ANY),
                  pl.BlockSpec((tc, TOP_K), lambda i: (i, 0)),
                  pl.BlockSpec((tc, d), lambda i: (i, 0)),
                  pl.BlockSpec((1, 1, d), lambda i: (i // tiles_per_seq, 0, 0)),
                  const((1, d)), const((1, d))],
        out_specs=pl.BlockSpec((tc, d), lambda i: (i, 0)),
        out_shape=jax.ShapeDtypeStruct((t_tokens, d), F32),
        scratch_shapes=[pltpu.VMEM((n_copies * PACK_CHUNKS, PACK_LANES), U32),
                        pltpu.VMEM((n_copies * PACK_CHUNKS, PACK_LANES), U32),
                        pltpu.SemaphoreType.DMA((2,))],
        compiler_params=_params(("arbitrary",)),
        name="combine",
    )(dest_tiles, dest_tiles, y_sorted, w_tok, base, g2, ln_g.reshape(1, d), ln_b.reshape(1, d))


def _dispatch_tables(e_t, pos_t, counts, t_tokens):
    rb = MOE_ROW_BLOCK
    n_blocks = t_tokens * TOP_K // rb + N_EXPERTS
    n_rows = n_blocks * rb
    padded = (counts + rb - 1) // rb * rb
    pends = jnp.cumsum(padded)
    pstarts = pends - padded
    experts = jnp.arange(N_EXPERTS, dtype=I32)
    pstart_of = jnp.sum(jnp.where(e_t[..., None] == experts, pstarts.astype(I32), 0), axis=-1)
    dest = pstart_of + pos_t
    tok = jnp.broadcast_to(jnp.arange(t_tokens, dtype=I32)[None, :], dest.shape)
    row_tok = jnp.zeros((n_rows,), I32).at[dest.reshape(-1)].set(
        tok.reshape(-1), unique_indices=True, mode="promise_in_bounds")
    nused = (pends[-1] // rb).astype(I32)
    blk = jnp.arange(n_blocks, dtype=I32)
    last_blk = jnp.minimum(blk, jnp.maximum(nused - 1, 0))
    block_e = jnp.sum((pends[None, :] <= (last_blk * rb)[:, None]).astype(I32), axis=1)
    block_e = jnp.minimum(block_e, N_EXPERTS - 1).astype(I32)
    return dest.astype(I32), row_tok, block_e, nused.reshape(1)


def _layer(x, c, w_ada, b_ada, w_in, g_cq, w_uq, g_ckv, w_ukv, w_o, ln1_g, ln1_b,
           w_router, router_bias, w_gate, w_up, w_down, ws_gate, ws_up, ws_down, ln2_g, ln2_b):
    bsz, seq, d = x.shape
    t_tokens = bsz * seq
    x2d = x.reshape(t_tokens, d)

    mod = ada_modulation(c, w_ada, b_ada)
    sh1, sc1, g1, sh2, sc2, g2 = [m.reshape(bsz, 1, d) for m in jnp.split(mod, 6, axis=-1)]

    qa, ka, va, qm, kn, vm, kr = input_projection(x2d, sc1, sh1, w_in, g_cq, w_uq, g_ckv, w_ukv,
                                                  bsz, seq)
    attn_a = moba_attention(qa, ka, va, bsz, seq)
    attn_m = mla_attention(qm, kn, kr, vm, bsz, seq)
    h2, h2p, base = output_projection(attn_a, attn_m, x2d, w_o, g1, sc2, sh2, g2, ln1_g, ln1_b,
                                 ws_gate, ws_up, ws_down, seq)

    e_t, pos_t, w_t, counts = router(h2, w_router, router_bias)
    dest, row_tok, block_e, nused = _dispatch_tables(e_t, pos_t, counts, t_tokens)
    y_sorted = routed_experts(h2p, row_tok, block_e, nused, w_gate, w_up, w_down)

    tc = COMBINE_TILE
    dest_tiles = dest.reshape(TOP_K, t_tokens // tc, tc).transpose(1, 0, 2).reshape(-1)
    out = combine(y_sorted, dest_tiles, w_t.T, base, g2, ln2_g, ln2_b, seq)
    return out.reshape(bsz, seq, d)


def kernel(x, c, w_ada, b_ada, w_in, g_cq, w_uq, g_ckv, w_ukv, w_o, ln1_g, ln1_b, w_router,
           router_bias, w_gate, w_up, w_down, ws_gate, ws_up, ws_down, ln2_g, ln2_b):
    depth = w_ada.shape[0]
    assert depth == 1, "DeepNorm constants are baked for a single layer"
    return _layer(x, c, w_ada[0], b_ada[0], w_in[0], g_cq[0], w_uq[0], g_ckv[0], w_ukv[0], w_o[0],
                  ln1_g[0], ln1_b[0], w_router[0], router_bias[0], w_gate[0], w_up[0], w_down[0],
                  ws_gate[0], ws_up[0], ws_down[0], ln2_g[0], ln2_b[0])
```

```python
import functools

import jax
import jax.numpy as jnp
import numpy as np
from jax import lax
from jax.experimental import pallas as pl
from jax.experimental.pallas import tpu as pltpu

F32 = jnp.float32
BF16 = jnp.bfloat16
I32 = jnp.int32

D_MODEL = 2048
HEAD_DIM = 128
N_HEADS = 8
MOBA_BLOCK = 256
MOBA_TOPK = 3
MLA_ROPE_DIM = 64
MLA_RANK = 512
MLA_QK_DIM = HEAD_DIM + MLA_ROPE_DIM
GROUP_WIDTH = N_HEADS * HEAD_DIM
ROPE_THETA = 10000.0
N_EXPERTS = 64
TOP_K = 8
N_GROUPS = 8
TOPK_GROUPS = 4
EXPERT_FF = 512
ROUTED_SCALE = 2.5
MOE_ROW_BLOCK = 256
LN_EPS = 1e-5
RMS_EPS = 1e-6
DEEPNORM_ALPHA = 2.0 ** 0.25

VMEM_LIMIT_BYTES = 56 * 1024 * 1024

NT_DIMS = (((1,), (1,)), ((), ()))


def _params(semantics):
    return pltpu.CompilerParams(dimension_semantics=semantics,
                                vmem_limit_bytes=VMEM_LIMIT_BYTES)


def _silu(x):
    return x * jax.nn.sigmoid(x)


PACK_CHUNKS = 8
PACK_LANES = 128
U32 = jnp.uint32
HIGH_HALF = np.uint32(0xFFFF0000)


def _store_packed_rows(ref, x):
    n_rows, half = x.shape[0], x.shape[1] // 2
    lo = lax.bitcast_convert_type(x[:, :half].astype(BF16).astype(F32), U32) >> 16
    hi = lax.bitcast_convert_type(x[:, half:].astype(BF16).astype(F32), U32) & HIGH_HALF
    packed = lo | hi
    for j in range(PACK_CHUNKS):
        ref[pl.ds(j, n_rows, stride=PACK_CHUNKS), :] = packed[:, j * PACK_LANES:(j + 1) * PACK_LANES]


def _load_packed_rows(ref, first_row, n_rows):
    lo, hi = [], []
    for j in range(PACK_CHUNKS):
        u = ref[pl.ds(first_row * PACK_CHUNKS + j, n_rows, stride=PACK_CHUNKS), :]
        lo.append(lax.bitcast_convert_type(u << 16, F32))
        hi.append(lax.bitcast_convert_type(u & HIGH_HALF, F32))
    return jnp.concatenate(lo, axis=1), jnp.concatenate(hi, axis=1)


def _ada_kernel(c_ref, w_ref, b_ref, o_ref):
    a = _silu(c_ref[...]).astype(BF16)
    o_ref[...] = jnp.dot(a, w_ref[...].astype(BF16), preferred_element_type=F32) + b_ref[...]


def ada_modulation(c, w_ada, b_ada):
    bsz, d = c.shape
    rows = -(-bsz // 16) * 16
    c_pad = jnp.pad(c, ((0, rows - bsz), (0, 0)))
    n = w_ada.shape[1]
    tn = 1024
    out = pl.pallas_call(
        _ada_kernel,
        grid=(n // tn,),
        in_specs=[pl.BlockSpec((rows, d), lambda j: (0, 0)),
                  pl.BlockSpec((d, tn), lambda j: (0, j)),
                  pl.BlockSpec((1, tn), lambda j: (0, j))],
        out_specs=pl.BlockSpec((rows, tn), lambda j: (0, j)),
        out_shape=jax.ShapeDtypeStruct((rows, n), F32),
        compiler_params=_params(("arbitrary",)),
        name="ada_modulation",
    )(c_pad, w_ada, b_ada.reshape(1, n))
    return out[:bsz]


def _rms(x, g):
    return x * lax.rsqrt(jnp.mean(x * x, axis=-1, keepdims=True) + RMS_EPS) * g


def _rope64(x, cos, sin):
    return x * cos + pltpu.roll(x, 64, axis=1) * sin


def _rope32(x, cos, s1, s2):
    return x * cos + pltpu.roll(x, 96, axis=1) * s1 + pltpu.roll(x, 32, axis=1) * s2


def _inproj_kernel(x_ref, sc_ref, sh_ref, w_ref, wuq_ref, wuk_ref, wuv_ref, gcq_ref, gckv_ref,
                   cosa_ref, sina_ref, cosr_ref, s1_ref, s2_ref,
                   qa_ref, ka_ref, va_ref, qm_ref, kn_ref, vm_ref, kr_ref):
    h = (x_ref[...] * (1.0 + sc_ref[0]) + sh_ref[0]).astype(BF16)
    cosa, sina = cosa_ref[...], sina_ref[...]
    cosr, s1, s2 = cosr_ref[...], s1_ref[...], s2_ref[...]
    gw = GROUP_WIDTH

    def proj(lo, hi):
        return jnp.dot(h, w_ref[:, lo:hi], preferred_element_type=F32)

    for out_ref, base in ((qa_ref, 0), (ka_ref, gw)):
        t = proj(base, base + gw)
        for hd in range(N_HEADS):
            sl = slice(hd * HEAD_DIM, (hd + 1) * HEAD_DIM)
            out_ref[:, sl] = _rope64(t[:, sl], cosa, sina).astype(BF16)
    va_ref[...] = proj(2 * gw, 3 * gw).astype(BF16)

    cq = _rms(proj(3 * gw, 3 * gw + MLA_RANK), gcq_ref[...]).astype(BF16)
    qm = jnp.dot(cq, wuq_ref[...], preferred_element_type=F32)
    for hd in range(N_HEADS):
        b0 = hd * 2 * HEAD_DIM
        qm_ref[:, b0:b0 + HEAD_DIM] = qm[:, b0:b0 + HEAD_DIM].astype(BF16)
        qm_ref[:, b0 + HEAD_DIM:b0 + 2 * HEAD_DIM] = _rope32(
            qm[:, b0 + HEAD_DIM:b0 + 2 * HEAD_DIM], cosr, s1, s2).astype(BF16)

    ckv = _rms(proj(3 * gw + MLA_RANK, 3 * gw + 2 * MLA_RANK), gckv_ref[...]).astype(BF16)
    kn_ref[...] = jnp.dot(ckv, wuk_ref[...], preferred_element_type=F32).astype(BF16)
    vm_ref[...] = jnp.dot(ckv, wuv_ref[...], preferred_element_type=F32).astype(BF16)

    kr = proj(3 * gw + 2 * MLA_RANK, 3 * gw + 2 * MLA_RANK + HEAD_DIM)
    kr_ref[...] = _rope32(kr, cosr, s1, s2).astype(BF16)


def _rope_tables(seq):
    pos = jnp.arange(seq, dtype=F32)[:, None]
    inv_a = 1.0 / (ROPE_THETA ** (jnp.arange(0, HEAD_DIM, 2, dtype=F32) / HEAD_DIM))
    ang_a = pos * inv_a[None, :]
    cosa = jnp.concatenate([jnp.cos(ang_a), jnp.cos(ang_a)], axis=1)
    sina = jnp.concatenate([-jnp.sin(ang_a), jnp.sin(ang_a)], axis=1)
    inv_r = 1.0 / (ROPE_THETA ** (jnp.arange(0, MLA_ROPE_DIM, 2, dtype=F32) / MLA_ROPE_DIM))
    ang_r = pos * inv_r[None, :]
    cr, sr = jnp.cos(ang_r), jnp.sin(ang_r)
    z32 = jnp.zeros_like(cr)
    z64 = jnp.zeros((seq, 64), F32)
    cosr = jnp.concatenate([cr, cr, z64], axis=1)
    s1 = jnp.concatenate([-sr, z32, z64], axis=1)
    s2 = jnp.concatenate([z32, sr, z64], axis=1)
    return cosa, sina, cosr, s1, s2


def input_projection(x2d, sc1, sh1, w_in, g_cq, w_uq, g_ckv, w_ukv, bsz, seq):
    t_tokens, d = x2d.shape
    tm = 256
    tiles_per_seq = seq // tm
    gw = GROUP_WIDTH
    in_w = 3 * gw + 2 * MLA_RANK + HEAD_DIM
    w_cat = jnp.pad(w_in, ((0, 0), (0, in_w - w_in.shape[1]))).astype(BF16)
    wuq = jnp.pad(w_uq.reshape(MLA_RANK, N_HEADS, MLA_QK_DIM),
                  ((0, 0), (0, 0), (0, 2 * HEAD_DIM - MLA_QK_DIM)))
    wuq = wuq.reshape(MLA_RANK, N_HEADS * 2 * HEAD_DIM).astype(BF16)
    wukv = w_ukv.reshape(MLA_RANK, N_HEADS, 2 * HEAD_DIM)
    wuk = wukv[:, :, :HEAD_DIM].reshape(MLA_RANK, gw).astype(BF16)
    wuv = wukv[:, :, HEAD_DIM:].reshape(MLA_RANK, gw).astype(BF16)
    tables = _rope_tables(seq)

    def const(shape):
        return pl.BlockSpec(shape, lambda i: (0,) * len(shape), pipeline_mode=pl.Buffered(1))

    def rows(width):
        return pl.BlockSpec((tm, width), lambda i: (i, 0))

    mod_spec = pl.BlockSpec((1, 1, d), lambda i: (i // tiles_per_seq, 0, 0))
    tab_spec = pl.BlockSpec((tm, HEAD_DIM), lambda i: (i % tiles_per_seq, 0))
    widths = (gw, gw, gw, 2 * gw, gw, gw, HEAD_DIM)
    return pl.pallas_call(
        _inproj_kernel,
        grid=(t_tokens // tm,),
        in_specs=[rows(d), mod_spec, mod_spec, const((d, in_w)),
                  const((MLA_RANK, 2 * gw)), const((MLA_RANK, gw)), const((MLA_RANK, gw)),
                  const((1, MLA_RANK)), const((1, MLA_RANK))] + [tab_spec] * 5,
        out_specs=[rows(w) for w in widths],
        out_shape=[jax.ShapeDtypeStruct((t_tokens, w), BF16) for w in widths],
        compiler_params=_params(("arbitrary",)),
        name="input_projection",
    )(x2d, sc1, sh1, w_cat, wuq, wuk, wuv, g_cq.reshape(1, -1), g_ckv.reshape(1, -1), *tables)


ATTN_BLOCK = 256
LOG2_E = 1.4426950408889634


def _causal_mask_t():
    key = lax.broadcasted_iota(I32, (ATTN_BLOCK, ATTN_BLOCK), 0)
    qry = lax.broadcasted_iota(I32, (ATTN_BLOCK, ATTN_BLOCK), 1)
    return key <= qry


def _scores_t(q, k_blk):
    return lax.dot_general(k_blk, q, NT_DIMS, preferred_element_type=F32)


def _first_block(s, vt_blk, c, acc_ref):
    s = jnp.where(_causal_mask_t(), s, -jnp.inf)
    m = jnp.max(s, axis=0, keepdims=True)
    p = jnp.exp2((s - m) * c)
    acc_ref[...] = jnp.dot(vt_blk, p.astype(BF16), preferred_element_type=F32)
    return m, jnp.sum(p, axis=0, keepdims=True)


def _next_block(s, vt_blk, bias, c, m, l, acc_ref):
    m_blk = jnp.max(s, axis=0, keepdims=True)
    if bias is not None:
        m_blk = m_blk + bias
    m_new = jnp.maximum(m, m_blk)
    alpha = jnp.exp2((m - m_new) * c)
    p = jnp.exp2((s - (m_new if bias is None else m_new - bias)) * c)
    acc_ref[...] = alpha * acc_ref[...] + jnp.dot(vt_blk, p.astype(BF16), preferred_element_type=F32)
    return m_new, alpha * l + jnp.sum(p, axis=0, keepdims=True)


HEADS_PER_STEP = 8


def _head_cols(g, width=HEAD_DIM):
    return slice(g * width, (g + 1) * width)


def _moba_block_bias(kmean, q, qi, n_blocks):
    gate = lax.dot_general(kmean, q.astype(F32), NT_DIMS, precision=lax.Precision.HIGHEST,
                           preferred_element_type=F32)
    row = lax.broadcasted_iota(I32, gate.shape, 0)
    gt = jnp.where(row < qi, gate, -jnp.inf)
    cnt = jnp.zeros(gate.shape, F32)
    for m in range(n_blocks - 1):
        gm = gt[m:m + 1, :]
        cnt = cnt + jnp.where((gm > gt) | ((gm == gt) & (m < row)), 1.0, 0.0)
    return jnp.where(cnt < float(MOBA_TOPK), 0.0, -jnp.inf).astype(F32)


def _moba_kernel(q_ref, k_ref, v_ref, o_ref, kmean_ref, vt_ref, acc_ref, *, n_blocks):
    qi = pl.program_id(2)
    blk = MOBA_BLOCK
    heads = HEADS_PER_STEP

    @pl.when(qi == 0)
    def _():
        kmean_ref[...] = jnp.zeros(kmean_ref.shape, F32)
        for g in range(heads):
            vt_ref[g] = v_ref[0, :, _head_cols(g)].astype(F32).T.astype(BF16)
            for n in range(n_blocks):
                kb = k_ref[0, n * blk:(n + 1) * blk, _head_cols(g)].astype(F32)
                kmean_ref[g, n:n + 1, :] = jnp.sum(kb, axis=0, keepdims=True) * (1.0 / blk)

    c = (HEAD_DIM ** -0.5) * LOG2_E
    own = pl.multiple_of(qi * blk, blk)
    qs = [q_ref[0, :, _head_cols(g)] for g in range(heads)]
    scores = [_scores_t(qs[g], k_ref[0, pl.ds(own, blk), _head_cols(g)]) for g in range(heads)]
    biases = [_moba_block_bias(kmean_ref[g], qs[g], qi, n_blocks) for g in range(heads)]
    carry0 = []
    for g in range(heads):
        carry0 += _first_block(scores[g], vt_ref[g, :, pl.ds(own, blk)], c, acc_ref.at[g])

    def body(n, carry):
        off = pl.multiple_of(n * blk, blk)
        scores = [_scores_t(qs[g], k_ref[0, pl.ds(off, blk), _head_cols(g)]) for g in range(heads)]
        row = lax.broadcasted_iota(I32, biases[0].shape, 0)
        out = []
        for g in range(heads):
            bias = jnp.sum(jnp.where(row == n, biases[g], 0.0), axis=0, keepdims=True)
            out += _next_block(scores[g], vt_ref[g, :, pl.ds(off, blk)], bias,
                               c, carry[2 * g], carry[2 * g + 1], acc_ref.at[g])
        return tuple(out)

    final = lax.fori_loop(0, qi, body, tuple(carry0))
    for g in range(heads):
        o_ref[0, :, _head_cols(g)] = (acc_ref[g] / final[2 * g + 1]).T.astype(o_ref.dtype)


def moba_attention(qa, ka, va, bsz, seq):
    blk = MOBA_BLOCK
    n_blocks = seq // blk
    heads = HEADS_PER_STEP
    gate_rows = -(-n_blocks // 8) * 8
    q3, k3, v3 = (t.reshape(bsz, seq, GROUP_WIDTH) for t in (qa, ka, va))
    q_spec = pl.BlockSpec((1, blk, heads * HEAD_DIM), lambda b, h, i: (b, i, h))
    kv_spec = pl.BlockSpec((1, seq, heads * HEAD_DIM), lambda b, h, i: (b, 0, h))
    out = pl.pallas_call(
        functools.partial(_moba_kernel, n_blocks=n_blocks),
        grid=(bsz, N_HEADS // heads, n_blocks),
        in_specs=[q_spec, kv_spec, kv_spec],
        out_specs=q_spec,
        out_shape=jax.ShapeDtypeStruct((bsz, seq, GROUP_WIDTH), BF16),
        scratch_shapes=[pltpu.VMEM((heads, gate_rows, HEAD_DIM), F32),
                        pltpu.VMEM((heads, HEAD_DIM, seq), BF16),
                        pltpu.VMEM((heads, HEAD_DIM, blk), F32)],
        compiler_params=_params(("arbitrary", "arbitrary", "arbitrary")),
        name="moba_attention",
    )(q3, k3, v3)
    return out.reshape(bsz * seq, GROUP_WIDTH)


def _mla_kernel(q_ref, kn_ref, kr_ref, v_ref, o_ref, kcat_ref, vt_ref, acc_ref):
    qi = pl.program_id(2)
    blk = ATTN_BLOCK

    heads = HEADS_PER_STEP

    @pl.when(qi == 0)
    def _():
        for g in range(heads):
            vt_ref[g] = v_ref[0, :, _head_cols(g)].astype(F32).T.astype(BF16)
            kcat_ref[g, :, :HEAD_DIM] = kn_ref[0, :, _head_cols(g)]
            kcat_ref[g, :, HEAD_DIM:] = kr_ref[0]

    c = (MLA_QK_DIM ** -0.5) * LOG2_E
    own = pl.multiple_of(qi * blk, blk)
    qs = [q_ref[0, :, _head_cols(g, 2 * HEAD_DIM)] for g in range(heads)]
    scores = [_scores_t(qs[g], kcat_ref[g, pl.ds(own, blk), :]) for g in range(heads)]
    carry0 = []
    for g in range(heads):
        carry0 += _first_block(scores[g], vt_ref[g, :, pl.ds(own, blk)], c, acc_ref.at[g])

    def body(n, carry):
        off = pl.multiple_of(n * blk, blk)
        scores = [_scores_t(qs[g], kcat_ref[g, pl.ds(off, blk), :]) for g in range(heads)]
        out = []
        for g in range(heads):
            out += _next_block(scores[g], vt_ref[g, :, pl.ds(off, blk)], None, c,
                               carry[2 * g], carry[2 * g + 1], acc_ref.at[g])
        return tuple(out)

    final = lax.fori_loop(0, qi, body, tuple(carry0))
    for g in range(heads):
        o_ref[0, :, _head_cols(g)] = (acc_ref[g] / final[2 * g + 1]).T.astype(o_ref.dtype)


def mla_attention(qm, kn, kr, vm, bsz, seq):
    blk = ATTN_BLOCK
    heads = HEADS_PER_STEP
    q3 = qm.reshape(bsz, seq, 2 * GROUP_WIDTH)
    kn3, vm3 = kn.reshape(bsz, seq, GROUP_WIDTH), vm.reshape(bsz, seq, GROUP_WIDTH)
    kr3 = kr.reshape(bsz, seq, HEAD_DIM)
    out = pl.pallas_call(
        _mla_kernel,
        grid=(bsz, N_HEADS // heads, seq // blk),
        in_specs=[pl.BlockSpec((1, blk, heads * 2 * HEAD_DIM), lambda b, h, i: (b, i, h)),
                  pl.BlockSpec((1, seq, heads * HEAD_DIM), lambda b, h, i: (b, 0, h)),
                  pl.BlockSpec((1, seq, HEAD_DIM), lambda b, h, i: (b, 0, 0)),
                  pl.BlockSpec((1, seq, heads * HEAD_DIM), lambda b, h, i: (b, 0, h))],
        out_specs=pl.BlockSpec((1, blk, heads * HEAD_DIM), lambda b, h, i: (b, i, h)),
        out_shape=jax.ShapeDtypeStruct((bsz, seq, GROUP_WIDTH), BF16),
        scratch_shapes=[pltpu.VMEM((heads, seq, 2 * HEAD_DIM), BF16),
                        pltpu.VMEM((heads, HEAD_DIM, seq), BF16),
                        pltpu.VMEM((heads, HEAD_DIM, blk), F32)],
        compiler_params=_params(("arbitrary", "arbitrary", "arbitrary")),
        name="mla_attention",
    )(q3, kn3, kr3, vm3)
    return out.reshape(bsz * seq, GROUP_WIDTH)


def _layer_norm(y, g, b):
    mu = jnp.mean(y, axis=-1, keepdims=True)
    yc = y - mu
    var = jnp.mean(yc * yc, axis=-1, keepdims=True)
    return yc * lax.rsqrt(var + LN_EPS) * g + b


def _outproj_kernel(a_ref, m_ref, x_ref, woa_ref, wom_ref, g1_ref, sc2_ref, sh2_ref, g2_ref,
                    lng_ref, lnb_ref, wsg_ref, wsu_ref, wsd_ref, h2_ref, h2p_ref, base_ref):
    f = (jnp.dot(a_ref[...], woa_ref[...], preferred_element_type=F32)
         + jnp.dot(m_ref[...], wom_ref[...], preferred_element_type=F32))
    x1 = _layer_norm(DEEPNORM_ALPHA * x_ref[...] + (1.0 + g1_ref[0]) * f, lng_ref[...], lnb_ref[...])
    h2 = x1 * (1.0 + sc2_ref[0]) + sh2_ref[0]
    h2_ref[...] = h2
    _store_packed_rows(h2p_ref, h2)
    hb = h2.astype(BF16)
    act = (_silu(jnp.dot(hb, wsg_ref[...], preferred_element_type=F32))
           * jnp.dot(hb, wsu_ref[...], preferred_element_type=F32)).astype(BF16)
    shared = jnp.dot(act, wsd_ref[...], preferred_element_type=F32)
    base_ref[...] = DEEPNORM_ALPHA * x1 + (1.0 + g2_ref[0]) * shared


def output_projection(attn_a, attn_m, x2d, w_o, g1, sc2, sh2, g2, ln_g, ln_b,
                      ws_gate, ws_up, ws_down, seq):
    t_tokens, d = x2d.shape
    tm = 256
    tiles_per_seq = seq // tm
    gw = GROUP_WIDTH
    ff = ws_gate.shape[1]

    def const(shape):
        return pl.BlockSpec(shape, lambda i: (0,) * len(shape), pipeline_mode=pl.Buffered(1))

    def rows(width):
        return pl.BlockSpec((tm, width), lambda i: (i, 0))

    mod_spec = pl.BlockSpec((1, 1, d), lambda i: (i // tiles_per_seq, 0, 0))
    return pl.pallas_call(
        _outproj_kernel,
        grid=(t_tokens // tm,),
        in_specs=[rows(gw), rows(gw), rows(d), const((gw, d)), const((gw, d)),
                  mod_spec, mod_spec, mod_spec, mod_spec, const((1, d)), const((1, d)),
                  const((d, ff)), const((d, ff)), const((ff, d))],
        out_specs=[rows(d), pl.BlockSpec((tm * PACK_CHUNKS, PACK_LANES), lambda i: (i, 0)), rows(d)],
        out_shape=[jax.ShapeDtypeStruct((t_tokens, d), F32),
                   jax.ShapeDtypeStruct((t_tokens * PACK_CHUNKS, PACK_LANES), U32),
                   jax.ShapeDtypeStruct((t_tokens, d), F32)],
        compiler_params=_params(("arbitrary",)),
        name="output_projection",
    )(attn_a, attn_m, x2d, w_o[:gw].astype(BF16), w_o[gw:].astype(BF16), g1, sc2, sh2, g2,
      ln_g.reshape(1, d), ln_b.reshape(1, d),
      ws_gate.astype(BF16), ws_up.astype(BF16), ws_down.astype(BF16))


ROUTER_TILE = 512


def _router_kernel(h_ref, wrt_ref, bias_ref, tri_ref, ltri_ref,
                   e_ref, pos_ref, w_ref, cnt_ref, run_ref, msk_ref):
    tr = ROUTER_TILE
    epg = N_EXPERTS // N_GROUPS

    @pl.when(pl.program_id(0) == 0)
    def _():
        run_ref[...] = jnp.zeros(run_ref.shape, F32)

    logits = lax.dot_general(wrt_ref[...], h_ref[...], NT_DIMS, precision=lax.Precision.HIGHEST,
                             preferred_element_type=F32)
    scores = jax.nn.sigmoid(logits)
    sel_scores = scores + bias_ref[...]

    sub = lax.broadcasted_iota(I32, (epg, tr), 0)
    grp_rows = []
    for g in range(N_GROUPS):
        xg = sel_scores[g * epg:(g + 1) * epg, :]
        m1 = jnp.max(xg, axis=0, keepdims=True)
        first = jnp.min(jnp.where(xg == m1, sub, epg), axis=0, keepdims=True)
        m2 = jnp.max(jnp.where(sub == first, -jnp.inf, xg), axis=0, keepdims=True)
        grp_rows.append(m1 + m2)
    grp = jnp.concatenate(grp_rows, axis=0)

    giota = lax.broadcasted_iota(I32, (N_GROUPS, tr), 0)
    gcnt = jnp.zeros((N_GROUPS, tr), F32)
    for m in range(N_GROUPS):
        rm = grp[m:m + 1, :]
        gcnt = gcnt + jnp.where((rm > grp) | ((rm == grp) & (m < giota)), 1.0, 0.0)
    gkeep = jnp.where(gcnt < float(TOPK_GROUPS), 1.0, 0.0)
    gmask = jnp.concatenate(
        [jnp.broadcast_to(gkeep[g:g + 1, :], (epg, tr)) for g in range(N_GROUPS)], axis=0)
    masked = jnp.where(gmask > 0.0, sel_scores, -jnp.inf)
    msk_ref[...] = masked

    eiota = lax.broadcasted_iota(I32, (N_EXPERTS, tr), 0)

    def rank_body(m, cnt):
        rm = msk_ref[pl.ds(m, 1), :]
        return cnt + jnp.where((rm > masked) | ((rm == masked) & (m < eiota)), 1.0, 0.0)

    ecnt = lax.fori_loop(0, N_EXPERTS, rank_body, jnp.zeros((N_EXPERTS, tr), F32))
    sel = ecnt < float(TOP_K)

    sw = jnp.where(sel, scores, 0.0)
    wn = sw / jnp.sum(sw, axis=0, keepdims=True) * ROUTED_SCALE

    selb = jnp.where(sel, 1.0, 0.0).astype(BF16)
    cum = jnp.dot(selb, tri_ref[...], preferred_element_type=F32)
    pos = run_ref[...] + cum - 1.0
    run_ref[...] = run_ref[...] + cum[:, tr - 1:tr]
    cnt_ref[...] = jnp.broadcast_to(run_ref[...], cnt_ref.shape).astype(I32)

    slot = jnp.dot(ltri_ref[...], selb, preferred_element_type=F32)
    ef = eiota.astype(F32)
    e_rows, p_rows, w_rows = [], [], []
    for j in range(TOP_K):
        mj = sel & (slot == float(j))
        e_rows.append(jnp.sum(jnp.where(mj, ef, 0.0), axis=0, keepdims=True))
        p_rows.append(jnp.sum(jnp.where(mj, pos, 0.0), axis=0, keepdims=True))
        w_rows.append(jnp.sum(jnp.where(mj, wn, 0.0), axis=0, keepdims=True))
    e_ref[...] = jnp.concatenate(e_rows, axis=0).astype(I32)
    pos_ref[...] = jnp.concatenate(p_rows, axis=0).astype(I32)
    w_ref[...] = jnp.concatenate(w_rows, axis=0)


def router(h2, w_router, router_bias):
    t_tokens, d = h2.shape
    tr = ROUTER_TILE
    tri = jnp.asarray(np.triu(np.ones((tr, tr), np.float32)), BF16)
    ltri = jnp.asarray(np.tril(np.ones((N_EXPERTS, N_EXPERTS), np.float32), -1), BF16)

    def const(shape):
        return pl.BlockSpec(shape, lambda i: (0,) * len(shape))

    tok = pl.BlockSpec((TOP_K, tr), lambda i: (0, i))
    e_t, pos_t, w_t, cnt = pl.pallas_call(
        _router_kernel,
        grid=(t_tokens // tr,),
        in_specs=[pl.BlockSpec((tr, d), lambda i: (i, 0)), const((N_EXPERTS, d)),
                  const((N_EXPERTS, 1)), const((tr, tr)), const((N_EXPERTS, N_EXPERTS))],
        out_specs=[tok, tok, tok, const((N_EXPERTS, 128))],
        out_shape=[jax.ShapeDtypeStruct((TOP_K, t_tokens), I32),
                   jax.ShapeDtypeStruct((TOP_K, t_tokens), I32),
                   jax.ShapeDtypeStruct((TOP_K, t_tokens), F32),
                   jax.ShapeDtypeStruct((N_EXPERTS, 128), I32)],
        scratch_shapes=[pltpu.VMEM((N_EXPERTS, 1), F32), pltpu.VMEM((N_EXPERTS, tr), F32)],
        compiler_params=_params(("arbitrary",)),
        name="router",
    )(h2, w_router.T, router_bias.reshape(N_EXPERTS, 1), tri, ltri)
    return e_t, pos_t, w_t, cnt[:, 0]


DISPATCH_TILE = 128


def _dispatch_kernel(d_ref, h_ref, xs_in_ref, xs_ref, sem):
    del xs_in_ref
    tt = DISPATCH_TILE
    pc = PACK_CHUNKS
    n_copies = tt * TOP_K
    for j in range(tt):
        for k in range(TOP_K):
            c = j * TOP_K + k
            dst = xs_ref.at[pl.ds(pl.multiple_of(d_ref[c] * pc, pc), pc), :]
            pltpu.make_async_copy(h_ref.at[pl.ds(j * pc, pc), :], dst, sem.at[0]).start(priority=c % 2)
    tile_rows = xs_ref.at[pl.ds(0, n_copies * pc), :]
    pltpu.make_async_copy(tile_rows, tile_rows, sem.at[0]).wait()


def dispatch(h2p, dest_tok_major, n_rows):
    tt = DISPATCH_TILE
    pc = PACK_CHUNKS
    t_tokens = h2p.shape[0] // pc
    n_copies = tt * TOP_K
    zeros = jnp.zeros((n_rows * pc, PACK_LANES), U32)
    return pl.pallas_call(
        _dispatch_kernel,
        grid=(t_tokens // tt,),
        in_specs=[pl.BlockSpec((n_copies,), lambda i: (i,), memory_space=pltpu.SMEM),
                  pl.BlockSpec((tt * pc, PACK_LANES), lambda i: (i, 0)),
                  pl.BlockSpec(memory_space=pl.ANY)],
        out_specs=pl.BlockSpec(memory_space=pl.ANY),
        out_shape=jax.ShapeDtypeStruct((n_rows * pc, PACK_LANES), U32),
        scratch_shapes=[pltpu.SemaphoreType.DMA((1,))],
        input_output_aliases={2: 0},
        compiler_params=_params(("arbitrary",)),
        name="dispatch",
    )(dest_tok_major, h2p, zeros)


def _experts_kernel(be_ref, nused_ref, x_ref, wg_ref, wu_ref, wd_ref, y_ref, wg_bf, wu_bf, wd_bf):
    i = pl.program_id(0)
    nused = nused_ref[0]

    @pl.when(i < nused)
    def _():
        @pl.when((i == 0) | (be_ref[i] != be_ref[jnp.maximum(i - 1, 0)]))
        def _():
            wg_bf[...] = wg_ref[0].astype(BF16)
            wu_bf[...] = wu_ref[0].astype(BF16)
            wd_bf[...] = wd_ref[0].astype(BF16)

        lo, hi = _load_packed_rows(x_ref, 0, MOE_ROW_BLOCK)
        xb = jnp.concatenate([lo.astype(BF16), hi.astype(BF16)], axis=1)
        act = (_silu(jnp.dot(xb, wg_bf[...], preferred_element_type=F32))
               * jnp.dot(xb, wu_bf[...], preferred_element_type=F32)).astype(BF16)
        _store_packed_rows(y_ref, jnp.dot(act, wd_bf[...], preferred_element_type=F32))

    @pl.when(i >= nused)
    def _():
        y_ref[...] = jnp.zeros(y_ref.shape, U32)


def routed_experts(x_sorted, block_e, nused, w_gate, w_up, w_down):
    rb = MOE_ROW_BLOCK
    pc = PACK_CHUNKS
    n_blocks = x_sorted.shape[0] // (rb * pc)
    d, ff = w_gate.shape[1], w_gate.shape[2]
    grid_spec = pltpu.PrefetchScalarGridSpec(
        num_scalar_prefetch=2,
        grid=(n_blocks,),
        in_specs=[pl.BlockSpec((rb * pc, PACK_LANES),
                               lambda i, be, nu: (jnp.minimum(i, jnp.maximum(nu[0] - 1, 0)), 0)),
                  pl.BlockSpec((1, d, ff), lambda i, be, nu: (be[i], 0, 0)),
                  pl.BlockSpec((1, d, ff), lambda i, be, nu: (be[i], 0, 0)),
                  pl.BlockSpec((1, ff, d), lambda i, be, nu: (be[i], 0, 0))],
        out_specs=pl.BlockSpec((rb * pc, PACK_LANES), lambda i, be, nu: (i, 0)),
        scratch_shapes=[pltpu.VMEM((d, ff), BF16), pltpu.VMEM((d, ff), BF16),
                        pltpu.VMEM((ff, d), BF16)],
    )
    return pl.pallas_call(
        _experts_kernel,
        grid_spec=grid_spec,
        out_shape=jax.ShapeDtypeStruct(x_sorted.shape, U32),
        compiler_params=_params(("arbitrary",)),
        name="routed_experts",
    )(block_e, nused, x_sorted, w_gate, w_up, w_down)


COMBINE_TILE = 128


def _combine_kernel(d_cur_ref, d_next_ref, y_ref, w_ref, base_ref, g2_ref, lng_ref, lnb_ref,
                    o_ref, ybuf_a, ybuf_b, sem):
    i = pl.program_id(0)
    n_steps = pl.num_programs(0)
    tc = COMBINE_TILE
    n_copies = TOP_K * tc
    pc = PACK_CHUNKS

    def row_copy(row, buf, r, s):
        src = y_ref.at[pl.ds(pl.multiple_of(row * pc, pc), pc), :]
        return pltpu.make_async_copy(src, buf.at[pl.ds(r * pc, pc), :], sem.at[s])

    def wait_rows(buf, s):
        pltpu.make_async_copy(y_ref.at[pl.ds(0, n_copies * pc), :], buf, sem.at[s]).wait()

    @pl.when(i == 0)
    def _():
        def body(r, carry):
            src = y_ref.at[pl.ds(pl.multiple_of(d_cur_ref[r] * pc, pc), pc), :]
            dst = ybuf_a.at[pl.ds(pl.multiple_of(r * pc, pc), pc), :]
            pltpu.make_async_copy(src, dst, sem.at[0]).start()
            return carry
        lax.fori_loop(0, n_copies, body, 0, unroll=8)

    def step(cur, cur_s, nxt, nxt_s):
        wait_rows(cur, cur_s)
        for r in range(n_copies):
            row_copy(d_next_ref[r], nxt, r, nxt_s).start(priority=r % 2)
        w = w_ref[...]
        lo, hi = _load_packed_rows(cur, 0, tc)
        routed_lo, routed_hi = w[:, 0:1] * lo, w[:, 0:1] * hi
        for k in range(1, TOP_K):
            lo, hi = _load_packed_rows(cur, k * tc, tc)
            routed_lo = routed_lo + w[:, k:k + 1] * lo
            routed_hi = routed_hi + w[:, k:k + 1] * hi
        routed = jnp.concatenate([routed_lo, routed_hi], axis=1)
        y = base_ref[...] + (1.0 + g2_ref[0]) * routed
        o_ref[...] = _layer_norm(y, lng_ref[...], lnb_ref[...])

    @pl.when(i % 2 == 0)
    def _():
        step(ybuf_a, 0, ybuf_b, 1)

    @pl.when(i % 2 == 1)
    def _():
        step(ybuf_b, 1, ybuf_a, 0)

    @pl.when((i == n_steps - 1) & (i % 2 == 0))
    def _():
        wait_rows(ybuf_b, 1)

    @pl.when((i == n_steps - 1) & (i % 2 == 1))
    def _():
        wait_rows(ybuf_a, 0)


def combine(y_sorted, dest_tiles, w_tok, base, g2, ln_g, ln_b, seq):
    t_tokens, d = base.shape
    tc = COMBINE_TILE
    n_steps = t_tokens // tc
    tiles_per_seq = seq // tc
    n_copies = TOP_K * tc

    def const(shape):
        return pl.BlockSpec(shape, lambda i: (0,) * len(shape))

    return pl.pallas_call(
        _combine_kernel,
        grid=(n_steps,),
        in_specs=[pl.BlockSpec((n_copies,), lambda i: (i,), memory_space=pltpu.SMEM),
                  pl.BlockSpec((n_copies,), lambda i: (jnp.minimum(i + 1, n_steps - 1),),
                               memory_space=pltpu.SMEM),
                  pl.BlockSpec(memory_space=pl.ANY),
                  pl.BlockSpec((tc, TOP_K), lambda i: (i, 0)),
                  pl.BlockSpec((tc, d), lambda i: (i, 0)),
                  pl.BlockSpec((1, 1, d), lambda i: (i // tiles_per_seq, 0, 0)),
                  const((1, d)), const((1, d))],
        out_specs=pl.BlockSpec((tc, d), lambda i: (i, 0)),
        out_shape=jax.ShapeDtypeStruct((t_tokens, d), F32),
        scratch_shapes=[pltpu.VMEM((n_copies * PACK_CHUNKS, PACK_LANES), U32),
                        pltpu.VMEM((n_copies * PACK_CHUNKS, PACK_LANES), U32),
                        pltpu.SemaphoreType.DMA((2,))],
        compiler_params=_params(("arbitrary",)),
        name="combine",
    )(dest_tiles, dest_tiles, y_sorted, w_tok, base, g2, ln_g.reshape(1, d), ln_b.reshape(1, d))


def _dispatch_tables(e_t, pos_t, counts, t_tokens):
    rb = MOE_ROW_BLOCK
    n_blocks = t_tokens * TOP_K // rb + N_EXPERTS
    padded = (counts + rb - 1) // rb * rb
    pends = jnp.cumsum(padded)
    pstarts = pends - padded
    experts = jnp.arange(N_EXPERTS, dtype=I32)
    pstart_of = jnp.sum(jnp.where(e_t[..., None] == experts, pstarts.astype(I32), 0), axis=-1)
    dest = (pstart_of + pos_t).astype(I32)
    nused = (pends[-1] // rb).astype(I32)
    blk = jnp.arange(n_blocks, dtype=I32)
    last_blk = jnp.minimum(blk, jnp.maximum(nused - 1, 0))
    block_e = jnp.sum((pends[None, :] <= (last_blk * rb)[:, None]).astype(I32), axis=1)
    block_e = jnp.minimum(block_e, N_EXPERTS - 1).astype(I32)
    return dest, block_e, nused.reshape(1), n_blocks * rb


def _layer(x, c, w_ada, b_ada, w_in, g_cq, w_uq, g_ckv, w_ukv, w_o, ln1_g, ln1_b,
           w_router, router_bias, w_gate, w_up, w_down, ws_gate, ws_up, ws_down, ln2_g, ln2_b):
    bsz, seq, d = x.shape
    t_tokens = bsz * seq
    x2d = x.reshape(t_tokens, d)

    mod = ada_modulation(c, w_ada, b_ada)
    sh1, sc1, g1, sh2, sc2, g2 = [m.reshape(bsz, 1, d) for m in jnp.split(mod, 6, axis=-1)]

    qa, ka, va, qm, kn, vm, kr = input_projection(x2d, sc1, sh1, w_in, g_cq, w_uq, g_ckv, w_ukv,
                                                  bsz, seq)
    attn_a = moba_attention(qa, ka, va, bsz, seq)
    attn_m = mla_attention(qm, kn, kr, vm, bsz, seq)
    h2, h2p, base = output_projection(attn_a, attn_m, x2d, w_o, g1, sc2, sh2, g2, ln1_g, ln1_b,
                                 ws_gate, ws_up, ws_down, seq)

    e_t, pos_t, w_t, counts = router(h2, w_router, router_bias)
    dest, block_e, nused, n_rows = _dispatch_tables(e_t, pos_t, counts, t_tokens)
    x_sorted = dispatch(h2p, dest.T.reshape(-1), n_rows)
    y_sorted = routed_experts(x_sorted, block_e, nused, w_gate, w_up, w_down)

    tc = COMBINE_TILE
    dest_tiles = dest.reshape(TOP_K, t_tokens // tc, tc).transpose(1, 0, 2).reshape(-1)
    out = combine(y_sorted, dest_tiles, w_t.T, base, g2, ln2_g, ln2_b, seq)
    return out.reshape(bsz, seq, d)


def kernel(x, c, w_ada, b_ada, w_in, g_cq, w_uq, g_ckv, w_ukv, w_o, ln1_g, ln1_b, w_router,
           router_bias, w_gate, w_up, w_down, ws_gate, ws_up, ws_down, ln2_g, ln2_b):
    depth = w_ada.shape[0]
    assert depth == 1, "DeepNorm constants are baked for a single layer"
    return _layer(x, c, w_ada[0], b_ada[0], w_in[0], g_cq[0], w_uq[0], g_ckv[0], w_ukv[0], w_o[0],
                  ln1_g[0], ln1_b[0], w_router[0], router_bias[0], w_gate[0], w_up[0], w_down[0],
                  ws_gate[0], ws_up[0], ws_down[0], ln2_g[0], ln2_b[0])
```

```python
import functools

import jax
import jax.numpy as jnp
import numpy as np
from jax import lax
from jax.experimental import pallas as pl
from jax.experimental.pallas import tpu as pltpu

F32 = jnp.float32
BF16 = jnp.bfloat16
I32 = jnp.int32

D_MODEL = 2048
HEAD_DIM = 128
N_HEADS = 8
MOBA_BLOCK = 256
MOBA_TOPK = 3
MLA_ROPE_DIM = 64
MLA_RANK = 512
MLA_QK_DIM = HEAD_DIM + MLA_ROPE_DIM
GROUP_WIDTH = N_HEADS * HEAD_DIM
ROPE_THETA = 10000.0
N_EXPERTS = 64
TOP_K = 8
N_GROUPS = 8
TOPK_GROUPS = 4
EXPERT_FF = 512
ROUTED_SCALE = 2.5
MOE_ROW_BLOCK = 256
LN_EPS = 1e-5
RMS_EPS = 1e-6
DEEPNORM_ALPHA = 2.0 ** 0.25

VMEM_LIMIT_BYTES = 56 * 1024 * 1024

NT_DIMS = (((1,), (1,)), ((), ()))


def _params(semantics):
    return pltpu.CompilerParams(dimension_semantics=semantics,
                                vmem_limit_bytes=VMEM_LIMIT_BYTES)


def _silu(x):
    return x * jax.nn.sigmoid(x)


PACK_CHUNKS = 8
PACK_LANES = 128
U32 = jnp.uint32
HIGH_HALF = np.uint32(0xFFFF0000)


def _store_packed_rows(ref, x):
    n_rows, half = x.shape[0], x.shape[1] // 2
    lo = lax.bitcast_convert_type(x[:, :half].astype(BF16).astype(F32), U32) >> 16
    hi = lax.bitcast_convert_type(x[:, half:].astype(BF16).astype(F32), U32) & HIGH_HALF
    packed = lo | hi
    for j in range(PACK_CHUNKS):
        ref[pl.ds(j, n_rows, stride=PACK_CHUNKS), :] = packed[:, j * PACK_LANES:(j + 1) * PACK_LANES]


def _load_packed_rows(ref, first_row, n_rows):
    lo, hi = [], []
    for j in range(PACK_CHUNKS):
        u = ref[pl.ds(first_row * PACK_CHUNKS + j, n_rows, stride=PACK_CHUNKS), :]
        lo.append(lax.bitcast_convert_type(u << 16, F32))
        hi.append(lax.bitcast_convert_type(u & HIGH_HALF, F32))
    return jnp.concatenate(lo, axis=1), jnp.concatenate(hi, axis=1)


def _ada_kernel(c_ref, w_ref, b_ref, o_ref):
    a = _silu(c_ref[...]).astype(BF16)
    o_ref[...] = jnp.dot(a, w_ref[...].astype(BF16), preferred_element_type=F32) + b_ref[...]


def ada_modulation(c, w_ada, b_ada):
    bsz, d = c.shape
    rows = -(-bsz // 16) * 16
    c_pad = jnp.pad(c, ((0, rows - bsz), (0, 0)))
    n = w_ada.shape[1]
    tn = 1024
    out = pl.pallas_call(
        _ada_kernel,
        grid=(n // tn,),
        in_specs=[pl.BlockSpec((rows, d), lambda j: (0, 0)),
                  pl.BlockSpec((d, tn), lambda j: (0, j)),
                  pl.BlockSpec((1, tn), lambda j: (0, j))],
        out_specs=pl.BlockSpec((rows, tn), lambda j: (0, j)),
        out_shape=jax.ShapeDtypeStruct((rows, n), F32),
        compiler_params=_params(("arbitrary",)),
        name="ada_modulation",
    )(c_pad, w_ada, b_ada.reshape(1, n))
    return out[:bsz]


def _rms(x, g):
    return x * lax.rsqrt(jnp.mean(x * x, axis=-1, keepdims=True) + RMS_EPS) * g


def _rope64(x, cos, sin):
    return x * cos + pltpu.roll(x, 64, axis=1) * sin


def _rope32(x, cos, s1, s2):
    return x * cos + pltpu.roll(x, 96, axis=1) * s1 + pltpu.roll(x, 32, axis=1) * s2


def _inproj_kernel(x_ref, sc_ref, sh_ref, w_ref, wuq_ref, wuk_ref, wuv_ref, gcq_ref, gckv_ref,
                   cosa_ref, sina_ref, cosr_ref, s1_ref, s2_ref,
                   qa_ref, ka_ref, va_ref, qm_ref, kn_ref, vm_ref, kr_ref):
    h = (x_ref[...] * (1.0 + sc_ref[0]) + sh_ref[0]).astype(BF16)
    cosa, sina = cosa_ref[...], sina_ref[...]
    cosr, s1, s2 = cosr_ref[...], s1_ref[...], s2_ref[...]
    gw = GROUP_WIDTH

    def proj(lo, hi):
        return jnp.dot(h, w_ref[:, lo:hi], preferred_element_type=F32)

    for out_ref, base in ((qa_ref, 0), (ka_ref, gw)):
        t = proj(base, base + gw)
        for hd in range(N_HEADS):
            sl = slice(hd * HEAD_DIM, (hd + 1) * HEAD_DIM)
            out_ref[:, sl] = _rope64(t[:, sl], cosa, sina).astype(BF16)
    va_ref[...] = proj(2 * gw, 3 * gw).astype(BF16)

    cq = _rms(proj(3 * gw, 3 * gw + MLA_RANK), gcq_ref[...]).astype(BF16)
    qm = jnp.dot(cq, wuq_ref[...], preferred_element_type=F32)
    for hd in range(N_HEADS):
        b0 = hd * 2 * HEAD_DIM
        qm_ref[:, b0:b0 + HEAD_DIM] = qm[:, b0:b0 + HEAD_DIM].astype(BF16)
        qm_ref[:, b0 + HEAD_DIM:b0 + 2 * HEAD_DIM] = _rope32(
            qm[:, b0 + HEAD_DIM:b0 + 2 * HEAD_DIM], cosr, s1, s2).astype(BF16)

    ckv = _rms(proj(3 * gw + MLA_RANK, 3 * gw + 2 * MLA_RANK), gckv_ref[...]).astype(BF16)
    kn_ref[...] = jnp.dot(ckv, wuk_ref[...], preferred_element_type=F32).astype(BF16)
    vm_ref[...] = jnp.dot(ckv, wuv_ref[...], preferred_element_type=F32).astype(BF16)

    kr = proj(3 * gw + 2 * MLA_RANK, 3 * gw + 2 * MLA_RANK + HEAD_DIM)
    kr_ref[...] = _rope32(kr, cosr, s1, s2).astype(BF16)


def _rope_tables(seq):
    pos = jnp.arange(seq, dtype=F32)[:, None]
    inv_a = 1.0 / (ROPE_THETA ** (jnp.arange(0, HEAD_DIM, 2, dtype=F32) / HEAD_DIM))
    ang_a = pos * inv_a[None, :]
    cosa = jnp.concatenate([jnp.cos(ang_a), jnp.cos(ang_a)], axis=1)
    sina = jnp.concatenate([-jnp.sin(ang_a), jnp.sin(ang_a)], axis=1)
    inv_r = 1.0 / (ROPE_THETA ** (jnp.arange(0, MLA_ROPE_DIM, 2, dtype=F32) / MLA_ROPE_DIM))
    ang_r = pos * inv_r[None, :]
    cr, sr = jnp.cos(ang_r), jnp.sin(ang_r)
    z32 = jnp.zeros_like(cr)
    z64 = jnp.zeros((seq, 64), F32)
    cosr = jnp.concatenate([cr, cr, z64], axis=1)
    s1 = jnp.concatenate([-sr, z32, z64], axis=1)
    s2 = jnp.concatenate([z32, sr, z64], axis=1)
    return cosa, sina, cosr, s1, s2


def input_projection(x2d, sc1, sh1, w_in, g_cq, w_uq, g_ckv, w_ukv, bsz, seq):
    t_tokens, d = x2d.shape
    tm = 256
    tiles_per_seq = seq // tm
    gw = GROUP_WIDTH
    in_w = 3 * gw + 2 * MLA_RANK + HEAD_DIM
    w_cat = jnp.pad(w_in, ((0, 0), (0, in_w - w_in.shape[1]))).astype(BF16)
    wuq = jnp.pad(w_uq.reshape(MLA_RANK, N_HEADS, MLA_QK_DIM),
                  ((0, 0), (0, 0), (0, 2 * HEAD_DIM - MLA_QK_DIM)))
    wuq = wuq.reshape(MLA_RANK, N_HEADS * 2 * HEAD_DIM).astype(BF16)
    wukv = w_ukv.reshape(MLA_RANK, N_HEADS, 2 * HEAD_DIM)
    wuk = wukv[:, :, :HEAD_DIM].reshape(MLA_RANK, gw).astype(BF16)
    wuv = wukv[:, :, HEAD_DIM:].reshape(MLA_RANK, gw).astype(BF16)
    tables = _rope_tables(seq)

    def const(shape):
        return pl.BlockSpec(shape, lambda i: (0,) * len(shape), pipeline_mode=pl.Buffered(1))

    def rows(width):
        return pl.BlockSpec((tm, width), lambda i: (i, 0))

    mod_spec = pl.BlockSpec((1, 1, d), lambda i: (i // tiles_per_seq, 0, 0))
    tab_spec = pl.BlockSpec((tm, HEAD_DIM), lambda i: (i % tiles_per_seq, 0))
    widths = (gw, gw, gw, 2 * gw, gw, gw, HEAD_DIM)
    return pl.pallas_call(
        _inproj_kernel,
        grid=(t_tokens // tm,),
        in_specs=[rows(d), mod_spec, mod_spec, const((d, in_w)),
                  const((MLA_RANK, 2 * gw)), const((MLA_RANK, gw)), const((MLA_RANK, gw)),
                  const((1, MLA_RANK)), const((1, MLA_RANK))] + [tab_spec] * 5,
        out_specs=[rows(w) for w in widths],
        out_shape=[jax.ShapeDtypeStruct((t_tokens, w), BF16) for w in widths],
        compiler_params=_params(("arbitrary",)),
        name="input_projection",
    )(x2d, sc1, sh1, w_cat, wuq, wuk, wuv, g_cq.reshape(1, -1), g_ckv.reshape(1, -1), *tables)


ATTN_BLOCK = 256
LOG2_E = 1.4426950408889634


def _causal_mask_t():
    key = lax.broadcasted_iota(I32, (ATTN_BLOCK, ATTN_BLOCK), 0)
    qry = lax.broadcasted_iota(I32, (ATTN_BLOCK, ATTN_BLOCK), 1)
    return key <= qry


def _scores_t(q, k_blk):
    return lax.dot_general(k_blk, q, NT_DIMS, preferred_element_type=F32)


def _first_block(s, vt_blk, c, acc_ref):
    s = jnp.where(_causal_mask_t(), s, -jnp.inf)
    m = jnp.max(s, axis=0, keepdims=True)
    p = jnp.exp2((s - m) * c)
    acc_ref[...] = jnp.dot(vt_blk, p.astype(BF16), preferred_element_type=F32)
    return m, jnp.sum(p, axis=0, keepdims=True)


def _next_block(s, vt_blk, bias, c, m, l, acc_ref):
    m_blk = jnp.max(s, axis=0, keepdims=True)
    if bias is not None:
        m_blk = m_blk + bias
    m_new = jnp.maximum(m, m_blk)
    alpha = jnp.exp2((m - m_new) * c)
    p = jnp.exp2((s - (m_new if bias is None else m_new - bias)) * c)
    acc_ref[...] = alpha * acc_ref[...] + jnp.dot(vt_blk, p.astype(BF16), preferred_element_type=F32)
    return m_new, alpha * l + jnp.sum(p, axis=0, keepdims=True)


HEADS_PER_STEP = 8


def _head_cols(g, width=HEAD_DIM):
    return slice(g * width, (g + 1) * width)


def _moba_block_bias(kmean, q, qi, n_blocks):
    gate = lax.dot_general(kmean, q.astype(F32), NT_DIMS, precision=lax.Precision.HIGHEST,
                           preferred_element_type=F32)
    row = lax.broadcasted_iota(I32, gate.shape, 0)
    gt = jnp.where(row < qi, gate, -jnp.inf)
    cnt = jnp.zeros(gate.shape, F32)
    for m in range(n_blocks - 1):
        gm = gt[m:m + 1, :]
        cnt = cnt + jnp.where((gm > gt) | ((gm == gt) & (m < row)), 1.0, 0.0)
    return jnp.where(cnt < float(MOBA_TOPK), 0.0, -jnp.inf).astype(F32)


def _moba_kernel(q_ref, k_ref, v_ref, o_ref, kmean_ref, vt_ref, acc_ref, *, n_blocks):
    qi = pl.program_id(2)
    blk = MOBA_BLOCK
    heads = HEADS_PER_STEP

    @pl.when(qi == 0)
    def _():
        kmean_ref[...] = jnp.zeros(kmean_ref.shape, F32)
        for g in range(heads):
            vt_ref[g] = v_ref[0, :, _head_cols(g)].astype(F32).T.astype(BF16)
            for n in range(n_blocks):
                kb = k_ref[0, n * blk:(n + 1) * blk, _head_cols(g)].astype(F32)
                kmean_ref[g, n:n + 1, :] = jnp.sum(kb, axis=0, keepdims=True) * (1.0 / blk)

    c = (HEAD_DIM ** -0.5) * LOG2_E
    own = pl.multiple_of(qi * blk, blk)
    qs = [q_ref[0, :, _head_cols(g)] for g in range(heads)]
    scores = [_scores_t(qs[g], k_ref[0, pl.ds(own, blk), _head_cols(g)]) for g in range(heads)]
    biases = [_moba_block_bias(kmean_ref[g], qs[g], qi, n_blocks) for g in range(heads)]
    carry0 = []
    for g in range(heads):
        carry0 += _first_block(scores[g], vt_ref[g, :, pl.ds(own, blk)], c, acc_ref.at[g])

    def body(n, carry):
        off = pl.multiple_of(n * blk, blk)
        scores = [_scores_t(qs[g], k_ref[0, pl.ds(off, blk), _head_cols(g)]) for g in range(heads)]
        row = lax.broadcasted_iota(I32, biases[0].shape, 0)
        out = []
        for g in range(heads):
            bias = jnp.sum(jnp.where(row == n, biases[g], 0.0), axis=0, keepdims=True)
            out += _next_block(scores[g], vt_ref[g, :, pl.ds(off, blk)], bias,
                               c, carry[2 * g], carry[2 * g + 1], acc_ref.at[g])
        return tuple(out)

    final = lax.fori_loop(0, qi, body, tuple(carry0))
    for g in range(heads):
        o_ref[0, :, _head_cols(g)] = (acc_ref[g] / final[2 * g + 1]).T.astype(o_ref.dtype)


def moba_attention(qa, ka, va, bsz, seq):
    blk = MOBA_BLOCK
    n_blocks = seq // blk
    heads = HEADS_PER_STEP
    gate_rows = -(-n_blocks // 8) * 8
    q3, k3, v3 = (t.reshape(bsz, seq, GROUP_WIDTH) for t in (qa, ka, va))
    q_spec = pl.BlockSpec((1, blk, heads * HEAD_DIM), lambda b, h, i: (b, i, h))
    kv_spec = pl.BlockSpec((1, seq, heads * HEAD_DIM), lambda b, h, i: (b, 0, h))
    out = pl.pallas_call(
        functools.partial(_moba_kernel, n_blocks=n_blocks),
        grid=(bsz, N_HEADS // heads, n_blocks),
        in_specs=[q_spec, kv_spec, kv_spec],
        out_specs=q_spec,
        out_shape=jax.ShapeDtypeStruct((bsz, seq, GROUP_WIDTH), BF16),
        scratch_shapes=[pltpu.VMEM((heads, gate_rows, HEAD_DIM), F32),
                        pltpu.VMEM((heads, HEAD_DIM, seq), BF16),
                        pltpu.VMEM((heads, HEAD_DIM, blk), F32)],
        compiler_params=_params(("arbitrary", "arbitrary", "arbitrary")),
        name="moba_attention",
    )(q3, k3, v3)
    return out.reshape(bsz * seq, GROUP_WIDTH)


def _mla_kernel(q_ref, kn_ref, kr_ref, v_ref, o_ref, kcat_ref, vt_ref, acc_ref):
    qi = pl.program_id(2)
    blk = ATTN_BLOCK

    heads = HEADS_PER_STEP

    @pl.when(qi == 0)
    def _():
        for g in range(heads):
            vt_ref[g] = v_ref[0, :, _head_cols(g)].astype(F32).T.astype(BF16)
            kcat_ref[g, :, :HEAD_DIM] = kn_ref[0, :, _head_cols(g)]
            kcat_ref[g, :, HEAD_DIM:] = kr_ref[0]

    c = (MLA_QK_DIM ** -0.5) * LOG2_E
    own = pl.multiple_of(qi * blk, blk)
    qs = [q_ref[0, :, _head_cols(g, 2 * HEAD_DIM)] for g in range(heads)]
    scores = [_scores_t(qs[g], kcat_ref[g, pl.ds(own, blk), :]) for g in range(heads)]
    carry0 = []
    for g in range(heads):
        carry0 += _first_block(scores[g], vt_ref[g, :, pl.ds(own, blk)], c, acc_ref.at[g])

    def body(n, carry):
        off = pl.multiple_of(n * blk, blk)
        scores = [_scores_t(qs[g], kcat_ref[g, pl.ds(off, blk), :]) for g in range(heads)]
        out = []
        for g in range(heads):
            out += _next_block(scores[g], vt_ref[g, :, pl.ds(off, blk)], None, c,
                               carry[2 * g], carry[2 * g + 1], acc_ref.at[g])
        return tuple(out)

    final = lax.fori_loop(0, qi, body, tuple(carry0))
    for g in range(heads):
        o_ref[0, :, _head_cols(g)] = (acc_ref[g] / final[2 * g + 1]).T.astype(o_ref.dtype)


def mla_attention(qm, kn, kr, vm, bsz, seq):
    blk = ATTN_BLOCK
    heads = HEADS_PER_STEP
    q3 = qm.reshape(bsz, seq, 2 * GROUP_WIDTH)
    kn3, vm3 = kn.reshape(bsz, seq, GROUP_WIDTH), vm.reshape(bsz, seq, GROUP_WIDTH)
    kr3 = kr.reshape(bsz, seq, HEAD_DIM)
    out = pl.pallas_call(
        _mla_kernel,
        grid=(bsz, N_HEADS // heads, seq // blk),
        in_specs=[pl.BlockSpec((1, blk, heads * 2 * HEAD_DIM), lambda b, h, i: (b, i, h)),
                  pl.BlockSpec((1, seq, heads * HEAD_DIM), lambda b, h, i: (b, 0, h)),
                  pl.BlockSpec((1, seq, HEAD_DIM), lambda b, h, i: (b, 0, 0)),
                  pl.BlockSpec((1, seq, heads * HEAD_DIM), lambda b, h, i: (b, 0, h))],
        out_specs=pl.BlockSpec((1, blk, heads * HEAD_DIM), lambda b, h, i: (b, i, h)),
        out_shape=jax.ShapeDtypeStruct((bsz, seq, GROUP_WIDTH), BF16),
        scratch_shapes=[pltpu.VMEM((heads, seq, 2 * HEAD_DIM), BF16),
                        pltpu.VMEM((heads, HEAD_DIM, seq), BF16),
                        pltpu.VMEM((heads, HEAD_DIM, blk), F32)],
        compiler_params=_params(("arbitrary", "arbitrary", "arbitrary")),
        name="mla_attention",
    )(q3, kn3, kr3, vm3)
    return out.reshape(bsz * seq, GROUP_WIDTH)


def _layer_norm(y, g, b):
    mu = jnp.mean(y, axis=-1, keepdims=True)
    yc = y - mu
    var = jnp.mean(yc * yc, axis=-1, keepdims=True)
    return yc * lax.rsqrt(var + LN_EPS) * g + b


def _outproj_kernel(a_ref, m_ref, x_ref, woa_ref, wom_ref, g1_ref, sc2_ref, sh2_ref, g2_ref,
                    lng_ref, lnb_ref, wsg_ref, wsu_ref, wsd_ref, h2_ref, h2p_ref, base_ref):
    f = (jnp.dot(a_ref[...], woa_ref[...], preferred_element_type=F32)
         + jnp.dot(m_ref[...], wom_ref[...], preferred_element_type=F32))
    x1 = _layer_norm(DEEPNORM_ALPHA * x_ref[...] + (1.0 + g1_ref[0]) * f, lng_ref[...], lnb_ref[...])
    h2 = x1 * (1.0 + sc2_ref[0]) + sh2_ref[0]
    h2_ref[...] = h2
    _store_packed_rows(h2p_ref, h2)
    hb = h2.astype(BF16)
    act = (_silu(jnp.dot(hb, wsg_ref[...], preferred_element_type=F32))
           * jnp.dot(hb, wsu_ref[...], preferred_element_type=F32)).astype(BF16)
    shared = jnp.dot(act, wsd_ref[...], preferred_element_type=F32)
    base_ref[...] = DEEPNORM_ALPHA * x1 + (1.0 + g2_ref[0]) * shared


def output_projection(attn_a, attn_m, x2d, w_o, g1, sc2, sh2, g2, ln_g, ln_b,
                      ws_gate, ws_up, ws_down, seq):
    t_tokens, d = x2d.shape
    tm = 256
    tiles_per_seq = seq // tm
    gw = GROUP_WIDTH
    ff = ws_gate.shape[1]

    def const(shape):
        return pl.BlockSpec(shape, lambda i: (0,) * len(shape), pipeline_mode=pl.Buffered(1))

    def rows(width):
        return pl.BlockSpec((tm, width), lambda i: (i, 0))

    mod_spec = pl.BlockSpec((1, 1, d), lambda i: (i // tiles_per_seq, 0, 0))
    return pl.pallas_call(
        _outproj_kernel,
        grid=(t_tokens // tm,),
        in_specs=[rows(gw), rows(gw), rows(d), const((gw, d)), const((gw, d)),
                  mod_spec, mod_spec, mod_spec, mod_spec, const((1, d)), const((1, d)),
                  const((d, ff)), const((d, ff)), const((ff, d))],
        out_specs=[rows(d), pl.BlockSpec((tm * PACK_CHUNKS, PACK_LANES), lambda i: (i, 0)), rows(d)],
        out_shape=[jax.ShapeDtypeStruct((t_tokens, d), F32),
                   jax.ShapeDtypeStruct((t_tokens * PACK_CHUNKS, PACK_LANES), U32),
                   jax.ShapeDtypeStruct((t_tokens, d), F32)],
        compiler_params=_params(("arbitrary",)),
        name="output_projection",
    )(attn_a, attn_m, x2d, w_o[:gw].astype(BF16), w_o[gw:].astype(BF16), g1, sc2, sh2, g2,
      ln_g.reshape(1, d), ln_b.reshape(1, d),
      ws_gate.astype(BF16), ws_up.astype(BF16), ws_down.astype(BF16))


ROUTER_TILE = 512


def _router_kernel(h_ref, wrt_ref, bias_ref, tri_ref, ltri_ref,
                   e_ref, pos_ref, w_ref, cnt_ref, run_ref, msk_ref):
    tr = ROUTER_TILE
    epg = N_EXPERTS // N_GROUPS

    @pl.when(pl.program_id(0) == 0)
    def _():
        run_ref[...] = jnp.zeros(run_ref.shape, F32)

    logits = lax.dot_general(wrt_ref[...], h_ref[...], NT_DIMS, precision=lax.Precision.HIGHEST,
                             preferred_element_type=F32)
    scores = jax.nn.sigmoid(logits)
    sel_scores = scores + bias_ref[...]

    sub = lax.broadcasted_iota(I32, (epg, tr), 0)
    grp_rows = []
    for g in range(N_GROUPS):
        xg = sel_scores[g * epg:(g + 1) * epg, :]
        m1 = jnp.max(xg, axis=0, keepdims=True)
        first = jnp.min(jnp.where(xg == m1, sub, epg), axis=0, keepdims=True)
        m2 = jnp.max(jnp.where(sub == first, -jnp.inf, xg), axis=0, keepdims=True)
        grp_rows.append(m1 + m2)
    grp = jnp.concatenate(grp_rows, axis=0)

    giota = lax.broadcasted_iota(I32, (N_GROUPS, tr), 0)
    gcnt = jnp.zeros((N_GROUPS, tr), F32)
    for m in range(N_GROUPS):
        rm = grp[m:m + 1, :]
        gcnt = gcnt + jnp.where((rm > grp) | ((rm == grp) & (m < giota)), 1.0, 0.0)
    gkeep = jnp.where(gcnt < float(TOPK_GROUPS), 1.0, 0.0)
    gmask = jnp.concatenate(
        [jnp.broadcast_to(gkeep[g:g + 1, :], (epg, tr)) for g in range(N_GROUPS)], axis=0)
    masked = jnp.where(gmask > 0.0, sel_scores, -jnp.inf)
    msk_ref[...] = masked

    eiota = lax.broadcasted_iota(I32, (N_EXPERTS, tr), 0)

    def rank_body(m, cnt):
        rm = msk_ref[pl.ds(m, 1), :]
        return cnt + jnp.where((rm > masked) | ((rm == masked) & (m < eiota)), 1.0, 0.0)

    ecnt = lax.fori_loop(0, N_EXPERTS, rank_body, jnp.zeros((N_EXPERTS, tr), F32))
    sel = ecnt < float(TOP_K)

    sw = jnp.where(sel, scores, 0.0)
    wn = sw / jnp.sum(sw, axis=0, keepdims=True) * ROUTED_SCALE

    selb = jnp.where(sel, 1.0, 0.0).astype(BF16)
    cum = jnp.dot(selb, tri_ref[...], preferred_element_type=F32)
    pos = run_ref[...] + cum - 1.0
    run_ref[...] = run_ref[...] + cum[:, tr - 1:tr]
    cnt_ref[...] = jnp.broadcast_to(run_ref[...], cnt_ref.shape).astype(I32)

    slot = jnp.dot(ltri_ref[...], selb, preferred_element_type=F32)
    ef = eiota.astype(F32)
    e_rows, p_rows, w_rows = [], [], []
    for j in range(TOP_K):
        mj = sel & (slot == float(j))
        e_rows.append(jnp.sum(jnp.where(mj, ef, 0.0), axis=0, keepdims=True))
        p_rows.append(jnp.sum(jnp.where(mj, pos, 0.0), axis=0, keepdims=True))
        w_rows.append(jnp.sum(jnp.where(mj, wn, 0.0), axis=0, keepdims=True))
    e_ref[...] = jnp.concatenate(e_rows, axis=0).astype(I32)
    pos_ref[...] = jnp.concatenate(p_rows, axis=0).astype(I32)
    w_ref[...] = jnp.concatenate(w_rows, axis=0)


def router(h2, w_router, router_bias):
    t_tokens, d = h2.shape
    tr = ROUTER_TILE
    tri = jnp.asarray(np.triu(np.ones((tr, tr), np.float32)), BF16)
    ltri = jnp.asarray(np.tril(np.ones((N_EXPERTS, N_EXPERTS), np.float32), -1), BF16)

    def const(shape):
        return pl.BlockSpec(shape, lambda i: (0,) * len(shape))

    tok = pl.BlockSpec((TOP_K, tr), lambda i: (0, i))
    e_t, pos_t, w_t, cnt = pl.pallas_call(
        _router_kernel,
        grid=(t_tokens // tr,),
        in_specs=[pl.BlockSpec((tr, d), lambda i: (i, 0)), const((N_EXPERTS, d)),
                  const((N_EXPERTS, 1)), const((tr, tr)), const((N_EXPERTS, N_EXPERTS))],
        out_specs=[tok, tok, tok, const((N_EXPERTS, 128))],
        out_shape=[jax.ShapeDtypeStruct((TOP_K, t_tokens), I32),
                   jax.ShapeDtypeStruct((TOP_K, t_tokens), I32),
                   jax.ShapeDtypeStruct((TOP_K, t_tokens), F32),
                   jax.ShapeDtypeStruct((N_EXPERTS, 128), I32)],
        scratch_shapes=[pltpu.VMEM((N_EXPERTS, 1), F32), pltpu.VMEM((N_EXPERTS, tr), F32)],
        compiler_params=_params(("arbitrary",)),
        name="router",
    )(h2, w_router.T, router_bias.reshape(N_EXPERTS, 1), tri, ltri)
    return e_t, pos_t, w_t, cnt[:, 0]


DISPATCH_TILE = 128


def _dispatch_kernel(pad_start_ref, pad_count_ref, nused_ref, d_ref, h_ref, xs_ref, zero_ref, sem):
    tt = DISPATCH_TILE
    pc = PACK_CHUNKS
    rb = MOE_ROW_BLOCK
    n_copies = tt * TOP_K
    n_blocks = xs_ref.shape[0] // (rb * pc)

    @pl.when(pl.program_id(0) == 0)
    def _():
        zero_ref[...] = jnp.zeros(zero_ref.shape, U32)
        nused = nused_ref[0]

        def pad_row(row):
            dst = xs_ref.at[pl.ds(pl.multiple_of(row * pc, pc), pc), :]
            return pltpu.make_async_copy(zero_ref.at[pl.ds(0, pc), :], dst, sem.at[1])

        def tail_block(b):
            dst = xs_ref.at[pl.ds(pl.multiple_of(b * (rb * pc), rb * pc), rb * pc), :]
            return pltpu.make_async_copy(zero_ref, dst, sem.at[1])

        def per_expert(e, total):
            base, n = pad_start_ref[e], pad_count_ref[e]

            def start_row(j, carry):
                pad_row(base + j).start()
                return carry

            lax.fori_loop(0, n, start_row, 0)
            return total + n

        def start_tail(b, carry):
            tail_block(b).start()
            return carry

        def wait_row(j, carry):
            pad_row(0).wait()
            return carry

        def wait_tail(b, carry):
            tail_block(b).wait()
            return carry

        total = lax.fori_loop(0, N_EXPERTS, per_expert, 0)
        lax.fori_loop(nused, n_blocks, start_tail, 0)
        lax.fori_loop(0, total, wait_row, 0)
        lax.fori_loop(nused, n_blocks, wait_tail, 0)

    for j in range(tt):
        for k in range(TOP_K):
            c = j * TOP_K + k
            dst = xs_ref.at[pl.ds(pl.multiple_of(d_ref[c] * pc, pc), pc), :]
            pltpu.make_async_copy(h_ref.at[pl.ds(j * pc, pc), :], dst, sem.at[0]).start(priority=c % 2)
    tile_rows = xs_ref.at[pl.ds(0, n_copies * pc), :]
    pltpu.make_async_copy(tile_rows, tile_rows, sem.at[0]).wait()


def dispatch(h2p, dest_tok_major, pad_start, pad_count, nused, n_rows):
    tt = DISPATCH_TILE
    pc = PACK_CHUNKS
    t_tokens = h2p.shape[0] // pc
    n_copies = tt * TOP_K
    grid_spec = pltpu.PrefetchScalarGridSpec(
        num_scalar_prefetch=3,
        grid=(t_tokens // tt,),
        in_specs=[pl.BlockSpec((n_copies,), lambda i, ps, pn, nu: (i,), memory_space=pltpu.SMEM),
                  pl.BlockSpec((tt * pc, PACK_LANES), lambda i, ps, pn, nu: (i, 0))],
        out_specs=pl.BlockSpec(memory_space=pl.ANY),
        scratch_shapes=[pltpu.VMEM((MOE_ROW_BLOCK * pc, PACK_LANES), U32),
                        pltpu.SemaphoreType.DMA((2,))],
    )
    return pl.pallas_call(
        _dispatch_kernel,
        grid_spec=grid_spec,
        out_shape=jax.ShapeDtypeStruct((n_rows * pc, PACK_LANES), U32),
        compiler_params=_params(("arbitrary",)),
        name="dispatch",
    )(pad_start, pad_count, nused, dest_tok_major, h2p)


def _experts_kernel(be_ref, nused_ref, x_ref, wg_ref, wu_ref, wd_ref, y_ref, wg_bf, wu_bf, wd_bf):
    i = pl.program_id(0)
    nused = nused_ref[0]

    @pl.when(i < nused)
    def _():
        @pl.when((i == 0) | (be_ref[i] != be_ref[jnp.maximum(i - 1, 0)]))
        def _():
            wg_bf[...] = wg_ref[0].astype(BF16)
            wu_bf[...] = wu_ref[0].astype(BF16)
            wd_bf[...] = wd_ref[0].astype(BF16)

        lo, hi = _load_packed_rows(x_ref, 0, MOE_ROW_BLOCK)
        xb = jnp.concatenate([lo.astype(BF16), hi.astype(BF16)], axis=1)
        act = (_silu(jnp.dot(xb, wg_bf[...], preferred_element_type=F32))
               * jnp.dot(xb, wu_bf[...], preferred_element_type=F32)).astype(BF16)
        _store_packed_rows(y_ref, jnp.dot(act, wd_bf[...], preferred_element_type=F32))

    @pl.when(i >= nused)
    def _():
        y_ref[...] = jnp.zeros(y_ref.shape, U32)


def routed_experts(x_sorted, block_e, nused, w_gate, w_up, w_down):
    rb = MOE_ROW_BLOCK
    pc = PACK_CHUNKS
    n_blocks = x_sorted.shape[0] // (rb * pc)
    d, ff = w_gate.shape[1], w_gate.shape[2]
    grid_spec = pltpu.PrefetchScalarGridSpec(
        num_scalar_prefetch=2,
        grid=(n_blocks,),
        in_specs=[pl.BlockSpec((rb * pc, PACK_LANES),
                               lambda i, be, nu: (jnp.minimum(i, jnp.maximum(nu[0] - 1, 0)), 0)),
                  pl.BlockSpec((1, d, ff), lambda i, be, nu: (be[i], 0, 0)),
                  pl.BlockSpec((1, d, ff), lambda i, be, nu: (be[i], 0, 0)),
                  pl.BlockSpec((1, ff, d), lambda i, be, nu: (be[i], 0, 0))],
        out_specs=pl.BlockSpec((rb * pc, PACK_LANES), lambda i, be, nu: (i, 0)),
        scratch_shapes=[pltpu.VMEM((d, ff), BF16), pltpu.VMEM((d, ff), BF16),
                        pltpu.VMEM((ff, d), BF16)],
    )
    return pl.pallas_call(
        _experts_kernel,
        grid_spec=grid_spec,
        out_shape=jax.ShapeDtypeStruct(x_sorted.shape, U32),
        compiler_params=_params(("arbitrary",)),
        name="routed_experts",
    )(block_e, nused, x_sorted, w_gate, w_up, w_down)


COMBINE_TILE = 128


def _combine_kernel(d_cur_ref, d_next_ref, y_ref, w_ref, base_ref, g2_ref, lng_ref, lnb_ref,
                    o_ref, ybuf_a, ybuf_b, sem):
    i = pl.program_id(0)
    n_steps = pl.num_programs(0)
    tc = COMBINE_TILE
    n_copies = TOP_K * tc
    pc = PACK_CHUNKS

    def row_copy(row, buf, r, s):
        src = y_ref.at[pl.ds(pl.multiple_of(row * pc, pc), pc), :]
        return pltpu.make_async_copy(src, buf.at[pl.ds(r * pc, pc), :], sem.at[s])

    def wait_rows(buf, s):
        pltpu.make_async_copy(y_ref.at[pl.ds(0, n_copies * pc), :], buf, sem.at[s]).wait()

    @pl.when(i == 0)
    def _():
        def body(r, carry):
            src = y_ref.at[pl.ds(pl.multiple_of(d_cur_ref[r] * pc, pc), pc), :]
            dst = ybuf_a.at[pl.ds(pl.multiple_of(r * pc, pc), pc), :]
            pltpu.make_async_copy(src, dst, sem.at[0]).start()
            return carry
        lax.fori_loop(0, n_copies, body, 0, unroll=8)

    def step(cur, cur_s, nxt, nxt_s):
        wait_rows(cur, cur_s)
        for r in range(n_copies):
            row_copy(d_next_ref[r], nxt, r, nxt_s).start(priority=r % 2)
        w = w_ref[...]
        lo, hi = _load_packed_rows(cur, 0, tc)
        routed_lo, routed_hi = w[:, 0:1] * lo, w[:, 0:1] * hi
        for k in range(1, TOP_K):
            lo, hi = _load_packed_rows(cur, k * tc, tc)
            routed_lo = routed_lo + w[:, k:k + 1] * lo
            routed_hi = routed_hi + w[:, k:k + 1] * hi
        routed = jnp.concatenate([routed_lo, routed_hi], axis=1)
        y = base_ref[...] + (1.0 + g2_ref[0]) * routed
        o_ref[...] = _layer_norm(y, lng_ref[...], lnb_ref[...])

    @pl.when(i % 2 == 0)
    def _():
        step(ybuf_a, 0, ybuf_b, 1)

    @pl.when(i % 2 == 1)
    def _():
        step(ybuf_b, 1, ybuf_a, 0)

    @pl.when((i == n_steps - 1) & (i % 2 == 0))
    def _():
        wait_rows(ybuf_b, 1)

    @pl.when((i == n_steps - 1) & (i % 2 == 1))
    def _():
        wait_rows(ybuf_a, 0)


def combine(y_sorted, dest_tiles, w_tok, base, g2, ln_g, ln_b, seq):
    t_tokens, d = base.shape
    tc = COMBINE_TILE
    n_steps = t_tokens // tc
    tiles_per_seq = seq // tc
    n_copies = TOP_K * tc

    def const(shape):
        return pl.BlockSpec(shape, lambda i: (0,) * len(shape))

    return pl.pallas_call(
        _combine_kernel,
        grid=(n_steps,),
        in_specs=[pl.BlockSpec((n_copies,), lambda i: (i,), memory_space=pltpu.SMEM),
                  pl.BlockSpec((n_copies,), lambda i: (jnp.minimum(i + 1, n_steps - 1),),
                               memory_space=pltpu.SMEM),
                  pl.BlockSpec(memory_space=pl.ANY),
                  pl.BlockSpec((tc, TOP_K), lambda i: (i, 0)),
                  pl.BlockSpec((tc, d), lambda i: (i, 0)),
                  pl.BlockSpec((1, 1, d), lambda i: (i // tiles_per_seq, 0, 0)),
                  const((1, d)), const((1, d))],
        out_specs=pl.BlockSpec((tc, d), lambda i: (i, 0)),
        out_shape=jax.ShapeDtypeStruct((t_tokens, d), F32),
        scratch_shapes=[pltpu.VMEM((n_copies * PACK_CHUNKS, PACK_LANES), U32),
                        pltpu.VMEM((n_copies * PACK_CHUNKS, PACK_LANES), U32),
                        pltpu.SemaphoreType.DMA((2,))],
        compiler_params=_params(("arbitrary",)),
        name="combine",
    )(dest_tiles, dest_tiles, y_sorted, w_tok, base, g2, ln_g.reshape(1, d), ln_b.reshape(1, d))


def _dispatch_tables(e_t, pos_t, counts, t_tokens):
    rb = MOE_ROW_BLOCK
    n_blocks = t_tokens * TOP_K // rb + N_EXPERTS
    padded = (counts + rb - 1) // rb * rb
    pends = jnp.cumsum(padded)
    pstarts = pends - padded
    experts = jnp.arange(N_EXPERTS, dtype=I32)
    pstart_of = jnp.sum(jnp.where(e_t[..., None] == experts, pstarts.astype(I32), 0), axis=-1)
    dest = (pstart_of + pos_t).astype(I32)
    nused = (pends[-1] // rb).astype(I32)
    blk = jnp.arange(n_blocks, dtype=I32)
    last_blk = jnp.minimum(blk, jnp.maximum(nused - 1, 0))
    block_e = jnp.sum((pends[None, :] <= (last_blk * rb)[:, None]).astype(I32), axis=1)
    block_e = jnp.minimum(block_e, N_EXPERTS - 1).astype(I32)
    pad_start = (pstarts + counts).astype(I32)
    pad_count = (padded - counts).astype(I32)
    return dest, block_e, nused.reshape(1), pad_start, pad_count, n_blocks * rb


def _layer(x, c, w_ada, b_ada, w_in, g_cq, w_uq, g_ckv, w_ukv, w_o, ln1_g, ln1_b,
           w_router, router_bias, w_gate, w_up, w_down, ws_gate, ws_up, ws_down, ln2_g, ln2_b):
    bsz, seq, d = x.shape
    t_tokens = bsz * seq
    x2d = x.reshape(t_tokens, d)

    mod = ada_modulation(c, w_ada, b_ada)
    sh1, sc1, g1, sh2, sc2, g2 = [m.reshape(bsz, 1, d) for m in jnp.split(mod, 6, axis=-1)]

    qa, ka, va, qm, kn, vm, kr = input_projection(x2d, sc1, sh1, w_in, g_cq, w_uq, g_ckv, w_ukv,
                                                  bsz, seq)
    attn_a = moba_attention(qa, ka, va, bsz, seq)
    attn_m = mla_attention(qm, kn, kr, vm, bsz, seq)
    h2, h2p, base = output_projection(attn_a, attn_m, x2d, w_o, g1, sc2, sh2, g2, ln1_g, ln1_b,
                                 ws_gate, ws_up, ws_down, seq)

    e_t, pos_t, w_t, counts = router(h2, w_router, router_bias)
    dest, block_e, nused, pad_start, pad_count, n_rows = _dispatch_tables(e_t, pos_t, counts, t_tokens)
    x_sorted = dispatch(h2p, dest.T.reshape(-1), pad_start, pad_count, nused, n_rows)
    y_sorted = routed_experts(x_sorted, block_e, nused, w_gate, w_up, w_down)

    tc = COMBINE_TILE
    dest_tiles = dest.reshape(TOP_K, t_tokens // tc, tc).transpose(1, 0, 2).reshape(-1)
    out = combine(y_sorted, dest_tiles, w_t.T, base, g2, ln2_g, ln2_b, seq)
    return out.reshape(bsz, seq, d)


def kernel(x, c, w_ada, b_ada, w_in, g_cq, w_uq, g_ckv, w_ukv, w_o, ln1_g, ln1_b, w_router,
           router_bias, w_gate, w_up, w_down, ws_gate, ws_up, ws_down, ln2_g, ln2_b):
    depth = w_ada.shape[0]
    assert depth == 1, "DeepNorm constants are baked for a single layer"
    return _layer(x, c, w_ada[0], b_ada[0], w_in[0], g_cq[0], w_uq[0], g_ckv[0], w_ukv[0], w_o[0],
                  ln1_g[0], ln1_b[0], w_router[0], router_bias[0], w_gate[0], w_up[0], w_down[0],
                  ws_gate[0], ws_up[0], ws_down[0], ln2_g[0], ln2_b[0])
```

```python
import functools

import jax
import jax.numpy as jnp
import numpy as np
from jax import lax
from jax.experimental import pallas as pl
from jax.experimental.pallas import tpu as pltpu

F32 = jnp.float32
BF16 = jnp.bfloat16
I32 = jnp.int32

D_MODEL = 2048
HEAD_DIM = 128
N_HEADS = 8
MOBA_BLOCK = 256
MOBA_TOPK = 3
MLA_ROPE_DIM = 64
MLA_RANK = 512
MLA_QK_DIM = HEAD_DIM + MLA_ROPE_DIM
GROUP_WIDTH = N_HEADS * HEAD_DIM
ROPE_THETA = 10000.0
N_EXPERTS = 64
TOP_K = 8
N_GROUPS = 8
TOPK_GROUPS = 4
EXPERT_FF = 512
ROUTED_SCALE = 2.5
MOE_ROW_BLOCK = 256
LN_EPS = 1e-5
RMS_EPS = 1e-6
DEEPNORM_ALPHA = 2.0 ** 0.25

VMEM_LIMIT_BYTES = 56 * 1024 * 1024

NT_DIMS = (((1,), (1,)), ((), ()))


def _params(semantics):
    return pltpu.CompilerParams(dimension_semantics=semantics,
                                vmem_limit_bytes=VMEM_LIMIT_BYTES)


def _silu(x):
    return x * jax.nn.sigmoid(x)


PACK_CHUNKS = 8
PACK_LANES = 128
U32 = jnp.uint32
HIGH_HALF = np.uint32(0xFFFF0000)


def _store_packed_rows(ref, x):
    n_rows, half = x.shape[0], x.shape[1] // 2
    lo = lax.bitcast_convert_type(x[:, :half].astype(BF16).astype(F32), U32) >> 16
    hi = lax.bitcast_convert_type(x[:, half:].astype(BF16).astype(F32), U32) & HIGH_HALF
    packed = lo | hi
    for j in range(PACK_CHUNKS):
        ref[pl.ds(j, n_rows, stride=PACK_CHUNKS), :] = packed[:, j * PACK_LANES:(j + 1) * PACK_LANES]


def _load_packed_rows(ref, first_row, n_rows):
    lo, hi = [], []
    for j in range(PACK_CHUNKS):
        u = ref[pl.ds(first_row * PACK_CHUNKS + j, n_rows, stride=PACK_CHUNKS), :]
        lo.append(lax.bitcast_convert_type(u << 16, F32))
        hi.append(lax.bitcast_convert_type(u & HIGH_HALF, F32))
    return jnp.concatenate(lo, axis=1), jnp.concatenate(hi, axis=1)


def _ada_kernel(c_ref, w_ref, b_ref, o_ref):
    a = _silu(c_ref[...]).astype(BF16)
    o_ref[...] = jnp.dot(a, w_ref[...].astype(BF16), preferred_element_type=F32) + b_ref[...]


def ada_modulation(c, w_ada, b_ada):
    bsz, d = c.shape
    rows = -(-bsz // 16) * 16
    c_pad = jnp.pad(c, ((0, rows - bsz), (0, 0)))
    n = w_ada.shape[1]
    tn = 1024
    out = pl.pallas_call(
        _ada_kernel,
        grid=(n // tn,),
        in_specs=[pl.BlockSpec((rows, d), lambda j: (0, 0)),
                  pl.BlockSpec((d, tn), lambda j: (0, j)),
                  pl.BlockSpec((1, tn), lambda j: (0, j))],
        out_specs=pl.BlockSpec((rows, tn), lambda j: (0, j)),
        out_shape=jax.ShapeDtypeStruct((rows, n), F32),
        compiler_params=_params(("arbitrary",)),
        name="ada_modulation",
    )(c_pad, w_ada, b_ada.reshape(1, n))
    return out[:bsz]


def _rms(x, g):
    return x * lax.rsqrt(jnp.mean(x * x, axis=-1, keepdims=True) + RMS_EPS) * g


def _rope64(x, cos, sin):
    return x * cos + pltpu.roll(x, 64, axis=1) * sin


def _rope32(x, cos, s1, s2):
    return x * cos + pltpu.roll(x, 96, axis=1) * s1 + pltpu.roll(x, 32, axis=1) * s2


def _inproj_kernel(x_ref, sc_ref, sh_ref, w_ref, wuq_ref, wuk_ref, wuv_ref, gcq_ref, gckv_ref,
                   cosa_ref, sina_ref, cosr_ref, s1_ref, s2_ref,
                   qa_ref, ka_ref, va_ref, qm_ref, kn_ref, vm_ref, kr_ref):
    h = (x_ref[...] * (1.0 + sc_ref[0]) + sh_ref[0]).astype(BF16)
    cosa, sina = cosa_ref[...], sina_ref[...]
    cosr, s1, s2 = cosr_ref[...], s1_ref[...], s2_ref[...]
    gw = GROUP_WIDTH

    def proj(lo, hi):
        return jnp.dot(h, w_ref[:, lo:hi], preferred_element_type=F32)

    for out_ref, base in ((qa_ref, 0), (ka_ref, gw)):
        t = proj(base, base + gw)
        for hd in range(N_HEADS):
            sl = slice(hd * HEAD_DIM, (hd + 1) * HEAD_DIM)
            out_ref[:, sl] = _rope64(t[:, sl], cosa, sina).astype(BF16)
    va_ref[...] = proj(2 * gw, 3 * gw).astype(BF16)

    cq = _rms(proj(3 * gw, 3 * gw + MLA_RANK), gcq_ref[...]).astype(BF16)
    qm = jnp.dot(cq, wuq_ref[...], preferred_element_type=F32)
    for hd in range(N_HEADS):
        b0 = hd * 2 * HEAD_DIM
        qm_ref[:, b0:b0 + HEAD_DIM] = qm[:, b0:b0 + HEAD_DIM].astype(BF16)
        qm_ref[:, b0 + HEAD_DIM:b0 + 2 * HEAD_DIM] = _rope32(
            qm[:, b0 + HEAD_DIM:b0 + 2 * HEAD_DIM], cosr, s1, s2).astype(BF16)

    ckv = _rms(proj(3 * gw + MLA_RANK, 3 * gw + 2 * MLA_RANK), gckv_ref[...]).astype(BF16)
    kn_ref[...] = jnp.dot(ckv, wuk_ref[...], preferred_element_type=F32).astype(BF16)
    vm_ref[...] = jnp.dot(ckv, wuv_ref[...], preferred_element_type=F32).astype(BF16)

    kr = proj(3 * gw + 2 * MLA_RANK, 3 * gw + 2 * MLA_RANK + HEAD_DIM)
    kr_ref[...] = _rope32(kr, cosr, s1, s2).astype(BF16)


def _rope_tables(seq):
    pos = jnp.arange(seq, dtype=F32)[:, None]
    inv_a = 1.0 / (ROPE_THETA ** (jnp.arange(0, HEAD_DIM, 2, dtype=F32) / HEAD_DIM))
    ang_a = pos * inv_a[None, :]
    cosa = jnp.concatenate([jnp.cos(ang_a), jnp.cos(ang_a)], axis=1)
    sina = jnp.concatenate([-jnp.sin(ang_a), jnp.sin(ang_a)], axis=1)
    inv_r = 1.0 / (ROPE_THETA ** (jnp.arange(0, MLA_ROPE_DIM, 2, dtype=F32) / MLA_ROPE_DIM))
    ang_r = pos * inv_r[None, :]
    cr, sr = jnp.cos(ang_r), jnp.sin(ang_r)
    z32 = jnp.zeros_like(cr)
    z64 = jnp.zeros((seq, 64), F32)
    cosr = jnp.concatenate([cr, cr, z64], axis=1)
    s1 = jnp.concatenate([-sr, z32, z64], axis=1)
    s2 = jnp.concatenate([z32, sr, z64], axis=1)
    return cosa, sina, cosr, s1, s2


def input_projection(x2d, sc1, sh1, w_in, g_cq, w_uq, g_ckv, w_ukv, bsz, seq):
    t_tokens, d = x2d.shape
    tm = 256
    tiles_per_seq = seq // tm
    gw = GROUP_WIDTH
    in_w = 3 * gw + 2 * MLA_RANK + HEAD_DIM
    w_cat = jnp.pad(w_in, ((0, 0), (0, in_w - w_in.shape[1]))).astype(BF16)
    wuq = jnp.pad(w_uq.reshape(MLA_RANK, N_HEADS, MLA_QK_DIM),
                  ((0, 0), (0, 0), (0, 2 * HEAD_DIM - MLA_QK_DIM)))
    wuq = wuq.reshape(MLA_RANK, N_HEADS * 2 * HEAD_DIM).astype(BF16)
    wukv = w_ukv.reshape(MLA_RANK, N_HEADS, 2 * HEAD_DIM)
    wuk = wukv[:, :, :HEAD_DIM].reshape(MLA_RANK, gw).astype(BF16)
    wuv = wukv[:, :, HEAD_DIM:].reshape(MLA_RANK, gw).astype(BF16)
    tables = _rope_tables(seq)

    def const(shape):
        return pl.BlockSpec(shape, lambda i: (0,) * len(shape), pipeline_mode=pl.Buffered(1))

    def rows(width):
        return pl.BlockSpec((tm, width), lambda i: (i, 0))

    mod_spec = pl.BlockSpec((1, 1, d), lambda i: (i // tiles_per_seq, 0, 0))
    tab_spec = pl.BlockSpec((tm, HEAD_DIM), lambda i: (i % tiles_per_seq, 0))
    widths = (gw, gw, gw, 2 * gw, gw, gw, HEAD_DIM)
    return pl.pallas_call(
        _inproj_kernel,
        grid=(t_tokens // tm,),
        in_specs=[rows(d), mod_spec, mod_spec, const((d, in_w)),
                  const((MLA_RANK, 2 * gw)), const((MLA_RANK, gw)), const((MLA_RANK, gw)),
                  const((1, MLA_RANK)), const((1, MLA_RANK))] + [tab_spec] * 5,
        out_specs=[rows(w) for w in widths],
        out_shape=[jax.ShapeDtypeStruct((t_tokens, w), BF16) for w in widths],
        compiler_params=_params(("arbitrary",)),
        name="input_projection",
    )(x2d, sc1, sh1, w_cat, wuq, wuk, wuv, g_cq.reshape(1, -1), g_ckv.reshape(1, -1), *tables)


ATTN_BLOCK = 256
LOG2_E = 1.4426950408889634


def _causal_mask_t():
    key = lax.broadcasted_iota(I32, (ATTN_BLOCK, ATTN_BLOCK), 0)
    qry = lax.broadcasted_iota(I32, (ATTN_BLOCK, ATTN_BLOCK), 1)
    return key <= qry


def _scores_t(q, k_blk):
    return lax.dot_general(k_blk, q, NT_DIMS, preferred_element_type=F32)


def _first_block(s, vt_blk, c, acc_ref):
    s = jnp.where(_causal_mask_t(), s, -jnp.inf)
    m = jnp.max(s, axis=0, keepdims=True)
    p = jnp.exp2((s - m) * c)
    acc_ref[...] = jnp.dot(vt_blk, p.astype(BF16), preferred_element_type=F32)
    return m, jnp.sum(p, axis=0, keepdims=True)


def _next_block(s, vt_blk, bias, c, m, l, acc_ref):
    m_blk = jnp.max(s, axis=0, keepdims=True)
    if bias is not None:
        m_blk = m_blk + bias
    m_new = jnp.maximum(m, m_blk)
    alpha = jnp.exp2((m - m_new) * c)
    p = jnp.exp2((s - (m_new if bias is None else m_new - bias)) * c)
    acc_ref[...] = alpha * acc_ref[...] + jnp.dot(vt_blk, p.astype(BF16), preferred_element_type=F32)
    return m_new, alpha * l + jnp.sum(p, axis=0, keepdims=True)


HEADS_PER_STEP = 8


def _head_cols(g, width=HEAD_DIM):
    return slice(g * width, (g + 1) * width)


def _moba_block_bias(kmean, q, qi, n_blocks):
    gate = lax.dot_general(kmean, q.astype(F32), NT_DIMS, precision=lax.Precision.HIGHEST,
                           preferred_element_type=F32)
    row = lax.broadcasted_iota(I32, gate.shape, 0)
    gt = jnp.where(row < qi, gate, -jnp.inf)
    cnt = jnp.zeros(gate.shape, F32)
    for m in range(n_blocks - 1):
        gm = gt[m:m + 1, :]
        cnt = cnt + jnp.where((gm > gt) | ((gm == gt) & (m < row)), 1.0, 0.0)
    return jnp.where(cnt < float(MOBA_TOPK), 0.0, -jnp.inf).astype(F32)


def _moba_kernel(q_ref, k_ref, v_ref, o_ref, kmean_ref, vt_ref, acc_ref, *, n_blocks):
    qi = pl.program_id(2)
    blk = MOBA_BLOCK
    heads = HEADS_PER_STEP

    @pl.when(qi == 0)
    def _():
        kmean_ref[...] = jnp.zeros(kmean_ref.shape, F32)
        for g in range(heads):
            vt_ref[g] = v_ref[0, :, _head_cols(g)].astype(F32).T.astype(BF16)
            for n in range(n_blocks):
                kb = k_ref[0, n * blk:(n + 1) * blk, _head_cols(g)].astype(F32)
                kmean_ref[g, n:n + 1, :] = jnp.sum(kb, axis=0, keepdims=True) * (1.0 / blk)

    c = (HEAD_DIM ** -0.5) * LOG2_E
    own = pl.multiple_of(qi * blk, blk)
    qs = [q_ref[0, :, _head_cols(g)] for g in range(heads)]
    scores = [_scores_t(qs[g], k_ref[0, pl.ds(own, blk), _head_cols(g)]) for g in range(heads)]
    biases = [_moba_block_bias(kmean_ref[g], qs[g], qi, n_blocks) for g in range(heads)]
    carry0 = []
    for g in range(heads):
        carry0 += _first_block(scores[g], vt_ref[g, :, pl.ds(own, blk)], c, acc_ref.at[g])

    def body(n, carry):
        off = pl.multiple_of(n * blk, blk)
        scores = [_scores_t(qs[g], k_ref[0, pl.ds(off, blk), _head_cols(g)]) for g in range(heads)]
        row = lax.broadcasted_iota(I32, biases[0].shape, 0)
        out = []
        for g in range(heads):
            bias = jnp.sum(jnp.where(row == n, biases[g], 0.0), axis=0, keepdims=True)
            out += _next_block(scores[g], vt_ref[g, :, pl.ds(off, blk)], bias,
                               c, carry[2 * g], carry[2 * g + 1], acc_ref.at[g])
        return tuple(out)

    final = lax.fori_loop(0, qi, body, tuple(carry0))
    for g in range(heads):
        o_ref[0, :, _head_cols(g)] = (acc_ref[g] / final[2 * g + 1]).T.astype(o_ref.dtype)


def moba_attention(qa, ka, va, bsz, seq):
    blk = MOBA_BLOCK
    n_blocks = seq // blk
    heads = HEADS_PER_STEP
    gate_rows = -(-n_blocks // 8) * 8
    q3, k3, v3 = (t.reshape(bsz, seq, GROUP_WIDTH) for t in (qa, ka, va))
    q_spec = pl.BlockSpec((1, blk, heads * HEAD_DIM), lambda b, h, i: (b, i, h))
    kv_spec = pl.BlockSpec((1, seq, heads * HEAD_DIM), lambda b, h, i: (b, 0, h))
    out = pl.pallas_call(
        functools.partial(_moba_kernel, n_blocks=n_blocks),
        grid=(bsz, N_HEADS // heads, n_blocks),
        in_specs=[q_spec, kv_spec, kv_spec],
        out_specs=q_spec,
        out_shape=jax.ShapeDtypeStruct((bsz, seq, GROUP_WIDTH), BF16),
        scratch_shapes=[pltpu.VMEM((heads, gate_rows, HEAD_DIM), F32),
                        pltpu.VMEM((heads, HEAD_DIM, seq), BF16),
                        pltpu.VMEM((heads, HEAD_DIM, blk), F32)],
        compiler_params=_params(("arbitrary", "arbitrary", "arbitrary")),
        name="moba_attention",
    )(q3, k3, v3)
    return out.reshape(bsz * seq, GROUP_WIDTH)


def _mla_kernel(q_ref, kn_ref, kr_ref, v_ref, o_ref, kcat_ref, vt_ref, acc_ref):
    qi = pl.program_id(2)
    blk = ATTN_BLOCK

    heads = HEADS_PER_STEP

    @pl.when(qi == 0)
    def _():
        for g in range(heads):
            vt_ref[g] = v_ref[0, :, _head_cols(g)].astype(F32).T.astype(BF16)
            kcat_ref[g, :, :HEAD_DIM] = kn_ref[0, :, _head_cols(g)]
            kcat_ref[g, :, HEAD_DIM:] = kr_ref[0]

    c = (MLA_QK_DIM ** -0.5) * LOG2_E
    own = pl.multiple_of(qi * blk, blk)
    qs = [q_ref[0, :, _head_cols(g, 2 * HEAD_DIM)] for g in range(heads)]
    scores = [_scores_t(qs[g], kcat_ref[g, pl.ds(own, blk), :]) for g in range(heads)]
    carry0 = []
    for g in range(heads):
        carry0 += _first_block(scores[g], vt_ref[g, :, pl.ds(own, blk)], c, acc_ref.at[g])

    def body(n, carry):
        off = pl.multiple_of(n * blk, blk)
        scores = [_scores_t(qs[g], kcat_ref[g, pl.ds(off, blk), :]) for g in range(heads)]
        out = []
        for g in range(heads):
            out += _next_block(scores[g], vt_ref[g, :, pl.ds(off, blk)], None, c,
                               carry[2 * g], carry[2 * g + 1], acc_ref.at[g])
        return tuple(out)

    final = lax.fori_loop(0, qi, body, tuple(carry0))
    for g in range(heads):
        o_ref[0, :, _head_cols(g)] = (acc_ref[g] / final[2 * g + 1]).T.astype(o_ref.dtype)


def mla_attention(qm, kn, kr, vm, bsz, seq):
    blk = ATTN_BLOCK
    heads = HEADS_PER_STEP
    q3 = qm.reshape(bsz, seq, 2 * GROUP_WIDTH)
    kn3, vm3 = kn.reshape(bsz, seq, GROUP_WIDTH), vm.reshape(bsz, seq, GROUP_WIDTH)
    kr3 = kr.reshape(bsz, seq, HEAD_DIM)
    out = pl.pallas_call(
        _mla_kernel,
        grid=(bsz, N_HEADS // heads, seq // blk),
        in_specs=[pl.BlockSpec((1, blk, heads * 2 * HEAD_DIM), lambda b, h, i: (b, i, h)),
                  pl.BlockSpec((1, seq, heads * HEAD_DIM), lambda b, h, i: (b, 0, h)),
                  pl.BlockSpec((1, seq, HEAD_DIM), lambda b, h, i: (b, 0, 0)),
                  pl.BlockSpec((1, seq, heads * HEAD_DIM), lambda b, h, i: (b, 0, h))],
        out_specs=pl.BlockSpec((1, blk, heads * HEAD_DIM), lambda b, h, i: (b, i, h)),
        out_shape=jax.ShapeDtypeStruct((bsz, seq, GROUP_WIDTH), BF16),
        scratch_shapes=[pltpu.VMEM((heads, seq, 2 * HEAD_DIM), BF16),
                        pltpu.VMEM((heads, HEAD_DIM, seq), BF16),
                        pltpu.VMEM((heads, HEAD_DIM, blk), F32)],
        compiler_params=_params(("arbitrary", "arbitrary", "arbitrary")),
        name="mla_attention",
    )(q3, kn3, kr3, vm3)
    return out.reshape(bsz * seq, GROUP_WIDTH)


def _layer_norm(y, g, b):
    mu = jnp.mean(y, axis=-1, keepdims=True)
    yc = y - mu
    var = jnp.mean(yc * yc, axis=-1, keepdims=True)
    return yc * lax.rsqrt(var + LN_EPS) * g + b


def _outproj_kernel(a_ref, m_ref, x_ref, woa_ref, wom_ref, g1_ref, sc2_ref, sh2_ref, g2_ref,
                    lng_ref, lnb_ref, wsg_ref, wsu_ref, wsd_ref, h2_ref, h2p_ref, base_ref):
    f = (jnp.dot(a_ref[...], woa_ref[...], preferred_element_type=F32)
         + jnp.dot(m_ref[...], wom_ref[...], preferred_element_type=F32))
    x1 = _layer_norm(DEEPNORM_ALPHA * x_ref[...] + (1.0 + g1_ref[0]) * f, lng_ref[...], lnb_ref[...])
    h2 = x1 * (1.0 + sc2_ref[0]) + sh2_ref[0]
    h2_ref[...] = h2
    _store_packed_rows(h2p_ref, h2)
    hb = h2.astype(BF16)
    act = (_silu(jnp.dot(hb, wsg_ref[...], preferred_element_type=F32))
           * jnp.dot(hb, wsu_ref[...], preferred_element_type=F32)).astype(BF16)
    shared = jnp.dot(act, wsd_ref[...], preferred_element_type=F32)
    base_ref[...] = DEEPNORM_ALPHA * x1 + (1.0 + g2_ref[0]) * shared


def output_projection(attn_a, attn_m, x2d, w_o, g1, sc2, sh2, g2, ln_g, ln_b,
                      ws_gate, ws_up, ws_down, seq):
    t_tokens, d = x2d.shape
    tm = 256
    tiles_per_seq = seq // tm
    gw = GROUP_WIDTH
    ff = ws_gate.shape[1]

    def const(shape):
        return pl.BlockSpec(shape, lambda i: (0,) * len(shape), pipeline_mode=pl.Buffered(1))

    def rows(width):
        return pl.BlockSpec((tm, width), lambda i: (i, 0))

    mod_spec = pl.BlockSpec((1, 1, d), lambda i: (i // tiles_per_seq, 0, 0))
    return pl.pallas_call(
        _outproj_kernel,
        grid=(t_tokens // tm,),
        in_specs=[rows(gw), rows(gw), rows(d), const((gw, d)), const((gw, d)),
                  mod_spec, mod_spec, mod_spec, mod_spec, const((1, d)), const((1, d)),
                  const((d, ff)), const((d, ff)), const((ff, d))],
        out_specs=[rows(d), pl.BlockSpec((tm * PACK_CHUNKS, PACK_LANES), lambda i: (i, 0)), rows(d)],
        out_shape=[jax.ShapeDtypeStruct((t_tokens, d), F32),
                   jax.ShapeDtypeStruct((t_tokens * PACK_CHUNKS, PACK_LANES), U32),
                   jax.ShapeDtypeStruct((t_tokens, d), F32)],
        compiler_params=_params(("arbitrary",)),
        name="output_projection",
    )(attn_a, attn_m, x2d, w_o[:gw].astype(BF16), w_o[gw:].astype(BF16), g1, sc2, sh2, g2,
      ln_g.reshape(1, d), ln_b.reshape(1, d),
      ws_gate.astype(BF16), ws_up.astype(BF16), ws_down.astype(BF16))


ROUTER_TILE = 512


def _router_kernel(h_ref, wrt_ref, bias_ref, tri_ref, ltri_ref,
                   e_ref, pos_ref, w_ref, cnt_ref, run_ref, msk_ref):
    tr = ROUTER_TILE
    epg = N_EXPERTS // N_GROUPS

    @pl.when(pl.program_id(0) == 0)
    def _():
        run_ref[...] = jnp.zeros(run_ref.shape, F32)

    logits = lax.dot_general(wrt_ref[...], h_ref[...], NT_DIMS, precision=lax.Precision.HIGHEST,
                             preferred_element_type=F32)
    scores = jax.nn.sigmoid(logits)
    sel_scores = scores + bias_ref[...]

    sub = lax.broadcasted_iota(I32, (epg, tr), 0)
    grp_rows = []
    for g in range(N_GROUPS):
        xg = sel_scores[g * epg:(g + 1) * epg, :]
        m1 = jnp.max(xg, axis=0, keepdims=True)
        first = jnp.min(jnp.where(xg == m1, sub, epg), axis=0, keepdims=True)
        m2 = jnp.max(jnp.where(sub == first, -jnp.inf, xg), axis=0, keepdims=True)
        grp_rows.append(m1 + m2)
    grp = jnp.concatenate(grp_rows, axis=0)

    giota = lax.broadcasted_iota(I32, (N_GROUPS, tr), 0)
    gcnt = jnp.zeros((N_GROUPS, tr), F32)
    for m in range(N_GROUPS):
        rm = grp[m:m + 1, :]
        gcnt = gcnt + jnp.where((rm > grp) | ((rm == grp) & (m < giota)), 1.0, 0.0)
    gkeep = jnp.where(gcnt < float(TOPK_GROUPS), 1.0, 0.0)
    gmask = jnp.concatenate(
        [jnp.broadcast_to(gkeep[g:g + 1, :], (epg, tr)) for g in range(N_GROUPS)], axis=0)
    masked = jnp.where(gmask > 0.0, sel_scores, -jnp.inf)
    msk_ref[...] = masked

    eiota = lax.broadcasted_iota(I32, (N_EXPERTS, tr), 0)

    def rank_body(m, cnt):
        rm = msk_ref[pl.ds(m, 1), :]
        return cnt + jnp.where((rm > masked) | ((rm == masked) & (m < eiota)), 1.0, 0.0)

    ecnt = lax.fori_loop(0, N_EXPERTS, rank_body, jnp.zeros((N_EXPERTS, tr), F32))
    sel = ecnt < float(TOP_K)

    sw = jnp.where(sel, scores, 0.0)
    wn = sw / jnp.sum(sw, axis=0, keepdims=True) * ROUTED_SCALE

    selb = jnp.where(sel, 1.0, 0.0).astype(BF16)
    cum = jnp.dot(selb, tri_ref[...], preferred_element_type=F32)
    pos = run_ref[...] + cum - 1.0
    run_ref[...] = run_ref[...] + cum[:, tr - 1:tr]
    cnt_ref[...] = jnp.broadcast_to(run_ref[...], cnt_ref.shape).astype(I32)

    slot = jnp.dot(ltri_ref[...], selb, preferred_element_type=F32)
    ef = eiota.astype(F32)
    e_rows, p_rows, w_rows = [], [], []
    for j in range(TOP_K):
        mj = sel & (slot == float(j))
        e_rows.append(jnp.sum(jnp.where(mj, ef, 0.0), axis=0, keepdims=True))
        p_rows.append(jnp.sum(jnp.where(mj, pos, 0.0), axis=0, keepdims=True))
        w_rows.append(jnp.sum(jnp.where(mj, wn, 0.0), axis=0, keepdims=True))
    e_ref[...] = jnp.concatenate(e_rows, axis=0).astype(I32)
    pos_ref[...] = jnp.concatenate(p_rows, axis=0).astype(I32)
    w_ref[...] = jnp.concatenate(w_rows, axis=0)


def router(h2, w_router, router_bias):
    t_tokens, d = h2.shape
    tr = ROUTER_TILE
    tri = jnp.asarray(np.triu(np.ones((tr, tr), np.float32)), BF16)
    ltri = jnp.asarray(np.tril(np.ones((N_EXPERTS, N_EXPERTS), np.float32), -1), BF16)

    def const(shape):
        return pl.BlockSpec(shape, lambda i: (0,) * len(shape))

    tok = pl.BlockSpec((TOP_K, tr), lambda i: (0, i))
    e_t, pos_t, w_t, cnt = pl.pallas_call(
        _router_kernel,
        grid=(t_tokens // tr,),
        in_specs=[pl.BlockSpec((tr, d), lambda i: (i, 0)), const((N_EXPERTS, d)),
                  const((N_EXPERTS, 1)), const((tr, tr)), const((N_EXPERTS, N_EXPERTS))],
        out_specs=[tok, tok, tok, const((N_EXPERTS, 128))],
        out_shape=[jax.ShapeDtypeStruct((TOP_K, t_tokens), I32),
                   jax.ShapeDtypeStruct((TOP_K, t_tokens), I32),
                   jax.ShapeDtypeStruct((TOP_K, t_tokens), F32),
                   jax.ShapeDtypeStruct((N_EXPERTS, 128), I32)],
        scratch_shapes=[pltpu.VMEM((N_EXPERTS, 1), F32), pltpu.VMEM((N_EXPERTS, tr), F32)],
        compiler_params=_params(("arbitrary",)),
        name="router",
    )(h2, w_router.T, router_bias.reshape(N_EXPERTS, 1), tri, ltri)
    return e_t, pos_t, w_t, cnt[:, 0]


DISPATCH_TILE = 128


def _dispatch_kernel(pad_start_ref, pad_count_ref, nused_ref, d_ref, h_ref, xs_ref, zero_ref, sem):
    tt = DISPATCH_TILE
    pc = PACK_CHUNKS
    rb = MOE_ROW_BLOCK
    n_copies = tt * TOP_K
    n_blocks = xs_ref.shape[0] // (rb * pc)

    @pl.when(pl.program_id(0) == 0)
    def _():
        zero_ref[...] = jnp.zeros(zero_ref.shape, U32)
        nused = nused_ref[0]

        def pad_rows(row, n):
            dst = xs_ref.at[pl.ds(pl.multiple_of(row * pc, pc), n * pc), :]
            return pltpu.make_async_copy(zero_ref.at[pl.ds(0, n * pc), :], dst, sem.at[1])

        def tail_block(b):
            dst = xs_ref.at[pl.ds(pl.multiple_of(b * (rb * pc), rb * pc), rb * pc), :]
            return pltpu.make_async_copy(zero_ref, dst, sem.at[1])

        def per_expert(e, carry, wait):
            base, n = pad_start_ref[e], pad_count_ref[e]
            for bit in reversed(range(rb.bit_length() - 1)):
                size = 1 << bit
                offset = (n >> (bit + 1)) << (bit + 1)

                @pl.when(((n >> bit) & 1) == 1)
                def _():
                    copy = pad_rows(base + offset, size)
                    copy.wait() if wait else copy.start()
            return carry

        def start_tail(b, carry):
            tail_block(b).start()
            return carry

        def wait_tail(b, carry):
            tail_block(b).wait()
            return carry

        lax.fori_loop(0, N_EXPERTS, functools.partial(per_expert, wait=False), 0)
        lax.fori_loop(nused, n_blocks, start_tail, 0)
        lax.fori_loop(0, N_EXPERTS, functools.partial(per_expert, wait=True), 0)
        lax.fori_loop(nused, n_blocks, wait_tail, 0)

    for j in range(tt):
        for k in range(TOP_K):
            c = j * TOP_K + k
            dst = xs_ref.at[pl.ds(pl.multiple_of(d_ref[c] * pc, pc), pc), :]
            pltpu.make_async_copy(h_ref.at[pl.ds(j * pc, pc), :], dst, sem.at[0]).start(priority=c % 2)
    tile_rows = xs_ref.at[pl.ds(0, n_copies * pc), :]
    pltpu.make_async_copy(tile_rows, tile_rows, sem.at[0]).wait()


def dispatch(h2p, dest_tok_major, pad_start, pad_count, nused, n_rows):
    tt = DISPATCH_TILE
    pc = PACK_CHUNKS
    t_tokens = h2p.shape[0] // pc
    n_copies = tt * TOP_K
    grid_spec = pltpu.PrefetchScalarGridSpec(
        num_scalar_prefetch=3,
        grid=(t_tokens // tt,),
        in_specs=[pl.BlockSpec((n_copies,), lambda i, ps, pn, nu: (i,), memory_space=pltpu.SMEM),
                  pl.BlockSpec((tt * pc, PACK_LANES), lambda i, ps, pn, nu: (i, 0))],
        out_specs=pl.BlockSpec(memory_space=pl.ANY),
        scratch_shapes=[pltpu.VMEM((MOE_ROW_BLOCK * pc, PACK_LANES), U32),
                        pltpu.SemaphoreType.DMA((2,))],
    )
    return pl.pallas_call(
        _dispatch_kernel,
        grid_spec=grid_spec,
        out_shape=jax.ShapeDtypeStruct((n_rows * pc, PACK_LANES), U32),
        compiler_params=_params(("arbitrary",)),
        name="dispatch",
    )(pad_start, pad_count, nused, dest_tok_major, h2p)


def _experts_kernel(be_ref, nused_ref, ord_ref, next_ref, x_ref, wg_hbm, wu_hbm, wd_hbm, y_ref,
                    wg_stage, wu_stage, wd_stage, wg_bf, wu_bf, wd_bf, sem):
    i = pl.program_id(0)
    nused = nused_ref[0]

    def weight_copies(e, slot):
        return (pltpu.make_async_copy(wg_hbm.at[e], wg_stage.at[slot], sem.at[slot]),
                pltpu.make_async_copy(wu_hbm.at[e], wu_stage.at[slot], sem.at[slot]),
                pltpu.make_async_copy(wd_hbm.at[e], wd_stage.at[slot], sem.at[slot]))

    @pl.when(i < nused)
    def _():
        @pl.when((i == 0) | (be_ref[i] != be_ref[jnp.maximum(i - 1, 0)]))
        def _():
            e = be_ref[i]
            slot = ord_ref[i] % 2

            @pl.when(i == 0)
            def _():
                for c in weight_copies(e, slot):
                    c.start()

            for c in weight_copies(e, slot):
                c.wait()
            wg_bf[...] = wg_stage[slot].astype(BF16)
            wu_bf[...] = wu_stage[slot].astype(BF16)
            wd_bf[...] = wd_stage[slot].astype(BF16)

            @pl.when(next_ref[i] < N_EXPERTS)
            def _():
                for c in weight_copies(next_ref[i], 1 - slot):
                    c.start()

        lo, hi = _load_packed_rows(x_ref, 0, MOE_ROW_BLOCK)
        xb = jnp.concatenate([lo.astype(BF16), hi.astype(BF16)], axis=1)
        act = (_silu(jnp.dot(xb, wg_bf[...], preferred_element_type=F32))
               * jnp.dot(xb, wu_bf[...], preferred_element_type=F32)).astype(BF16)
        _store_packed_rows(y_ref, jnp.dot(act, wd_bf[...], preferred_element_type=F32))

    @pl.when(i >= nused)
    def _():
        y_ref[...] = jnp.zeros(y_ref.shape, U32)


def routed_experts(x_sorted, block_e, nused, block_ord, block_next, w_gate, w_up, w_down):
    rb = MOE_ROW_BLOCK
    pc = PACK_CHUNKS
    n_blocks = x_sorted.shape[0] // (rb * pc)
    d, ff = w_gate.shape[1], w_gate.shape[2]
    hbm = pl.BlockSpec(memory_space=pl.ANY)
    grid_spec = pltpu.PrefetchScalarGridSpec(
        num_scalar_prefetch=4,
        grid=(n_blocks,),
        in_specs=[pl.BlockSpec((rb * pc, PACK_LANES),
                               lambda i, be, nu, od, nx: (jnp.minimum(i, jnp.maximum(nu[0] - 1, 0)), 0)),
                  hbm, hbm, hbm],
        out_specs=pl.BlockSpec((rb * pc, PACK_LANES), lambda i, be, nu, od, nx: (i, 0)),
        scratch_shapes=[pltpu.VMEM((2, d, ff), F32), pltpu.VMEM((2, d, ff), F32),
                        pltpu.VMEM((2, ff, d), F32),
                        pltpu.VMEM((d, ff), BF16), pltpu.VMEM((d, ff), BF16),
                        pltpu.VMEM((ff, d), BF16), pltpu.SemaphoreType.DMA((2,))],
    )
    return pl.pallas_call(
        _experts_kernel,
        grid_spec=grid_spec,
        out_shape=jax.ShapeDtypeStruct(x_sorted.shape, U32),
        compiler_params=_params(("arbitrary",)),
        name="routed_experts",
    )(block_e, nused, block_ord, block_next, x_sorted, w_gate, w_up, w_down)


COMBINE_TILE = 128


def _combine_kernel(d_cur_ref, d_next_ref, y_ref, w_ref, base_ref, g2_ref, lng_ref, lnb_ref,
                    o_ref, ybuf_a, ybuf_b, sem):
    i = pl.program_id(0)
    n_steps = pl.num_programs(0)
    tc = COMBINE_TILE
    n_copies = TOP_K * tc
    pc = PACK_CHUNKS

    def row_copy(row, buf, r, s):
        src = y_ref.at[pl.ds(pl.multiple_of(row * pc, pc), pc), :]
        return pltpu.make_async_copy(src, buf.at[pl.ds(r * pc, pc), :], sem.at[s])

    def wait_rows(buf, s):
        pltpu.make_async_copy(y_ref.at[pl.ds(0, n_copies * pc), :], buf, sem.at[s]).wait()

    @pl.when(i == 0)
    def _():
        def body(r, carry):
            src = y_ref.at[pl.ds(pl.multiple_of(d_cur_ref[r] * pc, pc), pc), :]
            dst = ybuf_a.at[pl.ds(pl.multiple_of(r * pc, pc), pc), :]
            pltpu.make_async_copy(src, dst, sem.at[0]).start()
            return carry
        lax.fori_loop(0, n_copies, body, 0, unroll=8)

    def step(cur, cur_s, nxt, nxt_s):
        wait_rows(cur, cur_s)
        for r in range(n_copies):
            row_copy(d_next_ref[r], nxt, r, nxt_s).start(priority=r % 2)
        w = w_ref[...]
        lo, hi = _load_packed_rows(cur, 0, tc)
        routed_lo, routed_hi = w[:, 0:1] * lo, w[:, 0:1] * hi
        for k in range(1, TOP_K):
            lo, hi = _load_packed_rows(cur, k * tc, tc)
            routed_lo = routed_lo + w[:, k:k + 1] * lo
            routed_hi = routed_hi + w[:, k:k + 1] * hi
        routed = jnp.concatenate([routed_lo, routed_hi], axis=1)
        y = base_ref[...] + (1.0 + g2_ref[0]) * routed
        o_ref[...] = _layer_norm(y, lng_ref[...], lnb_ref[...])

    @pl.when(i % 2 == 0)
    def _():
        step(ybuf_a, 0, ybuf_b, 1)

    @pl.when(i % 2 == 1)
    def _():
        step(ybuf_b, 1, ybuf_a, 0)

    @pl.when((i == n_steps - 1) & (i % 2 == 0))
    def _():
        wait_rows(ybuf_b, 1)

    @pl.when((i == n_steps - 1) & (i % 2 == 1))
    def _():
        wait_rows(ybuf_a, 0)


def combine(y_sorted, dest_tiles, w_tok, base, g2, ln_g, ln_b, seq):
    t_tokens, d = base.shape
    tc = COMBINE_TILE
    n_steps = t_tokens // tc
    tiles_per_seq = seq // tc
    n_copies = TOP_K * tc

    def const(shape):
        return pl.BlockSpec(shape, lambda i: (0,) * len(shape))

    return pl.pallas_call(
        _combine_kernel,
        grid=(n_steps,),
        in_specs=[pl.BlockSpec((n_copies,), lambda i: (i,), memory_space=pltpu.SMEM),
                  pl.BlockSpec((n_copies,), lambda i: (jnp.minimum(i + 1, n_steps - 1),),
                               memory_space=pltpu.SMEM),
                  pl.BlockSpec(memory_space=pl.ANY),
                  pl.BlockSpec((tc, TOP_K), lambda i: (i, 0)),
                  pl.BlockSpec((tc, d), lambda i: (i, 0)),
                  pl.BlockSpec((1, 1, d), lambda i: (i // tiles_per_seq, 0, 0)),
                  const((1, d)), const((1, d))],
        out_specs=pl.BlockSpec((tc, d), lambda i: (i, 0)),
        out_shape=jax.ShapeDtypeStruct((t_tokens, d), F32),
        scratch_shapes=[pltpu.VMEM((n_copies * PACK_CHUNKS, PACK_LANES), U32),
                        pltpu.VMEM((n_copies * PACK_CHUNKS, PACK_LANES), U32),
                        pltpu.SemaphoreType.DMA((2,))],
        compiler_params=_params(("arbitrary",)),
        name="combine",
    )(dest_tiles, dest_tiles, y_sorted, w_tok, base, g2, ln_g.reshape(1, d), ln_b.reshape(1, d))


def _dispatch_tables(e_t, pos_t, counts, t_tokens):
    rb = MOE_ROW_BLOCK
    n_blocks = t_tokens * TOP_K // rb + N_EXPERTS
    padded = (counts + rb - 1) // rb * rb
    pends = jnp.cumsum(padded)
    pstarts = pends - padded
    experts = jnp.arange(N_EXPERTS, dtype=I32)
    pstart_of = jnp.sum(jnp.where(e_t[..., None] == experts, pstarts.astype(I32), 0), axis=-1)
    dest = (pstart_of + pos_t).astype(I32)
    nused = (pends[-1] // rb).astype(I32)
    blk = jnp.arange(n_blocks, dtype=I32)
    last_blk = jnp.minimum(blk, jnp.maximum(nused - 1, 0))
    block_e = jnp.sum((pends[None, :] <= (last_blk * rb)[:, None]).astype(I32), axis=1)
    block_e = jnp.minimum(block_e, N_EXPERTS - 1).astype(I32)
    first = (blk < nused) & ((blk == 0) | (block_e != jnp.roll(block_e, 1)))
    block_ord = (jnp.cumsum(first.astype(I32)) - 1).astype(I32)
    later = (experts[None, :] > experts[:, None]) & (padded[None, :] > 0)
    next_of = jnp.min(jnp.where(later, experts[None, :], N_EXPERTS), axis=1)
    block_next = jnp.sum(jnp.where(block_e[:, None] == experts[None, :], next_of[None, :], 0),
                         axis=1).astype(I32)
    pad_start = (pstarts + counts).astype(I32)
    pad_count = (padded - counts).astype(I32)
    return (dest, block_e, nused.reshape(1), block_ord, block_next, pad_start, pad_count,
            n_blocks * rb)


def _layer(x, c, w_ada, b_ada, w_in, g_cq, w_uq, g_ckv, w_ukv, w_o, ln1_g, ln1_b,
           w_router, router_bias, w_gate, w_up, w_down, ws_gate, ws_up, ws_down, ln2_g, ln2_b):
    bsz, seq, d = x.shape
    t_tokens = bsz * seq
    x2d = x.reshape(t_tokens, d)

    mod = ada_modulation(c, w_ada, b_ada)
    sh1, sc1, g1, sh2, sc2, g2 = [m.reshape(bsz, 1, d) for m in jnp.split(mod, 6, axis=-1)]

    qa, ka, va, qm, kn, vm, kr = input_projection(x2d, sc1, sh1, w_in, g_cq, w_uq, g_ckv, w_ukv,
                                                  bsz, seq)
    attn_a = moba_attention(qa, ka, va, bsz, seq)
    attn_m = mla_attention(qm, kn, kr, vm, bsz, seq)
    h2, h2p, base = output_projection(attn_a, attn_m, x2d, w_o, g1, sc2, sh2, g2, ln1_g, ln1_b,
                                 ws_gate, ws_up, ws_down, seq)

    e_t, pos_t, w_t, counts = router(h2, w_router, router_bias)
    (dest, block_e, nused, block_ord, block_next, pad_start, pad_count,
     n_rows) = _dispatch_tables(e_t, pos_t, counts, t_tokens)
    x_sorted = dispatch(h2p, dest.T.reshape(-1), pad_start, pad_count, nused, n_rows)
    y_sorted = routed_experts(x_sorted, block_e, nused, block_ord, block_next, w_gate, w_up, w_down)

    tc = COMBINE_TILE
    dest_tiles = dest.reshape(TOP_K, t_tokens // tc, tc).transpose(1, 0, 2).reshape(-1)
    out = combine(y_sorted, dest_tiles, w_t.T, base, g2, ln2_g, ln2_b, seq)
    return out.reshape(bsz, seq, d)


def kernel(x, c, w_ada, b_ada, w_in, g_cq, w_uq, g_ckv, w_ukv, w_o, ln1_g, ln1_b, w_router,
           router_bias, w_gate, w_up, w_down, ws_gate, ws_up, ws_down, ln2_g, ln2_b):
    depth = w_ada.shape[0]
    assert depth == 1, "DeepNorm constants are baked for a single layer"
    return _layer(x, c, w_ada[0], b_ada[0], w_in[0], g_cq[0], w_uq[0], g_ckv[0], w_ukv[0], w_o[0],
                  ln1_g[0], ln1_b[0], w_router[0], router_bias[0], w_gate[0], w_up[0], w_down[0],
                  ws_gate[0], ws_up[0], ws_down[0], ln2_g[0], ln2_b[0])
```

```python
import functools

import jax
import jax.numpy as jnp
import numpy as np
from jax import lax
from jax.experimental import pallas as pl
from jax.experimental.pallas import tpu as pltpu

F32 = jnp.float32
BF16 = jnp.bfloat16
I32 = jnp.int32

D_MODEL = 2048
HEAD_DIM = 128
N_HEADS = 8
MOBA_BLOCK = 256
MOBA_TOPK = 3
MLA_ROPE_DIM = 64
MLA_RANK = 512
MLA_QK_DIM = HEAD_DIM + MLA_ROPE_DIM
GROUP_WIDTH = N_HEADS * HEAD_DIM
ROPE_THETA = 10000.0
N_EXPERTS = 64
TOP_K = 8
N_GROUPS = 8
TOPK_GROUPS = 4
EXPERT_FF = 512
ROUTED_SCALE = 2.5
MOE_ROW_BLOCK = 256
LN_EPS = 1e-5
RMS_EPS = 1e-6
DEEPNORM_ALPHA = 2.0 ** 0.25

VMEM_LIMIT_BYTES = 56 * 1024 * 1024

NT_DIMS = (((1,), (1,)), ((), ()))


def _params(semantics):
    return pltpu.CompilerParams(dimension_semantics=semantics,
                                vmem_limit_bytes=VMEM_LIMIT_BYTES)


def _silu(x):
    return x * jax.nn.sigmoid(x)


PACK_CHUNKS = 8
PACK_LANES = 128
U32 = jnp.uint32
HIGH_HALF = np.uint32(0xFFFF0000)


def _store_packed_rows(ref, x):
    n_rows, half = x.shape[0], x.shape[1] // 2
    lo = lax.bitcast_convert_type(x[:, :half].astype(BF16).astype(F32), U32) >> 16
    hi = lax.bitcast_convert_type(x[:, half:].astype(BF16).astype(F32), U32) & HIGH_HALF
    packed = lo | hi
    for j in range(PACK_CHUNKS):
        ref[pl.ds(j, n_rows, stride=PACK_CHUNKS), :] = packed[:, j * PACK_LANES:(j + 1) * PACK_LANES]


def _load_packed_rows(ref, first_row, n_rows):
    lo, hi = [], []
    for j in range(PACK_CHUNKS):
        u = ref[pl.ds(first_row * PACK_CHUNKS + j, n_rows, stride=PACK_CHUNKS), :]
        lo.append(lax.bitcast_convert_type(u << 16, F32))
        hi.append(lax.bitcast_convert_type(u & HIGH_HALF, F32))
    return jnp.concatenate(lo, axis=1), jnp.concatenate(hi, axis=1)


def _ada_kernel(c_ref, w_ref, b_ref, o_ref):
    a = _silu(c_ref[...]).astype(BF16)
    o_ref[...] = jnp.dot(a, w_ref[...].astype(BF16), preferred_element_type=F32) + b_ref[...]


def ada_modulation(c, w_ada, b_ada):
    bsz, d = c.shape
    rows = -(-bsz // 16) * 16
    c_pad = jnp.pad(c, ((0, rows - bsz), (0, 0)))
    n = w_ada.shape[1]
    tn = 1024
    out = pl.pallas_call(
        _ada_kernel,
        grid=(n // tn,),
        in_specs=[pl.BlockSpec((rows, d), lambda j: (0, 0)),
                  pl.BlockSpec((d, tn), lambda j: (0, j)),
                  pl.BlockSpec((1, tn), lambda j: (0, j))],
        out_specs=pl.BlockSpec((rows, tn), lambda j: (0, j)),
        out_shape=jax.ShapeDtypeStruct((rows, n), F32),
        compiler_params=_params(("arbitrary",)),
        name="ada_modulation",
    )(c_pad, w_ada, b_ada.reshape(1, n))
    return out[:bsz]


def _rms(x, g):
    return x * lax.rsqrt(jnp.mean(x * x, axis=-1, keepdims=True) + RMS_EPS) * g


def _rope64(x, cos, sin):
    return x * cos + pltpu.roll(x, 64, axis=1) * sin


def _rope32(x, cos, s1, s2):
    return x * cos + pltpu.roll(x, 96, axis=1) * s1 + pltpu.roll(x, 32, axis=1) * s2


def _inproj_kernel(x_ref, sc_ref, sh_ref, w_ref, wkr_ref, wuq_ref, wuk_ref, wuv_ref, gcq_ref, gckv_ref,
                   cosa_ref, sina_ref, cosr_ref, s1_ref, s2_ref,
                   qa_ref, ka_ref, va_ref, qm_ref, kn_ref, vm_ref, kr_ref):
    h = (x_ref[...] * (1.0 + sc_ref[0]) + sh_ref[0]).astype(BF16)
    cosa, sina = cosa_ref[...], sina_ref[...]
    cosr, s1, s2 = cosr_ref[...], s1_ref[...], s2_ref[...]
    gw = GROUP_WIDTH

    def proj(lo, hi):
        return jnp.dot(h, w_ref[:, lo:hi], preferred_element_type=F32)

    for out_ref, base in ((qa_ref, 0), (ka_ref, gw)):
        t = proj(base, base + gw)
        for hd in range(N_HEADS):
            sl = slice(hd * HEAD_DIM, (hd + 1) * HEAD_DIM)
            out_ref[:, sl] = _rope64(t[:, sl], cosa, sina).astype(BF16)
    va_ref[...] = proj(2 * gw, 3 * gw).astype(BF16)

    cq = _rms(proj(3 * gw, 3 * gw + MLA_RANK), gcq_ref[...]).astype(BF16)
    qm = jnp.dot(cq, wuq_ref[...], preferred_element_type=F32)
    for hd in range(N_HEADS):
        b0 = hd * 2 * HEAD_DIM
        qm_ref[:, b0:b0 + HEAD_DIM] = qm[:, b0:b0 + HEAD_DIM].astype(BF16)
        qm_ref[:, b0 + HEAD_DIM:b0 + 2 * HEAD_DIM] = _rope32(
            qm[:, b0 + HEAD_DIM:b0 + 2 * HEAD_DIM], cosr, s1, s2).astype(BF16)

    ckv = _rms(proj(3 * gw + MLA_RANK, 3 * gw + 2 * MLA_RANK), gckv_ref[...]).astype(BF16)
    kn_ref[...] = jnp.dot(ckv, wuk_ref[...], preferred_element_type=F32).astype(BF16)
    vm_ref[...] = jnp.dot(ckv, wuv_ref[...], preferred_element_type=F32).astype(BF16)

    kr = jnp.dot(h, wkr_ref[...], preferred_element_type=F32)
    kr_ref[...] = _rope32(kr, cosr, s1, s2).astype(BF16)


def _rope_tables(seq):
    pos = jnp.arange(seq, dtype=F32)[:, None]
    inv_a = 1.0 / (ROPE_THETA ** (jnp.arange(0, HEAD_DIM, 2, dtype=F32) / HEAD_DIM))
    ang_a = pos * inv_a[None, :]
    cosa = jnp.concatenate([jnp.cos(ang_a), jnp.cos(ang_a)], axis=1)
    sina = jnp.concatenate([-jnp.sin(ang_a), jnp.sin(ang_a)], axis=1)
    inv_r = 1.0 / (ROPE_THETA ** (jnp.arange(0, MLA_ROPE_DIM, 2, dtype=F32) / MLA_ROPE_DIM))
    ang_r = pos * inv_r[None, :]
    cr, sr = jnp.cos(ang_r), jnp.sin(ang_r)
    z32 = jnp.zeros_like(cr)
    z64 = jnp.zeros((seq, 64), F32)
    cosr = jnp.concatenate([cr, cr, z64], axis=1)
    s1 = jnp.concatenate([-sr, z32, z64], axis=1)
    s2 = jnp.concatenate([z32, sr, z64], axis=1)
    return cosa, sina, cosr, s1, s2


def input_projection(x2d, sc1, sh1, w_in, g_cq, w_uq, g_ckv, w_ukv, bsz, seq):
    t_tokens, d = x2d.shape
    tm = 256
    tiles_per_seq = seq // tm
    gw = GROUP_WIDTH
    main_w = 3 * gw + 2 * MLA_RANK
    w_main = w_in[:, :main_w].astype(BF16)
    w_kr = jnp.pad(w_in[:, main_w:], ((0, 0), (0, HEAD_DIM - MLA_ROPE_DIM))).astype(BF16)
    wuq = jnp.pad(w_uq.reshape(MLA_RANK, N_HEADS, MLA_QK_DIM),
                  ((0, 0), (0, 0), (0, 2 * HEAD_DIM - MLA_QK_DIM)))
    wuq = wuq.reshape(MLA_RANK, N_HEADS * 2 * HEAD_DIM).astype(BF16)
    wukv = w_ukv.reshape(MLA_RANK, N_HEADS, 2 * HEAD_DIM)
    wuk = wukv[:, :, :HEAD_DIM].reshape(MLA_RANK, gw).astype(BF16)
    wuv = wukv[:, :, HEAD_DIM:].reshape(MLA_RANK, gw).astype(BF16)
    tables = _rope_tables(seq)

    def const(shape):
        return pl.BlockSpec(shape, lambda i: (0,) * len(shape), pipeline_mode=pl.Buffered(1))

    def rows(width):
        return pl.BlockSpec((tm, width), lambda i: (i, 0))

    mod_spec = pl.BlockSpec((1, 1, d), lambda i: (i // tiles_per_seq, 0, 0))
    tab_spec = pl.BlockSpec((tm, HEAD_DIM), lambda i: (i % tiles_per_seq, 0))
    widths = (gw, gw, gw, 2 * gw, gw, gw, HEAD_DIM)
    return pl.pallas_call(
        _inproj_kernel,
        grid=(t_tokens // tm,),
        in_specs=[rows(d), mod_spec, mod_spec, const((d, main_w)), const((d, HEAD_DIM)),
                  const((MLA_RANK, 2 * gw)), const((MLA_RANK, gw)), const((MLA_RANK, gw)),
                  const((1, MLA_RANK)), const((1, MLA_RANK))] + [tab_spec] * 5,
        out_specs=[rows(w) for w in widths],
        out_shape=[jax.ShapeDtypeStruct((t_tokens, w), BF16) for w in widths],
        compiler_params=_params(("arbitrary",)),
        name="input_projection",
    )(x2d, sc1, sh1, w_main, w_kr, wuq, wuk, wuv, g_cq.reshape(1, -1), g_ckv.reshape(1, -1), *tables)


ATTN_BLOCK = 256
LOG2_E = 1.4426950408889634


def _causal_mask_t():
    key = lax.broadcasted_iota(I32, (ATTN_BLOCK, ATTN_BLOCK), 0)
    qry = lax.broadcasted_iota(I32, (ATTN_BLOCK, ATTN_BLOCK), 1)
    return key <= qry


def _scores_t(q, k_blk):
    return lax.dot_general(k_blk, q, NT_DIMS, preferred_element_type=F32)


def _first_block(s, vt_blk, c, acc_ref):
    s = jnp.where(_causal_mask_t(), s, -jnp.inf)
    m = jnp.max(s, axis=0, keepdims=True)
    p = jnp.exp2((s - m) * c)
    acc_ref[...] = jnp.dot(vt_blk, p.astype(BF16), preferred_element_type=F32)
    return m, jnp.sum(p, axis=0, keepdims=True)


def _next_block(s, vt_blk, bias, c, m, l, acc_ref):
    m_blk = jnp.max(s, axis=0, keepdims=True)
    if bias is not None:
        m_blk = m_blk + bias
    m_new = jnp.maximum(m, m_blk)
    alpha = jnp.exp2((m - m_new) * c)
    p = jnp.exp2((s - (m_new if bias is None else m_new - bias)) * c)
    acc_ref[...] = alpha * acc_ref[...] + jnp.dot(vt_blk, p.astype(BF16), preferred_element_type=F32)
    return m_new, alpha * l + jnp.sum(p, axis=0, keepdims=True)


HEADS_PER_STEP = 8


def _head_cols(g, width=HEAD_DIM):
    return slice(g * width, (g + 1) * width)


def _moba_block_bias(kmean, q, qi, n_blocks):
    gate = lax.dot_general(kmean, q.astype(F32), NT_DIMS, precision=lax.Precision.HIGHEST,
                           preferred_element_type=F32)
    row = lax.broadcasted_iota(I32, gate.shape, 0)
    gt = jnp.where(row < qi, gate, -jnp.inf)
    cnt = jnp.zeros(gate.shape, F32)
    for m in range(n_blocks - 1):
        gm = gt[m:m + 1, :]
        cnt = cnt + jnp.where((gm > gt) | ((gm == gt) & (m < row)), 1.0, 0.0)
    return jnp.where(cnt < float(MOBA_TOPK), 0.0, -jnp.inf).astype(F32)


def _moba_kernel(q_ref, k_ref, v_ref, o_ref, kmean_ref, vt_ref, acc_ref, *, n_blocks):
    qi = pl.program_id(2)
    blk = MOBA_BLOCK
    heads = HEADS_PER_STEP

    @pl.when(qi == 0)
    def _():
        kmean_ref[...] = jnp.zeros(kmean_ref.shape, F32)
        for g in range(heads):
            vt_ref[g] = v_ref[0, :, _head_cols(g)].astype(F32).T.astype(BF16)
            for n in range(n_blocks):
                kb = k_ref[0, n * blk:(n + 1) * blk, _head_cols(g)].astype(F32)
                kmean_ref[g, n:n + 1, :] = jnp.sum(kb, axis=0, keepdims=True) * (1.0 / blk)

    c = (HEAD_DIM ** -0.5) * LOG2_E
    own = pl.multiple_of(qi * blk, blk)
    qs = [q_ref[0, :, _head_cols(g)] for g in range(heads)]
    scores = [_scores_t(qs[g], k_ref[0, pl.ds(own, blk), _head_cols(g)]) for g in range(heads)]
    biases = [_moba_block_bias(kmean_ref[g], qs[g], qi, n_blocks) for g in range(heads)]
    carry0 = []
    for g in range(heads):
        carry0 += _first_block(scores[g], vt_ref[g, :, pl.ds(own, blk)], c, acc_ref.at[g])

    def body(n, carry):
        off = pl.multiple_of(n * blk, blk)
        scores = [_scores_t(qs[g], k_ref[0, pl.ds(off, blk), _head_cols(g)]) for g in range(heads)]
        row = lax.broadcasted_iota(I32, biases[0].shape, 0)
        out = []
        for g in range(heads):
            bias = jnp.sum(jnp.where(row == n, biases[g], 0.0), axis=0, keepdims=True)
            out += _next_block(scores[g], vt_ref[g, :, pl.ds(off, blk)], bias,
                               c, carry[2 * g], carry[2 * g + 1], acc_ref.at[g])
        return tuple(out)

    final = lax.fori_loop(0, qi, body, tuple(carry0))
    for g in range(heads):
        o_ref[0, :, _head_cols(g)] = (acc_ref[g] / final[2 * g + 1]).T.astype(o_ref.dtype)


def moba_attention(qa, ka, va, bsz, seq):
    blk = MOBA_BLOCK
    n_blocks = seq // blk
    heads = HEADS_PER_STEP
    gate_rows = -(-n_blocks // 8) * 8
    q3, k3, v3 = (t.reshape(bsz, seq, GROUP_WIDTH) for t in (qa, ka, va))
    q_spec = pl.BlockSpec((1, blk, heads * HEAD_DIM), lambda b, h, i: (b, i, h))
    kv_spec = pl.BlockSpec((1, seq, heads * HEAD_DIM), lambda b, h, i: (b, 0, h))
    out = pl.pallas_call(
        functools.partial(_moba_kernel, n_blocks=n_blocks),
        grid=(bsz, N_HEADS // heads, n_blocks),
        in_specs=[q_spec, kv_spec, kv_spec],
        out_specs=q_spec,
        out_shape=jax.ShapeDtypeStruct((bsz, seq, GROUP_WIDTH), BF16),
        scratch_shapes=[pltpu.VMEM((heads, gate_rows, HEAD_DIM), F32),
                        pltpu.VMEM((heads, HEAD_DIM, seq), BF16),
                        pltpu.VMEM((heads, HEAD_DIM, blk), F32)],
        compiler_params=_params(("arbitrary", "arbitrary", "arbitrary")),
        name="moba_attention",
    )(q3, k3, v3)
    return out.reshape(bsz * seq, GROUP_WIDTH)


def _mla_kernel(q_ref, kn_ref, kr_ref, v_ref, o_ref, kcat_ref, vt_ref, acc_ref):
    qi = pl.program_id(2)
    blk = ATTN_BLOCK

    heads = HEADS_PER_STEP

    @pl.when(qi == 0)
    def _():
        for g in range(heads):
            vt_ref[g] = v_ref[0, :, _head_cols(g)].astype(F32).T.astype(BF16)
            kcat_ref[g, :, :HEAD_DIM] = kn_ref[0, :, _head_cols(g)]
            kcat_ref[g, :, HEAD_DIM:] = kr_ref[0]

    c = (MLA_QK_DIM ** -0.5) * LOG2_E
    own = pl.multiple_of(qi * blk, blk)
    qs = [q_ref[0, :, _head_cols(g, 2 * HEAD_DIM)] for g in range(heads)]
    scores = [_scores_t(qs[g], kcat_ref[g, pl.ds(own, blk), :]) for g in range(heads)]
    carry0 = []
    for g in range(heads):
        carry0 += _first_block(scores[g], vt_ref[g, :, pl.ds(own, blk)], c, acc_ref.at[g])

    def body(n, carry):
        off = pl.multiple_of(n * blk, blk)
        scores = [_scores_t(qs[g], kcat_ref[g, pl.ds(off, blk), :]) for g in range(heads)]
        out = []
        for g in range(heads):
            out += _next_block(scores[g], vt_ref[g, :, pl.ds(off, blk)], None, c,
                               carry[2 * g], carry[2 * g + 1], acc_ref.at[g])
        return tuple(out)

    final = lax.fori_loop(0, qi, body, tuple(carry0))
    for g in range(heads):
        o_ref[0, :, _head_cols(g)] = (acc_ref[g] / final[2 * g + 1]).T.astype(o_ref.dtype)


def mla_attention(qm, kn, kr, vm, bsz, seq):
    blk = ATTN_BLOCK
    heads = HEADS_PER_STEP
    q3 = qm.reshape(bsz, seq, 2 * GROUP_WIDTH)
    kn3, vm3 = kn.reshape(bsz, seq, GROUP_WIDTH), vm.reshape(bsz, seq, GROUP_WIDTH)
    kr3 = kr.reshape(bsz, seq, HEAD_DIM)
    out = pl.pallas_call(
        _mla_kernel,
        grid=(bsz, N_HEADS // heads, seq // blk),
        in_specs=[pl.BlockSpec((1, blk, heads * 2 * HEAD_DIM), lambda b, h, i: (b, i, h)),
                  pl.BlockSpec((1, seq, heads * HEAD_DIM), lambda b, h, i: (b, 0, h)),
                  pl.BlockSpec((1, seq, HEAD_DIM), lambda b, h, i: (b, 0, 0)),
                  pl.BlockSpec((1, seq, heads * HEAD_DIM), lambda b, h, i: (b, 0, h))],
        out_specs=pl.BlockSpec((1, blk, heads * HEAD_DIM), lambda b, h, i: (b, i, h)),
        out_shape=jax.ShapeDtypeStruct((bsz, seq, GROUP_WIDTH), BF16),
        scratch_shapes=[pltpu.VMEM((heads, seq, 2 * HEAD_DIM), BF16),
                        pltpu.VMEM((heads, HEAD_DIM, seq), BF16),
                        pltpu.VMEM((heads, HEAD_DIM, blk), F32)],
        compiler_params=_params(("arbitrary", "arbitrary", "arbitrary")),
        name="mla_attention",
    )(q3, kn3, kr3, vm3)
    return out.reshape(bsz * seq, GROUP_WIDTH)


def _layer_norm(y, g, b):
    mu = jnp.mean(y, axis=-1, keepdims=True)
    yc = y - mu
    var = jnp.mean(yc * yc, axis=-1, keepdims=True)
    return yc * lax.rsqrt(var + LN_EPS) * g + b


def _outproj_kernel(a_ref, m_ref, x_ref, woa_ref, wom_ref, g1_ref, sc2_ref, sh2_ref, g2_ref,
                    lng_ref, lnb_ref, wsg_ref, wsu_ref, wsd_ref, h2_ref, h2p_ref, base_ref):
    f = (jnp.dot(a_ref[...], woa_ref[...], preferred_element_type=F32)
         + jnp.dot(m_ref[...], wom_ref[...], preferred_element_type=F32))
    x1 = _layer_norm(DEEPNORM_ALPHA * x_ref[...] + (1.0 + g1_ref[0]) * f, lng_ref[...], lnb_ref[...])
    h2 = x1 * (1.0 + sc2_ref[0]) + sh2_ref[0]
    h2_ref[...] = h2
    _store_packed_rows(h2p_ref, h2)
    hb = h2.astype(BF16)
    act = (_silu(jnp.dot(hb, wsg_ref[...], preferred_element_type=F32))
           * jnp.dot(hb, wsu_ref[...], preferred_element_type=F32)).astype(BF16)
    shared = jnp.dot(act, wsd_ref[...], preferred_element_type=F32)
    base_ref[...] = DEEPNORM_ALPHA * x1 + (1.0 + g2_ref[0]) * shared


def output_projection(attn_a, attn_m, x2d, w_o, g1, sc2, sh2, g2, ln_g, ln_b,
                      ws_gate, ws_up, ws_down, seq):
    t_tokens, d = x2d.shape
    tm = 256
    tiles_per_seq = seq // tm
    gw = GROUP_WIDTH
    ff = ws_gate.shape[1]

    def const(shape):
        return pl.BlockSpec(shape, lambda i: (0,) * len(shape), pipeline_mode=pl.Buffered(1))

    def rows(width):
        return pl.BlockSpec((tm, width), lambda i: (i, 0))

    mod_spec = pl.BlockSpec((1, 1, d), lambda i: (i // tiles_per_seq, 0, 0))
    return pl.pallas_call(
        _outproj_kernel,
        grid=(t_tokens // tm,),
        in_specs=[rows(gw), rows(gw), rows(d), const((gw, d)), const((gw, d)),
                  mod_spec, mod_spec, mod_spec, mod_spec, const((1, d)), const((1, d)),
                  const((d, ff)), const((d, ff)), const((ff, d))],
        out_specs=[rows(d), pl.BlockSpec((tm * PACK_CHUNKS, PACK_LANES), lambda i: (i, 0)), rows(d)],
        out_shape=[jax.ShapeDtypeStruct((t_tokens, d), F32),
                   jax.ShapeDtypeStruct((t_tokens * PACK_CHUNKS, PACK_LANES), U32),
                   jax.ShapeDtypeStruct((t_tokens, d), F32)],
        compiler_params=_params(("arbitrary",)),
        name="output_projection",
    )(attn_a, attn_m, x2d, w_o[:gw].astype(BF16), w_o[gw:].astype(BF16), g1, sc2, sh2, g2,
      ln_g.reshape(1, d), ln_b.reshape(1, d),
      ws_gate.astype(BF16), ws_up.astype(BF16), ws_down.astype(BF16))


ROUTER_TILE = 512


def _router_kernel(h_ref, wrt_ref, bias_ref, tri_ref, ltri_ref,
                   e_ref, pos_ref, w_ref, cnt_ref, run_ref):
    tr = ROUTER_TILE
    epg = N_EXPERTS // N_GROUPS

    @pl.when(pl.program_id(0) == 0)
    def _():
        run_ref[...] = jnp.zeros(run_ref.shape, F32)

    logits = lax.dot_general(wrt_ref[...], h_ref[...], NT_DIMS, precision=lax.Precision.HIGHEST,
                             preferred_element_type=F32)
    scores = jax.nn.sigmoid(logits)
    sel_scores = scores + bias_ref[...]

    sub = lax.broadcasted_iota(I32, (epg, tr), 0)
    grp_rows = []
    for g in range(N_GROUPS):
        xg = sel_scores[g * epg:(g + 1) * epg, :]
        m1 = jnp.max(xg, axis=0, keepdims=True)
        first = jnp.min(jnp.where(xg == m1, sub, epg), axis=0, keepdims=True)
        m2 = jnp.max(jnp.where(sub == first, -jnp.inf, xg), axis=0, keepdims=True)
        grp_rows.append(m1 + m2)
    grp = jnp.concatenate(grp_rows, axis=0)

    giota = lax.broadcasted_iota(I32, (N_GROUPS, tr), 0)
    gcnt = jnp.zeros((N_GROUPS, tr), F32)
    for m in range(N_GROUPS):
        rm = grp[m:m + 1, :]
        gcnt = gcnt + jnp.where((rm > grp) | ((rm == grp) & (m < giota)), 1.0, 0.0)
    gkeep = jnp.where(gcnt < float(TOPK_GROUPS), 1.0, 0.0)
    gmask = jnp.concatenate(
        [jnp.broadcast_to(gkeep[g:g + 1, :], (epg, tr)) for g in range(N_GROUPS)], axis=0)
    masked = jnp.where(gmask > 0.0, sel_scores, -jnp.inf)

    eiota = lax.broadcasted_iota(I32, (N_EXPERTS, tr), 0)

    free = jnp.ones((N_EXPERTS, tr), F32)
    for _ in range(TOP_K):
        top = jnp.max(jnp.where(free > 0.0, masked, -jnp.inf), axis=0, keepdims=True)
        first = jnp.min(jnp.where((masked == top) & (free > 0.0), eiota, N_EXPERTS),
                        axis=0, keepdims=True)
        free = jnp.where(eiota == first, 0.0, free)
    sel = free == 0.0

    sw = jnp.where(sel, scores, 0.0)
    wn = sw / jnp.sum(sw, axis=0, keepdims=True) * ROUTED_SCALE

    selb = jnp.where(sel, 1.0, 0.0).astype(BF16)
    cum = jnp.dot(selb, tri_ref[...], preferred_element_type=F32)
    pos = run_ref[...] + cum - 1.0
    run_ref[...] = run_ref[...] + cum[:, tr - 1:tr]
    cnt_ref[...] = jnp.broadcast_to(run_ref[...], cnt_ref.shape).astype(I32)

    slot = jnp.dot(ltri_ref[...], selb, preferred_element_type=F32)
    ef = eiota.astype(F32)
    e_rows, p_rows, w_rows = [], [], []
    for j in range(TOP_K):
        mj = sel & (slot == float(j))
        e_rows.append(jnp.sum(jnp.where(mj, ef, 0.0), axis=0, keepdims=True))
        p_rows.append(jnp.sum(jnp.where(mj, pos, 0.0), axis=0, keepdims=True))
        w_rows.append(jnp.sum(jnp.where(mj, wn, 0.0), axis=0, keepdims=True))
    e_ref[...] = jnp.concatenate(e_rows, axis=0).astype(I32)
    pos_ref[...] = jnp.concatenate(p_rows, axis=0).astype(I32)
    w_ref[...] = jnp.concatenate(w_rows, axis=0)


def router(h2, w_router, router_bias):
    t_tokens, d = h2.shape
    tr = ROUTER_TILE
    tri = jnp.asarray(np.triu(np.ones((tr, tr), np.float32)), BF16)
    ltri = jnp.asarray(np.tril(np.ones((N_EXPERTS, N_EXPERTS), np.float32), -1), BF16)

    def const(shape):
        return pl.BlockSpec(shape, lambda i: (0,) * len(shape))

    tok = pl.BlockSpec((TOP_K, tr), lambda i: (0, i))
    e_t, pos_t, w_t, cnt = pl.pallas_call(
        _router_kernel,
        grid=(t_tokens // tr,),
        in_specs=[pl.BlockSpec((tr, d), lambda i: (i, 0)), const((N_EXPERTS, d)),
                  const((N_EXPERTS, 1)), const((tr, tr)), const((N_EXPERTS, N_EXPERTS))],
        out_specs=[tok, tok, tok, const((N_EXPERTS, 128))],
        out_shape=[jax.ShapeDtypeStruct((TOP_K, t_tokens), I32),
                   jax.ShapeDtypeStruct((TOP_K, t_tokens), I32),
                   jax.ShapeDtypeStruct((TOP_K, t_tokens), F32),
                   jax.ShapeDtypeStruct((N_EXPERTS, 128), I32)],
        scratch_shapes=[pltpu.VMEM((N_EXPERTS, 1), F32)],
        compiler_params=_params(("arbitrary",)),
        name="router",
    )(h2, w_router.T, router_bias.reshape(N_EXPERTS, 1), tri, ltri)
    return e_t, pos_t, w_t, cnt[:, 0]


DISPATCH_TILE = 128


def _dispatch_kernel(pad_start_ref, pad_count_ref, nused_ref, d_ref, h_ref, xs_ref, zero_ref, sem):
    tt = DISPATCH_TILE
    pc = PACK_CHUNKS
    rb = MOE_ROW_BLOCK
    n_copies = tt * TOP_K
    n_blocks = xs_ref.shape[0] // (rb * pc)

    @pl.when(pl.program_id(0) == 0)
    def _():
        zero_ref[...] = jnp.zeros(zero_ref.shape, U32)
        nused = nused_ref[0]

        def pad_rows(row, n):
            dst = xs_ref.at[pl.ds(pl.multiple_of(row * pc, pc), n * pc), :]
            return pltpu.make_async_copy(zero_ref.at[pl.ds(0, n * pc), :], dst, sem.at[1])

        def tail_block(b):
            dst = xs_ref.at[pl.ds(pl.multiple_of(b * (rb * pc), rb * pc), rb * pc), :]
            return pltpu.make_async_copy(zero_ref, dst, sem.at[1])

        def per_expert(e, carry, wait):
            base, n = pad_start_ref[e], pad_count_ref[e]
            for bit in reversed(range(rb.bit_length() - 1)):
                size = 1 << bit
                offset = (n >> (bit + 1)) << (bit + 1)

                @pl.when(((n >> bit) & 1) == 1)
                def _():
                    copy = pad_rows(base + offset, size)
                    copy.wait() if wait else copy.start()
            return carry

        def start_tail(b, carry):
            tail_block(b).start()
            return carry

        def wait_tail(b, carry):
            tail_block(b).wait()
            return carry

        lax.fori_loop(0, N_EXPERTS, functools.partial(per_expert, wait=False), 0)
        lax.fori_loop(nused, n_blocks, start_tail, 0)
        lax.fori_loop(0, N_EXPERTS, functools.partial(per_expert, wait=True), 0)
        lax.fori_loop(nused, n_blocks, wait_tail, 0)

    for j in range(tt):
        for k in range(TOP_K):
            c = j * TOP_K + k
            dst = xs_ref.at[pl.ds(pl.multiple_of(d_ref[c] * pc, pc), pc), :]
            pltpu.make_async_copy(h_ref.at[pl.ds(j * pc, pc), :], dst, sem.at[0]).start(priority=c % 2)
    tile_rows = xs_ref.at[pl.ds(0, n_copies * pc), :]
    pltpu.make_async_copy(tile_rows, tile_rows, sem.at[0]).wait()


def dispatch(h2p, dest_tok_major, pad_start, pad_count, nused, n_rows):
    tt = DISPATCH_TILE
    pc = PACK_CHUNKS
    t_tokens = h2p.shape[0] // pc
    n_copies = tt * TOP_K
    grid_spec = pltpu.PrefetchScalarGridSpec(
        num_scalar_prefetch=3,
        grid=(t_tokens // tt,),
        in_specs=[pl.BlockSpec((n_copies,), lambda i, ps, pn, nu: (i,), memory_space=pltpu.SMEM),
                  pl.BlockSpec((tt * pc, PACK_LANES), lambda i, ps, pn, nu: (i, 0))],
        out_specs=pl.BlockSpec(memory_space=pl.ANY),
        scratch_shapes=[pltpu.VMEM((MOE_ROW_BLOCK * pc, PACK_LANES), U32),
                        pltpu.SemaphoreType.DMA((2,))],
    )
    return pl.pallas_call(
        _dispatch_kernel,
        grid_spec=grid_spec,
        out_shape=jax.ShapeDtypeStruct((n_rows * pc, PACK_LANES), U32),
        compiler_params=_params(("arbitrary",)),
        name="dispatch",
    )(pad_start, pad_count, nused, dest_tok_major, h2p)


def _experts_kernel(be_ref, nused_ref, ord_ref, next_ref, x_ref, wg_hbm, wu_hbm, wd_hbm, y_ref,
                    wg_stage, wu_stage, wd_stage, wg_bf, wu_bf, wd_bf, sem):
    i = pl.program_id(0)
    nused = nused_ref[0]

    def weight_copies(e, slot):
        return (pltpu.make_async_copy(wg_hbm.at[e], wg_stage.at[slot], sem.at[slot]),
                pltpu.make_async_copy(wu_hbm.at[e], wu_stage.at[slot], sem.at[slot]),
                pltpu.make_async_copy(wd_hbm.at[e], wd_stage.at[slot], sem.at[slot]))

    @pl.when(i < nused)
    def _():
        @pl.when((i == 0) | (be_ref[i] != be_ref[jnp.maximum(i - 1, 0)]))
        def _():
            e = be_ref[i]
            slot = ord_ref[i] % 2

            @pl.when(i == 0)
            def _():
                for c in weight_copies(e, slot):
                    c.start()

            for c in weight_copies(e, slot):
                c.wait()
            wg_bf[...] = wg_stage[slot].astype(BF16)
            wu_bf[...] = wu_stage[slot].astype(BF16)
            wd_bf[...] = wd_stage[slot].astype(BF16)

            @pl.when(next_ref[i] < N_EXPERTS)
            def _():
                for c in weight_copies(next_ref[i], 1 - slot):
                    c.start()

        lo, hi = _load_packed_rows(x_ref, 0, MOE_ROW_BLOCK)
        xb = jnp.concatenate([lo.astype(BF16), hi.astype(BF16)], axis=1)
        act = (_silu(jnp.dot(xb, wg_bf[...], preferred_element_type=F32))
               * jnp.dot(xb, wu_bf[...], preferred_element_type=F32)).astype(BF16)
        _store_packed_rows(y_ref, jnp.dot(act, wd_bf[...], preferred_element_type=F32))

    @pl.when(i >= nused)
    def _():
        y_ref[...] = jnp.zeros(y_ref.shape, U32)


def routed_experts(x_sorted, block_e, nused, block_ord, block_next, w_gate, w_up, w_down):
    rb = MOE_ROW_BLOCK
    pc = PACK_CHUNKS
    n_blocks = x_sorted.shape[0] // (rb * pc)
    d, ff = w_gate.shape[1], w_gate.shape[2]
    hbm = pl.BlockSpec(memory_space=pl.ANY)
    grid_spec = pltpu.PrefetchScalarGridSpec(
        num_scalar_prefetch=4,
        grid=(n_blocks,),
        in_specs=[pl.BlockSpec((rb * pc, PACK_LANES),
                               lambda i, be, nu, od, nx: (jnp.minimum(i, jnp.maximum(nu[0] - 1, 0)), 0)),
                  hbm, hbm, hbm],
        out_specs=pl.BlockSpec((rb * pc, PACK_LANES), lambda i, be, nu, od, nx: (i, 0)),
        scratch_shapes=[pltpu.VMEM((2, d, ff), F32), pltpu.VMEM((2, d, ff), F32),
                        pltpu.VMEM((2, ff, d), F32),
                        pltpu.VMEM((d, ff), BF16), pltpu.VMEM((d, ff), BF16),
                        pltpu.VMEM((ff, d), BF16), pltpu.SemaphoreType.DMA((2,))],
    )
    return pl.pallas_call(
        _experts_kernel,
        grid_spec=grid_spec,
        out_shape=jax.ShapeDtypeStruct(x_sorted.shape, U32),
        compiler_params=_params(("arbitrary",)),
        name="routed_experts",
    )(block_e, nused, block_ord, block_next, x_sorted, w_gate, w_up, w_down)


COMBINE_TILE = 128


def _combine_kernel(d_cur_ref, d_next_ref, y_ref, w_ref, base_ref, g2_ref, lng_ref, lnb_ref,
                    o_ref, ybuf_a, ybuf_b, sem):
    i = pl.program_id(0)
    n_steps = pl.num_programs(0)
    tc = COMBINE_TILE
    n_copies = TOP_K * tc
    pc = PACK_CHUNKS

    def row_copy(row, buf, r, s):
        src = y_ref.at[pl.ds(pl.multiple_of(row * pc, pc), pc), :]
        return pltpu.make_async_copy(src, buf.at[pl.ds(r * pc, pc), :], sem.at[s])

    def wait_rows(buf, s):
        pltpu.make_async_copy(y_ref.at[pl.ds(0, n_copies * pc), :], buf, sem.at[s]).wait()

    @pl.when(i == 0)
    def _():
        def body(r, carry):
            src = y_ref.at[pl.ds(pl.multiple_of(d_cur_ref[r] * pc, pc), pc), :]
            dst = ybuf_a.at[pl.ds(pl.multiple_of(r * pc, pc), pc), :]
            pltpu.make_async_copy(src, dst, sem.at[0]).start()
            return carry
        lax.fori_loop(0, n_copies, body, 0, unroll=8)

    def step(cur, cur_s, nxt, nxt_s):
        wait_rows(cur, cur_s)
        for r in range(n_copies):
            row_copy(d_next_ref[r], nxt, r, nxt_s).start(priority=r % 2)
        w = w_ref[...]
        lo, hi = _load_packed_rows(cur, 0, tc)
        routed_lo, routed_hi = w[:, 0:1] * lo, w[:, 0:1] * hi
        for k in range(1, TOP_K):
            lo, hi = _load_packed_rows(cur, k * tc, tc)
            routed_lo = routed_lo + w[:, k:k + 1] * lo
            routed_hi = routed_hi + w[:, k:k + 1] * hi
        routed = jnp.concatenate([routed_lo, routed_hi], axis=1)
        y = base_ref[...] + (1.0 + g2_ref[0]) * routed
        o_ref[...] = _layer_norm(y, lng_ref[...], lnb_ref[...])

    @pl.when(i % 2 == 0)
    def _():
        step(ybuf_a, 0, ybuf_b, 1)

    @pl.when(i % 2 == 1)
    def _():
        step(ybuf_b, 1, ybuf_a, 0)

    @pl.when((i == n_steps - 1) & (i % 2 == 0))
    def _():
        wait_rows(ybuf_b, 1)

    @pl.when((i == n_steps - 1) & (i % 2 == 1))
    def _():
        wait_rows(ybuf_a, 0)


def combine(y_sorted, dest_tiles, w_tok, base, g2, ln_g, ln_b, seq):
    t_tokens, d = base.shape
    tc = COMBINE_TILE
    n_steps = t_tokens // tc
    tiles_per_seq = seq // tc
    n_copies = TOP_K * tc

    def const(shape):
        return pl.BlockSpec(shape, lambda i: (0,) * len(shape))

    return pl.pallas_call(
        _combine_kernel,
        grid=(n_steps,),
        in_specs=[pl.BlockSpec((n_copies,), lambda i: (i,), memory_space=pltpu.SMEM),
                  pl.BlockSpec((n_copies,), lambda i: (jnp.minimum(i + 1, n_steps - 1),),
                               memory_space=pltpu.SMEM),
                  pl.BlockSpec(memory_space=pl.ANY),
                  pl.BlockSpec((tc, TOP_K), lambda i: (i, 0)),
                  pl.BlockSpec((tc, d), lambda i: (i, 0)),
                  pl.BlockSpec((1, 1, d), lambda i: (i // tiles_per_seq, 0, 0)),
                  const((1, d)), const((1, d))],
        out_specs=pl.BlockSpec((tc, d), lambda i: (i, 0)),
        out_shape=jax.ShapeDtypeStruct((t_tokens, d), F32),
        scratch_shapes=[pltpu.VMEM((n_copies * PACK_CHUNKS, PACK_LANES), U32),
                        pltpu.VMEM((n_copies * PACK_CHUNKS, PACK_LANES), U32),
                        pltpu.SemaphoreType.DMA((2,))],
        compiler_params=_params(("arbitrary",)),
        name="combine",
    )(dest_tiles, dest_tiles, y_sorted, w_tok, base, g2, ln_g.reshape(1, d), ln_b.reshape(1, d))


def _dispatch_tables(e_t, pos_t, counts, t_tokens):
    rb = MOE_ROW_BLOCK
    n_blocks = t_tokens * TOP_K // rb + N_EXPERTS
    padded = (counts + rb - 1) // rb * rb
    pends = jnp.cumsum(padded)
    pstarts = pends - padded
    experts = jnp.arange(N_EXPERTS, dtype=I32)
    pstart_of = jnp.sum(jnp.where(e_t[..., None] == experts, pstarts.astype(I32), 0), axis=-1)
    dest = (pstart_of + pos_t).astype(I32)
    nused = (pends[-1] // rb).astype(I32)
    blk = jnp.arange(n_blocks, dtype=I32)
    last_blk = jnp.minimum(blk, jnp.maximum(nused - 1, 0))
    block_e = jnp.sum((pends[None, :] <= (last_blk * rb)[:, None]).astype(I32), axis=1)
    block_e = jnp.minimum(block_e, N_EXPERTS - 1).astype(I32)
    first = (blk < nused) & ((blk == 0) | (block_e != jnp.roll(block_e, 1)))
    block_ord = (jnp.cumsum(first.astype(I32)) - 1).astype(I32)
    later = (experts[None, :] > experts[:, None]) & (padded[None, :] > 0)
    next_of = jnp.min(jnp.where(later, experts[None, :], N_EXPERTS), axis=1)
    block_next = jnp.sum(jnp.where(block_e[:, None] == experts[None, :], next_of[None, :], 0),
                         axis=1).astype(I32)
    pad_start = (pstarts + counts).astype(I32)
    pad_count = (padded - counts).astype(I32)
    return (dest, block_e, nused.reshape(1), block_ord, block_next, pad_start, pad_count,
            n_blocks * rb)


def _layer(x, c, w_ada, b_ada, w_in, g_cq, w_uq, g_ckv, w_ukv, w_o, ln1_g, ln1_b,
           w_router, router_bias, w_gate, w_up, w_down, ws_gate, ws_up, ws_down, ln2_g, ln2_b):
    bsz, seq, d = x.shape
    t_tokens = bsz * seq
    x2d = x.reshape(t_tokens, d)

    mod = ada_modulation(c, w_ada, b_ada)
    sh1, sc1, g1, sh2, sc2, g2 = [m.reshape(bsz, 1, d) for m in jnp.split(mod, 6, axis=-1)]

    qa, ka, va, qm, kn, vm, kr = input_projection(x2d, sc1, sh1, w_in, g_cq, w_uq, g_ckv, w_ukv,
                                                  bsz, seq)
    attn_a = moba_attention(qa, ka, va, bsz, seq)
    attn_m = mla_attention(qm, kn, kr, vm, bsz, seq)
    h2, h2p, base = output_projection(attn_a, attn_m, x2d, w_o, g1, sc2, sh2, g2, ln1_g, ln1_b,
                                 ws_gate, ws_up, ws_down, seq)

    e_t, pos_t, w_t, counts = router(h2, w_router, router_bias)
    (dest, block_e, nused, block_ord, block_next, pad_start, pad_count,
     n_rows) = _dispatch_tables(e_t, pos_t, counts, t_tokens)
    x_sorted = dispatch(h2p, dest.T.reshape(-1), pad_start, pad_count, nused, n_rows)
    y_sorted = routed_experts(x_sorted, block_e, nused, block_ord, block_next, w_gate, w_up, w_down)

    tc = COMBINE_TILE
    dest_tiles = dest.reshape(TOP_K, t_tokens // tc, tc).transpose(1, 0, 2).reshape(-1)
    out = combine(y_sorted, dest_tiles, w_t.T, base, g2, ln2_g, ln2_b, seq)
    return out.reshape(bsz, seq, d)


def kernel(x, c, w_ada, b_ada, w_in, g_cq, w_uq, g_ckv, w_ukv, w_o, ln1_g, ln1_b, w_router,
           router_bias, w_gate, w_up, w_down, ws_gate, ws_up, ws_down, ln2_g, ln2_b):
    depth = w_ada.shape[0]
    assert depth == 1, "DeepNorm constants are baked for a single layer"
    return _layer(x, c, w_ada[0], b_ada[0], w_in[0], g_cq[0], w_uq[0], g_ckv[0], w_ukv[0], w_o[0],
                  ln1_g[0], ln1_b[0], w_router[0], router_bias[0], w_gate[0], w_up[0], w_down[0],
                  ws_gate[0], ws_up[0], ws_down[0], ln2_g[0], ln2_b[0])
```

```python
import functools

import jax
import jax.numpy as jnp
import numpy as np
from jax import lax
from jax.experimental import pallas as pl
from jax.experimental.pallas import tpu as pltpu

F32 = jnp.float32
BF16 = jnp.bfloat16
I32 = jnp.int32

D_MODEL = 2048
HEAD_DIM = 128
N_HEADS = 8
MOBA_BLOCK = 256
MOBA_TOPK = 3
MLA_ROPE_DIM = 64
MLA_RANK = 512
MLA_QK_DIM = HEAD_DIM + MLA_ROPE_DIM
GROUP_WIDTH = N_HEADS * HEAD_DIM
ROPE_THETA = 10000.0
N_EXPERTS = 64
TOP_K = 8
N_GROUPS = 8
TOPK_GROUPS = 4
EXPERT_FF = 512
ROUTED_SCALE = 2.5
MOE_ROW_BLOCK = 256
LN_EPS = 1e-5
RMS_EPS = 1e-6
DEEPNORM_ALPHA = 2.0 ** 0.25

VMEM_LIMIT_BYTES = 56 * 1024 * 1024

NT_DIMS = (((1,), (1,)), ((), ()))


def _params(semantics):
    return pltpu.CompilerParams(dimension_semantics=semantics,
                                vmem_limit_bytes=VMEM_LIMIT_BYTES)


def _silu(x):
    return x * jax.nn.sigmoid(x)


PACK_CHUNKS = 8
PACK_LANES = 128
U32 = jnp.uint32
HIGH_HALF = np.uint32(0xFFFF0000)


def _store_packed_rows(ref, x):
    n_rows, half = x.shape[0], x.shape[1] // 2
    lo = lax.bitcast_convert_type(x[:, :half].astype(BF16).astype(F32), U32) >> 16
    hi = lax.bitcast_convert_type(x[:, half:].astype(BF16).astype(F32), U32) & HIGH_HALF
    packed = lo | hi
    for j in range(PACK_CHUNKS):
        ref[pl.ds(j, n_rows, stride=PACK_CHUNKS), :] = packed[:, j * PACK_LANES:(j + 1) * PACK_LANES]


def _load_packed_rows(ref, first_row, n_rows):
    lo, hi = [], []
    for j in range(PACK_CHUNKS):
        u = ref[pl.ds(first_row * PACK_CHUNKS + j, n_rows, stride=PACK_CHUNKS), :]
        lo.append(lax.bitcast_convert_type(u << 16, F32))
        hi.append(lax.bitcast_convert_type(u & HIGH_HALF, F32))
    return jnp.concatenate(lo, axis=1), jnp.concatenate(hi, axis=1)


def _ada_kernel(c_ref, w_ref, b_ref, o_ref):
    a = _silu(c_ref[...]).astype(BF16)
    o_ref[...] = jnp.dot(a, w_ref[...].astype(BF16), preferred_element_type=F32) + b_ref[...]


def ada_modulation(c, w_ada, b_ada):
    bsz, d = c.shape
    rows = -(-bsz // 16) * 16
    c_pad = jnp.pad(c, ((0, rows - bsz), (0, 0)))
    n = w_ada.shape[1]
    tn = 1024
    out = pl.pallas_call(
        _ada_kernel,
        grid=(n // tn,),
        in_specs=[pl.BlockSpec((rows, d), lambda j: (0, 0)),
                  pl.BlockSpec((d, tn), lambda j: (0, j)),
                  pl.BlockSpec((1, tn), lambda j: (0, j))],
        out_specs=pl.BlockSpec((rows, tn), lambda j: (0, j)),
        out_shape=jax.ShapeDtypeStruct((rows, n), F32),
        compiler_params=_params(("arbitrary",)),
        name="ada_modulation",
    )(c_pad, w_ada, b_ada.reshape(1, n))
    return out[:bsz]


def _rms(x, g):
    return x * lax.rsqrt(jnp.mean(x * x, axis=-1, keepdims=True) + RMS_EPS) * g


def _rope64(x, cos, sin):
    return x * cos + pltpu.roll(x, 64, axis=1) * sin


def _rope32(x, cos, s1, s2):
    return x * cos + pltpu.roll(x, 96, axis=1) * s1 + pltpu.roll(x, 32, axis=1) * s2


def _inproj_kernel(x_ref, sc_ref, sh_ref, w_ref, wkr_ref, wuq_ref, wuk_ref, wuv_ref, gcq_ref, gckv_ref,
                   cosa_ref, sina_ref, cosr_ref, s1_ref, s2_ref,
                   qa_ref, ka_ref, va_ref, qm_ref, kn_ref, vm_ref, kr_ref):
    h = (x_ref[...] * (1.0 + sc_ref[0]) + sh_ref[0]).astype(BF16)
    cosa, sina = cosa_ref[...], sina_ref[...]
    cosr, s1, s2 = cosr_ref[...], s1_ref[...], s2_ref[...]
    gw = GROUP_WIDTH

    def proj(lo, hi):
        return jnp.dot(h, w_ref[:, lo:hi], preferred_element_type=F32)

    for out_ref, base in ((qa_ref, 0), (ka_ref, gw)):
        t = proj(base, base + gw)
        for hd in range(N_HEADS):
            sl = slice(hd * HEAD_DIM, (hd + 1) * HEAD_DIM)
            out_ref[:, sl] = _rope64(t[:, sl], cosa, sina).astype(BF16)
    va_ref[...] = proj(2 * gw, 3 * gw).T.astype(BF16)

    cq = _rms(proj(3 * gw, 3 * gw + MLA_RANK), gcq_ref[...]).astype(BF16)
    qm = jnp.dot(cq, wuq_ref[...], preferred_element_type=F32)
    for hd in range(N_HEADS):
        b0 = hd * 2 * HEAD_DIM
        qm_ref[:, b0:b0 + HEAD_DIM] = qm[:, b0:b0 + HEAD_DIM].astype(BF16)
        qm_ref[:, b0 + HEAD_DIM:b0 + 2 * HEAD_DIM] = _rope32(
            qm[:, b0 + HEAD_DIM:b0 + 2 * HEAD_DIM], cosr, s1, s2).astype(BF16)

    ckv = _rms(proj(3 * gw + MLA_RANK, 3 * gw + 2 * MLA_RANK), gckv_ref[...]).astype(BF16)
    kn_ref[...] = jnp.dot(ckv, wuk_ref[...], preferred_element_type=F32).astype(BF16)
    vm_ref[...] = jnp.dot(ckv, wuv_ref[...], preferred_element_type=F32).T.astype(BF16)

    kr = jnp.dot(h, wkr_ref[...], preferred_element_type=F32)
    kr_ref[...] = _rope32(kr, cosr, s1, s2).astype(BF16)


def _rope_tables(seq):
    pos = jnp.arange(seq, dtype=F32)[:, None]
    inv_a = 1.0 / (ROPE_THETA ** (jnp.arange(0, HEAD_DIM, 2, dtype=F32) / HEAD_DIM))
    ang_a = pos * inv_a[None, :]
    cosa = jnp.concatenate([jnp.cos(ang_a), jnp.cos(ang_a)], axis=1)
    sina = jnp.concatenate([-jnp.sin(ang_a), jnp.sin(ang_a)], axis=1)
    inv_r = 1.0 / (ROPE_THETA ** (jnp.arange(0, MLA_ROPE_DIM, 2, dtype=F32) / MLA_ROPE_DIM))
    ang_r = pos * inv_r[None, :]
    cr, sr = jnp.cos(ang_r), jnp.sin(ang_r)
    z32 = jnp.zeros_like(cr)
    z64 = jnp.zeros((seq, 64), F32)
    cosr = jnp.concatenate([cr, cr, z64], axis=1)
    s1 = jnp.concatenate([-sr, z32, z64], axis=1)
    s2 = jnp.concatenate([z32, sr, z64], axis=1)
    return cosa, sina, cosr, s1, s2


def input_projection(x2d, sc1, sh1, w_in, g_cq, w_uq, g_ckv, w_ukv, bsz, seq):
    t_tokens, d = x2d.shape
    tm = 256
    tiles_per_seq = seq // tm
    gw = GROUP_WIDTH
    main_w = 3 * gw + 2 * MLA_RANK
    w_main = w_in[:, :main_w].astype(BF16)
    w_kr = jnp.pad(w_in[:, main_w:], ((0, 0), (0, HEAD_DIM - MLA_ROPE_DIM))).astype(BF16)
    wuq = jnp.pad(w_uq.reshape(MLA_RANK, N_HEADS, MLA_QK_DIM),
                  ((0, 0), (0, 0), (0, 2 * HEAD_DIM - MLA_QK_DIM)))
    wuq = wuq.reshape(MLA_RANK, N_HEADS * 2 * HEAD_DIM).astype(BF16)
    wukv = w_ukv.reshape(MLA_RANK, N_HEADS, 2 * HEAD_DIM)
    wuk = wukv[:, :, :HEAD_DIM].reshape(MLA_RANK, gw).astype(BF16)
    wuv = wukv[:, :, HEAD_DIM:].reshape(MLA_RANK, gw).astype(BF16)
    tables = _rope_tables(seq)

    def const(shape):
        return pl.BlockSpec(shape, lambda i: (0,) * len(shape), pipeline_mode=pl.Buffered(1))

    def rows(width):
        return pl.BlockSpec((tm, width), lambda i: (i, 0))

    mod_spec = pl.BlockSpec((1, 1, d), lambda i: (i // tiles_per_seq, 0, 0))
    tab_spec = pl.BlockSpec((tm, HEAD_DIM), lambda i: (i % tiles_per_seq, 0))
    widths = (gw, gw, gw, 2 * gw, gw, gw, HEAD_DIM)
    transposed = (False, False, True, False, False, True, False)

    def out_spec(width, is_t):
        return pl.BlockSpec((width, tm), lambda i: (0, i)) if is_t else rows(width)

    def out_shape(width, is_t):
        return jax.ShapeDtypeStruct((width, t_tokens) if is_t else (t_tokens, width), BF16)

    return pl.pallas_call(
        _inproj_kernel,
        grid=(t_tokens // tm,),
        in_specs=[rows(d), mod_spec, mod_spec, const((d, main_w)), const((d, HEAD_DIM)),
                  const((MLA_RANK, 2 * gw)), const((MLA_RANK, gw)), const((MLA_RANK, gw)),
                  const((1, MLA_RANK)), const((1, MLA_RANK))] + [tab_spec] * 5,
        out_specs=[out_spec(w, t) for w, t in zip(widths, transposed)],
        out_shape=[out_shape(w, t) for w, t in zip(widths, transposed)],
        compiler_params=_params(("arbitrary",)),
        name="input_projection",
    )(x2d, sc1, sh1, w_main, w_kr, wuq, wuk, wuv, g_cq.reshape(1, -1), g_ckv.reshape(1, -1), *tables)


ATTN_BLOCK = 256
LOG2_E = 1.4426950408889634


def _causal_mask_t():
    key = lax.broadcasted_iota(I32, (ATTN_BLOCK, ATTN_BLOCK), 0)
    qry = lax.broadcasted_iota(I32, (ATTN_BLOCK, ATTN_BLOCK), 1)
    return key <= qry


def _scores_t(q, k_blk):
    return lax.dot_general(k_blk, q, NT_DIMS, preferred_element_type=F32)


def _first_block(s, vt_blk, c, acc_ref):
    s = jnp.where(_causal_mask_t(), s, -jnp.inf)
    m = jnp.max(s, axis=0, keepdims=True)
    p = jnp.exp2((s - m) * c)
    acc_ref[...] = jnp.dot(vt_blk, p.astype(BF16), preferred_element_type=F32)
    return m, jnp.sum(p, axis=0, keepdims=True)


def _next_block(s, vt_blk, bias, c, m, l, acc_ref):
    m_blk = jnp.max(s, axis=0, keepdims=True)
    if bias is not None:
        m_blk = m_blk + bias
    m_new = jnp.maximum(m, m_blk)
    alpha = jnp.exp2((m - m_new) * c)
    p = jnp.exp2((s - (m_new if bias is None else m_new - bias)) * c)
    acc_ref[...] = alpha * acc_ref[...] + jnp.dot(vt_blk, p.astype(BF16), preferred_element_type=F32)
    return m_new, alpha * l + jnp.sum(p, axis=0, keepdims=True)


HEADS_PER_STEP = 8


def _head_cols(g, width=HEAD_DIM):
    return slice(g * width, (g + 1) * width)


def _moba_block_bias(kmean, q, qi, n_blocks):
    gate = lax.dot_general(kmean, q.astype(F32), NT_DIMS, precision=lax.Precision.HIGHEST,
                           preferred_element_type=F32)
    row = lax.broadcasted_iota(I32, gate.shape, 0)
    gt = jnp.where(row < qi, gate, -jnp.inf)
    cnt = jnp.zeros(gate.shape, F32)
    for m in range(n_blocks - 1):
        gm = gt[m:m + 1, :]
        cnt = cnt + jnp.where((gm > gt) | ((gm == gt) & (m < row)), 1.0, 0.0)
    return jnp.where(cnt < float(MOBA_TOPK), 0.0, -jnp.inf).astype(F32)


def _moba_kernel(q_ref, k_ref, vt_ref, o_ref, kmean_ref, acc_ref, *, n_blocks):
    qi = pl.program_id(2)
    blk = MOBA_BLOCK
    heads = HEADS_PER_STEP

    @pl.when(qi == 0)
    def _():
        kmean_ref[...] = jnp.zeros(kmean_ref.shape, F32)
        for g in range(heads):
            for n in range(n_blocks):
                kb = k_ref[0, n * blk:(n + 1) * blk, _head_cols(g)].astype(F32)
                kmean_ref[g, n:n + 1, :] = jnp.sum(kb, axis=0, keepdims=True) * (1.0 / blk)

    c = (HEAD_DIM ** -0.5) * LOG2_E
    own = pl.multiple_of(qi * blk, blk)
    qs = [q_ref[0, :, _head_cols(g)] for g in range(heads)]
    scores = [_scores_t(qs[g], k_ref[0, pl.ds(own, blk), _head_cols(g)]) for g in range(heads)]
    biases = [_moba_block_bias(kmean_ref[g], qs[g], qi, n_blocks) for g in range(heads)]
    carry0 = []
    for g in range(heads):
        carry0 += _first_block(scores[g], vt_ref[_head_cols(g), pl.ds(own, blk)], c, acc_ref.at[g])

    def body(n, carry):
        off = pl.multiple_of(n * blk, blk)
        scores = [_scores_t(qs[g], k_ref[0, pl.ds(off, blk), _head_cols(g)]) for g in range(heads)]
        row = lax.broadcasted_iota(I32, biases[0].shape, 0)
        out = []
        for g in range(heads):
            bias = jnp.sum(jnp.where(row == n, biases[g], 0.0), axis=0, keepdims=True)
            out += _next_block(scores[g], vt_ref[_head_cols(g), pl.ds(off, blk)], bias,
                               c, carry[2 * g], carry[2 * g + 1], acc_ref.at[g])
        return tuple(out)

    final = lax.fori_loop(0, qi, body, tuple(carry0))
    for g in range(heads):
        o_ref[0, :, _head_cols(g)] = (acc_ref[g] / final[2 * g + 1]).T.astype(o_ref.dtype)


def moba_attention(qa, ka, vat, bsz, seq):
    blk = MOBA_BLOCK
    n_blocks = seq // blk
    heads = HEADS_PER_STEP
    gate_rows = -(-n_blocks // 8) * 8
    q3, k3 = (t.reshape(bsz, seq, GROUP_WIDTH) for t in (qa, ka))
    q_spec = pl.BlockSpec((1, blk, heads * HEAD_DIM), lambda b, h, i: (b, i, h))
    kv_spec = pl.BlockSpec((1, seq, heads * HEAD_DIM), lambda b, h, i: (b, 0, h))
    out = pl.pallas_call(
        functools.partial(_moba_kernel, n_blocks=n_blocks),
        grid=(bsz, N_HEADS // heads, n_blocks),
        in_specs=[q_spec, kv_spec, pl.BlockSpec((heads * HEAD_DIM, seq), lambda b, h, i: (h, b))],
        out_specs=q_spec,
        out_shape=jax.ShapeDtypeStruct((bsz, seq, GROUP_WIDTH), BF16),
        scratch_shapes=[pltpu.VMEM((heads, gate_rows, HEAD_DIM), F32),
                        pltpu.VMEM((heads, HEAD_DIM, blk), F32)],
        compiler_params=_params(("arbitrary", "arbitrary", "arbitrary")),
        name="moba_attention",
    )(q3, k3, vat)
    return out.reshape(bsz * seq, GROUP_WIDTH)


def _mla_kernel(q_ref, kn_ref, kr_ref, vt_ref, o_ref, kcat_ref, acc_ref):
    qi = pl.program_id(2)
    blk = ATTN_BLOCK

    heads = HEADS_PER_STEP

    @pl.when(qi == 0)
    def _():
        for g in range(heads):
            kcat_ref[g, :, :HEAD_DIM] = kn_ref[0, :, _head_cols(g)]
            kcat_ref[g, :, HEAD_DIM:] = kr_ref[0]

    c = (MLA_QK_DIM ** -0.5) * LOG2_E
    own = pl.multiple_of(qi * blk, blk)
    qs = [q_ref[0, :, _head_cols(g, 2 * HEAD_DIM)] for g in range(heads)]
    scores = [_scores_t(qs[g], kcat_ref[g, pl.ds(own, blk), :]) for g in range(heads)]
    carry0 = []
    for g in range(heads):
        carry0 += _first_block(scores[g], vt_ref[_head_cols(g), pl.ds(own, blk)], c, acc_ref.at[g])

    def body(n, carry):
        off = pl.multiple_of(n * blk, blk)
        scores = [_scores_t(qs[g], kcat_ref[g, pl.ds(off, blk), :]) for g in range(heads)]
        out = []
        for g in range(heads):
            out += _next_block(scores[g], vt_ref[_head_cols(g), pl.ds(off, blk)], None, c,
                               carry[2 * g], carry[2 * g + 1], acc_ref.at[g])
        return tuple(out)

    final = lax.fori_loop(0, qi, body, tuple(carry0))
    for g in range(heads):
        o_ref[0, :, _head_cols(g)] = (acc_ref[g] / final[2 * g + 1]).T.astype(o_ref.dtype)


def mla_attention(qm, kn, kr, vmt, bsz, seq):
    blk = ATTN_BLOCK
    heads = HEADS_PER_STEP
    q3 = qm.reshape(bsz, seq, 2 * GROUP_WIDTH)
    kn3 = kn.reshape(bsz, seq, GROUP_WIDTH)
    kr3 = kr.reshape(bsz, seq, HEAD_DIM)
    out = pl.pallas_call(
        _mla_kernel,
        grid=(bsz, N_HEADS // heads, seq // blk),
        in_specs=[pl.BlockSpec((1, blk, heads * 2 * HEAD_DIM), lambda b, h, i: (b, i, h)),
                  pl.BlockSpec((1, seq, heads * HEAD_DIM), lambda b, h, i: (b, 0, h)),
                  pl.BlockSpec((1, seq, HEAD_DIM), lambda b, h, i: (b, 0, 0)),
                  pl.BlockSpec((heads * HEAD_DIM, seq), lambda b, h, i: (h, b))],
        out_specs=pl.BlockSpec((1, blk, heads * HEAD_DIM), lambda b, h, i: (b, i, h)),
        out_shape=jax.ShapeDtypeStruct((bsz, seq, GROUP_WIDTH), BF16),
        scratch_shapes=[pltpu.VMEM((heads, seq, 2 * HEAD_DIM), BF16),
                        pltpu.VMEM((heads, HEAD_DIM, blk), F32)],
        compiler_params=_params(("arbitrary", "arbitrary", "arbitrary")),
        name="mla_attention",
    )(q3, kn3, kr3, vmt)
    return out.reshape(bsz * seq, GROUP_WIDTH)


def _layer_norm(y, g, b):
    mu = jnp.mean(y, axis=-1, keepdims=True)
    yc = y - mu
    var = jnp.mean(yc * yc, axis=-1, keepdims=True)
    return yc * lax.rsqrt(var + LN_EPS) * g + b


def _outproj_kernel(a_ref, m_ref, x_ref, woa_ref, wom_ref, g1_ref, sc2_ref, sh2_ref, g2_ref,
                    lng_ref, lnb_ref, wsg_ref, wsu_ref, wsd_ref, h2_ref, h2p_ref, base_ref):
    f = (jnp.dot(a_ref[...], woa_ref[...], preferred_element_type=F32)
         + jnp.dot(m_ref[...], wom_ref[...], preferred_element_type=F32))
    x1 = _layer_norm(DEEPNORM_ALPHA * x_ref[...] + (1.0 + g1_ref[0]) * f, lng_ref[...], lnb_ref[...])
    h2 = x1 * (1.0 + sc2_ref[0]) + sh2_ref[0]
    h2_ref[...] = h2
    _store_packed_rows(h2p_ref, h2)
    hb = h2.astype(BF16)
    act = (_silu(jnp.dot(hb, wsg_ref[...], preferred_element_type=F32))
           * jnp.dot(hb, wsu_ref[...], preferred_element_type=F32)).astype(BF16)
    shared = jnp.dot(act, wsd_ref[...], preferred_element_type=F32)
    base_ref[...] = DEEPNORM_ALPHA * x1 + (1.0 + g2_ref[0]) * shared


def output_projection(attn_a, attn_m, x2d, w_o, g1, sc2, sh2, g2, ln_g, ln_b,
                      ws_gate, ws_up, ws_down, seq):
    t_tokens, d = x2d.shape
    tm = 256
    tiles_per_seq = seq // tm
    gw = GROUP_WIDTH
    ff = ws_gate.shape[1]

    def const(shape):
        return pl.BlockSpec(shape, lambda i: (0,) * len(shape), pipeline_mode=pl.Buffered(1))

    def rows(width):
        return pl.BlockSpec((tm, width), lambda i: (i, 0))

    mod_spec = pl.BlockSpec((1, 1, d), lambda i: (i // tiles_per_seq, 0, 0))
    return pl.pallas_call(
        _outproj_kernel,
        grid=(t_tokens // tm,),
        in_specs=[rows(gw), rows(gw), rows(d), const((gw, d)), const((gw, d)),
                  mod_spec, mod_spec, mod_spec, mod_spec, const((1, d)), const((1, d)),
                  const((d, ff)), const((d, ff)), const((ff, d))],
        out_specs=[rows(d), pl.BlockSpec((tm * PACK_CHUNKS, PACK_LANES), lambda i: (i, 0)), rows(d)],
        out_shape=[jax.ShapeDtypeStruct((t_tokens, d), F32),
                   jax.ShapeDtypeStruct((t_tokens * PACK_CHUNKS, PACK_LANES), U32),
                   jax.ShapeDtypeStruct((t_tokens, d), F32)],
        compiler_params=_params(("arbitrary",)),
        name="output_projection",
    )(attn_a, attn_m, x2d, w_o[:gw].astype(BF16), w_o[gw:].astype(BF16), g1, sc2, sh2, g2,
      ln_g.reshape(1, d), ln_b.reshape(1, d),
      ws_gate.astype(BF16), ws_up.astype(BF16), ws_down.astype(BF16))


ROUTER_TILE = 512


def _router_kernel(h_ref, wrt_ref, bias_ref, tri_ref, ltri_ref,
                   e_ref, pos_ref, w_ref, cnt_ref, run_ref):
    tr = ROUTER_TILE
    epg = N_EXPERTS // N_GROUPS

    @pl.when(pl.program_id(0) == 0)
    def _():
        run_ref[...] = jnp.zeros(run_ref.shape, F32)

    logits = lax.dot_general(wrt_ref[...], h_ref[...], NT_DIMS, precision=lax.Precision.HIGHEST,
                             preferred_element_type=F32)
    scores = jax.nn.sigmoid(logits)
    sel_scores = scores + bias_ref[...]

    sub = lax.broadcasted_iota(I32, (epg, tr), 0)
    grp_rows = []
    for g in range(N_GROUPS):
        xg = sel_scores[g * epg:(g + 1) * epg, :]
        m1 = jnp.max(xg, axis=0, keepdims=True)
        first = jnp.min(jnp.where(xg == m1, sub, epg), axis=0, keepdims=True)
        m2 = jnp.max(jnp.where(sub == first, -jnp.inf, xg), axis=0, keepdims=True)
        grp_rows.append(m1 + m2)
    grp = jnp.concatenate(grp_rows, axis=0)

    giota = lax.broadcasted_iota(I32, (N_GROUPS, tr), 0)
    gcnt = jnp.zeros((N_GROUPS, tr), F32)
    for m in range(N_GROUPS):
        rm = grp[m:m + 1, :]
        gcnt = gcnt + jnp.where((rm > grp) | ((rm == grp) & (m < giota)), 1.0, 0.0)
    gkeep = jnp.where(gcnt < float(TOPK_GROUPS), 1.0, 0.0)
    gmask = jnp.concatenate(
        [jnp.broadcast_to(gkeep[g:g + 1, :], (epg, tr)) for g in range(N_GROUPS)], axis=0)
    masked = jnp.where(gmask > 0.0, sel_scores, -jnp.inf)

    eiota = lax.broadcasted_iota(I32, (N_EXPERTS, tr), 0)

    free = jnp.ones((N_EXPERTS, tr), F32)
    for _ in range(TOP_K):
        top = jnp.max(jnp.where(free > 0.0, masked, -jnp.inf), axis=0, keepdims=True)
        first = jnp.min(jnp.where((masked == top) & (free > 0.0), eiota, N_EXPERTS),
                        axis=0, keepdims=True)
        free = jnp.where(eiota == first, 0.0, free)
    sel = free == 0.0

    sw = jnp.where(sel, scores, 0.0)
    wn = sw / jnp.sum(sw, axis=0, keepdims=True) * ROUTED_SCALE

    selb = jnp.where(sel, 1.0, 0.0).astype(BF16)
    cum = jnp.dot(selb, tri_ref[...], preferred_element_type=F32)
    pos = run_ref[...] + cum - 1.0
    run_ref[...] = run_ref[...] + cum[:, tr - 1:tr]
    cnt_ref[...] = jnp.broadcast_to(run_ref[...], cnt_ref.shape).astype(I32)

    slot = jnp.dot(ltri_ref[...], selb, preferred_element_type=F32)
    ef = eiota.astype(F32)
    e_rows, p_rows, w_rows = [], [], []
    for j in range(TOP_K):
        mj = sel & (slot == float(j))
        e_rows.append(jnp.sum(jnp.where(mj, ef, 0.0), axis=0, keepdims=True))
        p_rows.append(jnp.sum(jnp.where(mj, pos, 0.0), axis=0, keepdims=True))
        w_rows.append(jnp.sum(jnp.where(mj, wn, 0.0), axis=0, keepdims=True))
    e_ref[...] = jnp.concatenate(e_rows, axis=0).astype(I32)
    pos_ref[...] = jnp.concatenate(p_rows, axis=0).astype(I32)
    w_ref[...] = jnp.concatenate(w_rows, axis=0)


def router(h2, w_router, router_bias):
    t_tokens, d = h2.shape
    tr = ROUTER_TILE
    tri = jnp.asarray(np.triu(np.ones((tr, tr), np.float32)), BF16)
    ltri = jnp.asarray(np.tril(np.ones((N_EXPERTS, N_EXPERTS), np.float32), -1), BF16)

    def const(shape):
        return pl.BlockSpec(shape, lambda i: (0,) * len(shape))

    tok = pl.BlockSpec((TOP_K, tr), lambda i: (0, i))
    e_t, pos_t, w_t, cnt = pl.pallas_call(
        _router_kernel,
        grid=(t_tokens // tr,),
        in_specs=[pl.BlockSpec((tr, d), lambda i: (i, 0)), const((N_EXPERTS, d)),
                  const((N_EXPERTS, 1)), const((tr, tr)), const((N_EXPERTS, N_EXPERTS))],
        out_specs=[tok, tok, tok, const((N_EXPERTS, 128))],
        out_shape=[jax.ShapeDtypeStruct((TOP_K, t_tokens), I32),
                   jax.ShapeDtypeStruct((TOP_K, t_tokens), I32),
                   jax.ShapeDtypeStruct((TOP_K, t_tokens), F32),
                   jax.ShapeDtypeStruct((N_EXPERTS, 128), I32)],
        scratch_shapes=[pltpu.VMEM((N_EXPERTS, 1), F32)],
        compiler_params=_params(("arbitrary",)),
        name="router",
    )(h2, w_router.T, router_bias.reshape(N_EXPERTS, 1), tri, ltri)
    return e_t, pos_t, w_t, cnt[:, 0]


DISPATCH_TILE = 128


def _dispatch_kernel(pad_start_ref, pad_count_ref, nused_ref, d_ref, h_ref, xs_ref, zero_ref, sem):
    tt = DISPATCH_TILE
    pc = PACK_CHUNKS
    rb = MOE_ROW_BLOCK
    n_copies = tt * TOP_K
    n_blocks = xs_ref.shape[0] // (rb * pc)

    @pl.when(pl.program_id(0) == 0)
    def _():
        zero_ref[...] = jnp.zeros(zero_ref.shape, U32)
        nused = nused_ref[0]

        def pad_rows(row, n):
            dst = xs_ref.at[pl.ds(pl.multiple_of(row * pc, pc), n * pc), :]
            return pltpu.make_async_copy(zero_ref.at[pl.ds(0, n * pc), :], dst, sem.at[1])

        def tail_block(b):
            dst = xs_ref.at[pl.ds(pl.multiple_of(b * (rb * pc), rb * pc), rb * pc), :]
            return pltpu.make_async_copy(zero_ref, dst, sem.at[1])

        def per_expert(e, carry, wait):
            base, n = pad_start_ref[e], pad_count_ref[e]
            for bit in reversed(range(rb.bit_length() - 1)):
                size = 1 << bit
                offset = (n >> (bit + 1)) << (bit + 1)

                @pl.when(((n >> bit) & 1) == 1)
                def _():
                    copy = pad_rows(base + offset, size)
                    copy.wait() if wait else copy.start()
            return carry

        def start_tail(b, carry):
            tail_block(b).start()
            return carry

        def wait_tail(b, carry):
            tail_block(b).wait()
            return carry

        lax.fori_loop(0, N_EXPERTS, functools.partial(per_expert, wait=False), 0)
        lax.fori_loop(nused, n_blocks, start_tail, 0)
        lax.fori_loop(0, N_EXPERTS, functools.partial(per_expert, wait=True), 0)
        lax.fori_loop(nused, n_blocks, wait_tail, 0)

    for j in range(tt):
        for k in range(TOP_K):
            c = j * TOP_K + k
            dst = xs_ref.at[pl.ds(pl.multiple_of(d_ref[c] * pc, pc), pc), :]
            pltpu.make_async_copy(h_ref.at[pl.ds(j * pc, pc), :], dst, sem.at[0]).start(priority=c % 2)
    tile_rows = xs_ref.at[pl.ds(0, n_copies * pc), :]
    pltpu.make_async_copy(tile_rows, tile_rows, sem.at[0]).wait()


def dispatch(h2p, dest_tok_major, pad_start, pad_count, nused, n_rows):
    tt = DISPATCH_TILE
    pc = PACK_CHUNKS
    t_tokens = h2p.shape[0] // pc
    n_copies = tt * TOP_K
    grid_spec = pltpu.PrefetchScalarGridSpec(
        num_scalar_prefetch=3,
        grid=(t_tokens // tt,),
        in_specs=[pl.BlockSpec((n_copies,), lambda i, ps, pn, nu: (i,), memory_space=pltpu.SMEM),
                  pl.BlockSpec((tt * pc, PACK_LANES), lambda i, ps, pn, nu: (i, 0))],
        out_specs=pl.BlockSpec(memory_space=pl.ANY),
        scratch_shapes=[pltpu.VMEM((MOE_ROW_BLOCK * pc, PACK_LANES), U32),
                        pltpu.SemaphoreType.DMA((2,))],
    )
    return pl.pallas_call(
        _dispatch_kernel,
        grid_spec=grid_spec,
        out_shape=jax.ShapeDtypeStruct((n_rows * pc, PACK_LANES), U32),
        compiler_params=_params(("arbitrary",)),
        name="dispatch",
    )(pad_start, pad_count, nused, dest_tok_major, h2p)


def _experts_kernel(be_ref, nused_ref, ord_ref, next_ref, x_ref, wg_hbm, wu_hbm, wd_hbm, y_ref,
                    wg_stage, wu_stage, wd_stage, wg_bf, wu_bf, wd_bf, sem):
    i = pl.program_id(0)
    nused = nused_ref[0]

    def weight_copies(e, slot):
        return (pltpu.make_async_copy(wg_hbm.at[e], wg_stage.at[slot], sem.at[slot]),
                pltpu.make_async_copy(wu_hbm.at[e], wu_stage.at[slot], sem.at[slot]),
                pltpu.make_async_copy(wd_hbm.at[e], wd_stage.at[slot], sem.at[slot]))

    @pl.when(i < nused)
    def _():
        @pl.when((i == 0) | (be_ref[i] != be_ref[jnp.maximum(i - 1, 0)]))
        def _():
            e = be_ref[i]
            slot = ord_ref[i] % 2

            @pl.when(i == 0)
            def _():
                for c in weight_copies(e, slot):
                    c.start()

            for c in weight_copies(e, slot):
                c.wait()
            wg_bf[...] = wg_stage[slot].astype(BF16)
            wu_bf[...] = wu_stage[slot].astype(BF16)
            wd_bf[...] = wd_stage[slot].astype(BF16)

            @pl.when(next_ref[i] < N_EXPERTS)
            def _():
                for c in weight_copies(next_ref[i], 1 - slot):
                    c.start()

        lo, hi = _load_packed_rows(x_ref, 0, MOE_ROW_BLOCK)
        xb = jnp.concatenate([lo.astype(BF16), hi.astype(BF16)], axis=1)
        act = (_silu(jnp.dot(xb, wg_bf[...], preferred_element_type=F32))
               * jnp.dot(xb, wu_bf[...], preferred_element_type=F32)).astype(BF16)
        _store_packed_rows(y_ref, jnp.dot(act, wd_bf[...], preferred_element_type=F32))

    @pl.when(i >= nused)
    def _():
        y_ref[...] = jnp.zeros(y_ref.shape, U32)


def routed_experts(x_sorted, block_e, nused, block_ord, block_next, w_gate, w_up, w_down):
    rb = MOE_ROW_BLOCK
    pc = PACK_CHUNKS
    n_blocks = x_sorted.shape[0] // (rb * pc)
    d, ff = w_gate.shape[1], w_gate.shape[2]
    hbm = pl.BlockSpec(memory_space=pl.ANY)
    grid_spec = pltpu.PrefetchScalarGridSpec(
        num_scalar_prefetch=4,
        grid=(n_blocks,),
        in_specs=[pl.BlockSpec((rb * pc, PACK_LANES),
                               lambda i, be, nu, od, nx: (jnp.minimum(i, jnp.maximum(nu[0] - 1, 0)), 0)),
                  hbm, hbm, hbm],
        out_specs=pl.BlockSpec((rb * pc, PACK_LANES), lambda i, be, nu, od, nx: (i, 0)),
        scratch_shapes=[pltpu.VMEM((2, d, ff), F32), pltpu.VMEM((2, d, ff), F32),
                        pltpu.VMEM((2, ff, d), F32),
                        pltpu.VMEM((d, ff), BF16), pltpu.VMEM((d, ff), BF16),
                        pltpu.VMEM((ff, d), BF16), pltpu.SemaphoreType.DMA((2,))],
    )
    return pl.pallas_call(
        _experts_kernel,
        grid_spec=grid_spec,
        out_shape=jax.ShapeDtypeStruct(x_sorted.shape, U32),
        compiler_params=_params(("arbitrary",)),
        name="routed_experts",
    )(block_e, nused, block_ord, block_next, x_sorted, w_gate, w_up, w_down)


COMBINE_TILE = 128


def _combine_kernel(d_cur_ref, d_next_ref, y_ref, w_ref, base_ref, g2_ref, lng_ref, lnb_ref,
                    o_ref, ybuf_a, ybuf_b, sem):
    i = pl.program_id(0)
    n_steps = pl.num_programs(0)
    tc = COMBINE_TILE
    n_copies = TOP_K * tc
    pc = PACK_CHUNKS

    def row_copy(row, buf, r, s):
        src = y_ref.at[pl.ds(pl.multiple_of(row * pc, pc), pc), :]
        return pltpu.make_async_copy(src, buf.at[pl.ds(r * pc, pc), :], sem.at[s])

    def wait_rows(buf, s):
        pltpu.make_async_copy(y_ref.at[pl.ds(0, n_copies * pc), :], buf, sem.at[s]).wait()

    @pl.when(i == 0)
    def _():
        def body(r, carry):
            src = y_ref.at[pl.ds(pl.multiple_of(d_cur_ref[r] * pc, pc), pc), :]
            dst = ybuf_a.at[pl.ds(pl.multiple_of(r * pc, pc), pc), :]
            pltpu.make_async_copy(src, dst, sem.at[0]).start()
            return carry
        lax.fori_loop(0, n_copies, body, 0, unroll=8)

    def step(cur, cur_s, nxt, nxt_s):
        wait_rows(cur, cur_s)
        for r in range(n_copies):
            row_copy(d_next_ref[r], nxt, r, nxt_s).start(priority=r % 2)
        w = w_ref[...]
        lo, hi = _load_packed_rows(cur, 0, tc)
        routed_lo, routed_hi = w[:, 0:1] * lo, w[:, 0:1] * hi
        for k in range(1, TOP_K):
            lo, hi = _load_packed_rows(cur, k * tc, tc)
            routed_lo = routed_lo + w[:, k:k + 1] * lo
            routed_hi = routed_hi + w[:, k:k + 1] * hi
        routed = jnp.concatenate([routed_lo, routed_hi], axis=1)
        y = base_ref[...] + (1.0 + g2_ref[0]) * routed
        o_ref[...] = _layer_norm(y, lng_ref[...], lnb_ref[...])

    @pl.when(i % 2 == 0)
    def _():
        step(ybuf_a, 0, ybuf_b, 1)

    @pl.when(i % 2 == 1)
    def _():
        step(ybuf_b, 1, ybuf_a, 0)

    @pl.when((i == n_steps - 1) & (i % 2 == 0))
    def _():
        wait_rows(ybuf_b, 1)

    @pl.when((i == n_steps - 1) & (i % 2 == 1))
    def _():
        wait_rows(ybuf_a, 0)


def combine(y_sorted, dest_tiles, w_tok, base, g2, ln_g, ln_b, seq):
    t_tokens, d = base.shape
    tc = COMBINE_TILE
    n_steps = t_tokens // tc
    tiles_per_seq = seq // tc
    n_copies = TOP_K * tc

    def const(shape):
        return pl.BlockSpec(shape, lambda i: (0,) * len(shape))

    return pl.pallas_call(
        _combine_kernel,
        grid=(n_steps,),
        in_specs=[pl.BlockSpec((n_copies,), lambda i: (i,), memory_space=pltpu.SMEM),
                  pl.BlockSpec((n_copies,), lambda i: (jnp.minimum(i + 1, n_steps - 1),),
                               memory_space=pltpu.SMEM),
                  pl.BlockSpec(memory_space=pl.ANY),
                  pl.BlockSpec((tc, TOP_K), lambda i: (i, 0)),
                  pl.BlockSpec((tc, d), lambda i: (i, 0)),
                  pl.BlockSpec((1, 1, d), lambda i: (i // tiles_per_seq, 0, 0)),
                  const((1, d)), const((1, d))],
        out_specs=pl.BlockSpec((tc, d), lambda i: (i, 0)),
        out_shape=jax.ShapeDtypeStruct((t_tokens, d), F32),
        scratch_shapes=[pltpu.VMEM((n_copies * PACK_CHUNKS, PACK_LANES), U32),
                        pltpu.VMEM((n_copies * PACK_CHUNKS, PACK_LANES), U32),
                        pltpu.SemaphoreType.DMA((2,))],
        compiler_params=_params(("arbitrary",)),
        name="combine",
    )(dest_tiles, dest_tiles, y_sorted, w_tok, base, g2, ln_g.reshape(1, d), ln_b.reshape(1, d))


def _dispatch_tables(e_t, pos_t, counts, t_tokens):
    rb = MOE_ROW_BLOCK
    n_blocks = t_tokens * TOP_K // rb + N_EXPERTS
    padded = (counts + rb - 1) // rb * rb
    pends = jnp.cumsum(padded)
    pstarts = pends - padded
    experts = jnp.arange(N_EXPERTS, dtype=I32)
    pstart_of = jnp.sum(jnp.where(e_t[..., None] == experts, pstarts.astype(I32), 0), axis=-1)
    dest = (pstart_of + pos_t).astype(I32)
    nused = (pends[-1] // rb).astype(I32)
    blk = jnp.arange(n_blocks, dtype=I32)
    last_blk = jnp.minimum(blk, jnp.maximum(nused - 1, 0))
    block_e = jnp.sum((pends[None, :] <= (last_blk * rb)[:, None]).astype(I32), axis=1)
    block_e = jnp.minimum(block_e, N_EXPERTS - 1).astype(I32)
    first = (blk < nused) & ((blk == 0) | (block_e != jnp.roll(block_e, 1)))
    block_ord = (jnp.cumsum(first.astype(I32)) - 1).astype(I32)
    later = (experts[None, :] > experts[:, None]) & (padded[None, :] > 0)
    next_of = jnp.min(jnp.where(later, experts[None, :], N_EXPERTS), axis=1)
    block_next = jnp.sum(jnp.where(block_e[:, None] == experts[None, :], next_of[None, :], 0),
                         axis=1).astype(I32)
    pad_start = (pstarts + counts).astype(I32)
    pad_count = (padded - counts).astype(I32)
    return (dest, block_e, nused.reshape(1), block_ord, block_next, pad_start, pad_count,
            n_blocks * rb)


def _layer(x, c, w_ada, b_ada, w_in, g_cq, w_uq, g_ckv, w_ukv, w_o, ln1_g, ln1_b,
           w_router, router_bias, w_gate, w_up, w_down, ws_gate, ws_up, ws_down, ln2_g, ln2_b):
    bsz, seq, d = x.shape
    t_tokens = bsz * seq
    x2d = x.reshape(t_tokens, d)

    mod = ada_modulation(c, w_ada, b_ada)
    sh1, sc1, g1, sh2, sc2, g2 = [m.reshape(bsz, 1, d) for m in jnp.split(mod, 6, axis=-1)]

    qa, ka, va, qm, kn, vm, kr = input_projection(x2d, sc1, sh1, w_in, g_cq, w_uq, g_ckv, w_ukv,
                                                  bsz, seq)
    attn_a = moba_attention(qa, ka, va, bsz, seq)
    attn_m = mla_attention(qm, kn, kr, vm, bsz, seq)
    h2, h2p, base = output_projection(attn_a, attn_m, x2d, w_o, g1, sc2, sh2, g2, ln1_g, ln1_b,
                                 ws_gate, ws_up, ws_down, seq)

    e_t, pos_t, w_t, counts = router(h2, w_router, router_bias)
    (dest, block_e, nused, block_ord, block_next, pad_start, pad_count,
     n_rows) = _dispatch_tables(e_t, pos_t, counts, t_tokens)
    x_sorted = dispatch(h2p, dest.T.reshape(-1), pad_start, pad_count, nused, n_rows)
    y_sorted = routed_experts(x_sorted, block_e, nused, block_ord, block_next, w_gate, w_up, w_down)

    tc = COMBINE_TILE
    dest_tiles = dest.reshape(TOP_K, t_tokens // tc, tc).transpose(1, 0, 2).reshape(-1)
    out = combine(y_sorted, dest_tiles, w_t.T, base, g2, ln2_g, ln2_b, seq)
    return out.reshape(bsz, seq, d)


def kernel(x, c, w_ada, b_ada, w_in, g_cq, w_uq, g_ckv, w_ukv, w_o, ln1_g, ln1_b, w_router,
           router_bias, w_gate, w_up, w_down, ws_gate, ws_up, ws_down, ln2_g, ln2_b):
    depth = w_ada.shape[0]
    assert depth == 1, "DeepNorm constants are baked for a single layer"
    return _layer(x, c, w_ada[0], b_ada[0], w_in[0], g_cq[0], w_uq[0], g_ckv[0], w_ukv[0], w_o[0],
                  ln1_g[0], ln1_b[0], w_router[0], router_bias[0], w_gate[0], w_up[0], w_down[0],
                  ws_gate[0], ws_up[0], ws_down[0], ln2_g[0], ln2_b[0])
```

```python
import functools

import jax
import jax.numpy as jnp
import numpy as np
from jax import lax
from jax.experimental import pallas as pl
from jax.experimental.pallas import tpu as pltpu

F32 = jnp.float32
BF16 = jnp.bfloat16
I32 = jnp.int32

HEAD_DIM = 128
N_HEADS = 8
MOBA_BLOCK = 256
MOBA_TOPK = 3
MLA_ROPE_DIM = 64
MLA_RANK = 512
MLA_QK_DIM = HEAD_DIM + MLA_ROPE_DIM
GROUP_WIDTH = N_HEADS * HEAD_DIM
ROPE_THETA = 10000.0
N_EXPERTS = 64
TOP_K = 8
N_GROUPS = 8
TOPK_GROUPS = 4
ROUTED_SCALE = 2.5
MOE_ROW_BLOCK = 256
LN_EPS = 1e-5
RMS_EPS = 1e-6
DEEPNORM_ALPHA = 2.0 ** 0.25

VMEM_LIMIT_BYTES = 56 * 1024 * 1024

NT_DIMS = (((1,), (1,)), ((), ()))


def _params(semantics):
    return pltpu.CompilerParams(dimension_semantics=semantics,
                                vmem_limit_bytes=VMEM_LIMIT_BYTES)


def _silu(x):
    return x * jax.nn.sigmoid(x)


PACK_CHUNKS = 8
PACK_LANES = 128
U32 = jnp.uint32
HIGH_HALF = np.uint32(0xFFFF0000)


def _store_packed_rows(ref, x):
    n_rows, half = x.shape[0], x.shape[1] // 2
    lo = lax.bitcast_convert_type(x[:, :half].astype(BF16).astype(F32), U32) >> 16
    hi = lax.bitcast_convert_type(x[:, half:].astype(BF16).astype(F32), U32) & HIGH_HALF
    packed = lo | hi
    for j in range(PACK_CHUNKS):
        ref[pl.ds(j, n_rows, stride=PACK_CHUNKS), :] = packed[:, j * PACK_LANES:(j + 1) * PACK_LANES]


def _load_packed_rows(ref, first_row, n_rows):
    lo, hi = [], []
    for j in range(PACK_CHUNKS):
        u = ref[pl.ds(first_row * PACK_CHUNKS + j, n_rows, stride=PACK_CHUNKS), :]
        lo.append(lax.bitcast_convert_type(u << 16, F32))
        hi.append(lax.bitcast_convert_type(u & HIGH_HALF, F32))
    return jnp.concatenate(lo, axis=1), jnp.concatenate(hi, axis=1)


def _ada_kernel(c_ref, w_ref, b_ref, o_ref):
    a = _silu(c_ref[...]).astype(BF16)
    o_ref[...] = jnp.dot(a, w_ref[...].astype(BF16), preferred_element_type=F32) + b_ref[...]


def ada_modulation(c, w_ada, b_ada):
    bsz, d = c.shape
    rows = -(-bsz // 16) * 16
    c_pad = jnp.pad(c, ((0, rows - bsz), (0, 0)))
    n = w_ada.shape[1]
    tn = 1024
    out = pl.pallas_call(
        _ada_kernel,
        grid=(n // tn,),
        in_specs=[pl.BlockSpec((rows, d), lambda j: (0, 0)),
                  pl.BlockSpec((d, tn), lambda j: (0, j)),
                  pl.BlockSpec((1, tn), lambda j: (0, j))],
        out_specs=pl.BlockSpec((rows, tn), lambda j: (0, j)),
        out_shape=jax.ShapeDtypeStruct((rows, n), F32),
        compiler_params=_params(("arbitrary",)),
        name="ada_modulation",
    )(c_pad, w_ada, b_ada.reshape(1, n))
    return out[:bsz]


def _rms(x, g):
    return x * lax.rsqrt(jnp.mean(x * x, axis=-1, keepdims=True) + RMS_EPS) * g


def _rope64(x, cos, sin):
    return x * cos + pltpu.roll(x, 64, axis=1) * sin


def _rope32(x, cos, s1, s2):
    return x * cos + pltpu.roll(x, 96, axis=1) * s1 + pltpu.roll(x, 32, axis=1) * s2


def _inproj_kernel(x_ref, sc_ref, sh_ref, w_ref, wkr_ref, wuq_ref, wuk_ref, wuv_ref, gcq_ref, gckv_ref,
                   cosa_ref, sina_ref, cosr_ref, s1_ref, s2_ref,
                   qa_ref, ka_ref, va_ref, qm_ref, kn_ref, vm_ref, kr_ref):
    h = (x_ref[...] * (1.0 + sc_ref[0]) + sh_ref[0]).astype(BF16)
    cosa, sina = cosa_ref[...], sina_ref[...]
    cosr, s1, s2 = cosr_ref[...], s1_ref[...], s2_ref[...]
    gw = GROUP_WIDTH

    def proj(lo, hi):
        return jnp.dot(h, w_ref[:, lo:hi], preferred_element_type=F32)

    for out_ref, base in ((qa_ref, 0), (ka_ref, gw)):
        t = proj(base, base + gw)
        for hd in range(N_HEADS):
            sl = slice(hd * HEAD_DIM, (hd + 1) * HEAD_DIM)
            out_ref[:, sl] = _rope64(t[:, sl], cosa, sina).astype(BF16)
    va_ref[...] = proj(2 * gw, 3 * gw).T.astype(BF16)

    cq = _rms(proj(3 * gw, 3 * gw + MLA_RANK), gcq_ref[...]).astype(BF16)
    qm = jnp.dot(cq, wuq_ref[...], preferred_element_type=F32)
    for hd in range(N_HEADS):
        b0 = hd * 2 * HEAD_DIM
        qm_ref[:, b0:b0 + HEAD_DIM] = qm[:, b0:b0 + HEAD_DIM].astype(BF16)
        qm_ref[:, b0 + HEAD_DIM:b0 + 2 * HEAD_DIM] = _rope32(
            qm[:, b0 + HEAD_DIM:b0 + 2 * HEAD_DIM], cosr, s1, s2).astype(BF16)

    ckv = _rms(proj(3 * gw + MLA_RANK, 3 * gw + 2 * MLA_RANK), gckv_ref[...]).astype(BF16)
    kn_ref[...] = jnp.dot(ckv, wuk_ref[...], preferred_element_type=F32).astype(BF16)
    vm_ref[...] = jnp.dot(ckv, wuv_ref[...], preferred_element_type=F32).T.astype(BF16)

    kr = jnp.dot(h, wkr_ref[...], preferred_element_type=F32)
    kr_ref[...] = _rope32(kr, cosr, s1, s2).astype(BF16)


def _rope_tables(seq):
    pos = jnp.arange(seq, dtype=F32)[:, None]
    inv_a = 1.0 / (ROPE_THETA ** (jnp.arange(0, HEAD_DIM, 2, dtype=F32) / HEAD_DIM))
    ang_a = pos * inv_a[None, :]
    cosa = jnp.concatenate([jnp.cos(ang_a), jnp.cos(ang_a)], axis=1)
    sina = jnp.concatenate([-jnp.sin(ang_a), jnp.sin(ang_a)], axis=1)
    inv_r = 1.0 / (ROPE_THETA ** (jnp.arange(0, MLA_ROPE_DIM, 2, dtype=F32) / MLA_ROPE_DIM))
    ang_r = pos * inv_r[None, :]
    cr, sr = jnp.cos(ang_r), jnp.sin(ang_r)
    z32 = jnp.zeros_like(cr)
    z64 = jnp.zeros((seq, 64), F32)
    cosr = jnp.concatenate([cr, cr, z64], axis=1)
    s1 = jnp.concatenate([-sr, z32, z64], axis=1)
    s2 = jnp.concatenate([z32, sr, z64], axis=1)
    return cosa, sina, cosr, s1, s2


def input_projection(x2d, sc1, sh1, w_in, g_cq, w_uq, g_ckv, w_ukv, seq):
    t_tokens, d = x2d.shape
    tm = 256
    tiles_per_seq = seq // tm
    gw = GROUP_WIDTH
    main_w = 3 * gw + 2 * MLA_RANK
    w_main = w_in[:, :main_w].astype(BF16)
    w_kr = jnp.pad(w_in[:, main_w:], ((0, 0), (0, HEAD_DIM - MLA_ROPE_DIM))).astype(BF16)
    wuq = jnp.pad(w_uq.reshape(MLA_RANK, N_HEADS, MLA_QK_DIM),
                  ((0, 0), (0, 0), (0, 2 * HEAD_DIM - MLA_QK_DIM)))
    wuq = wuq.reshape(MLA_RANK, N_HEADS * 2 * HEAD_DIM).astype(BF16)
    wukv = w_ukv.reshape(MLA_RANK, N_HEADS, 2 * HEAD_DIM)
    wuk = wukv[:, :, :HEAD_DIM].reshape(MLA_RANK, gw).astype(BF16)
    wuv = wukv[:, :, HEAD_DIM:].reshape(MLA_RANK, gw).astype(BF16)
    tables = _rope_tables(seq)

    def const(shape):
        return pl.BlockSpec(shape, lambda i: (0,) * len(shape), pipeline_mode=pl.Buffered(1))

    def rows(width):
        return pl.BlockSpec((tm, width), lambda i: (i, 0))

    mod_spec = pl.BlockSpec((1, 1, d), lambda i: (i // tiles_per_seq, 0, 0))
    tab_spec = pl.BlockSpec((tm, HEAD_DIM), lambda i: (i % tiles_per_seq, 0))
    widths = (gw, gw, gw, 2 * gw, gw, gw, HEAD_DIM)
    transposed = (False, False, True, False, False, True, False)

    def out_spec(width, is_t):
        return pl.BlockSpec((width, tm), lambda i: (0, i)) if is_t else rows(width)

    def out_shape(width, is_t):
        return jax.ShapeDtypeStruct((width, t_tokens) if is_t else (t_tokens, width), BF16)

    return pl.pallas_call(
        _inproj_kernel,
        grid=(t_tokens // tm,),
        in_specs=[rows(d), mod_spec, mod_spec, const((d, main_w)), const((d, HEAD_DIM)),
                  const((MLA_RANK, 2 * gw)), const((MLA_RANK, gw)), const((MLA_RANK, gw)),
                  const((1, MLA_RANK)), const((1, MLA_RANK))] + [tab_spec] * 5,
        out_specs=[out_spec(w, t) for w, t in zip(widths, transposed)],
        out_shape=[out_shape(w, t) for w, t in zip(widths, transposed)],
        compiler_params=_params(("arbitrary",)),
        name="input_projection",
    )(x2d, sc1, sh1, w_main, w_kr, wuq, wuk, wuv, g_cq.reshape(1, -1), g_ckv.reshape(1, -1), *tables)


ATTN_BLOCK = 256
LOG2_E = 1.4426950408889634


def _causal_mask_t():
    key = lax.broadcasted_iota(I32, (ATTN_BLOCK, ATTN_BLOCK), 0)
    qry = lax.broadcasted_iota(I32, (ATTN_BLOCK, ATTN_BLOCK), 1)
    return key <= qry


def _scores_t(q, k_blk):
    return lax.dot_general(k_blk, q, NT_DIMS, preferred_element_type=F32)


def _first_block(s, vt_blk, c, acc_ref):
    s = jnp.where(_causal_mask_t(), s, -jnp.inf)
    m = jnp.max(s, axis=0, keepdims=True)
    p = jnp.exp2((s - m) * c)
    acc_ref[...] = jnp.dot(vt_blk, p.astype(BF16), preferred_element_type=F32)
    return m, jnp.sum(p, axis=0, keepdims=True)


def _next_block(s, vt_blk, bias, c, m, l, acc_ref):
    m_blk = jnp.max(s, axis=0, keepdims=True)
    if bias is not None:
        m_blk = m_blk + bias
    m_new = jnp.maximum(m, m_blk)
    alpha = jnp.exp2((m - m_new) * c)
    p = jnp.exp2((s - (m_new if bias is None else m_new - bias)) * c)
    acc_ref[...] = alpha * acc_ref[...] + jnp.dot(vt_blk, p.astype(BF16), preferred_element_type=F32)
    return m_new, alpha * l + jnp.sum(p, axis=0, keepdims=True)


HEADS_PER_STEP = 8


def _head_cols(g, width=HEAD_DIM):
    return slice(g * width, (g + 1) * width)


def _moba_block_bias(kmean, q, qi, n_blocks):
    gate = lax.dot_general(kmean, q.astype(F32), NT_DIMS, precision=lax.Precision.HIGHEST,
                           preferred_element_type=F32)
    row = lax.broadcasted_iota(I32, gate.shape, 0)
    gt = jnp.where(row < qi, gate, -jnp.inf)
    cnt = jnp.zeros(gate.shape, F32)
    for m in range(n_blocks - 1):
        gm = gt[m:m + 1, :]
        cnt = cnt + jnp.where((gm > gt) | ((gm == gt) & (m < row)), 1.0, 0.0)
    return jnp.where(cnt < float(MOBA_TOPK), 0.0, -jnp.inf).astype(F32)


def _moba_kernel(q_ref, k_ref, vt_ref, o_ref, kmean_ref, acc_ref, *, n_blocks):
    qi = pl.program_id(2)
    blk = MOBA_BLOCK
    heads = HEADS_PER_STEP

    @pl.when(qi == 0)
    def _():
        kmean_ref[...] = jnp.zeros(kmean_ref.shape, F32)
        for g in range(heads):
            for n in range(n_blocks):
                kb = k_ref[0, n * blk:(n + 1) * blk, _head_cols(g)].astype(F32)
                kmean_ref[g, n:n + 1, :] = jnp.sum(kb, axis=0, keepdims=True) * (1.0 / blk)

    c = (HEAD_DIM ** -0.5) * LOG2_E
    own = pl.multiple_of(qi * blk, blk)
    qs = [q_ref[0, :, _head_cols(g)] for g in range(heads)]
    scores = [_scores_t(qs[g], k_ref[0, pl.ds(own, blk), _head_cols(g)]) for g in range(heads)]
    biases = [_moba_block_bias(kmean_ref[g], qs[g], qi, n_blocks) for g in range(heads)]
    carry0 = []
    for g in range(heads):
        carry0 += _first_block(scores[g], vt_ref[_head_cols(g), pl.ds(own, blk)], c, acc_ref.at[g])

    def body(n, carry):
        off = pl.multiple_of(n * blk, blk)
        scores = [_scores_t(qs[g], k_ref[0, pl.ds(off, blk), _head_cols(g)]) for g in range(heads)]
        row = lax.broadcasted_iota(I32, biases[0].shape, 0)
        out = []
        for g in range(heads):
            bias = jnp.sum(jnp.where(row == n, biases[g], 0.0), axis=0, keepdims=True)
            out += _next_block(scores[g], vt_ref[_head_cols(g), pl.ds(off, blk)], bias,
                               c, carry[2 * g], carry[2 * g + 1], acc_ref.at[g])
        return tuple(out)

    final = lax.fori_loop(0, qi, body, tuple(carry0))
    for g in range(heads):
        o_ref[0, :, _head_cols(g)] = (acc_ref[g] / final[2 * g + 1]).T.astype(o_ref.dtype)


def moba_attention(qa, ka, vat, bsz, seq):
    blk = MOBA_BLOCK
    n_blocks = seq // blk
    heads = HEADS_PER_STEP
    gate_rows = -(-n_blocks // 8) * 8
    q3, k3 = (t.reshape(bsz, seq, GROUP_WIDTH) for t in (qa, ka))
    q_spec = pl.BlockSpec((1, blk, heads * HEAD_DIM), lambda b, h, i: (b, i, h))
    kv_spec = pl.BlockSpec((1, seq, heads * HEAD_DIM), lambda b, h, i: (b, 0, h))
    out = pl.pallas_call(
        functools.partial(_moba_kernel, n_blocks=n_blocks),
        grid=(bsz, N_HEADS // heads, n_blocks),
        in_specs=[q_spec, kv_spec, pl.BlockSpec((heads * HEAD_DIM, seq), lambda b, h, i: (h, b))],
        out_specs=q_spec,
        out_shape=jax.ShapeDtypeStruct((bsz, seq, GROUP_WIDTH), BF16),
        scratch_shapes=[pltpu.VMEM((heads, gate_rows, HEAD_DIM), F32),
                        pltpu.VMEM((heads, HEAD_DIM, blk), F32)],
        compiler_params=_params(("arbitrary", "arbitrary", "arbitrary")),
        name="moba_attention",
    )(q3, k3, vat)
    return out.reshape(bsz * seq, GROUP_WIDTH)


def _mla_kernel(q_ref, kn_ref, kr_ref, vt_ref, o_ref, kcat_ref, acc_ref):
    qi = pl.program_id(2)
    blk = ATTN_BLOCK

    heads = HEADS_PER_STEP

    @pl.when(qi == 0)
    def _():
        for g in range(heads):
            kcat_ref[g, :, :HEAD_DIM] = kn_ref[0, :, _head_cols(g)]
            kcat_ref[g, :, HEAD_DIM:] = kr_ref[0]

    c = (MLA_QK_DIM ** -0.5) * LOG2_E
    own = pl.multiple_of(qi * blk, blk)
    qs = [q_ref[0, :, _head_cols(g, 2 * HEAD_DIM)] for g in range(heads)]
    scores = [_scores_t(qs[g], kcat_ref[g, pl.ds(own, blk), :]) for g in range(heads)]
    carry0 = []
    for g in range(heads):
        carry0 += _first_block(scores[g], vt_ref[_head_cols(g), pl.ds(own, blk)], c, acc_ref.at[g])

    def body(n, carry):
        off = pl.multiple_of(n * blk, blk)
        scores = [_scores_t(qs[g], kcat_ref[g, pl.ds(off, blk), :]) for g in range(heads)]
        out = []
        for g in range(heads):
            out += _next_block(scores[g], vt_ref[_head_cols(g), pl.ds(off, blk)], None, c,
                               carry[2 * g], carry[2 * g + 1], acc_ref.at[g])
        return tuple(out)

    final = lax.fori_loop(0, qi, body, tuple(carry0))
    for g in range(heads):
        o_ref[0, :, _head_cols(g)] = (acc_ref[g] / final[2 * g + 1]).T.astype(o_ref.dtype)


def mla_attention(qm, kn, kr, vmt, bsz, seq):
    blk = ATTN_BLOCK
    heads = HEADS_PER_STEP
    q3 = qm.reshape(bsz, seq, 2 * GROUP_WIDTH)
    kn3 = kn.reshape(bsz, seq, GROUP_WIDTH)
    kr3 = kr.reshape(bsz, seq, HEAD_DIM)
    out = pl.pallas_call(
        _mla_kernel,
        grid=(bsz, N_HEADS // heads, seq // blk),
        in_specs=[pl.BlockSpec((1, blk, heads * 2 * HEAD_DIM), lambda b, h, i: (b, i, h)),
                  pl.BlockSpec((1, seq, heads * HEAD_DIM), lambda b, h, i: (b, 0, h)),
                  pl.BlockSpec((1, seq, HEAD_DIM), lambda b, h, i: (b, 0, 0)),
                  pl.BlockSpec((heads * HEAD_DIM, seq), lambda b, h, i: (h, b))],
        out_specs=pl.BlockSpec((1, blk, heads * HEAD_DIM), lambda b, h, i: (b, i, h)),
        out_shape=jax.ShapeDtypeStruct((bsz, seq, GROUP_WIDTH), BF16),
        scratch_shapes=[pltpu.VMEM((heads, seq, 2 * HEAD_DIM), BF16),
                        pltpu.VMEM((heads, HEAD_DIM, blk), F32)],
        compiler_params=_params(("arbitrary", "arbitrary", "arbitrary")),
        name="mla_attention",
    )(q3, kn3, kr3, vmt)
    return out.reshape(bsz * seq, GROUP_WIDTH)


def _layer_norm(y, g, b):
    mu = jnp.mean(y, axis=-1, keepdims=True)
    yc = y - mu
    var = jnp.mean(yc * yc, axis=-1, keepdims=True)
    return yc * lax.rsqrt(var + LN_EPS) * g + b


def _outproj_kernel(a_ref, m_ref, x_ref, woa_ref, wom_ref, g1_ref, sc2_ref, sh2_ref, g2_ref,
                    lng_ref, lnb_ref, wsg_ref, wsu_ref, wsd_ref, h2_ref, h2p_ref, base_ref):
    f = (jnp.dot(a_ref[...], woa_ref[...], preferred_element_type=F32)
         + jnp.dot(m_ref[...], wom_ref[...], preferred_element_type=F32))
    x1 = _layer_norm(DEEPNORM_ALPHA * x_ref[...] + (1.0 + g1_ref[0]) * f, lng_ref[...], lnb_ref[...])
    h2 = x1 * (1.0 + sc2_ref[0]) + sh2_ref[0]
    h2_ref[...] = h2
    _store_packed_rows(h2p_ref, h2)
    hb = h2.astype(BF16)
    act = (_silu(jnp.dot(hb, wsg_ref[...], preferred_element_type=F32))
           * jnp.dot(hb, wsu_ref[...], preferred_element_type=F32)).astype(BF16)
    shared = jnp.dot(act, wsd_ref[...], preferred_element_type=F32)
    base_ref[...] = DEEPNORM_ALPHA * x1 + (1.0 + g2_ref[0]) * shared


def output_projection(attn_a, attn_m, x2d, w_o, g1, sc2, sh2, g2, ln_g, ln_b,
                      ws_gate, ws_up, ws_down, seq):
    t_tokens, d = x2d.shape
    tm = 256
    tiles_per_seq = seq // tm
    gw = GROUP_WIDTH
    ff = ws_gate.shape[1]

    def const(shape):
        return pl.BlockSpec(shape, lambda i: (0,) * len(shape), pipeline_mode=pl.Buffered(1))

    def rows(width):
        return pl.BlockSpec((tm, width), lambda i: (i, 0))

    mod_spec = pl.BlockSpec((1, 1, d), lambda i: (i // tiles_per_seq, 0, 0))
    return pl.pallas_call(
        _outproj_kernel,
        grid=(t_tokens // tm,),
        in_specs=[rows(gw), rows(gw), rows(d), const((gw, d)), const((gw, d)),
                  mod_spec, mod_spec, mod_spec, mod_spec, const((1, d)), const((1, d)),
                  const((d, ff)), const((d, ff)), const((ff, d))],
        out_specs=[rows(d), pl.BlockSpec((tm * PACK_CHUNKS, PACK_LANES), lambda i: (i, 0)), rows(d)],
        out_shape=[jax.ShapeDtypeStruct((t_tokens, d), F32),
                   jax.ShapeDtypeStruct((t_tokens * PACK_CHUNKS, PACK_LANES), U32),
                   jax.ShapeDtypeStruct((t_tokens, d), F32)],
        compiler_params=_params(("arbitrary",)),
        name="output_projection",
    )(attn_a, attn_m, x2d, w_o[:gw].astype(BF16), w_o[gw:].astype(BF16), g1, sc2, sh2, g2,
      ln_g.reshape(1, d), ln_b.reshape(1, d),
      ws_gate.astype(BF16), ws_up.astype(BF16), ws_down.astype(BF16))


ROUTER_TILE = 512


def _router_kernel(h_ref, wrt_ref, bias_ref, tri_ref, ltri_ref,
                   e_ref, pos_ref, w_ref, cnt_ref, run_ref):
    tr = ROUTER_TILE
    epg = N_EXPERTS // N_GROUPS

    @pl.when(pl.program_id(0) == 0)
    def _():
        run_ref[...] = jnp.zeros(run_ref.shape, F32)

    logits = lax.dot_general(wrt_ref[...], h_ref[...], NT_DIMS, precision=lax.Precision.HIGHEST,
                             preferred_element_type=F32)
    scores = jax.nn.sigmoid(logits)
    sel_scores = scores + bias_ref[...]

    sub = lax.broadcasted_iota(I32, (epg, tr), 0)
    grp_rows = []
    for g in range(N_GROUPS):
        xg = sel_scores[g * epg:(g + 1) * epg, :]
        m1 = jnp.max(xg, axis=0, keepdims=True)
        first = jnp.min(jnp.where(xg == m1, sub, epg), axis=0, keepdims=True)
        m2 = jnp.max(jnp.where(sub == first, -jnp.inf, xg), axis=0, keepdims=True)
        grp_rows.append(m1 + m2)
    grp = jnp.concatenate(grp_rows, axis=0)

    giota = lax.broadcasted_iota(I32, (N_GROUPS, tr), 0)
    gcnt = jnp.zeros((N_GROUPS, tr), F32)
    for m in range(N_GROUPS):
        rm = grp[m:m + 1, :]
        gcnt = gcnt + jnp.where((rm > grp) | ((rm == grp) & (m < giota)), 1.0, 0.0)
    gkeep = jnp.where(gcnt < float(TOPK_GROUPS), 1.0, 0.0)
    gmask = jnp.concatenate(
        [jnp.broadcast_to(gkeep[g:g + 1, :], (epg, tr)) for g in range(N_GROUPS)], axis=0)
    masked = jnp.where(gmask > 0.0, sel_scores, -jnp.inf)

    eiota = lax.broadcasted_iota(I32, (N_EXPERTS, tr), 0)

    free = jnp.ones((N_EXPERTS, tr), F32)
    for _ in range(TOP_K):
        top = jnp.max(jnp.where(free > 0.0, masked, -jnp.inf), axis=0, keepdims=True)
        first = jnp.min(jnp.where((masked == top) & (free > 0.0), eiota, N_EXPERTS),
                        axis=0, keepdims=True)
        free = jnp.where(eiota == first, 0.0, free)
    sel = free == 0.0

    sw = jnp.where(sel, scores, 0.0)
    wn = sw / jnp.sum(sw, axis=0, keepdims=True) * ROUTED_SCALE

    selb = jnp.where(sel, 1.0, 0.0).astype(BF16)
    cum = jnp.dot(selb, tri_ref[...], preferred_element_type=F32)
    pos = run_ref[...] + cum - 1.0
    run_ref[...] = run_ref[...] + cum[:, tr - 1:tr]
    cnt_ref[...] = jnp.broadcast_to(run_ref[...], cnt_ref.shape).astype(I32)

    slot = jnp.dot(ltri_ref[...], selb, preferred_element_type=F32)
    ef = eiota.astype(F32)
    e_rows, p_rows, w_rows = [], [], []
    for j in range(TOP_K):
        mj = sel & (slot == float(j))
        e_rows.append(jnp.sum(jnp.where(mj, ef, 0.0), axis=0, keepdims=True))
        p_rows.append(jnp.sum(jnp.where(mj, pos, 0.0), axis=0, keepdims=True))
        w_rows.append(jnp.sum(jnp.where(mj, wn, 0.0), axis=0, keepdims=True))
    e_ref[...] = jnp.concatenate(e_rows, axis=0).astype(I32)
    pos_ref[...] = jnp.concatenate(p_rows, axis=0).astype(I32)
    w_ref[...] = jnp.concatenate(w_rows, axis=0)


def router(h2, w_router, router_bias):
    t_tokens, d = h2.shape
    tr = ROUTER_TILE
    tri = jnp.asarray(np.triu(np.ones((tr, tr), np.float32)), BF16)
    ltri = jnp.asarray(np.tril(np.ones((N_EXPERTS, N_EXPERTS), np.float32), -1), BF16)

    def const(shape):
        return pl.BlockSpec(shape, lambda i: (0,) * len(shape))

    tok = pl.BlockSpec((TOP_K, tr), lambda i: (0, i))
    e_t, pos_t, w_t, cnt = pl.pallas_call(
        _router_kernel,
        grid=(t_tokens // tr,),
        in_specs=[pl.BlockSpec((tr, d), lambda i: (i, 0)), const((N_EXPERTS, d)),
                  const((N_EXPERTS, 1)), const((tr, tr)), const((N_EXPERTS, N_EXPERTS))],
        out_specs=[tok, tok, tok, const((N_EXPERTS, 128))],
        out_shape=[jax.ShapeDtypeStruct((TOP_K, t_tokens), I32),
                   jax.ShapeDtypeStruct((TOP_K, t_tokens), I32),
                   jax.ShapeDtypeStruct((TOP_K, t_tokens), F32),
                   jax.ShapeDtypeStruct((N_EXPERTS, 128), I32)],
        scratch_shapes=[pltpu.VMEM((N_EXPERTS, 1), F32)],
        compiler_params=_params(("arbitrary",)),
        name="router",
    )(h2, w_router.T, router_bias.reshape(N_EXPERTS, 1), tri, ltri)
    return e_t, pos_t, w_t, cnt[:, 0]


DISPATCH_TILE = 128


def _dispatch_kernel(pad_start_ref, pad_count_ref, nused_ref, d_ref, h_ref, xs_ref, zero_ref, sem):
    tt = DISPATCH_TILE
    pc = PACK_CHUNKS
    rb = MOE_ROW_BLOCK
    n_copies = tt * TOP_K
    n_blocks = xs_ref.shape[0] // (rb * pc)

    @pl.when(pl.program_id(0) == 0)
    def _():
        zero_ref[...] = jnp.zeros(zero_ref.shape, U32)
        nused = nused_ref[0]

        def pad_rows(row, n):
            dst = xs_ref.at[pl.ds(pl.multiple_of(row * pc, pc), n * pc), :]
            return pltpu.make_async_copy(zero_ref.at[pl.ds(0, n * pc), :], dst, sem.at[1])

        def tail_block(b):
            dst = xs_ref.at[pl.ds(pl.multiple_of(b * (rb * pc), rb * pc), rb * pc), :]
            return pltpu.make_async_copy(zero_ref, dst, sem.at[1])

        def per_expert(e, carry, wait):
            base, n = pad_start_ref[e], pad_count_ref[e]
            for bit in reversed(range(rb.bit_length() - 1)):
                size = 1 << bit
                offset = (n >> (bit + 1)) << (bit + 1)

                @pl.when(((n >> bit) & 1) == 1)
                def _():
                    copy = pad_rows(base + offset, size)
                    if wait:
                        copy.wait()
                    else:
                        copy.start()
            return carry

        def start_tail(b, carry):
            tail_block(b).start()
            return carry

        def wait_tail(b, carry):
            tail_block(b).wait()
            return carry

        lax.fori_loop(0, N_EXPERTS, functools.partial(per_expert, wait=False), 0)
        lax.fori_loop(nused, n_blocks, start_tail, 0)
        lax.fori_loop(0, N_EXPERTS, functools.partial(per_expert, wait=True), 0)
        lax.fori_loop(nused, n_blocks, wait_tail, 0)

    for j in range(tt):
        for k in range(TOP_K):
            c = j * TOP_K + k
            dst = xs_ref.at[pl.ds(pl.multiple_of(d_ref[c] * pc, pc), pc), :]
            pltpu.make_async_copy(h_ref.at[pl.ds(j * pc, pc), :], dst, sem.at[0]).start(priority=c % 2)
    tile_rows = xs_ref.at[pl.ds(0, n_copies * pc), :]
    pltpu.make_async_copy(tile_rows, tile_rows, sem.at[0]).wait()


def dispatch(h2p, dest_tok_major, pad_start, pad_count, nused, n_rows):
    tt = DISPATCH_TILE
    pc = PACK_CHUNKS
    t_tokens = h2p.shape[0] // pc
    n_copies = tt * TOP_K
    grid_spec = pltpu.PrefetchScalarGridSpec(
        num_scalar_prefetch=3,
        grid=(t_tokens // tt,),
        in_specs=[pl.BlockSpec((n_copies,), lambda i, ps, pn, nu: (i,), memory_space=pltpu.SMEM),
                  pl.BlockSpec((tt * pc, PACK_LANES), lambda i, ps, pn, nu: (i, 0))],
        out_specs=pl.BlockSpec(memory_space=pl.ANY),
        scratch_shapes=[pltpu.VMEM((MOE_ROW_BLOCK * pc, PACK_LANES), U32),
                        pltpu.SemaphoreType.DMA((2,))],
    )
    return pl.pallas_call(
        _dispatch_kernel,
        grid_spec=grid_spec,
        out_shape=jax.ShapeDtypeStruct((n_rows * pc, PACK_LANES), U32),
        compiler_params=_params(("arbitrary",)),
        name="dispatch",
    )(pad_start, pad_count, nused, dest_tok_major, h2p)


def _experts_kernel(be_ref, nused_ref, ord_ref, next_ref, x_ref, wg_hbm, wu_hbm, wd_hbm, y_ref,
                    wg_stage, wu_stage, wd_stage, wg_bf, wu_bf, wd_bf, sem):
    i = pl.program_id(0)
    nused = nused_ref[0]

    def weight_copies(e, slot):
        return (pltpu.make_async_copy(wg_hbm.at[e], wg_stage.at[slot], sem.at[slot]),
                pltpu.make_async_copy(wu_hbm.at[e], wu_stage.at[slot], sem.at[slot]),
                pltpu.make_async_copy(wd_hbm.at[e], wd_stage.at[slot], sem.at[slot]))

    @pl.when(i < nused)
    def _():
        @pl.when((i == 0) | (be_ref[i] != be_ref[jnp.maximum(i - 1, 0)]))
        def _():
            e = be_ref[i]
            slot = ord_ref[i] % 2

            @pl.when(i == 0)
            def _():
                for c in weight_copies(e, slot):
                    c.start()

            for c in weight_copies(e, slot):
                c.wait()
            wg_bf[...] = wg_stage[slot].astype(BF16)
            wu_bf[...] = wu_stage[slot].astype(BF16)
            wd_bf[...] = wd_stage[slot].astype(BF16)

            @pl.when(next_ref[i] < N_EXPERTS)
            def _():
                for c in weight_copies(next_ref[i], 1 - slot):
                    c.start()

        lo, hi = _load_packed_rows(x_ref, 0, MOE_ROW_BLOCK)
        xb = jnp.concatenate([lo.astype(BF16), hi.astype(BF16)], axis=1)
        act = (_silu(jnp.dot(xb, wg_bf[...], preferred_element_type=F32))
               * jnp.dot(xb, wu_bf[...], preferred_element_type=F32)).astype(BF16)
        _store_packed_rows(y_ref, jnp.dot(act, wd_bf[...], preferred_element_type=F32))

    @pl.when(i >= nused)
    def _():
        y_ref[...] = jnp.zeros(y_ref.shape, U32)


def routed_experts(x_sorted, block_e, nused, block_ord, block_next, w_gate, w_up, w_down):
    rb = MOE_ROW_BLOCK
    pc = PACK_CHUNKS
    n_blocks = x_sorted.shape[0] // (rb * pc)
    d, ff = w_gate.shape[1], w_gate.shape[2]
    hbm = pl.BlockSpec(memory_space=pl.ANY)
    grid_spec = pltpu.PrefetchScalarGridSpec(
        num_scalar_prefetch=4,
        grid=(n_blocks,),
        in_specs=[pl.BlockSpec((rb * pc, PACK_LANES),
                               lambda i, be, nu, od, nx: (jnp.minimum(i, jnp.maximum(nu[0] - 1, 0)), 0)),
                  hbm, hbm, hbm],
        out_specs=pl.BlockSpec((rb * pc, PACK_LANES), lambda i, be, nu, od, nx: (i, 0)),
        scratch_shapes=[pltpu.VMEM((2, d, ff), F32), pltpu.VMEM((2, d, ff), F32),
                        pltpu.VMEM((2, ff, d), F32),
                        pltpu.VMEM((d, ff), BF16), pltpu.VMEM((d, ff), BF16),
                        pltpu.VMEM((ff, d), BF16), pltpu.SemaphoreType.DMA((2,))],
    )
    return pl.pallas_call(
        _experts_kernel,
        grid_spec=grid_spec,
        out_shape=jax.ShapeDtypeStruct(x_sorted.shape, U32),
        compiler_params=_params(("arbitrary",)),
        name="routed_experts",
    )(block_e, nused, block_ord, block_next, x_sorted, w_gate, w_up, w_down)


COMBINE_TILE = 128


def _combine_kernel(d_cur_ref, d_next_ref, y_ref, w_ref, base_ref, g2_ref, lng_ref, lnb_ref,
                    o_ref, ybuf_a, ybuf_b, sem):
    i = pl.program_id(0)
    n_steps = pl.num_programs(0)
    tc = COMBINE_TILE
    n_copies = TOP_K * tc
    pc = PACK_CHUNKS

    def row_copy(row, buf, r, s):
        src = y_ref.at[pl.ds(pl.multiple_of(row * pc, pc), pc), :]
        return pltpu.make_async_copy(src, buf.at[pl.ds(r * pc, pc), :], sem.at[s])

    def wait_rows(buf, s):
        pltpu.make_async_copy(y_ref.at[pl.ds(0, n_copies * pc), :], buf, sem.at[s]).wait()

    @pl.when(i == 0)
    def _():
        def body(r, carry):
            src = y_ref.at[pl.ds(pl.multiple_of(d_cur_ref[r] * pc, pc), pc), :]
            dst = ybuf_a.at[pl.ds(pl.multiple_of(r * pc, pc), pc), :]
            pltpu.make_async_copy(src, dst, sem.at[0]).start()
            return carry
        lax.fori_loop(0, n_copies, body, 0, unroll=8)

    def step(cur, cur_s, nxt, nxt_s):
        wait_rows(cur, cur_s)
        for r in range(n_copies):
            row_copy(d_next_ref[r], nxt, r, nxt_s).start(priority=r % 2)
        w = w_ref[...]
        lo, hi = _load_packed_rows(cur, 0, tc)
        routed_lo, routed_hi = w[:, 0:1] * lo, w[:, 0:1] * hi
        for k in range(1, TOP_K):
            lo, hi = _load_packed_rows(cur, k * tc, tc)
            routed_lo = routed_lo + w[:, k:k + 1] * lo
            routed_hi = routed_hi + w[:, k:k + 1] * hi
        routed = jnp.concatenate([routed_lo, routed_hi], axis=1)
        y = base_ref[...] + (1.0 + g2_ref[0]) * routed
        o_ref[...] = _layer_norm(y, lng_ref[...], lnb_ref[...])

    @pl.when(i % 2 == 0)
    def _():
        step(ybuf_a, 0, ybuf_b, 1)

    @pl.when(i % 2 == 1)
    def _():
        step(ybuf_b, 1, ybuf_a, 0)

    @pl.when((i == n_steps - 1) & (i % 2 == 0))
    def _():
        wait_rows(ybuf_b, 1)

    @pl.when((i == n_steps - 1) & (i % 2 == 1))
    def _():
        wait_rows(ybuf_a, 0)


def combine(y_sorted, dest_tiles, w_tok, base, g2, ln_g, ln_b, seq):
    t_tokens, d = base.shape
    tc = COMBINE_TILE
    n_steps = t_tokens // tc
    tiles_per_seq = seq // tc
    n_copies = TOP_K * tc

    def const(shape):
        return pl.BlockSpec(shape, lambda i: (0,) * len(shape))

    return pl.pallas_call(
        _combine_kernel,
        grid=(n_steps,),
        in_specs=[pl.BlockSpec((n_copies,), lambda i: (i,), memory_space=pltpu.SMEM),
                  pl.BlockSpec((n_copies,), lambda i: (jnp.minimum(i + 1, n_steps - 1),),
                               memory_space=pltpu.SMEM),
                  pl.BlockSpec(memory_space=pl.ANY),
                  pl.BlockSpec((tc, TOP_K), lambda i: (i, 0)),
                  pl.BlockSpec((tc, d), lambda i: (i, 0)),
                  pl.BlockSpec((1, 1, d), lambda i: (i // tiles_per_seq, 0, 0)),
                  const((1, d)), const((1, d))],
        out_specs=pl.BlockSpec((tc, d), lambda i: (i, 0)),
        out_shape=jax.ShapeDtypeStruct((t_tokens, d), F32),
        scratch_shapes=[pltpu.VMEM((n_copies * PACK_CHUNKS, PACK_LANES), U32),
                        pltpu.VMEM((n_copies * PACK_CHUNKS, PACK_LANES), U32),
                        pltpu.SemaphoreType.DMA((2,))],
        compiler_params=_params(("arbitrary",)),
        name="combine",
    )(dest_tiles, dest_tiles, y_sorted, w_tok, base, g2, ln_g.reshape(1, d), ln_b.reshape(1, d))


def _dispatch_tables(e_t, pos_t, counts, t_tokens):
    rb = MOE_ROW_BLOCK
    n_blocks = t_tokens * TOP_K // rb + N_EXPERTS
    padded = (counts + rb - 1) // rb * rb
    pends = jnp.cumsum(padded)
    pstarts = pends - padded
    experts = jnp.arange(N_EXPERTS, dtype=I32)
    pstart_of = jnp.sum(jnp.where(e_t[..., None] == experts, pstarts.astype(I32), 0), axis=-1)
    dest = (pstart_of + pos_t).astype(I32)
    nused = (pends[-1] // rb).astype(I32)
    blk = jnp.arange(n_blocks, dtype=I32)
    last_blk = jnp.minimum(blk, jnp.maximum(nused - 1, 0))
    block_e = jnp.sum((pends[None, :] <= (last_blk * rb)[:, None]).astype(I32), axis=1)
    block_e = jnp.minimum(block_e, N_EXPERTS - 1).astype(I32)
    first = (blk < nused) & ((blk == 0) | (block_e != jnp.roll(block_e, 1)))
    block_ord = (jnp.cumsum(first.astype(I32)) - 1).astype(I32)
    later = (experts[None, :] > experts[:, None]) & (padded[None, :] > 0)
    next_of = jnp.min(jnp.where(later, experts[None, :], N_EXPERTS), axis=1)
    block_next = jnp.sum(jnp.where(block_e[:, None] == experts[None, :], next_of[None, :], 0),
                         axis=1).astype(I32)
    pad_start = (pstarts + counts).astype(I32)
    pad_count = (padded - counts).astype(I32)
    return (dest, block_e, nused.reshape(1), block_ord, block_next, pad_start, pad_count,
            n_blocks * rb)


def _layer(x, c, w_ada, b_ada, w_in, g_cq, w_uq, g_ckv, w_ukv, w_o, ln1_g, ln1_b,
           w_router, router_bias, w_gate, w_up, w_down, ws_gate, ws_up, ws_down, ln2_g, ln2_b):
    bsz, seq, d = x.shape
    t_tokens = bsz * seq
    x2d = x.reshape(t_tokens, d)

    mod = ada_modulation(c, w_ada, b_ada)
    sh1, sc1, g1, sh2, sc2, g2 = [m.reshape(bsz, 1, d) for m in jnp.split(mod, 6, axis=-1)]

    qa, ka, vat, qm, kn, vmt, kr = input_projection(x2d, sc1, sh1, w_in, g_cq, w_uq, g_ckv, w_ukv, seq)
    attn_a = moba_attention(qa, ka, vat, bsz, seq)
    attn_m = mla_attention(qm, kn, kr, vmt, bsz, seq)
    h2, h2p, base = output_projection(attn_a, attn_m, x2d, w_o, g1, sc2, sh2, g2, ln1_g, ln1_b,
                                 ws_gate, ws_up, ws_down, seq)

    e_t, pos_t, w_t, counts = router(h2, w_router, router_bias)
    (dest, block_e, nused, block_ord, block_next, pad_start, pad_count,
     n_rows) = _dispatch_tables(e_t, pos_t, counts, t_tokens)
    x_sorted = dispatch(h2p, dest.T.reshape(-1), pad_start, pad_count, nused, n_rows)
    y_sorted = routed_experts(x_sorted, block_e, nused, block_ord, block_next, w_gate, w_up, w_down)

    tc = COMBINE_TILE
    dest_tiles = dest.reshape(TOP_K, t_tokens // tc, tc).transpose(1, 0, 2).reshape(-1)
    out = combine(y_sorted, dest_tiles, w_t.T, base, g2, ln2_g, ln2_b, seq)
    return out.reshape(bsz, seq, d)


def kernel(x, c, w_ada, b_ada, w_in, g_cq, w_uq, g_ckv, w_ukv, w_o, ln1_g, ln1_b, w_router,
           router_bias, w_gate, w_up, w_down, ws_gate, ws_up, ws_down, ln2_g, ln2_b):
    depth = w_ada.shape[0]
    assert depth == 1, "DeepNorm constants are baked for a single layer"
    return _layer(x, c, w_ada[0], b_ada[0], w_in[0], g_cq[0], w_uq[0], g_ckv[0], w_ukv[0], w_o[0],
                  ln1_g[0], ln1_b[0], w_router[0], router_bias[0], w_gate[0], w_up[0], w_down[0],
                  ws_gate[0], ws_up[0], ws_down[0], ln2_g[0], ln2_b[0])
```

```python
import functools

import jax
import jax.numpy as jnp
import numpy as np
from jax import lax
from jax.experimental import pallas as pl
from jax.experimental.pallas import tpu as pltpu

F32 = jnp.float32
BF16 = jnp.bfloat16
I32 = jnp.int32

HEAD_DIM = 128
N_HEADS = 8
MOBA_BLOCK = 256
MOBA_TOPK = 3
MLA_ROPE_DIM = 64
MLA_RANK = 512
MLA_QK_DIM = HEAD_DIM + MLA_ROPE_DIM
GROUP_WIDTH = N_HEADS * HEAD_DIM
ROPE_THETA = 10000.0
N_EXPERTS = 64
TOP_K = 8
N_GROUPS = 8
TOPK_GROUPS = 4
ROUTED_SCALE = 2.5
MOE_ROW_BLOCK = 256
LN_EPS = 1e-5
RMS_EPS = 1e-6
DEEPNORM_ALPHA = 2.0 ** 0.25

VMEM_LIMIT_BYTES = 56 * 1024 * 1024

NT_DIMS = (((1,), (1,)), ((), ()))


def _params(semantics):
    return pltpu.CompilerParams(dimension_semantics=semantics,
                                vmem_limit_bytes=VMEM_LIMIT_BYTES)


def _silu(x):
    return x * jax.nn.sigmoid(x)


PACK_CHUNKS = 8
PACK_LANES = 128
U32 = jnp.uint32
HIGH_HALF = np.uint32(0xFFFF0000)


def _store_packed_rows(ref, x):
    n_rows, half = x.shape[0], x.shape[1] // 2
    lo = lax.bitcast_convert_type(x[:, :half].astype(BF16).astype(F32), U32) >> 16
    hi = lax.bitcast_convert_type(x[:, half:].astype(BF16).astype(F32), U32) & HIGH_HALF
    packed = lo | hi
    for j in range(PACK_CHUNKS):
        ref[pl.ds(j, n_rows, stride=PACK_CHUNKS), :] = packed[:, j * PACK_LANES:(j + 1) * PACK_LANES]


def _load_packed_rows(ref, first_row, n_rows):
    lo, hi = [], []
    for j in range(PACK_CHUNKS):
        u = ref[pl.ds(first_row * PACK_CHUNKS + j, n_rows, stride=PACK_CHUNKS), :]
        lo.append(lax.bitcast_convert_type(u << 16, F32))
        hi.append(lax.bitcast_convert_type(u & HIGH_HALF, F32))
    return jnp.concatenate(lo, axis=1), jnp.concatenate(hi, axis=1)


def _ada_kernel(c_ref, w_ref, b_ref, o_ref):
    a = _silu(c_ref[...]).astype(BF16)
    o_ref[...] = jnp.dot(a, w_ref[...].astype(BF16), preferred_element_type=F32) + b_ref[...]


def ada_modulation(c, w_ada, b_ada):
    bsz, d = c.shape
    rows = -(-bsz // 16) * 16
    c_pad = jnp.pad(c, ((0, rows - bsz), (0, 0)))
    n = w_ada.shape[1]
    tn = 1024
    out = pl.pallas_call(
        _ada_kernel,
        grid=(n // tn,),
        in_specs=[pl.BlockSpec((rows, d), lambda j: (0, 0)),
                  pl.BlockSpec((d, tn), lambda j: (0, j)),
                  pl.BlockSpec((1, tn), lambda j: (0, j))],
        out_specs=pl.BlockSpec((rows, tn), lambda j: (0, j)),
        out_shape=jax.ShapeDtypeStruct((rows, n), F32),
        compiler_params=_params(("arbitrary",)),
        name="ada_modulation",
    )(c_pad, w_ada, b_ada.reshape(1, n))
    return out[:bsz]


def _rms(x, g):
    return x * lax.rsqrt(jnp.mean(x * x, axis=-1, keepdims=True) + RMS_EPS) * g


def _rope64(x, cos, sin):
    return x * cos + pltpu.roll(x, 64, axis=1) * sin


def _rope32(x, cos, s1, s2):
    return x * cos + pltpu.roll(x, 96, axis=1) * s1 + pltpu.roll(x, 32, axis=1) * s2


def _inproj_kernel(x_ref, sc_ref, sh_ref, w_ref, wkr_ref, wuq_ref, wuk_ref, wuv_ref, gcq_ref, gckv_ref,
                   cosa_ref, sina_ref, cosr_ref, s1_ref, s2_ref,
                   qa_ref, ka_ref, va_ref, qm_ref, kn_ref, vm_ref, kr_ref):
    h = (x_ref[...] * (1.0 + sc_ref[0]) + sh_ref[0]).astype(BF16)
    cosa, sina = cosa_ref[...], sina_ref[...]
    cosr, s1, s2 = cosr_ref[...], s1_ref[...], s2_ref[...]
    gw = GROUP_WIDTH

    def proj(lo, hi):
        return jnp.dot(h, w_ref[:, lo:hi], preferred_element_type=F32)

    for out_ref, base in ((qa_ref, 0), (ka_ref, gw)):
        t = proj(base, base + gw)
        for hd in range(N_HEADS):
            sl = slice(hd * HEAD_DIM, (hd + 1) * HEAD_DIM)
            out_ref[:, sl] = _rope64(t[:, sl], cosa, sina).astype(BF16)
    va_ref[...] = proj(2 * gw, 3 * gw).T.astype(BF16)

    cq = _rms(proj(3 * gw, 3 * gw + MLA_RANK), gcq_ref[...]).astype(BF16)
    qm = jnp.dot(cq, wuq_ref[...], preferred_element_type=F32)
    for hd in range(N_HEADS):
        b0 = hd * 2 * HEAD_DIM
        qm_ref[:, b0:b0 + HEAD_DIM] = qm[:, b0:b0 + HEAD_DIM].astype(BF16)
        qm_ref[:, b0 + HEAD_DIM:b0 + 2 * HEAD_DIM] = _rope32(
            qm[:, b0 + HEAD_DIM:b0 + 2 * HEAD_DIM], cosr, s1, s2).astype(BF16)

    ckv = _rms(proj(3 * gw + MLA_RANK, 3 * gw + 2 * MLA_RANK), gckv_ref[...]).astype(BF16)
    kn_ref[...] = jnp.dot(ckv, wuk_ref[...], preferred_element_type=F32).astype(BF16)
    vm_ref[...] = jnp.dot(ckv, wuv_ref[...], preferred_element_type=F32).T.astype(BF16)

    kr = jnp.dot(h, wkr_ref[...], preferred_element_type=F32)
    kr_ref[...] = _rope32(kr, cosr, s1, s2).astype(BF16)


def _rope_tables(seq):
    pos = jnp.arange(seq, dtype=F32)[:, None]
    inv_a = 1.0 / (ROPE_THETA ** (jnp.arange(0, HEAD_DIM, 2, dtype=F32) / HEAD_DIM))
    ang_a = pos * inv_a[None, :]
    cosa = jnp.concatenate([jnp.cos(ang_a), jnp.cos(ang_a)], axis=1)
    sina = jnp.concatenate([-jnp.sin(ang_a), jnp.sin(ang_a)], axis=1)
    inv_r = 1.0 / (ROPE_THETA ** (jnp.arange(0, MLA_ROPE_DIM, 2, dtype=F32) / MLA_ROPE_DIM))
    ang_r = pos * inv_r[None, :]
    cr, sr = jnp.cos(ang_r), jnp.sin(ang_r)
    z32 = jnp.zeros_like(cr)
    z64 = jnp.zeros((seq, 64), F32)
    cosr = jnp.concatenate([cr, cr, z64], axis=1)
    s1 = jnp.concatenate([-sr, z32, z64], axis=1)
    s2 = jnp.concatenate([z32, sr, z64], axis=1)
    return cosa, sina, cosr, s1, s2


def input_projection(x2d, sc1, sh1, w_in, g_cq, w_uq, g_ckv, w_ukv, seq):
    t_tokens, d = x2d.shape
    tm = 256
    tiles_per_seq = seq // tm
    gw = GROUP_WIDTH
    main_w = 3 * gw + 2 * MLA_RANK
    w_main = w_in[:, :main_w].astype(BF16)
    w_kr = jnp.pad(w_in[:, main_w:], ((0, 0), (0, HEAD_DIM - MLA_ROPE_DIM))).astype(BF16)
    wuq = jnp.pad(w_uq.reshape(MLA_RANK, N_HEADS, MLA_QK_DIM),
                  ((0, 0), (0, 0), (0, 2 * HEAD_DIM - MLA_QK_DIM)))
    wuq = wuq.reshape(MLA_RANK, N_HEADS * 2 * HEAD_DIM).astype(BF16)
    wukv = w_ukv.reshape(MLA_RANK, N_HEADS, 2 * HEAD_DIM)
    wuk = wukv[:, :, :HEAD_DIM].reshape(MLA_RANK, gw).astype(BF16)
    wuv = wukv[:, :, HEAD_DIM:].reshape(MLA_RANK, gw).astype(BF16)
    tables = _rope_tables(seq)

    def const(shape):
        return pl.BlockSpec(shape, lambda i: (0,) * len(shape), pipeline_mode=pl.Buffered(1))

    def rows(width):
        return pl.BlockSpec((tm, width), lambda i: (i, 0))

    mod_spec = pl.BlockSpec((1, 1, d), lambda i: (i // tiles_per_seq, 0, 0))
    tab_spec = pl.BlockSpec((tm, HEAD_DIM), lambda i: (i % tiles_per_seq, 0))
    widths = (gw, gw, gw, 2 * gw, gw, gw, HEAD_DIM)
    transposed = (False, False, True, False, False, True, False)

    def out_spec(width, is_t):
        return pl.BlockSpec((width, tm), lambda i: (0, i)) if is_t else rows(width)

    def out_shape(width, is_t):
        return jax.ShapeDtypeStruct((width, t_tokens) if is_t else (t_tokens, width), BF16)

    return pl.pallas_call(
        _inproj_kernel,
        grid=(t_tokens // tm,),
        in_specs=[rows(d), mod_spec, mod_spec, const((d, main_w)), const((d, HEAD_DIM)),
                  const((MLA_RANK, 2 * gw)), const((MLA_RANK, gw)), const((MLA_RANK, gw)),
                  const((1, MLA_RANK)), const((1, MLA_RANK))] + [tab_spec] * 5,
        out_specs=[out_spec(w, t) for w, t in zip(widths, transposed)],
        out_shape=[out_shape(w, t) for w, t in zip(widths, transposed)],
        compiler_params=_params(("arbitrary",)),
        name="input_projection",
    )(x2d, sc1, sh1, w_main, w_kr, wuq, wuk, wuv, g_cq.reshape(1, -1), g_ckv.reshape(1, -1), *tables)


ATTN_BLOCK = 256
LOG2_E = 1.4426950408889634


def _causal_mask_t():
    key = lax.broadcasted_iota(I32, (ATTN_BLOCK, ATTN_BLOCK), 0)
    qry = lax.broadcasted_iota(I32, (ATTN_BLOCK, ATTN_BLOCK), 1)
    return key <= qry


def _scores_t(q, k_blk):
    return lax.dot_general(k_blk, q, NT_DIMS, preferred_element_type=F32)


def _first_block(s, vt_blk, c, acc_ref):
    s = jnp.where(_causal_mask_t(), s, -jnp.inf)
    m = jnp.max(s, axis=0, keepdims=True)
    p = jnp.exp2((s - m) * c)
    acc_ref[...] = jnp.dot(vt_blk, p.astype(BF16), preferred_element_type=F32)
    return m, jnp.sum(p, axis=0, keepdims=True)


def _next_block(s, vt_blk, bias, c, m, l, acc_ref):
    m_blk = jnp.max(s, axis=0, keepdims=True)
    if bias is not None:
        m_blk = m_blk + bias
    m_new = jnp.maximum(m, m_blk)
    alpha = jnp.exp2((m - m_new) * c)
    p = jnp.exp2((s - (m_new if bias is None else m_new - bias)) * c)
    acc_ref[...] = alpha * acc_ref[...] + jnp.dot(vt_blk, p.astype(BF16), preferred_element_type=F32)
    return m_new, alpha * l + jnp.sum(p, axis=0, keepdims=True)


HEADS_PER_STEP = 8


def _head_cols(g, width=HEAD_DIM):
    return slice(g * width, (g + 1) * width)


def _moba_block_bias(kmean, q, qi, n_blocks):
    gate = lax.dot_general(kmean, q.astype(F32), NT_DIMS, precision=lax.Precision.HIGHEST,
                           preferred_element_type=F32)
    row = lax.broadcasted_iota(I32, gate.shape, 0)
    gt = jnp.where(row < qi, gate, -jnp.inf)
    cnt = jnp.zeros(gate.shape, F32)
    for m in range(n_blocks - 1):
        gm = gt[m:m + 1, :]
        cnt = cnt + jnp.where((gm > gt) | ((gm == gt) & (m < row)), 1.0, 0.0)
    return jnp.where(cnt < float(MOBA_TOPK), 0.0, -jnp.inf).astype(F32)


def _moba_kernel(q_ref, k_ref, vt_ref, o_ref, kmean_ref, acc_ref, *, n_blocks):
    qi = pl.program_id(2)
    blk = MOBA_BLOCK
    heads = HEADS_PER_STEP

    @pl.when(qi == 0)
    def _():
        kmean_ref[...] = jnp.zeros(kmean_ref.shape, F32)
        for g in range(heads):
            for n in range(n_blocks):
                kb = k_ref[0, n * blk:(n + 1) * blk, _head_cols(g)].astype(F32)
                kmean_ref[g, n:n + 1, :] = jnp.sum(kb, axis=0, keepdims=True) * (1.0 / blk)

    c = (HEAD_DIM ** -0.5) * LOG2_E
    own = pl.multiple_of(qi * blk, blk)
    qs = [q_ref[0, :, _head_cols(g)] for g in range(heads)]
    scores = [_scores_t(qs[g], k_ref[0, pl.ds(own, blk), _head_cols(g)]) for g in range(heads)]
    biases = [_moba_block_bias(kmean_ref[g], qs[g], qi, n_blocks) for g in range(heads)]
    carry0 = []
    for g in range(heads):
        carry0 += _first_block(scores[g], vt_ref[_head_cols(g), pl.ds(own, blk)], c, acc_ref.at[g])

    def body(n, carry):
        off = pl.multiple_of(n * blk, blk)
        scores = [_scores_t(qs[g], k_ref[0, pl.ds(off, blk), _head_cols(g)]) for g in range(heads)]
        row = lax.broadcasted_iota(I32, biases[0].shape, 0)
        out = []
        for g in range(heads):
            bias = jnp.sum(jnp.where(row == n, biases[g], 0.0), axis=0, keepdims=True)
            out += _next_block(scores[g], vt_ref[_head_cols(g), pl.ds(off, blk)], bias,
                               c, carry[2 * g], carry[2 * g + 1], acc_ref.at[g])
        return tuple(out)

    final = lax.fori_loop(0, qi, body, tuple(carry0))
    for g in range(heads):
        o_ref[0, :, _head_cols(g)] = (acc_ref[g] / final[2 * g + 1]).T.astype(o_ref.dtype)


def moba_attention(qa, ka, vat, bsz, seq):
    blk = MOBA_BLOCK
    n_blocks = seq // blk
    heads = HEADS_PER_STEP
    gate_rows = -(-n_blocks // 8) * 8
    q3, k3 = (t.reshape(bsz, seq, GROUP_WIDTH) for t in (qa, ka))
    q_spec = pl.BlockSpec((1, blk, heads * HEAD_DIM), lambda b, h, i: (b, i, h))
    kv_spec = pl.BlockSpec((1, seq, heads * HEAD_DIM), lambda b, h, i: (b, 0, h))
    out = pl.pallas_call(
        functools.partial(_moba_kernel, n_blocks=n_blocks),
        grid=(bsz, N_HEADS // heads, n_blocks),
        in_specs=[q_spec, kv_spec, pl.BlockSpec((heads * HEAD_DIM, seq), lambda b, h, i: (h, b))],
        out_specs=q_spec,
        out_shape=jax.ShapeDtypeStruct((bsz, seq, GROUP_WIDTH), BF16),
        scratch_shapes=[pltpu.VMEM((heads, gate_rows, HEAD_DIM), F32),
                        pltpu.VMEM((heads, HEAD_DIM, blk), F32)],
        compiler_params=_params(("arbitrary", "arbitrary", "arbitrary")),
        name="moba_attention",
    )(q3, k3, vat)
    return out.reshape(bsz * seq, GROUP_WIDTH)


def _mla_kernel(q_ref, kn_ref, kr_ref, vt_ref, o_ref, kcat_ref, acc_ref):
    qi = pl.program_id(2)
    blk = ATTN_BLOCK

    heads = HEADS_PER_STEP

    @pl.when(qi == 0)
    def _():
        for g in range(heads):
            kcat_ref[g, :, :HEAD_DIM] = kn_ref[0, :, _head_cols(g)]
            kcat_ref[g, :, HEAD_DIM:] = kr_ref[0]

    c = (MLA_QK_DIM ** -0.5) * LOG2_E
    own = pl.multiple_of(qi * blk, blk)
    qs = [q_ref[0, :, _head_cols(g, 2 * HEAD_DIM)] for g in range(heads)]
    scores = [_scores_t(qs[g], kcat_ref[g, pl.ds(own, blk), :]) for g in range(heads)]
    carry0 = []
    for g in range(heads):
        carry0 += _first_block(scores[g], vt_ref[_head_cols(g), pl.ds(own, blk)], c, acc_ref.at[g])

    def body(n, carry):
        off = pl.multiple_of(n * blk, blk)
        scores = [_scores_t(qs[g], kcat_ref[g, pl.ds(off, blk), :]) for g in range(heads)]
        out = []
        for g in range(heads):
            out += _next_block(scores[g], vt_ref[_head_cols(g), pl.ds(off, blk)], None, c,
                               carry[2 * g], carry[2 * g + 1], acc_ref.at[g])
        return tuple(out)

    final = lax.fori_loop(0, qi, body, tuple(carry0))
    for g in range(heads):
        o_ref[0, :, _head_cols(g)] = (acc_ref[g] / final[2 * g + 1]).T.astype(o_ref.dtype)


def mla_attention(qm, kn, kr, vmt, bsz, seq):
    blk = ATTN_BLOCK
    heads = HEADS_PER_STEP
    q3 = qm.reshape(bsz, seq, 2 * GROUP_WIDTH)
    kn3 = kn.reshape(bsz, seq, GROUP_WIDTH)
    kr3 = kr.reshape(bsz, seq, HEAD_DIM)
    out = pl.pallas_call(
        _mla_kernel,
        grid=(bsz, N_HEADS // heads, seq // blk),
        in_specs=[pl.BlockSpec((1, blk, heads * 2 * HEAD_DIM), lambda b, h, i: (b, i, h)),
                  pl.BlockSpec((1, seq, heads * HEAD_DIM), lambda b, h, i: (b, 0, h)),
                  pl.BlockSpec((1, seq, HEAD_DIM), lambda b, h, i: (b, 0, 0)),
                  pl.BlockSpec((heads * HEAD_DIM, seq), lambda b, h, i: (h, b))],
        out_specs=pl.BlockSpec((1, blk, heads * HEAD_DIM), lambda b, h, i: (b, i, h)),
        out_shape=jax.ShapeDtypeStruct((bsz, seq, GROUP_WIDTH), BF16),
        scratch_shapes=[pltpu.VMEM((heads, seq, 2 * HEAD_DIM), BF16),
                        pltpu.VMEM((heads, HEAD_DIM, blk), F32)],
        compiler_params=_params(("arbitrary", "arbitrary", "arbitrary")),
        name="mla_attention",
    )(q3, kn3, kr3, vmt)
    return out.reshape(bsz * seq, GROUP_WIDTH)


def _layer_norm(y, g, b):
    mu = jnp.mean(y, axis=-1, keepdims=True)
    yc = y - mu
    var = jnp.mean(yc * yc, axis=-1, keepdims=True)
    return yc * lax.rsqrt(var + LN_EPS) * g + b


def _outproj_kernel(a_ref, m_ref, x_ref, woa_ref, wom_ref, g1_ref, sc2_ref, sh2_ref, g2_ref,
                    lng_ref, lnb_ref, wsg_ref, wsu_ref, wsd_ref, h2_ref, h2p_ref, base_ref):
    f = (jnp.dot(a_ref[...], woa_ref[...], preferred_element_type=F32)
         + jnp.dot(m_ref[...], wom_ref[...], preferred_element_type=F32))
    x1 = _layer_norm(DEEPNORM_ALPHA * x_ref[...] + (1.0 + g1_ref[0]) * f, lng_ref[...], lnb_ref[...])
    h2 = x1 * (1.0 + sc2_ref[0]) + sh2_ref[0]
    h2_ref[...] = h2
    _store_packed_rows(h2p_ref, h2)
    hb = h2.astype(BF16)
    act = (_silu(jnp.dot(hb, wsg_ref[...], preferred_element_type=F32))
           * jnp.dot(hb, wsu_ref[...], preferred_element_type=F32)).astype(BF16)
    shared = jnp.dot(act, wsd_ref[...], preferred_element_type=F32)
    base_ref[...] = DEEPNORM_ALPHA * x1 + (1.0 + g2_ref[0]) * shared


def output_projection(attn_a, attn_m, x2d, w_o, g1, sc2, sh2, g2, ln_g, ln_b,
                      ws_gate, ws_up, ws_down, seq):
    t_tokens, d = x2d.shape
    tm = 256
    tiles_per_seq = seq // tm
    gw = GROUP_WIDTH
    ff = ws_gate.shape[1]

    def const(shape):
        return pl.BlockSpec(shape, lambda i: (0,) * len(shape), pipeline_mode=pl.Buffered(1))

    def rows(width):
        return pl.BlockSpec((tm, width), lambda i: (i, 0))

    mod_spec = pl.BlockSpec((1, 1, d), lambda i: (i // tiles_per_seq, 0, 0))
    return pl.pallas_call(
        _outproj_kernel,
        grid=(t_tokens // tm,),
        in_specs=[rows(gw), rows(gw), rows(d), const((gw, d)), const((gw, d)),
                  mod_spec, mod_spec, mod_spec, mod_spec, const((1, d)), const((1, d)),
                  const((d, ff)), const((d, ff)), const((ff, d))],
        out_specs=[rows(d), pl.BlockSpec((tm * PACK_CHUNKS, PACK_LANES), lambda i: (i, 0)), rows(d)],
        out_shape=[jax.ShapeDtypeStruct((t_tokens, d), F32),
                   jax.ShapeDtypeStruct((t_tokens * PACK_CHUNKS, PACK_LANES), U32),
                   jax.ShapeDtypeStruct((t_tokens, d), F32)],
        compiler_params=_params(("arbitrary",)),
        name="output_projection",
    )(attn_a, attn_m, x2d, w_o[:gw].astype(BF16), w_o[gw:].astype(BF16), g1, sc2, sh2, g2,
      ln_g.reshape(1, d), ln_b.reshape(1, d),
      ws_gate.astype(BF16), ws_up.astype(BF16), ws_down.astype(BF16))


ROUTER_TILE = 512


def _router_kernel(h_ref, wrt_ref, bias_ref, tri_ref, ltri_ref,
                   e_ref, pos_ref, w_ref, cnt_ref, run_ref):
    tr = ROUTER_TILE
    epg = N_EXPERTS // N_GROUPS

    @pl.when(pl.program_id(0) == 0)
    def _():
        run_ref[...] = jnp.zeros(run_ref.shape, F32)

    logits = lax.dot_general(wrt_ref[...], h_ref[...], NT_DIMS, precision=lax.Precision.HIGHEST,
                             preferred_element_type=F32)
    scores = jax.nn.sigmoid(logits)
    sel_scores = scores + bias_ref[...]

    sub = lax.broadcasted_iota(I32, (epg, tr), 0)
    grp_rows = []
    for g in range(N_GROUPS):
        xg = sel_scores[g * epg:(g + 1) * epg, :]
        m1 = jnp.max(xg, axis=0, keepdims=True)
        first = jnp.min(jnp.where(xg == m1, sub, epg), axis=0, keepdims=True)
        m2 = jnp.max(jnp.where(sub == first, -jnp.inf, xg), axis=0, keepdims=True)
        grp_rows.append(m1 + m2)
    grp = jnp.concatenate(grp_rows, axis=0)

    giota = lax.broadcasted_iota(I32, (N_GROUPS, tr), 0)
    gcnt = jnp.zeros((N_GROUPS, tr), F32)
    for m in range(N_GROUPS):
        rm = grp[m:m + 1, :]
        gcnt = gcnt + jnp.where((rm > grp) | ((rm == grp) & (m < giota)), 1.0, 0.0)
    gkeep = jnp.where(gcnt < float(TOPK_GROUPS), 1.0, 0.0)
    gmask = jnp.concatenate(
        [jnp.broadcast_to(gkeep[g:g + 1, :], (epg, tr)) for g in range(N_GROUPS)], axis=0)
    masked = jnp.where(gmask > 0.0, sel_scores, -jnp.inf)

    eiota = lax.broadcasted_iota(I32, (N_EXPERTS, tr), 0)

    free = jnp.ones((N_EXPERTS, tr), F32)
    for _ in range(TOP_K):
        top = jnp.max(jnp.where(free > 0.0, masked, -jnp.inf), axis=0, keepdims=True)
        first = jnp.min(jnp.where((masked == top) & (free > 0.0), eiota, N_EXPERTS),
                        axis=0, keepdims=True)
        free = jnp.where(eiota == first, 0.0, free)
    sel = free == 0.0

    sw = jnp.where(sel, scores, 0.0)
    wn = sw / jnp.sum(sw, axis=0, keepdims=True) * ROUTED_SCALE

    selb = jnp.where(sel, 1.0, 0.0).astype(BF16)
    cum = jnp.dot(selb, tri_ref[...], preferred_element_type=F32)
    pos = run_ref[...] + cum - 1.0
    run_ref[...] = run_ref[...] + cum[:, tr - 1:tr]
    cnt_ref[...] = jnp.broadcast_to(run_ref[...], cnt_ref.shape).astype(I32)

    slot = jnp.dot(ltri_ref[...], selb, preferred_element_type=F32)
    ef = eiota.astype(F32)
    e_rows, p_rows, w_rows = [], [], []
    for j in range(TOP_K):
        mj = sel & (slot == float(j))
        e_rows.append(jnp.sum(jnp.where(mj, ef, 0.0), axis=0, keepdims=True))
        p_rows.append(jnp.sum(jnp.where(mj, pos, 0.0), axis=0, keepdims=True))
        w_rows.append(jnp.sum(jnp.where(mj, wn, 0.0), axis=0, keepdims=True))
    e_ref[...] = jnp.concatenate(e_rows, axis=0).astype(I32)
    pos_ref[...] = jnp.concatenate(p_rows, axis=0).astype(I32)
    w_ref[...] = jnp.concatenate(w_rows, axis=0)


def router(h2, w_router, router_bias):
    t_tokens, d = h2.shape
    tr = ROUTER_TILE
    tri = jnp.asarray(np.triu(np.ones((tr, tr), np.float32)), BF16)
    ltri = jnp.asarray(np.tril(np.ones((N_EXPERTS, N_EXPERTS), np.float32), -1), BF16)

    def const(shape):
        return pl.BlockSpec(shape, lambda i: (0,) * len(shape))

    tok = pl.BlockSpec((TOP_K, tr), lambda i: (0, i))
    e_t, pos_t, w_t, cnt = pl.pallas_call(
        _router_kernel,
        grid=(t_tokens // tr,),
        in_specs=[pl.BlockSpec((tr, d), lambda i: (i, 0)), const((N_EXPERTS, d)),
                  const((N_EXPERTS, 1)), const((tr, tr)), const((N_EXPERTS, N_EXPERTS))],
        out_specs=[tok, tok, tok, const((N_EXPERTS, 128))],
        out_shape=[jax.ShapeDtypeStruct((TOP_K, t_tokens), I32),
                   jax.ShapeDtypeStruct((TOP_K, t_tokens), I32),
                   jax.ShapeDtypeStruct((TOP_K, t_tokens), F32),
                   jax.ShapeDtypeStruct((N_EXPERTS, 128), I32)],
        scratch_shapes=[pltpu.VMEM((N_EXPERTS, 1), F32)],
        compiler_params=_params(("arbitrary",)),
        name="router",
    )(h2, w_router.T, router_bias.reshape(N_EXPERTS, 1), tri, ltri)
    return e_t, pos_t, w_t, cnt[:, 0]


DISPATCH_TILE = 128


def _dispatch_kernel(pad_start_ref, pad_count_ref, nused_ref, d_ref, h_ref, xs_ref, zero_ref, sem):
    tt = DISPATCH_TILE
    pc = PACK_CHUNKS
    rb = MOE_ROW_BLOCK
    n_copies = tt * TOP_K
    n_blocks = xs_ref.shape[0] // (rb * pc)

    @pl.when(pl.program_id(0) == 0)
    def _():
        zero_ref[...] = jnp.zeros(zero_ref.shape, U32)
        nused = nused_ref[0]

        def pad_rows(row, n):
            dst = xs_ref.at[pl.ds(pl.multiple_of(row * pc, pc), n * pc), :]
            return pltpu.make_async_copy(zero_ref.at[pl.ds(0, n * pc), :], dst, sem.at[1])

        def tail_block(b):
            dst = xs_ref.at[pl.ds(pl.multiple_of(b * (rb * pc), rb * pc), rb * pc), :]
            return pltpu.make_async_copy(zero_ref, dst, sem.at[1])

        def per_expert(e, carry, wait):
            base, n = pad_start_ref[e], pad_count_ref[e]
            for bit in reversed(range(rb.bit_length() - 1)):
                size = 1 << bit
                offset = (n >> (bit + 1)) << (bit + 1)

                @pl.when(((n >> bit) & 1) == 1)
                def _():
                    copy = pad_rows(base + offset, size)
                    if wait:
                        copy.wait()
                    else:
                        copy.start()
            return carry

        def start_tail(b, carry):
            tail_block(b).start()
            return carry

        def wait_tail(b, carry):
            tail_block(b).wait()
            return carry

        lax.fori_loop(0, N_EXPERTS, functools.partial(per_expert, wait=False), 0)
        lax.fori_loop(nused, n_blocks, start_tail, 0)
        lax.fori_loop(0, N_EXPERTS, functools.partial(per_expert, wait=True), 0)
        lax.fori_loop(nused, n_blocks, wait_tail, 0)

    for j in range(tt):
        for k in range(TOP_K):
            c = j * TOP_K + k
            dst = xs_ref.at[pl.ds(pl.multiple_of(d_ref[c] * pc, pc), pc), :]
            pltpu.make_async_copy(h_ref.at[pl.ds(j * pc, pc), :], dst, sem.at[0]).start(priority=c % 2)
    tile_rows = xs_ref.at[pl.ds(0, n_copies * pc), :]
    pltpu.make_async_copy(tile_rows, tile_rows, sem.at[0]).wait()


def dispatch(h2p, dest_tok_major, pad_start, pad_count, nused, n_rows):
    tt = DISPATCH_TILE
    pc = PACK_CHUNKS
    t_tokens = h2p.shape[0] // pc
    n_copies = tt * TOP_K
    grid_spec = pltpu.PrefetchScalarGridSpec(
        num_scalar_prefetch=3,
        grid=(t_tokens // tt,),
        in_specs=[pl.BlockSpec((n_copies,), lambda i, ps, pn, nu: (i,), memory_space=pltpu.SMEM),
                  pl.BlockSpec((tt * pc, PACK_LANES), lambda i, ps, pn, nu: (i, 0))],
        out_specs=pl.BlockSpec(memory_space=pl.ANY),
        scratch_shapes=[pltpu.VMEM((MOE_ROW_BLOCK * pc, PACK_LANES), U32),
                        pltpu.SemaphoreType.DMA((2,))],
    )
    return pl.pallas_call(
        _dispatch_kernel,
        grid_spec=grid_spec,
        out_shape=jax.ShapeDtypeStruct((n_rows * pc, PACK_LANES), U32),
        compiler_params=_params(("arbitrary",)),
        name="dispatch",
    )(pad_start, pad_count, nused, dest_tok_major, h2p)


EXPERT_BLOCKS_PER_STEP = 2


def _experts_kernel(be_ref, nused_ref, ord_ref, next_ref, x_ref, wg_hbm, wu_hbm, wd_hbm, y_ref,
                    wg_stage, wu_stage, wd_stage, wg_bf, wu_bf, wd_bf, sem):
    nused = nused_ref[0]
    rb = MOE_ROW_BLOCK
    pc = PACK_CHUNKS

    def weight_copies(e, slot):
        return (pltpu.make_async_copy(wg_hbm.at[e], wg_stage.at[slot], sem.at[slot]),
                pltpu.make_async_copy(wu_hbm.at[e], wu_stage.at[slot], sem.at[slot]),
                pltpu.make_async_copy(wd_hbm.at[e], wd_stage.at[slot], sem.at[slot]))

    def one_block(i, sub):
        y_rows = y_ref.at[pl.ds(sub * rb * pc, rb * pc), :]

        @pl.when(i < nused)
        def _():
            @pl.when((i == 0) | (be_ref[i] != be_ref[jnp.maximum(i - 1, 0)]))
            def _():
                e = be_ref[i]
                slot = ord_ref[i] % 2

                @pl.when(i == 0)
                def _():
                    for c in weight_copies(e, slot):
                        c.start()

                for c in weight_copies(e, slot):
                    c.wait()
                wg_bf[...] = wg_stage[slot].astype(BF16)
                wu_bf[...] = wu_stage[slot].astype(BF16)
                wd_bf[...] = wd_stage[slot].astype(BF16)

                @pl.when(next_ref[i] < N_EXPERTS)
                def _():
                    for c in weight_copies(next_ref[i], 1 - slot):
                        c.start()

            lo, hi = _load_packed_rows(x_ref, sub * rb, rb)
            xb = jnp.concatenate([lo.astype(BF16), hi.astype(BF16)], axis=1)
            act = (_silu(jnp.dot(xb, wg_bf[...], preferred_element_type=F32))
                   * jnp.dot(xb, wu_bf[...], preferred_element_type=F32)).astype(BF16)
            _store_packed_rows(y_rows, jnp.dot(act, wd_bf[...], preferred_element_type=F32))

        @pl.when(i >= nused)
        def _():
            y_rows[...] = jnp.zeros(y_rows.shape, U32)

    for sub in range(EXPERT_BLOCKS_PER_STEP):
        one_block(pl.program_id(0) * EXPERT_BLOCKS_PER_STEP + sub, sub)


def routed_experts(x_sorted, block_e, nused, block_ord, block_next, w_gate, w_up, w_down):
    rb = MOE_ROW_BLOCK
    pc = PACK_CHUNKS
    per_step = EXPERT_BLOCKS_PER_STEP
    n_blocks = x_sorted.shape[0] // (rb * pc)
    assert n_blocks % per_step == 0
    d, ff = w_gate.shape[1], w_gate.shape[2]
    hbm = pl.BlockSpec(memory_space=pl.ANY)

    def x_index(i, be, nu, od, nx):
        return (jnp.minimum(i, jnp.maximum(nu[0] - 1, 0) // per_step), 0)

    grid_spec = pltpu.PrefetchScalarGridSpec(
        num_scalar_prefetch=4,
        grid=(n_blocks // per_step,),
        in_specs=[pl.BlockSpec((per_step * rb * pc, PACK_LANES), x_index), hbm, hbm, hbm],
        out_specs=pl.BlockSpec((per_step * rb * pc, PACK_LANES), lambda i, be, nu, od, nx: (i, 0)),
        scratch_shapes=[pltpu.VMEM((2, d, ff), F32), pltpu.VMEM((2, d, ff), F32),
                        pltpu.VMEM((2, ff, d), F32),
                        pltpu.VMEM((d, ff), BF16), pltpu.VMEM((d, ff), BF16),
                        pltpu.VMEM((ff, d), BF16), pltpu.SemaphoreType.DMA((2,))],
    )
    return pl.pallas_call(
        _experts_kernel,
        grid_spec=grid_spec,
        out_shape=jax.ShapeDtypeStruct(x_sorted.shape, U32),
        compiler_params=_params(("arbitrary",)),
        name="routed_experts",
    )(block_e, nused, block_ord, block_next, x_sorted, w_gate, w_up, w_down)


COMBINE_TILE = 128


def _combine_kernel(d_cur_ref, d_next_ref, y_ref, w_ref, base_ref, g2_ref, lng_ref, lnb_ref,
                    o_ref, ybuf_a, ybuf_b, sem):
    i = pl.program_id(0)
    n_steps = pl.num_programs(0)
    tc = COMBINE_TILE
    n_copies = TOP_K * tc
    pc = PACK_CHUNKS

    def row_copy(row, buf, r, s):
        src = y_ref.at[pl.ds(pl.multiple_of(row * pc, pc), pc), :]
        return pltpu.make_async_copy(src, buf.at[pl.ds(r * pc, pc), :], sem.at[s])

    def wait_rows(buf, s):
        pltpu.make_async_copy(y_ref.at[pl.ds(0, n_copies * pc), :], buf, sem.at[s]).wait()

    @pl.when(i == 0)
    def _():
        def body(r, carry):
            src = y_ref.at[pl.ds(pl.multiple_of(d_cur_ref[r] * pc, pc), pc), :]
            dst = ybuf_a.at[pl.ds(pl.multiple_of(r * pc, pc), pc), :]
            pltpu.make_async_copy(src, dst, sem.at[0]).start()
            return carry
        lax.fori_loop(0, n_copies, body, 0, unroll=8)

    def step(cur, cur_s, nxt, nxt_s):
        wait_rows(cur, cur_s)
        for r in range(n_copies):
            row_copy(d_next_ref[r], nxt, r, nxt_s).start(priority=r % 2)
        w = w_ref[...]
        lo, hi = _load_packed_rows(cur, 0, tc)
        routed_lo, routed_hi = w[:, 0:1] * lo, w[:, 0:1] * hi
        for k in range(1, TOP_K):
            lo, hi = _load_packed_rows(cur, k * tc, tc)
            routed_lo = routed_lo + w[:, k:k + 1] * lo
            routed_hi = routed_hi + w[:, k:k + 1] * hi
        routed = jnp.concatenate([routed_lo, routed_hi], axis=1)
        y = base_ref[...] + (1.0 + g2_ref[0]) * routed
        o_ref[...] = _layer_norm(y, lng_ref[...], lnb_ref[...])

    @pl.when(i % 2 == 0)
    def _():
        step(ybuf_a, 0, ybuf_b, 1)

    @pl.when(i % 2 == 1)
    def _():
        step(ybuf_b, 1, ybuf_a, 0)

    @pl.when((i == n_steps - 1) & (i % 2 == 0))
    def _():
        wait_rows(ybuf_b, 1)

    @pl.when((i == n_steps - 1) & (i % 2 == 1))
    def _():
        wait_rows(ybuf_a, 0)


def combine(y_sorted, dest_tiles, w_tok, base, g2, ln_g, ln_b, seq):
    t_tokens, d = base.shape
    tc = COMBINE_TILE
    n_steps = t_tokens // tc
    tiles_per_seq = seq // tc
    n_copies = TOP_K * tc

    def const(shape):
        return pl.BlockSpec(shape, lambda i: (0,) * len(shape))

    return pl.pallas_call(
        _combine_kernel,
        grid=(n_steps,),
        in_specs=[pl.BlockSpec((n_copies,), lambda i: (i,), memory_space=pltpu.SMEM),
                  pl.BlockSpec((n_copies,), lambda i: (jnp.minimum(i + 1, n_steps - 1),),
                               memory_space=pltpu.SMEM),
                  pl.BlockSpec(memory_space=pl.ANY),
                  pl.BlockSpec((tc, TOP_K), lambda i: (i, 0)),
                  pl.BlockSpec((tc, d), lambda i: (i, 0)),
                  pl.BlockSpec((1, 1, d), lambda i: (i // tiles_per_seq, 0, 0)),
                  const((1, d)), const((1, d))],
        out_specs=pl.BlockSpec((tc, d), lambda i: (i, 0)),
        out_shape=jax.ShapeDtypeStruct((t_tokens, d), F32),
        scratch_shapes=[pltpu.VMEM((n_copies * PACK_CHUNKS, PACK_LANES), U32),
                        pltpu.VMEM((n_copies * PACK_CHUNKS, PACK_LANES), U32),
                        pltpu.SemaphoreType.DMA((2,))],
        compiler_params=_params(("arbitrary",)),
        name="combine",
    )(dest_tiles, dest_tiles, y_sorted, w_tok, base, g2, ln_g.reshape(1, d), ln_b.reshape(1, d))


def _dispatch_tables(e_t, pos_t, counts, t_tokens):
    rb = MOE_ROW_BLOCK
    n_blocks = t_tokens * TOP_K // rb + N_EXPERTS
    padded = (counts + rb - 1) // rb * rb
    pends = jnp.cumsum(padded)
    pstarts = pends - padded
    experts = jnp.arange(N_EXPERTS, dtype=I32)
    pstart_of = jnp.sum(jnp.where(e_t[..., None] == experts, pstarts.astype(I32), 0), axis=-1)
    dest = (pstart_of + pos_t).astype(I32)
    nused = (pends[-1] // rb).astype(I32)
    blk = jnp.arange(n_blocks, dtype=I32)
    last_blk = jnp.minimum(blk, jnp.maximum(nused - 1, 0))
    block_e = jnp.sum((pends[None, :] <= (last_blk * rb)[:, None]).astype(I32), axis=1)
    block_e = jnp.minimum(block_e, N_EXPERTS - 1).astype(I32)
    first = (blk < nused) & ((blk == 0) | (block_e != jnp.roll(block_e, 1)))
    block_ord = (jnp.cumsum(first.astype(I32)) - 1).astype(I32)
    later = (experts[None, :] > experts[:, None]) & (padded[None, :] > 0)
    next_of = jnp.min(jnp.where(later, experts[None, :], N_EXPERTS), axis=1)
    block_next = jnp.sum(jnp.where(block_e[:, None] == experts[None, :], next_of[None, :], 0),
                         axis=1).astype(I32)
    pad_start = (pstarts + counts).astype(I32)
    pad_count = (padded - counts).astype(I32)
    return (dest, block_e, nused.reshape(1), block_ord, block_next, pad_start, pad_count,
            n_blocks * rb)


def _layer(x, c, w_ada, b_ada, w_in, g_cq, w_uq, g_ckv, w_ukv, w_o, ln1_g, ln1_b,
           w_router, router_bias, w_gate, w_up, w_down, ws_gate, ws_up, ws_down, ln2_g, ln2_b):
    bsz, seq, d = x.shape
    t_tokens = bsz * seq
    x2d = x.reshape(t_tokens, d)

    mod = ada_modulation(c, w_ada, b_ada)
    sh1, sc1, g1, sh2, sc2, g2 = [m.reshape(bsz, 1, d) for m in jnp.split(mod, 6, axis=-1)]

    qa, ka, vat, qm, kn, vmt, kr = input_projection(x2d, sc1, sh1, w_in, g_cq, w_uq, g_ckv, w_ukv, seq)
    attn_a = moba_attention(qa, ka, vat, bsz, seq)
    attn_m = mla_attention(qm, kn, kr, vmt, bsz, seq)
    h2, h2p, base = output_projection(attn_a, attn_m, x2d, w_o, g1, sc2, sh2, g2, ln1_g, ln1_b,
                                 ws_gate, ws_up, ws_down, seq)

    e_t, pos_t, w_t, counts = router(h2, w_router, router_bias)
    (dest, block_e, nused, block_ord, block_next, pad_start, pad_count,
     n_rows) = _dispatch_tables(e_t, pos_t, counts, t_tokens)
    x_sorted = dispatch(h2p, dest.T.reshape(-1), pad_start, pad_count, nused, n_rows)
    y_sorted = routed_experts(x_sorted, block_e, nused, block_ord, block_next, w_gate, w_up, w_down)

    tc = COMBINE_TILE
    dest_tiles = dest.reshape(TOP_K, t_tokens // tc, tc).transpose(1, 0, 2).reshape(-1)
    out = combine(y_sorted, dest_tiles, w_t.T, base, g2, ln2_g, ln2_b, seq)
    return out.reshape(bsz, seq, d)


def kernel(x, c, w_ada, b_ada, w_in, g_cq, w_uq, g_ckv, w_ukv, w_o, ln1_g, ln1_b, w_router,
           router_bias, w_gate, w_up, w_down, ws_gate, ws_up, ws_down, ln2_g, ln2_b):
    depth = w_ada.shape[0]
    assert depth == 1, "DeepNorm constants are baked for a single layer"
    return _layer(x, c, w_ada[0], b_ada[0], w_in[0], g_cq[0], w_uq[0], g_ckv[0], w_ukv[0], w_o[0],
                  ln1_g[0], ln1_b[0], w_router[0], router_bias[0], w_gate[0], w_up[0], w_down[0],
                  ws_gate[0], ws_up[0], ws_down[0], ln2_g[0], ln2_b[0])
```

```python
import functools

import jax
import jax.numpy as jnp
import numpy as np
from jax import lax
from jax.experimental import pallas as pl
from jax.experimental.pallas import tpu as pltpu

F32 = jnp.float32
BF16 = jnp.bfloat16
I32 = jnp.int32

HEAD_DIM = 128
N_HEADS = 8
MOBA_BLOCK = 256
MOBA_TOPK = 3
MLA_ROPE_DIM = 64
MLA_RANK = 512
MLA_QK_DIM = HEAD_DIM + MLA_ROPE_DIM
GROUP_WIDTH = N_HEADS * HEAD_DIM
ROPE_THETA = 10000.0
N_EXPERTS = 64
TOP_K = 8
N_GROUPS = 8
TOPK_GROUPS = 4
ROUTED_SCALE = 2.5
MOE_ROW_BLOCK = 256
LN_EPS = 1e-5
RMS_EPS = 1e-6
DEEPNORM_ALPHA = 2.0 ** 0.25

VMEM_LIMIT_BYTES = 56 * 1024 * 1024

NT_DIMS = (((1,), (1,)), ((), ()))


def _params(semantics):
    return pltpu.CompilerParams(dimension_semantics=semantics,
                                vmem_limit_bytes=VMEM_LIMIT_BYTES)


def _silu(x):
    return x * jax.nn.sigmoid(x)


PACK_CHUNKS = 8
PACK_LANES = 128
U32 = jnp.uint32
HIGH_HALF = np.uint32(0xFFFF0000)


def _store_packed_rows(ref, x):
    n_rows, half = x.shape[0], x.shape[1] // 2
    lo = lax.bitcast_convert_type(x[:, :half].astype(BF16).astype(F32), U32) >> 16
    hi = lax.bitcast_convert_type(x[:, half:].astype(BF16).astype(F32), U32) & HIGH_HALF
    packed = lo | hi
    for j in range(PACK_CHUNKS):
        ref[pl.ds(j, n_rows, stride=PACK_CHUNKS), :] = packed[:, j * PACK_LANES:(j + 1) * PACK_LANES]


def _load_packed_rows(ref, first_row, n_rows):
    lo, hi = [], []
    for j in range(PACK_CHUNKS):
        u = ref[pl.ds(first_row * PACK_CHUNKS + j, n_rows, stride=PACK_CHUNKS), :]
        lo.append(lax.bitcast_convert_type(u << 16, F32))
        hi.append(lax.bitcast_convert_type(u & HIGH_HALF, F32))
    return jnp.concatenate(lo, axis=1), jnp.concatenate(hi, axis=1)


def _ada_kernel(c_ref, w_ref, b_ref, o_ref):
    a = _silu(c_ref[...]).astype(BF16)
    o_ref[...] = jnp.dot(a, w_ref[...].astype(BF16), preferred_element_type=F32) + b_ref[...]


def ada_modulation(c, w_ada, b_ada):
    bsz, d = c.shape
    rows = -(-bsz // 16) * 16
    c_pad = jnp.pad(c, ((0, rows - bsz), (0, 0)))
    n = w_ada.shape[1]
    tn = 1024
    out = pl.pallas_call(
        _ada_kernel,
        grid=(n // tn,),
        in_specs=[pl.BlockSpec((rows, d), lambda j: (0, 0)),
                  pl.BlockSpec((d, tn), lambda j: (0, j)),
                  pl.BlockSpec((1, tn), lambda j: (0, j))],
        out_specs=pl.BlockSpec((rows, tn), lambda j: (0, j)),
        out_shape=jax.ShapeDtypeStruct((rows, n), F32),
        compiler_params=_params(("arbitrary",)),
        name="ada_modulation",
    )(c_pad, w_ada, b_ada.reshape(1, n))
    return out[:bsz]


def _rms(x, g):
    return x * lax.rsqrt(jnp.mean(x * x, axis=-1, keepdims=True) + RMS_EPS) * g


def _rope64(x, cos, sin):
    return x * cos + pltpu.roll(x, 64, axis=1) * sin


def _rope32(x, cos, s1, s2):
    return x * cos + pltpu.roll(x, 96, axis=1) * s1 + pltpu.roll(x, 32, axis=1) * s2


def _inproj_kernel(x_ref, sc_ref, sh_ref, w_ref, wkr_ref, wuq_ref, wuk_ref, wuv_ref, gcq_ref, gckv_ref,
                   cosa_ref, sina_ref, cosr_ref, s1_ref, s2_ref,
                   qa_ref, ka_ref, va_ref, qm_ref, kn_ref, vm_ref, kr_ref):
    h = (x_ref[...] * (1.0 + sc_ref[0]) + sh_ref[0]).astype(BF16)
    cosa, sina = cosa_ref[...], sina_ref[...]
    cosr, s1, s2 = cosr_ref[...], s1_ref[...], s2_ref[...]
    gw = GROUP_WIDTH

    def proj(lo, hi):
        return jnp.dot(h, w_ref[:, lo:hi], preferred_element_type=F32)

    for out_ref, base in ((qa_ref, 0), (ka_ref, gw)):
        t = proj(base, base + gw)
        for hd in range(N_HEADS):
            sl = slice(hd * HEAD_DIM, (hd + 1) * HEAD_DIM)
            out_ref[:, sl] = _rope64(t[:, sl], cosa, sina).astype(BF16)
    va_ref[...] = proj(2 * gw, 3 * gw).T.astype(BF16)

    cq = _rms(proj(3 * gw, 3 * gw + MLA_RANK), gcq_ref[...]).astype(BF16)
    qm = jnp.dot(cq, wuq_ref[...], preferred_element_type=F32)
    for hd in range(N_HEADS):
        b0 = hd * 2 * HEAD_DIM
        qm_ref[:, b0:b0 + HEAD_DIM] = qm[:, b0:b0 + HEAD_DIM].astype(BF16)
        qm_ref[:, b0 + HEAD_DIM:b0 + 2 * HEAD_DIM] = _rope32(
            qm[:, b0 + HEAD_DIM:b0 + 2 * HEAD_DIM], cosr, s1, s2).astype(BF16)

    ckv = _rms(proj(3 * gw + MLA_RANK, 3 * gw + 2 * MLA_RANK), gckv_ref[...]).astype(BF16)
    kn_ref[...] = jnp.dot(ckv, wuk_ref[...], preferred_element_type=F32).astype(BF16)
    vm_ref[...] = jnp.dot(ckv, wuv_ref[...], preferred_element_type=F32).T.astype(BF16)

    kr = jnp.dot(h, wkr_ref[...], preferred_element_type=F32)
    kr_ref[...] = _rope32(kr, cosr, s1, s2).astype(BF16)


def _rope_tables(seq):
    pos = jnp.arange(seq, dtype=F32)[:, None]
    inv_a = 1.0 / (ROPE_THETA ** (jnp.arange(0, HEAD_DIM, 2, dtype=F32) / HEAD_DIM))
    ang_a = pos * inv_a[None, :]
    cosa = jnp.concatenate([jnp.cos(ang_a), jnp.cos(ang_a)], axis=1)
    sina = jnp.concatenate([-jnp.sin(ang_a), jnp.sin(ang_a)], axis=1)
    inv_r = 1.0 / (ROPE_THETA ** (jnp.arange(0, MLA_ROPE_DIM, 2, dtype=F32) / MLA_ROPE_DIM))
    ang_r = pos * inv_r[None, :]
    cr, sr = jnp.cos(ang_r), jnp.sin(ang_r)
    z32 = jnp.zeros_like(cr)
    z64 = jnp.zeros((seq, 64), F32)
    cosr = jnp.concatenate([cr, cr, z64], axis=1)
    s1 = jnp.concatenate([-sr, z32, z64], axis=1)
    s2 = jnp.concatenate([z32, sr, z64], axis=1)
    return cosa, sina, cosr, s1, s2


def input_projection(x2d, sc1, sh1, w_in, g_cq, w_uq, g_ckv, w_ukv, seq):
    t_tokens, d = x2d.shape
    tm = 256
    tiles_per_seq = seq // tm
    gw = GROUP_WIDTH
    main_w = 3 * gw + 2 * MLA_RANK
    w_main = w_in[:, :main_w].astype(BF16)
    w_kr = jnp.pad(w_in[:, main_w:], ((0, 0), (0, HEAD_DIM - MLA_ROPE_DIM))).astype(BF16)
    wuq = jnp.pad(w_uq.reshape(MLA_RANK, N_HEADS, MLA_QK_DIM),
                  ((0, 0), (0, 0), (0, 2 * HEAD_DIM - MLA_QK_DIM)))
    wuq = wuq.reshape(MLA_RANK, N_HEADS * 2 * HEAD_DIM).astype(BF16)
    wukv = w_ukv.reshape(MLA_RANK, N_HEADS, 2 * HEAD_DIM)
    wuk = wukv[:, :, :HEAD_DIM].reshape(MLA_RANK, gw).astype(BF16)
    wuv = wukv[:, :, HEAD_DIM:].reshape(MLA_RANK, gw).astype(BF16)
    tables = _rope_tables(seq)

    def const(shape):
        return pl.BlockSpec(shape, lambda i: (0,) * len(shape), pipeline_mode=pl.Buffered(1))

    def rows(width):
        return pl.BlockSpec((tm, width), lambda i: (i, 0))

    mod_spec = pl.BlockSpec((1, 1, d), lambda i: (i // tiles_per_seq, 0, 0))
    tab_spec = pl.BlockSpec((tm, HEAD_DIM), lambda i: (i % tiles_per_seq, 0))
    widths = (gw, gw, gw, 2 * gw, gw, gw, HEAD_DIM)
    transposed = (False, False, True, False, False, True, False)

    def out_spec(width, is_t):
        return pl.BlockSpec((width, tm), lambda i: (0, i)) if is_t else rows(width)

    def out_shape(width, is_t):
        return jax.ShapeDtypeStruct((width, t_tokens) if is_t else (t_tokens, width), BF16)

    return pl.pallas_call(
        _inproj_kernel,
        grid=(t_tokens // tm,),
        in_specs=[rows(d), mod_spec, mod_spec, const((d, main_w)), const((d, HEAD_DIM)),
                  const((MLA_RANK, 2 * gw)), const((MLA_RANK, gw)), const((MLA_RANK, gw)),
                  const((1, MLA_RANK)), const((1, MLA_RANK))] + [tab_spec] * 5,
        out_specs=[out_spec(w, t) for w, t in zip(widths, transposed)],
        out_shape=[out_shape(w, t) for w, t in zip(widths, transposed)],
        compiler_params=_params(("arbitrary",)),
        name="input_projection",
    )(x2d, sc1, sh1, w_main, w_kr, wuq, wuk, wuv, g_cq.reshape(1, -1), g_ckv.reshape(1, -1), *tables)


ATTN_BLOCK = 256
LOG2_E = 1.4426950408889634


def _causal_mask_t():
    key = lax.broadcasted_iota(I32, (ATTN_BLOCK, ATTN_BLOCK), 0)
    qry = lax.broadcasted_iota(I32, (ATTN_BLOCK, ATTN_BLOCK), 1)
    return key <= qry


def _scores_t(q, k_blk):
    return lax.dot_general(k_blk, q, NT_DIMS, preferred_element_type=F32)


def _first_block(s, vt_blk, c, acc_ref):
    s = jnp.where(_causal_mask_t(), s, -jnp.inf)
    m = jnp.max(s, axis=0, keepdims=True)
    p = jnp.exp2((s - m) * c)
    acc_ref[...] = jnp.dot(vt_blk, p.astype(BF16), preferred_element_type=F32)
    return m, jnp.sum(p, axis=0, keepdims=True)


def _next_block(s, vt_blk, bias, c, m, l, acc_ref):
    m_blk = jnp.max(s, axis=0, keepdims=True)
    if bias is not None:
        m_blk = m_blk + bias
    m_new = jnp.maximum(m, m_blk)
    alpha = jnp.exp2((m - m_new) * c)
    p = jnp.exp2((s - (m_new if bias is None else m_new - bias)) * c)
    acc_ref[...] = alpha * acc_ref[...] + jnp.dot(vt_blk, p.astype(BF16), preferred_element_type=F32)
    return m_new, alpha * l + jnp.sum(p, axis=0, keepdims=True)


HEADS_PER_STEP = 8


def _head_cols(g, width=HEAD_DIM):
    return slice(g * width, (g + 1) * width)


def _moba_block_bias(kmean, q, qi, n_blocks):
    gate = lax.dot_general(kmean, q.astype(F32), NT_DIMS, precision=lax.Precision.HIGHEST,
                           preferred_element_type=F32)
    row = lax.broadcasted_iota(I32, gate.shape, 0)
    gt = jnp.where(row < qi, gate, -jnp.inf)
    cnt = jnp.zeros(gate.shape, F32)
    for m in range(n_blocks - 1):
        gm = gt[m:m + 1, :]
        cnt = cnt + jnp.where((gm > gt) | ((gm == gt) & (m < row)), 1.0, 0.0)
    return jnp.where(cnt < float(MOBA_TOPK), 0.0, -jnp.inf).astype(F32)


def _moba_kernel(q_ref, k_ref, vt_ref, o_ref, kmean_ref, acc_ref, *, n_blocks):
    qi = pl.program_id(2)
    blk = MOBA_BLOCK
    heads = HEADS_PER_STEP

    @pl.when(qi == 0)
    def _():
        kmean_ref[...] = jnp.zeros(kmean_ref.shape, F32)
        for g in range(heads):
            for n in range(n_blocks):
                kb = k_ref[0, n * blk:(n + 1) * blk, _head_cols(g)].astype(F32)
                kmean_ref[g, n:n + 1, :] = jnp.sum(kb, axis=0, keepdims=True) * (1.0 / blk)

    c = (HEAD_DIM ** -0.5) * LOG2_E
    own = pl.multiple_of(qi * blk, blk)
    qs = [q_ref[0, :, _head_cols(g)] for g in range(heads)]
    scores = [_scores_t(qs[g], k_ref[0, pl.ds(own, blk), _head_cols(g)]) for g in range(heads)]
    biases = [_moba_block_bias(kmean_ref[g], qs[g], qi, n_blocks) for g in range(heads)]
    carry0 = []
    for g in range(heads):
        carry0 += _first_block(scores[g], vt_ref[_head_cols(g), pl.ds(own, blk)], c, acc_ref.at[g])

    def body(n, carry):
        off = pl.multiple_of(n * blk, blk)
        scores = [_scores_t(qs[g], k_ref[0, pl.ds(off, blk), _head_cols(g)]) for g in range(heads)]
        row = lax.broadcasted_iota(I32, biases[0].shape, 0)
        out = []
        for g in range(heads):
            bias = jnp.sum(jnp.where(row == n, biases[g], 0.0), axis=0, keepdims=True)
            out += _next_block(scores[g], vt_ref[_head_cols(g), pl.ds(off, blk)], bias,
                               c, carry[2 * g], carry[2 * g + 1], acc_ref.at[g])
        return tuple(out)

    final = lax.fori_loop(0, qi, body, tuple(carry0))
    for g in range(heads):
        o_ref[0, :, _head_cols(g)] = (acc_ref[g] / final[2 * g + 1]).T.astype(o_ref.dtype)


def moba_attention(qa, ka, vat, bsz, seq):
    blk = MOBA_BLOCK
    n_blocks = seq // blk
    heads = HEADS_PER_STEP
    gate_rows = -(-n_blocks // 8) * 8
    q3, k3 = (t.reshape(bsz, seq, GROUP_WIDTH) for t in (qa, ka))
    q_spec = pl.BlockSpec((1, blk, heads * HEAD_DIM), lambda b, h, i: (b, i, h))
    kv_spec = pl.BlockSpec((1, seq, heads * HEAD_DIM), lambda b, h, i: (b, 0, h))
    out = pl.pallas_call(
        functools.partial(_moba_kernel, n_blocks=n_blocks),
        grid=(bsz, N_HEADS // heads, n_blocks),
        in_specs=[q_spec, kv_spec, pl.BlockSpec((heads * HEAD_DIM, seq), lambda b, h, i: (h, b))],
        out_specs=q_spec,
        out_shape=jax.ShapeDtypeStruct((bsz, seq, GROUP_WIDTH), BF16),
        scratch_shapes=[pltpu.VMEM((heads, gate_rows, HEAD_DIM), F32),
                        pltpu.VMEM((heads, HEAD_DIM, blk), F32)],
        compiler_params=_params(("arbitrary", "arbitrary", "arbitrary")),
        name="moba_attention",
    )(q3, k3, vat)
    return out.reshape(bsz * seq, GROUP_WIDTH)


def _mla_kernel(q_ref, kn_ref, kr_ref, vt_ref, o_ref, kcat_ref, acc_ref):
    qi = pl.program_id(2)
    blk = ATTN_BLOCK

    heads = HEADS_PER_STEP

    @pl.when(qi == 0)
    def _():
        for g in range(heads):
            kcat_ref[g, :, :HEAD_DIM] = kn_ref[0, :, _head_cols(g)]
            kcat_ref[g, :, HEAD_DIM:] = kr_ref[0]

    c = (MLA_QK_DIM ** -0.5) * LOG2_E
    own = pl.multiple_of(qi * blk, blk)
    qs = [q_ref[0, :, _head_cols(g, 2 * HEAD_DIM)] for g in range(heads)]
    scores = [_scores_t(qs[g], kcat_ref[g, pl.ds(own, blk), :]) for g in range(heads)]
    carry0 = []
    for g in range(heads):
        carry0 += _first_block(scores[g], vt_ref[_head_cols(g), pl.ds(own, blk)], c, acc_ref.at[g])

    def body(n, carry):
        off = pl.multiple_of(n * blk, blk)
        scores = [_scores_t(qs[g], kcat_ref[g, pl.ds(off, blk), :]) for g in range(heads)]
        out = []
        for g in range(heads):
            out += _next_block(scores[g], vt_ref[_head_cols(g), pl.ds(off, blk)], None, c,
                               carry[2 * g], carry[2 * g + 1], acc_ref.at[g])
        return tuple(out)

    final = lax.fori_loop(0, qi, body, tuple(carry0))
    for g in range(heads):
        o_ref[0, :, _head_cols(g)] = (acc_ref[g] / final[2 * g + 1]).T.astype(o_ref.dtype)


def mla_attention(qm, kn, kr, vmt, bsz, seq):
    blk = ATTN_BLOCK
    heads = HEADS_PER_STEP
    q3 = qm.reshape(bsz, seq, 2 * GROUP_WIDTH)
    kn3 = kn.reshape(bsz, seq, GROUP_WIDTH)
    kr3 = kr.reshape(bsz, seq, HEAD_DIM)
    out = pl.pallas_call(
        _mla_kernel,
        grid=(bsz, N_HEADS // heads, seq // blk),
        in_specs=[pl.BlockSpec((1, blk, heads * 2 * HEAD_DIM), lambda b, h, i: (b, i, h)),
                  pl.BlockSpec((1, seq, heads * HEAD_DIM), lambda b, h, i: (b, 0, h)),
                  pl.BlockSpec((1, seq, HEAD_DIM), lambda b, h, i: (b, 0, 0)),
                  pl.BlockSpec((heads * HEAD_DIM, seq), lambda b, h, i: (h, b))],
        out_specs=pl.BlockSpec((1, blk, heads * HEAD_DIM), lambda b, h, i: (b, i, h)),
        out_shape=jax.ShapeDtypeStruct((bsz, seq, GROUP_WIDTH), BF16),
        scratch_shapes=[pltpu.VMEM((heads, seq, 2 * HEAD_DIM), BF16),
                        pltpu.VMEM((heads, HEAD_DIM, blk), F32)],
        compiler_params=_params(("arbitrary", "arbitrary", "arbitrary")),
        name="mla_attention",
    )(q3, kn3, kr3, vmt)
    return out.reshape(bsz * seq, GROUP_WIDTH)


def _layer_norm(y, g, b):
    mu = jnp.mean(y, axis=-1, keepdims=True)
    yc = y - mu
    var = jnp.mean(yc * yc, axis=-1, keepdims=True)
    return yc * lax.rsqrt(var + LN_EPS) * g + b


def _outproj_kernel(a_ref, m_ref, x_ref, woa_ref, wom_ref, g1_ref, sc2_ref, sh2_ref, g2_ref,
                    lng_ref, lnb_ref, wsg_ref, wsu_ref, wsd_ref, h2_ref, h2p_ref, base_ref):
    f = (jnp.dot(a_ref[...], woa_ref[...], preferred_element_type=F32)
         + jnp.dot(m_ref[...], wom_ref[...], preferred_element_type=F32))
    x1 = _layer_norm(DEEPNORM_ALPHA * x_ref[...] + (1.0 + g1_ref[0]) * f, lng_ref[...], lnb_ref[...])
    h2 = x1 * (1.0 + sc2_ref[0]) + sh2_ref[0]
    h2_ref[...] = h2
    _store_packed_rows(h2p_ref, h2)
    hb = h2.astype(BF16)
    act = (_silu(jnp.dot(hb, wsg_ref[...], preferred_element_type=F32))
           * jnp.dot(hb, wsu_ref[...], preferred_element_type=F32)).astype(BF16)
    shared = jnp.dot(act, wsd_ref[...], preferred_element_type=F32)
    base_ref[...] = DEEPNORM_ALPHA * x1 + (1.0 + g2_ref[0]) * shared


def output_projection(attn_a, attn_m, x2d, w_o, g1, sc2, sh2, g2, ln_g, ln_b,
                      ws_gate, ws_up, ws_down, seq):
    t_tokens, d = x2d.shape
    tm = 256
    tiles_per_seq = seq // tm
    gw = GROUP_WIDTH
    ff = ws_gate.shape[1]

    def const(shape):
        return pl.BlockSpec(shape, lambda i: (0,) * len(shape), pipeline_mode=pl.Buffered(1))

    def rows(width):
        return pl.BlockSpec((tm, width), lambda i: (i, 0))

    mod_spec = pl.BlockSpec((1, 1, d), lambda i: (i // tiles_per_seq, 0, 0))
    return pl.pallas_call(
        _outproj_kernel,
        grid=(t_tokens // tm,),
        in_specs=[rows(gw), rows(gw), rows(d), const((gw, d)), const((gw, d)),
                  mod_spec, mod_spec, mod_spec, mod_spec, const((1, d)), const((1, d)),
                  const((d, ff)), const((d, ff)), const((ff, d))],
        out_specs=[rows(d), pl.BlockSpec((tm * PACK_CHUNKS, PACK_LANES), lambda i: (i, 0)), rows(d)],
        out_shape=[jax.ShapeDtypeStruct((t_tokens, d), F32),
                   jax.ShapeDtypeStruct((t_tokens * PACK_CHUNKS, PACK_LANES), U32),
                   jax.ShapeDtypeStruct((t_tokens, d), F32)],
        compiler_params=_params(("arbitrary",)),
        name="output_projection",
    )(attn_a, attn_m, x2d, w_o[:gw].astype(BF16), w_o[gw:].astype(BF16), g1, sc2, sh2, g2,
      ln_g.reshape(1, d), ln_b.reshape(1, d),
      ws_gate.astype(BF16), ws_up.astype(BF16), ws_down.astype(BF16))


ROUTER_TILE = 512


def _router_kernel(h_ref, wrt_ref, bias_ref, tri_ref, ltri_ref,
                   e_ref, pos_ref, w_ref, cnt_ref, run_ref):
    tr = ROUTER_TILE
    epg = N_EXPERTS // N_GROUPS

    @pl.when(pl.program_id(0) == 0)
    def _():
        run_ref[...] = jnp.zeros(run_ref.shape, F32)

    logits = lax.dot_general(wrt_ref[...], h_ref[...], NT_DIMS, precision=lax.Precision.HIGHEST,
                             preferred_element_type=F32)
    scores = jax.nn.sigmoid(logits)
    sel_scores = scores + bias_ref[...]

    sub = lax.broadcasted_iota(I32, (epg, tr), 0)
    grp_rows = []
    for g in range(N_GROUPS):
        xg = sel_scores[g * epg:(g + 1) * epg, :]
        m1 = jnp.max(xg, axis=0, keepdims=True)
        first = jnp.min(jnp.where(xg == m1, sub, epg), axis=0, keepdims=True)
        m2 = jnp.max(jnp.where(sub == first, -jnp.inf, xg), axis=0, keepdims=True)
        grp_rows.append(m1 + m2)
    grp = jnp.concatenate(grp_rows, axis=0)

    giota = lax.broadcasted_iota(I32, (N_GROUPS, tr), 0)
    gcnt = jnp.zeros((N_GROUPS, tr), F32)
    for m in range(N_GROUPS):
        rm = grp[m:m + 1, :]
        gcnt = gcnt + jnp.where((rm > grp) | ((rm == grp) & (m < giota)), 1.0, 0.0)
    gkeep = jnp.where(gcnt < float(TOPK_GROUPS), 1.0, 0.0)
    gmask = jnp.concatenate(
        [jnp.broadcast_to(gkeep[g:g + 1, :], (epg, tr)) for g in range(N_GROUPS)], axis=0)
    masked = jnp.where(gmask > 0.0, sel_scores, -jnp.inf)

    eiota = lax.broadcasted_iota(I32, (N_EXPERTS, tr), 0)

    free = jnp.ones((N_EXPERTS, tr), F32)
    for _ in range(TOP_K):
        top = jnp.max(jnp.where(free > 0.0, masked, -jnp.inf), axis=0, keepdims=True)
        first = jnp.min(jnp.where((masked == top) & (free > 0.0), eiota, N_EXPERTS),
                        axis=0, keepdims=True)
        free = jnp.where(eiota == first, 0.0, free)
    sel = free == 0.0

    sw = jnp.where(sel, scores, 0.0)
    wn = sw / jnp.sum(sw, axis=0, keepdims=True) * ROUTED_SCALE

    selb = jnp.where(sel, 1.0, 0.0).astype(BF16)
    cum = jnp.dot(selb, tri_ref[...], preferred_element_type=F32)
    pos = run_ref[...] + cum - 1.0
    run_ref[...] = run_ref[...] + cum[:, tr - 1:tr]
    cnt_ref[...] = jnp.broadcast_to(run_ref[...], cnt_ref.shape).astype(I32)

    slot = jnp.dot(ltri_ref[...], selb, preferred_element_type=F32)
    ef = eiota.astype(F32)
    e_rows, p_rows, w_rows = [], [], []
    for j in range(TOP_K):
        mj = sel & (slot == float(j))
        e_rows.append(jnp.sum(jnp.where(mj, ef, 0.0), axis=0, keepdims=True))
        p_rows.append(jnp.sum(jnp.where(mj, pos, 0.0), axis=0, keepdims=True))
        w_rows.append(jnp.sum(jnp.where(mj, wn, 0.0), axis=0, keepdims=True))
    e_ref[...] = jnp.concatenate(e_rows, axis=0).astype(I32)
    pos_ref[...] = jnp.concatenate(p_rows, axis=0).astype(I32)
    w_ref[...] = jnp.concatenate(w_rows, axis=0)


def router(h2, w_router, router_bias):
    t_tokens, d = h2.shape
    tr = ROUTER_TILE
    tri = jnp.asarray(np.triu(np.ones((tr, tr), np.float32)), BF16)
    ltri = jnp.asarray(np.tril(np.ones((N_EXPERTS, N_EXPERTS), np.float32), -1), BF16)

    def const(shape):
        return pl.BlockSpec(shape, lambda i: (0,) * len(shape))

    tok = pl.BlockSpec((TOP_K, tr), lambda i: (0, i))
    e_t, pos_t, w_t, cnt = pl.pallas_call(
        _router_kernel,
        grid=(t_tokens // tr,),
        in_specs=[pl.BlockSpec((tr, d), lambda i: (i, 0)), const((N_EXPERTS, d)),
                  const((N_EXPERTS, 1)), const((tr, tr)), const((N_EXPERTS, N_EXPERTS))],
        out_specs=[tok, tok, tok, const((N_EXPERTS, 128))],
        out_shape=[jax.ShapeDtypeStruct((TOP_K, t_tokens), I32),
                   jax.ShapeDtypeStruct((TOP_K, t_tokens), I32),
                   jax.ShapeDtypeStruct((TOP_K, t_tokens), F32),
                   jax.ShapeDtypeStruct((N_EXPERTS, 128), I32)],
        scratch_shapes=[pltpu.VMEM((N_EXPERTS, 1), F32)],
        compiler_params=_params(("arbitrary",)),
        name="router",
    )(h2, w_router.T, router_bias.reshape(N_EXPERTS, 1), tri, ltri)
    return e_t, pos_t, w_t, cnt[:, 0]


DISPATCH_TILE = 128


def _dispatch_kernel(pad_start_ref, pad_count_ref, nused_ref, d_ref, h_ref, xs_ref, zero_ref, sem):
    tt = DISPATCH_TILE
    pc = PACK_CHUNKS
    rb = MOE_ROW_BLOCK
    n_copies = tt * TOP_K
    n_blocks = xs_ref.shape[0] // (rb * pc)

    @pl.when(pl.program_id(0) == 0)
    def _():
        zero_ref[...] = jnp.zeros(zero_ref.shape, U32)
        nused = nused_ref[0]

        def pad_rows(row, n):
            dst = xs_ref.at[pl.ds(pl.multiple_of(row * pc, pc), n * pc), :]
            return pltpu.make_async_copy(zero_ref.at[pl.ds(0, n * pc), :], dst, sem.at[1])

        def tail_block(b):
            dst = xs_ref.at[pl.ds(pl.multiple_of(b * (rb * pc), rb * pc), rb * pc), :]
            return pltpu.make_async_copy(zero_ref, dst, sem.at[1])

        def per_expert(e, carry, wait):
            base, n = pad_start_ref[e], pad_count_ref[e]
            for bit in reversed(range(rb.bit_length() - 1)):
                size = 1 << bit
                offset = (n >> (bit + 1)) << (bit + 1)

                @pl.when(((n >> bit) & 1) == 1)
                def _():
                    copy = pad_rows(base + offset, size)
                    if wait:
                        copy.wait()
                    else:
                        copy.start()
            return carry

        def start_tail(b, carry):
            tail_block(b).start()
            return carry

        def wait_tail(b, carry):
            tail_block(b).wait()
            return carry

        lax.fori_loop(0, N_EXPERTS, functools.partial(per_expert, wait=False), 0)
        lax.fori_loop(nused, n_blocks, start_tail, 0)
        lax.fori_loop(0, N_EXPERTS, functools.partial(per_expert, wait=True), 0)
        lax.fori_loop(nused, n_blocks, wait_tail, 0)

    for j in range(tt):
        for k in range(TOP_K):
            c = j * TOP_K + k
            dst = xs_ref.at[pl.ds(pl.multiple_of(d_ref[c] * pc, pc), pc), :]
            pltpu.make_async_copy(h_ref.at[pl.ds(j * pc, pc), :], dst, sem.at[0]).start(priority=c % 2)
    tile_rows = xs_ref.at[pl.ds(0, n_copies * pc), :]
    pltpu.make_async_copy(tile_rows, tile_rows, sem.at[0]).wait()


def dispatch(h2p, dest_tok_major, pad_start, pad_count, nused, n_rows):
    tt = DISPATCH_TILE
    pc = PACK_CHUNKS
    t_tokens = h2p.shape[0] // pc
    n_copies = tt * TOP_K
    grid_spec = pltpu.PrefetchScalarGridSpec(
        num_scalar_prefetch=3,
        grid=(t_tokens // tt,),
        in_specs=[pl.BlockSpec((n_copies,), lambda i, ps, pn, nu: (i,), memory_space=pltpu.SMEM),
                  pl.BlockSpec((tt * pc, PACK_LANES), lambda i, ps, pn, nu: (i, 0))],
        out_specs=pl.BlockSpec(memory_space=pl.ANY),
        scratch_shapes=[pltpu.VMEM((MOE_ROW_BLOCK * pc, PACK_LANES), U32),
                        pltpu.SemaphoreType.DMA((2,))],
    )
    return pl.pallas_call(
        _dispatch_kernel,
        grid_spec=grid_spec,
        out_shape=jax.ShapeDtypeStruct((n_rows * pc, PACK_LANES), U32),
        compiler_params=_params(("arbitrary",)),
        name="dispatch",
    )(pad_start, pad_count, nused, dest_tok_major, h2p)


EXPERT_BLOCKS_PER_STEP = 4


def _experts_kernel(be_ref, nused_ref, ord_ref, next_ref, x_ref, wg_hbm, wu_hbm, wd_hbm, y_ref,
                    wg_stage, wu_stage, wd_stage, wg_bf, wu_bf, wd_bf, sem):
    nused = nused_ref[0]
    rb = MOE_ROW_BLOCK
    pc = PACK_CHUNKS

    def weight_copies(e, slot):
        return (pltpu.make_async_copy(wg_hbm.at[e], wg_stage.at[slot], sem.at[slot]),
                pltpu.make_async_copy(wu_hbm.at[e], wu_stage.at[slot], sem.at[slot]),
                pltpu.make_async_copy(wd_hbm.at[e], wd_stage.at[slot], sem.at[slot]))

    def one_block(i, sub):
        y_rows = y_ref.at[pl.ds(sub * rb * pc, rb * pc), :]

        @pl.when(i < nused)
        def _():
            @pl.when((i == 0) | (be_ref[i] != be_ref[jnp.maximum(i - 1, 0)]))
            def _():
                e = be_ref[i]
                slot = ord_ref[i] % 2

                @pl.when(i == 0)
                def _():
                    for c in weight_copies(e, slot):
                        c.start()

                for c in weight_copies(e, slot):
                    c.wait()
                wg_bf[...] = wg_stage[slot].astype(BF16)
                wu_bf[...] = wu_stage[slot].astype(BF16)
                wd_bf[...] = wd_stage[slot].astype(BF16)

                @pl.when(next_ref[i] < N_EXPERTS)
                def _():
                    for c in weight_copies(next_ref[i], 1 - slot):
                        c.start()

            lo, hi = _load_packed_rows(x_ref, sub * rb, rb)
            xb = jnp.concatenate([lo.astype(BF16), hi.astype(BF16)], axis=1)
            act = (_silu(jnp.dot(xb, wg_bf[...], preferred_element_type=F32))
                   * jnp.dot(xb, wu_bf[...], preferred_element_type=F32)).astype(BF16)
            _store_packed_rows(y_rows, jnp.dot(act, wd_bf[...], preferred_element_type=F32))

        @pl.when(i >= nused)
        def _():
            y_rows[...] = jnp.zeros(y_rows.shape, U32)

    for sub in range(EXPERT_BLOCKS_PER_STEP):
        one_block(pl.program_id(0) * EXPERT_BLOCKS_PER_STEP + sub, sub)


def routed_experts(x_sorted, block_e, nused, block_ord, block_next, w_gate, w_up, w_down):
    rb = MOE_ROW_BLOCK
    pc = PACK_CHUNKS
    per_step = EXPERT_BLOCKS_PER_STEP
    n_blocks = x_sorted.shape[0] // (rb * pc)
    assert n_blocks % per_step == 0
    d, ff = w_gate.shape[1], w_gate.shape[2]
    hbm = pl.BlockSpec(memory_space=pl.ANY)

    def x_index(i, be, nu, od, nx):
        return (jnp.minimum(i, jnp.maximum(nu[0] - 1, 0) // per_step), 0)

    grid_spec = pltpu.PrefetchScalarGridSpec(
        num_scalar_prefetch=4,
        grid=(n_blocks // per_step,),
        in_specs=[pl.BlockSpec((per_step * rb * pc, PACK_LANES), x_index), hbm, hbm, hbm],
        out_specs=pl.BlockSpec((per_step * rb * pc, PACK_LANES), lambda i, be, nu, od, nx: (i, 0)),
        scratch_shapes=[pltpu.VMEM((2, d, ff), F32), pltpu.VMEM((2, d, ff), F32),
                        pltpu.VMEM((2, ff, d), F32),
                        pltpu.VMEM((d, ff), BF16), pltpu.VMEM((d, ff), BF16),
                        pltpu.VMEM((ff, d), BF16), pltpu.SemaphoreType.DMA((2,))],
    )
    return pl.pallas_call(
        _experts_kernel,
        grid_spec=grid_spec,
        out_shape=jax.ShapeDtypeStruct(x_sorted.shape, U32),
        compiler_params=_params(("arbitrary",)),
        name="routed_experts",
    )(block_e, nused, block_ord, block_next, x_sorted, w_gate, w_up, w_down)


COMBINE_TILE = 128


def _combine_kernel(d_cur_ref, d_next_ref, y_ref, w_ref, base_ref, g2_ref, lng_ref, lnb_ref,
                    o_ref, ybuf_a, ybuf_b, sem):
    i = pl.program_id(0)
    n_steps = pl.num_programs(0)
    tc = COMBINE_TILE
    n_copies = TOP_K * tc
    pc = PACK_CHUNKS

    def row_copy(row, buf, r, s):
        src = y_ref.at[pl.ds(pl.multiple_of(row * pc, pc), pc), :]
        return pltpu.make_async_copy(src, buf.at[pl.ds(r * pc, pc), :], sem.at[s])

    def wait_rows(buf, s):
        pltpu.make_async_copy(y_ref.at[pl.ds(0, n_copies * pc), :], buf, sem.at[s]).wait()

    @pl.when(i == 0)
    def _():
        def body(r, carry):
            src = y_ref.at[pl.ds(pl.multiple_of(d_cur_ref[r] * pc, pc), pc), :]
            dst = ybuf_a.at[pl.ds(pl.multiple_of(r * pc, pc), pc), :]
            pltpu.make_async_copy(src, dst, sem.at[0]).start()
            return carry
        lax.fori_loop(0, n_copies, body, 0, unroll=8)

    def step(cur, cur_s, nxt, nxt_s):
        wait_rows(cur, cur_s)
        for r in range(n_copies):
            row_copy(d_next_ref[r], nxt, r, nxt_s).start(priority=r % 2)
        w = w_ref[...]
        lo, hi = _load_packed_rows(cur, 0, tc)
        routed_lo, routed_hi = w[:, 0:1] * lo, w[:, 0:1] * hi
        for k in range(1, TOP_K):
            lo, hi = _load_packed_rows(cur, k * tc, tc)
            routed_lo = routed_lo + w[:, k:k + 1] * lo
            routed_hi = routed_hi + w[:, k:k + 1] * hi
        routed = jnp.concatenate([routed_lo, routed_hi], axis=1)
        y = base_ref[...] + (1.0 + g2_ref[0]) * routed
        o_ref[...] = _layer_norm(y, lng_ref[...], lnb_ref[...])

    @pl.when(i % 2 == 0)
    def _():
        step(ybuf_a, 0, ybuf_b, 1)

    @pl.when(i % 2 == 1)
    def _():
        step(ybuf_b, 1, ybuf_a, 0)

    @pl.when((i == n_steps - 1) & (i % 2 == 0))
    def _():
        wait_rows(ybuf_b, 1)

    @pl.when((i == n_steps - 1) & (i % 2 == 1))
    def _():
        wait_rows(ybuf_a, 0)


def combine(y_sorted, dest_tiles, w_tok, base, g2, ln_g, ln_b, seq):
    t_tokens, d = base.shape
    tc = COMBINE_TILE
    n_steps = t_tokens // tc
    tiles_per_seq = seq // tc
    n_copies = TOP_K * tc

    def const(shape):
        return pl.BlockSpec(shape, lambda i: (0,) * len(shape))

    return pl.pallas_call(
        _combine_kernel,
        grid=(n_steps,),
        in_specs=[pl.BlockSpec((n_copies,), lambda i: (i,), memory_space=pltpu.SMEM),
                  pl.BlockSpec((n_copies,), lambda i: (jnp.minimum(i + 1, n_steps - 1),),
                               memory_space=pltpu.SMEM),
                  pl.BlockSpec(memory_space=pl.ANY),
                  pl.BlockSpec((tc, TOP_K), lambda i: (i, 0)),
                  pl.BlockSpec((tc, d), lambda i: (i, 0)),
                  pl.BlockSpec((1, 1, d), lambda i: (i // tiles_per_seq, 0, 0)),
                  const((1, d)), const((1, d))],
        out_specs=pl.BlockSpec((tc, d), lambda i: (i, 0)),
        out_shape=jax.ShapeDtypeStruct((t_tokens, d), F32),
        scratch_shapes=[pltpu.VMEM((n_copies * PACK_CHUNKS, PACK_LANES), U32),
                        pltpu.VMEM((n_copies * PACK_CHUNKS, PACK_LANES), U32),
                        pltpu.SemaphoreType.DMA((2,))],
        compiler_params=_params(("arbitrary",)),
        name="combine",
    )(dest_tiles, dest_tiles, y_sorted, w_tok, base, g2, ln_g.reshape(1, d), ln_b.reshape(1, d))


def _dispatch_tables(e_t, pos_t, counts, t_tokens):
    rb = MOE_ROW_BLOCK
    n_blocks = t_tokens * TOP_K // rb + N_EXPERTS
    padded = (counts + rb - 1) // rb * rb
    pends = jnp.cumsum(padded)
    pstarts = pends - padded
    experts = jnp.arange(N_EXPERTS, dtype=I32)
    pstart_of = jnp.sum(jnp.where(e_t[..., None] == experts, pstarts.astype(I32), 0), axis=-1)
    dest = (pstart_of + pos_t).astype(I32)
    nused = (pends[-1] // rb).astype(I32)
    blk = jnp.arange(n_blocks, dtype=I32)
    last_blk = jnp.minimum(blk, jnp.maximum(nused - 1, 0))
    block_e = jnp.sum((pends[None, :] <= (last_blk * rb)[:, None]).astype(I32), axis=1)
    block_e = jnp.minimum(block_e, N_EXPERTS - 1).astype(I32)
    first = (blk < nused) & ((blk == 0) | (block_e != jnp.roll(block_e, 1)))
    block_ord = (jnp.cumsum(first.astype(I32)) - 1).astype(I32)
    later = (experts[None, :] > experts[:, None]) & (padded[None, :] > 0)
    next_of = jnp.min(jnp.where(later, experts[None, :], N_EXPERTS), axis=1)
    block_next = jnp.sum(jnp.where(block_e[:, None] == experts[None, :], next_of[None, :], 0),
                         axis=1).astype(I32)
    pad_start = (pstarts + counts).astype(I32)
    pad_count = (padded - counts).astype(I32)
    return (dest, block_e, nused.reshape(1), block_ord, block_next, pad_start, pad_count,
            n_blocks * rb)


def _layer(x, c, w_ada, b_ada, w_in, g_cq, w_uq, g_ckv, w_ukv, w_o, ln1_g, ln1_b,
           w_router, router_bias, w_gate, w_up, w_down, ws_gate, ws_up, ws_down, ln2_g, ln2_b):
    bsz, seq, d = x.shape
    t_tokens = bsz * seq
    x2d = x.reshape(t_tokens, d)

    mod = ada_modulation(c, w_ada, b_ada)
    sh1, sc1, g1, sh2, sc2, g2 = [m.reshape(bsz, 1, d) for m in jnp.split(mod, 6, axis=-1)]

    qa, ka, vat, qm, kn, vmt, kr = input_projection(x2d, sc1, sh1, w_in, g_cq, w_uq, g_ckv, w_ukv, seq)
    attn_a = moba_attention(qa, ka, vat, bsz, seq)
    attn_m = mla_attention(qm, kn, kr, vmt, bsz, seq)
    h2, h2p, base = output_projection(attn_a, attn_m, x2d, w_o, g1, sc2, sh2, g2, ln1_g, ln1_b,
                                 ws_gate, ws_up, ws_down, seq)

    e_t, pos_t, w_t, counts = router(h2, w_router, router_bias)
    (dest, block_e, nused, block_ord, block_next, pad_start, pad_count,
     n_rows) = _dispatch_tables(e_t, pos_t, counts, t_tokens)
    x_sorted = dispatch(h2p, dest.T.reshape(-1), pad_start, pad_count, nused, n_rows)
    y_sorted = routed_experts(x_sorted, block_e, nused, block_ord, block_next, w_gate, w_up, w_down)

    tc = COMBINE_TILE
    dest_tiles = dest.reshape(TOP_K, t_tokens // tc, tc).transpose(1, 0, 2).reshape(-1)
    out = combine(y_sorted, dest_tiles, w_t.T, base, g2, ln2_g, ln2_b, seq)
    return out.reshape(bsz, seq, d)


def kernel(x, c, w_ada, b_ada, w_in, g_cq, w_uq, g_ckv, w_ukv, w_o, ln1_g, ln1_b, w_router,
           router_bias, w_gate, w_up, w_down, ws_gate, ws_up, ws_down, ln2_g, ln2_b):
    depth = w_ada.shape[0]
    assert depth == 1, "DeepNorm constants are baked for a single layer"
    return _layer(x, c, w_ada[0], b_ada[0], w_in[0], g_cq[0], w_uq[0], g_ckv[0], w_ukv[0], w_o[0],
                  ln1_g[0], ln1_b[0], w_router[0], router_bias[0], w_gate[0], w_up[0], w_down[0],
                  ws_gate[0], ws_up[0], ws_down[0], ln2_g[0], ln2_b[0])
```

```python
import functools

import jax
import jax.numpy as jnp
import numpy as np
from jax import lax
from jax.experimental import pallas as pl
from jax.experimental.pallas import tpu as pltpu

F32 = jnp.float32
BF16 = jnp.bfloat16
I32 = jnp.int32

HEAD_DIM = 128
N_HEADS = 8
MOBA_BLOCK = 256
MOBA_TOPK = 3
MLA_ROPE_DIM = 64
MLA_RANK = 512
MLA_QK_DIM = HEAD_DIM + MLA_ROPE_DIM
GROUP_WIDTH = N_HEADS * HEAD_DIM
ROPE_THETA = 10000.0
N_EXPERTS = 64
TOP_K = 8
N_GROUPS = 8
TOPK_GROUPS = 4
ROUTED_SCALE = 2.5
MOE_ROW_BLOCK = 256
LN_EPS = 1e-5
RMS_EPS = 1e-6
DEEPNORM_ALPHA = 2.0 ** 0.25

VMEM_LIMIT_BYTES = 56 * 1024 * 1024

NT_DIMS = (((1,), (1,)), ((), ()))


def _params(semantics):
    return pltpu.CompilerParams(dimension_semantics=semantics,
                                vmem_limit_bytes=VMEM_LIMIT_BYTES)


def _silu(x):
    return x * jax.nn.sigmoid(x)


PACK_CHUNKS = 8
PACK_LANES = 128
U32 = jnp.uint32
HIGH_HALF = np.uint32(0xFFFF0000)


def _store_packed_rows(ref, x):
    n_rows, half = x.shape[0], x.shape[1] // 2
    lo = lax.bitcast_convert_type(x[:, :half].astype(BF16).astype(F32), U32) >> 16
    hi = lax.bitcast_convert_type(x[:, half:].astype(BF16).astype(F32), U32) & HIGH_HALF
    packed = lo | hi
    for j in range(PACK_CHUNKS):
        ref[pl.ds(j, n_rows, stride=PACK_CHUNKS), :] = packed[:, j * PACK_LANES:(j + 1) * PACK_LANES]


def _load_packed_rows(ref, first_row, n_rows):
    lo, hi = [], []
    for j in range(PACK_CHUNKS):
        u = ref[pl.ds(first_row * PACK_CHUNKS + j, n_rows, stride=PACK_CHUNKS), :]
        lo.append(lax.bitcast_convert_type(u << 16, F32))
        hi.append(lax.bitcast_convert_type(u & HIGH_HALF, F32))
    return jnp.concatenate(lo, axis=1), jnp.concatenate(hi, axis=1)


def _ada_kernel(c_ref, w_ref, b_ref, o_ref):
    a = _silu(c_ref[...]).astype(BF16)
    o_ref[...] = jnp.dot(a, w_ref[...].astype(BF16), preferred_element_type=F32) + b_ref[...]


def ada_modulation(c, w_ada, b_ada):
    bsz, d = c.shape
    rows = -(-bsz // 16) * 16
    c_pad = jnp.pad(c, ((0, rows - bsz), (0, 0)))
    n = w_ada.shape[1]
    tn = 1024
    out = pl.pallas_call(
        _ada_kernel,
        grid=(n // tn,),
        in_specs=[pl.BlockSpec((rows, d), lambda j: (0, 0)),
                  pl.BlockSpec((d, tn), lambda j: (0, j)),
                  pl.BlockSpec((1, tn), lambda j: (0, j))],
        out_specs=pl.BlockSpec((rows, tn), lambda j: (0, j)),
        out_shape=jax.ShapeDtypeStruct((rows, n), F32),
        compiler_params=_params(("arbitrary",)),
        name="ada_modulation",
    )(c_pad, w_ada, b_ada.reshape(1, n))
    return out[:bsz]


def _rms(x, g):
    return x * lax.rsqrt(jnp.mean(x * x, axis=-1, keepdims=True) + RMS_EPS) * g


def _rope64(x, cos, sin):
    return x * cos + pltpu.roll(x, 64, axis=1) * sin


def _rope32(x, cos, s1, s2):
    return x * cos + pltpu.roll(x, 96, axis=1) * s1 + pltpu.roll(x, 32, axis=1) * s2


def _inproj_kernel(x_ref, sc_ref, sh_ref, w_ref, wkr_ref, wuq_ref, wuk_ref, wuv_ref, gcq_ref, gckv_ref,
                   cosa_ref, sina_ref, cosr_ref, s1_ref, s2_ref,
                   qa_ref, ka_ref, va_ref, qm_ref, kn_ref, vm_ref, kr_ref):
    h = (x_ref[...] * (1.0 + sc_ref[0]) + sh_ref[0]).astype(BF16)
    cosa, sina = cosa_ref[...], sina_ref[...]
    cosr, s1, s2 = cosr_ref[...], s1_ref[...], s2_ref[...]
    gw = GROUP_WIDTH

    def proj(lo, hi):
        return jnp.dot(h, w_ref[:, lo:hi], preferred_element_type=F32)

    for out_ref, base in ((qa_ref, 0), (ka_ref, gw)):
        t = proj(base, base + gw)
        for hd in range(N_HEADS):
            sl = slice(hd * HEAD_DIM, (hd + 1) * HEAD_DIM)
            out_ref[:, sl] = _rope64(t[:, sl], cosa, sina).astype(BF16)
    va_ref[...] = proj(2 * gw, 3 * gw).T.astype(BF16)

    cq = _rms(proj(3 * gw, 3 * gw + MLA_RANK), gcq_ref[...]).astype(BF16)
    qm = jnp.dot(cq, wuq_ref[...], preferred_element_type=F32)
    for hd in range(N_HEADS):
        b0 = hd * 2 * HEAD_DIM
        qm_ref[:, b0:b0 + HEAD_DIM] = qm[:, b0:b0 + HEAD_DIM].astype(BF16)
        qm_ref[:, b0 + HEAD_DIM:b0 + 2 * HEAD_DIM] = _rope32(
            qm[:, b0 + HEAD_DIM:b0 + 2 * HEAD_DIM], cosr, s1, s2).astype(BF16)

    ckv = _rms(proj(3 * gw + MLA_RANK, 3 * gw + 2 * MLA_RANK), gckv_ref[...]).astype(BF16)
    kn_ref[...] = jnp.dot(ckv, wuk_ref[...], preferred_element_type=F32).astype(BF16)
    vm_ref[...] = jnp.dot(ckv, wuv_ref[...], preferred_element_type=F32).T.astype(BF16)

    kr = jnp.dot(h, wkr_ref[...], preferred_element_type=F32)
    kr_ref[...] = _rope32(kr, cosr, s1, s2).astype(BF16)


def _rope_tables(seq):
    pos = jnp.arange(seq, dtype=F32)[:, None]
    inv_a = 1.0 / (ROPE_THETA ** (jnp.arange(0, HEAD_DIM, 2, dtype=F32) / HEAD_DIM))
    ang_a = pos * inv_a[None, :]
    cosa = jnp.concatenate([jnp.cos(ang_a), jnp.cos(ang_a)], axis=1)
    sina = jnp.concatenate([-jnp.sin(ang_a), jnp.sin(ang_a)], axis=1)
    inv_r = 1.0 / (ROPE_THETA ** (jnp.arange(0, MLA_ROPE_DIM, 2, dtype=F32) / MLA_ROPE_DIM))
    ang_r = pos * inv_r[None, :]
    cr, sr = jnp.cos(ang_r), jnp.sin(ang_r)
    z32 = jnp.zeros_like(cr)
    z64 = jnp.zeros((seq, 64), F32)
    cosr = jnp.concatenate([cr, cr, z64], axis=1)
    s1 = jnp.concatenate([-sr, z32, z64], axis=1)
    s2 = jnp.concatenate([z32, sr, z64], axis=1)
    return cosa, sina, cosr, s1, s2


def input_projection(x2d, sc1, sh1, w_in, g_cq, w_uq, g_ckv, w_ukv, seq):
    t_tokens, d = x2d.shape
    tm = 256
    tiles_per_seq = seq // tm
    gw = GROUP_WIDTH
    main_w = 3 * gw + 2 * MLA_RANK
    w_main = w_in[:, :main_w].astype(BF16)
    w_kr = jnp.pad(w_in[:, main_w:], ((0, 0), (0, HEAD_DIM - MLA_ROPE_DIM))).astype(BF16)
    wuq = jnp.pad(w_uq.reshape(MLA_RANK, N_HEADS, MLA_QK_DIM),
                  ((0, 0), (0, 0), (0, 2 * HEAD_DIM - MLA_QK_DIM)))
    wuq = wuq.reshape(MLA_RANK, N_HEADS * 2 * HEAD_DIM).astype(BF16)
    wukv = w_ukv.reshape(MLA_RANK, N_HEADS, 2 * HEAD_DIM)
    wuk = wukv[:, :, :HEAD_DIM].reshape(MLA_RANK, gw).astype(BF16)
    wuv = wukv[:, :, HEAD_DIM:].reshape(MLA_RANK, gw).astype(BF16)
    tables = _rope_tables(seq)

    def const(shape):
        return pl.BlockSpec(shape, lambda i: (0,) * len(shape), pipeline_mode=pl.Buffered(1))

    def rows(width):
        return pl.BlockSpec((tm, width), lambda i: (i, 0))

    mod_spec = pl.BlockSpec((1, 1, d), lambda i: (i // tiles_per_seq, 0, 0))
    tab_spec = pl.BlockSpec((tm, HEAD_DIM), lambda i: (i % tiles_per_seq, 0))
    widths = (gw, gw, gw, 2 * gw, gw, gw, HEAD_DIM)
    transposed = (False, False, True, False, False, True, False)

    def out_spec(width, is_t):
        return pl.BlockSpec((width, tm), lambda i: (0, i)) if is_t else rows(width)

    def out_shape(width, is_t):
        return jax.ShapeDtypeStruct((width, t_tokens) if is_t else (t_tokens, width), BF16)

    return pl.pallas_call(
        _inproj_kernel,
        grid=(t_tokens // tm,),
        in_specs=[rows(d), mod_spec, mod_spec, const((d, main_w)), const((d, HEAD_DIM)),
                  const((MLA_RANK, 2 * gw)), const((MLA_RANK, gw)), const((MLA_RANK, gw)),
                  const((1, MLA_RANK)), const((1, MLA_RANK))] + [tab_spec] * 5,
        out_specs=[out_spec(w, t) for w, t in zip(widths, transposed)],
        out_shape=[out_shape(w, t) for w, t in zip(widths, transposed)],
        compiler_params=_params(("arbitrary",)),
        name="input_projection",
    )(x2d, sc1, sh1, w_main, w_kr, wuq, wuk, wuv, g_cq.reshape(1, -1), g_ckv.reshape(1, -1), *tables)


ATTN_BLOCK = 256
LOG2_E = 1.4426950408889634


def _causal_mask_t():
    key = lax.broadcasted_iota(I32, (ATTN_BLOCK, ATTN_BLOCK), 0)
    qry = lax.broadcasted_iota(I32, (ATTN_BLOCK, ATTN_BLOCK), 1)
    return key <= qry


def _scores_t(q, k_blk):
    return lax.dot_general(k_blk, q, NT_DIMS, preferred_element_type=F32)


def _first_block(s, vt_blk, c, acc_ref):
    s = jnp.where(_causal_mask_t(), s, -jnp.inf)
    m = jnp.max(s, axis=0, keepdims=True)
    p = jnp.exp2((s - m) * c)
    acc_ref[...] = jnp.dot(vt_blk, p.astype(BF16), preferred_element_type=F32)
    return m, jnp.sum(p, axis=0, keepdims=True)


def _next_block(s, vt_blk, bias, c, m, l, acc_ref):
    m_blk = jnp.max(s, axis=0, keepdims=True)
    if bias is not None:
        m_blk = m_blk + bias
    m_new = jnp.maximum(m, m_blk)
    alpha = jnp.exp2((m - m_new) * c)
    p = jnp.exp2((s - (m_new if bias is None else m_new - bias)) * c)
    acc_ref[...] = alpha * acc_ref[...] + jnp.dot(vt_blk, p.astype(BF16), preferred_element_type=F32)
    return m_new, alpha * l + jnp.sum(p, axis=0, keepdims=True)


HEADS_PER_STEP = 8


def _head_cols(g, width=HEAD_DIM):
    return slice(g * width, (g + 1) * width)


def _moba_block_bias(kmean, q, qi, n_blocks):
    gate = lax.dot_general(kmean, q.astype(F32), NT_DIMS, precision=lax.Precision.HIGHEST,
                           preferred_element_type=F32)
    row = lax.broadcasted_iota(I32, gate.shape, 0)
    gt = jnp.where(row < qi, gate, -jnp.inf)
    cnt = jnp.zeros(gate.shape, F32)
    for m in range(n_blocks - 1):
        gm = gt[m:m + 1, :]
        cnt = cnt + jnp.where((gm > gt) | ((gm == gt) & (m < row)), 1.0, 0.0)
    return jnp.where(cnt < float(MOBA_TOPK), 0.0, -jnp.inf).astype(F32)


def _moba_kernel(q_ref, k_ref, vt_ref, o_ref, kmean_ref, acc_ref, *, n_blocks):
    qi = pl.program_id(2)
    blk = MOBA_BLOCK
    heads = HEADS_PER_STEP

    @pl.when(qi == 0)
    def _():
        kmean_ref[...] = jnp.zeros(kmean_ref.shape, F32)
        for g in range(heads):
            for n in range(n_blocks):
                kb = k_ref[0, n * blk:(n + 1) * blk, _head_cols(g)].astype(F32)
                kmean_ref[g, n:n + 1, :] = jnp.sum(kb, axis=0, keepdims=True) * (1.0 / blk)

    c = (HEAD_DIM ** -0.5) * LOG2_E
    own = pl.multiple_of(qi * blk, blk)
    qs = [q_ref[0, :, _head_cols(g)] for g in range(heads)]
    scores = [_scores_t(qs[g], k_ref[0, pl.ds(own, blk), _head_cols(g)]) for g in range(heads)]
    biases = [_moba_block_bias(kmean_ref[g], qs[g], qi, n_blocks) for g in range(heads)]
    carry0 = []
    for g in range(heads):
        carry0 += _first_block(scores[g], vt_ref[_head_cols(g), pl.ds(own, blk)], c, acc_ref.at[g])

    def body(n, carry):
        off = pl.multiple_of(n * blk, blk)
        scores = [_scores_t(qs[g], k_ref[0, pl.ds(off, blk), _head_cols(g)]) for g in range(heads)]
        row = lax.broadcasted_iota(I32, biases[0].shape, 0)
        out = []
        for g in range(heads):
            bias = jnp.sum(jnp.where(row == n, biases[g], 0.0), axis=0, keepdims=True)
            out += _next_block(scores[g], vt_ref[_head_cols(g), pl.ds(off, blk)], bias,
                               c, carry[2 * g], carry[2 * g + 1], acc_ref.at[g])
        return tuple(out)

    final = lax.fori_loop(0, qi, body, tuple(carry0))
    for g in range(heads):
        o_ref[0, :, _head_cols(g)] = (acc_ref[g] / final[2 * g + 1]).T.astype(o_ref.dtype)


def moba_attention(qa, ka, vat, bsz, seq):
    blk = MOBA_BLOCK
    n_blocks = seq // blk
    heads = HEADS_PER_STEP
    gate_rows = -(-n_blocks // 8) * 8
    q3, k3 = (t.reshape(bsz, seq, GROUP_WIDTH) for t in (qa, ka))
    q_spec = pl.BlockSpec((1, blk, heads * HEAD_DIM), lambda b, h, i: (b, i, h))
    kv_spec = pl.BlockSpec((1, seq, heads * HEAD_DIM), lambda b, h, i: (b, 0, h))
    out = pl.pallas_call(
        functools.partial(_moba_kernel, n_blocks=n_blocks),
        grid=(bsz, N_HEADS // heads, n_blocks),
        in_specs=[q_spec, kv_spec, pl.BlockSpec((heads * HEAD_DIM, seq), lambda b, h, i: (h, b))],
        out_specs=q_spec,
        out_shape=jax.ShapeDtypeStruct((bsz, seq, GROUP_WIDTH), BF16),
        scratch_shapes=[pltpu.VMEM((heads, gate_rows, HEAD_DIM), F32),
                        pltpu.VMEM((heads, HEAD_DIM, blk), F32)],
        compiler_params=_params(("arbitrary", "arbitrary", "arbitrary")),
        name="moba_attention",
    )(q3, k3, vat)
    return out.reshape(bsz * seq, GROUP_WIDTH)


def _mla_kernel(q_ref, kn_ref, kr_ref, vt_ref, o_ref, kcat_ref, acc_ref):
    qi = pl.program_id(2)
    blk = ATTN_BLOCK

    heads = HEADS_PER_STEP

    @pl.when(qi == 0)
    def _():
        for g in range(heads):
            kcat_ref[g, :, :HEAD_DIM] = kn_ref[0, :, _head_cols(g)]
            kcat_ref[g, :, HEAD_DIM:] = kr_ref[0]

    c = (MLA_QK_DIM ** -0.5) * LOG2_E
    own = pl.multiple_of(qi * blk, blk)
    qs = [q_ref[0, :, _head_cols(g, 2 * HEAD_DIM)] for g in range(heads)]
    scores = [_scores_t(qs[g], kcat_ref[g, pl.ds(own, blk), :]) for g in range(heads)]
    carry0 = []
    for g in range(heads):
        carry0 += _first_block(scores[g], vt_ref[_head_cols(g), pl.ds(own, blk)], c, acc_ref.at[g])

    def body(n, carry):
        off = pl.multiple_of(n * blk, blk)
        scores = [_scores_t(qs[g], kcat_ref[g, pl.ds(off, blk), :]) for g in range(heads)]
        out = []
        for g in range(heads):
            out += _next_block(scores[g], vt_ref[_head_cols(g), pl.ds(off, blk)], None, c,
                               carry[2 * g], carry[2 * g + 1], acc_ref.at[g])
        return tuple(out)

    final = lax.fori_loop(0, qi, body, tuple(carry0))
    for g in range(heads):
        o_ref[0, :, _head_cols(g)] = (acc_ref[g] / final[2 * g + 1]).T.astype(o_ref.dtype)


def mla_attention(qm, kn, kr, vmt, bsz, seq):
    blk = ATTN_BLOCK
    heads = HEADS_PER_STEP
    q3 = qm.reshape(bsz, seq, 2 * GROUP_WIDTH)
    kn3 = kn.reshape(bsz, seq, GROUP_WIDTH)
    kr3 = kr.reshape(bsz, seq, HEAD_DIM)
    out = pl.pallas_call(
        _mla_kernel,
        grid=(bsz, N_HEADS // heads, seq // blk),
        in_specs=[pl.BlockSpec((1, blk, heads * 2 * HEAD_DIM), lambda b, h, i: (b, i, h)),
                  pl.BlockSpec((1, seq, heads * HEAD_DIM), lambda b, h, i: (b, 0, h)),
                  pl.BlockSpec((1, seq, HEAD_DIM), lambda b, h, i: (b, 0, 0)),
                  pl.BlockSpec((heads * HEAD_DIM, seq), lambda b, h, i: (h, b))],
        out_specs=pl.BlockSpec((1, blk, heads * HEAD_DIM), lambda b, h, i: (b, i, h)),
        out_shape=jax.ShapeDtypeStruct((bsz, seq, GROUP_WIDTH), BF16),
        scratch_shapes=[pltpu.VMEM((heads, seq, 2 * HEAD_DIM), BF16),
                        pltpu.VMEM((heads, HEAD_DIM, blk), F32)],
        compiler_params=_params(("arbitrary", "arbitrary", "arbitrary")),
        name="mla_attention",
    )(q3, kn3, kr3, vmt)
    return out.reshape(bsz * seq, GROUP_WIDTH)


def _layer_norm(y, g, b):
    mu = jnp.mean(y, axis=-1, keepdims=True)
    yc = y - mu
    var = jnp.mean(yc * yc, axis=-1, keepdims=True)
    return yc * lax.rsqrt(var + LN_EPS) * g + b


def _outproj_kernel(a_ref, m_ref, x_ref, woa_ref, wom_ref, g1_ref, sc2_ref, sh2_ref, g2_ref,
                    lng_ref, lnb_ref, wsg_ref, wsu_ref, wsd_ref, h2_ref, h2p_ref, base_ref):
    f = (jnp.dot(a_ref[...], woa_ref[...], preferred_element_type=F32)
         + jnp.dot(m_ref[...], wom_ref[...], preferred_element_type=F32))
    x1 = _layer_norm(DEEPNORM_ALPHA * x_ref[...] + (1.0 + g1_ref[0]) * f, lng_ref[...], lnb_ref[...])
    h2 = x1 * (1.0 + sc2_ref[0]) + sh2_ref[0]
    h2_ref[...] = h2
    _store_packed_rows(h2p_ref, h2)
    hb = h2.astype(BF16)
    act = (_silu(jnp.dot(hb, wsg_ref[...], preferred_element_type=F32))
           * jnp.dot(hb, wsu_ref[...], preferred_element_type=F32)).astype(BF16)
    shared = jnp.dot(act, wsd_ref[...], preferred_element_type=F32)
    base_ref[...] = DEEPNORM_ALPHA * x1 + (1.0 + g2_ref[0]) * shared


def output_projection(attn_a, attn_m, x2d, w_o, g1, sc2, sh2, g2, ln_g, ln_b,
                      ws_gate, ws_up, ws_down, seq):
    t_tokens, d = x2d.shape
    tm = 256
    tiles_per_seq = seq // tm
    gw = GROUP_WIDTH
    ff = ws_gate.shape[1]

    def const(shape):
        return pl.BlockSpec(shape, lambda i: (0,) * len(shape), pipeline_mode=pl.Buffered(1))

    def rows(width):
        return pl.BlockSpec((tm, width), lambda i: (i, 0))

    mod_spec = pl.BlockSpec((1, 1, d), lambda i: (i // tiles_per_seq, 0, 0))
    return pl.pallas_call(
        _outproj_kernel,
        grid=(t_tokens // tm,),
        in_specs=[rows(gw), rows(gw), rows(d), const((gw, d)), const((gw, d)),
                  mod_spec, mod_spec, mod_spec, mod_spec, const((1, d)), const((1, d)),
                  const((d, ff)), const((d, ff)), const((ff, d))],
        out_specs=[rows(d), pl.BlockSpec((tm * PACK_CHUNKS, PACK_LANES), lambda i: (i, 0)), rows(d)],
        out_shape=[jax.ShapeDtypeStruct((t_tokens, d), F32),
                   jax.ShapeDtypeStruct((t_tokens * PACK_CHUNKS, PACK_LANES), U32),
                   jax.ShapeDtypeStruct((t_tokens, d), F32)],
        compiler_params=_params(("arbitrary",)),
        name="output_projection",
    )(attn_a, attn_m, x2d, w_o[:gw].astype(BF16), w_o[gw:].astype(BF16), g1, sc2, sh2, g2,
      ln_g.reshape(1, d), ln_b.reshape(1, d),
      ws_gate.astype(BF16), ws_up.astype(BF16), ws_down.astype(BF16))


ROUTER_TILE = 512


def _router_kernel(h_ref, wrt_ref, bias_ref, tri_ref, ltri_ref,
                   e_ref, pos_ref, w_ref, cnt_ref, run_ref):
    tr = ROUTER_TILE
    epg = N_EXPERTS // N_GROUPS

    @pl.when(pl.program_id(0) == 0)
    def _():
        run_ref[...] = jnp.zeros(run_ref.shape, F32)

    logits = lax.dot_general(wrt_ref[...], h_ref[...], NT_DIMS, precision=lax.Precision.HIGHEST,
                             preferred_element_type=F32)
    scores = jax.nn.sigmoid(logits)
    sel_scores = scores + bias_ref[...]

    sub = lax.broadcasted_iota(I32, (epg, tr), 0)
    grp_rows = []
    for g in range(N_GROUPS):
        xg = sel_scores[g * epg:(g + 1) * epg, :]
        m1 = jnp.max(xg, axis=0, keepdims=True)
        first = jnp.min(jnp.where(xg == m1, sub, epg), axis=0, keepdims=True)
        m2 = jnp.max(jnp.where(sub == first, -jnp.inf, xg), axis=0, keepdims=True)
        grp_rows.append(m1 + m2)
    grp = jnp.concatenate(grp_rows, axis=0)

    giota = lax.broadcasted_iota(I32, (N_GROUPS, tr), 0)
    gcnt = jnp.zeros((N_GROUPS, tr), F32)
    for m in range(N_GROUPS):
        rm = grp[m:m + 1, :]
        gcnt = gcnt + jnp.where((rm > grp) | ((rm == grp) & (m < giota)), 1.0, 0.0)
    gkeep = jnp.where(gcnt < float(TOPK_GROUPS), 1.0, 0.0)
    gmask = jnp.concatenate(
        [jnp.broadcast_to(gkeep[g:g + 1, :], (epg, tr)) for g in range(N_GROUPS)], axis=0)
    masked = jnp.where(gmask > 0.0, sel_scores, -jnp.inf)

    eiota = lax.broadcasted_iota(I32, (N_EXPERTS, tr), 0)

    free = jnp.ones((N_EXPERTS, tr), F32)
    for _ in range(TOP_K):
        top = jnp.max(jnp.where(free > 0.0, masked, -jnp.inf), axis=0, keepdims=True)
        first = jnp.min(jnp.where((masked == top) & (free > 0.0), eiota, N_EXPERTS),
                        axis=0, keepdims=True)
        free = jnp.where(eiota == first, 0.0, free)
    sel = free == 0.0

    sw = jnp.where(sel, scores, 0.0)
    wn = sw / jnp.sum(sw, axis=0, keepdims=True) * ROUTED_SCALE

    selb = jnp.where(sel, 1.0, 0.0).astype(BF16)
    cum = jnp.dot(selb, tri_ref[...], preferred_element_type=F32)
    pos = run_ref[...] + cum - 1.0
    run_ref[...] = run_ref[...] + cum[:, tr - 1:tr]
    cnt_ref[...] = jnp.broadcast_to(run_ref[...], cnt_ref.shape).astype(I32)

    slot = jnp.dot(ltri_ref[...], selb, preferred_element_type=F32)
    ef = eiota.astype(F32)
    e_rows, p_rows, w_rows = [], [], []
    for j in range(TOP_K):
        mj = sel & (slot == float(j))
        e_rows.append(jnp.sum(jnp.where(mj, ef, 0.0), axis=0, keepdims=True))
        p_rows.append(jnp.sum(jnp.where(mj, pos, 0.0), axis=0, keepdims=True))
        w_rows.append(jnp.sum(jnp.where(mj, wn, 0.0), axis=0, keepdims=True))
    e_ref[...] = jnp.concatenate(e_rows, axis=0).astype(I32)
    pos_ref[...] = jnp.concatenate(p_rows, axis=0).astype(I32)
    w_ref[...] = jnp.concatenate(w_rows, axis=0)


def router(h2, w_router, router_bias):
    t_tokens, d = h2.shape
    tr = ROUTER_TILE
    tri = jnp.asarray(np.triu(np.ones((tr, tr), np.float32)), BF16)
    ltri = jnp.asarray(np.tril(np.ones((N_EXPERTS, N_EXPERTS), np.float32), -1), BF16)

    def const(shape):
        return pl.BlockSpec(shape, lambda i: (0,) * len(shape))

    tok = pl.BlockSpec((TOP_K, tr), lambda i: (0, i))
    e_t, pos_t, w_t, cnt = pl.pallas_call(
        _router_kernel,
        grid=(t_tokens // tr,),
        in_specs=[pl.BlockSpec((tr, d), lambda i: (i, 0)), const((N_EXPERTS, d)),
                  const((N_EXPERTS, 1)), const((tr, tr)), const((N_EXPERTS, N_EXPERTS))],
        out_specs=[tok, tok, tok, const((N_EXPERTS, 128))],
        out_shape=[jax.ShapeDtypeStruct((TOP_K, t_tokens), I32),
                   jax.ShapeDtypeStruct((TOP_K, t_tokens), I32),
                   jax.ShapeDtypeStruct((TOP_K, t_tokens), F32),
                   jax.ShapeDtypeStruct((N_EXPERTS, 128), I32)],
        scratch_shapes=[pltpu.VMEM((N_EXPERTS, 1), F32)],
        compiler_params=_params(("arbitrary",)),
        name="router",
    )(h2, w_router.T, router_bias.reshape(N_EXPERTS, 1), tri, ltri)
    return e_t, pos_t, w_t, cnt[:, 0]


DISPATCH_TILE = 128


def _dispatch_kernel(pad_start_ref, pad_count_ref, nused_ref, d_ref, h_ref, xs_ref, zero_ref, sem):
    tt = DISPATCH_TILE
    pc = PACK_CHUNKS
    rb = MOE_ROW_BLOCK
    n_copies = tt * TOP_K
    n_blocks = xs_ref.shape[0] // (rb * pc)

    nused = nused_ref[0]

    def pad_rows(row, n):
        dst = xs_ref.at[pl.ds(pl.multiple_of(row * pc, pc), n * pc), :]
        return pltpu.make_async_copy(zero_ref.at[pl.ds(0, n * pc), :], dst, sem.at[1])

    def tail_block(b):
        dst = xs_ref.at[pl.ds(pl.multiple_of(b * (rb * pc), rb * pc), rb * pc), :]
        return pltpu.make_async_copy(zero_ref, dst, sem.at[1])

    def per_expert(e, carry, wait):
        base, n = pad_start_ref[e], pad_count_ref[e]
        for bit in reversed(range(rb.bit_length() - 1)):
            size = 1 << bit
            offset = (n >> (bit + 1)) << (bit + 1)

            @pl.when(((n >> bit) & 1) == 1)
            def _():
                copy = pad_rows(base + offset, size)
                if wait:
                    copy.wait()
                else:
                    copy.start()
        return carry

    def tail(b, carry, wait):
        if wait:
            tail_block(b).wait()
        else:
            tail_block(b).start()
        return carry

    def zero_fill(wait):
        lax.fori_loop(0, N_EXPERTS, functools.partial(per_expert, wait=wait), 0)
        lax.fori_loop(nused, n_blocks, functools.partial(tail, wait=wait), 0)

    @pl.when(pl.program_id(0) == 0)
    def _():
        zero_ref[...] = jnp.zeros(zero_ref.shape, U32)
        zero_fill(wait=False)

    for j in range(tt):
        for k in range(TOP_K):
            c = j * TOP_K + k
            dst = xs_ref.at[pl.ds(pl.multiple_of(d_ref[c] * pc, pc), pc), :]
            pltpu.make_async_copy(h_ref.at[pl.ds(j * pc, pc), :], dst, sem.at[0]).start(priority=c % 2)
    tile_rows = xs_ref.at[pl.ds(0, n_copies * pc), :]
    pltpu.make_async_copy(tile_rows, tile_rows, sem.at[0]).wait()

    @pl.when(pl.program_id(0) == pl.num_programs(0) - 1)
    def _():
        zero_fill(wait=True)


def dispatch(h2p, dest_tok_major, pad_start, pad_count, nused, n_rows):
    tt = DISPATCH_TILE
    pc = PACK_CHUNKS
    t_tokens = h2p.shape[0] // pc
    n_copies = tt * TOP_K
    grid_spec = pltpu.PrefetchScalarGridSpec(
        num_scalar_prefetch=3,
        grid=(t_tokens // tt,),
        in_specs=[pl.BlockSpec((n_copies,), lambda i, ps, pn, nu: (i,), memory_space=pltpu.SMEM),
                  pl.BlockSpec((tt * pc, PACK_LANES), lambda i, ps, pn, nu: (i, 0))],
        out_specs=pl.BlockSpec(memory_space=pl.ANY),
        scratch_shapes=[pltpu.VMEM((MOE_ROW_BLOCK * pc, PACK_LANES), U32),
                        pltpu.SemaphoreType.DMA((2,))],
    )
    return pl.pallas_call(
        _dispatch_kernel,
        grid_spec=grid_spec,
        out_shape=jax.ShapeDtypeStruct((n_rows * pc, PACK_LANES), U32),
        compiler_params=_params(("arbitrary",)),
        name="dispatch",
    )(pad_start, pad_count, nused, dest_tok_major, h2p)


EXPERT_BLOCKS_PER_STEP = 4


def _experts_kernel(be_ref, nused_ref, ord_ref, next_ref, x_ref, wg_hbm, wu_hbm, wd_hbm, y_ref,
                    wg_stage, wu_stage, wd_stage, wg_bf, wu_bf, wd_bf, sem):
    nused = nused_ref[0]
    rb = MOE_ROW_BLOCK
    pc = PACK_CHUNKS

    def weight_copies(e, slot):
        return (pltpu.make_async_copy(wg_hbm.at[e], wg_stage.at[slot], sem.at[slot]),
                pltpu.make_async_copy(wu_hbm.at[e], wu_stage.at[slot], sem.at[slot]),
                pltpu.make_async_copy(wd_hbm.at[e], wd_stage.at[slot], sem.at[slot]))

    def one_block(i, sub):
        y_rows = y_ref.at[pl.ds(sub * rb * pc, rb * pc), :]

        @pl.when(i < nused)
        def _():
            @pl.when((i == 0) | (be_ref[i] != be_ref[jnp.maximum(i - 1, 0)]))
            def _():
                e = be_ref[i]
                slot = ord_ref[i] % 2

                @pl.when(i == 0)
                def _():
                    for c in weight_copies(e, slot):
                        c.start()

                for c in weight_copies(e, slot):
                    c.wait()
                wg_bf[...] = wg_stage[slot].astype(BF16)
                wu_bf[...] = wu_stage[slot].astype(BF16)
                wd_bf[...] = wd_stage[slot].astype(BF16)

                @pl.when(next_ref[i] < N_EXPERTS)
                def _():
                    for c in weight_copies(next_ref[i], 1 - slot):
                        c.start()

            lo, hi = _load_packed_rows(x_ref, sub * rb, rb)
            xb = jnp.concatenate([lo.astype(BF16), hi.astype(BF16)], axis=1)
            act = (_silu(jnp.dot(xb, wg_bf[...], preferred_element_type=F32))
                   * jnp.dot(xb, wu_bf[...], preferred_element_type=F32)).astype(BF16)
            _store_packed_rows(y_rows, jnp.dot(act, wd_bf[...], preferred_element_type=F32))

        @pl.when(i >= nused)
        def _():
            y_rows[...] = jnp.zeros(y_rows.shape, U32)

    for sub in range(EXPERT_BLOCKS_PER_STEP):
        one_block(pl.program_id(0) * EXPERT_BLOCKS_PER_STEP + sub, sub)


def routed_experts(x_sorted, block_e, nused, block_ord, block_next, w_gate, w_up, w_down):
    rb = MOE_ROW_BLOCK
    pc = PACK_CHUNKS
    per_step = EXPERT_BLOCKS_PER_STEP
    n_blocks = x_sorted.shape[0] // (rb * pc)
    assert n_blocks % per_step == 0
    d, ff = w_gate.shape[1], w_gate.shape[2]
    hbm = pl.BlockSpec(memory_space=pl.ANY)

    def x_index(i, be, nu, od, nx):
        return (jnp.minimum(i, jnp.maximum(nu[0] - 1, 0) // per_step), 0)

    grid_spec = pltpu.PrefetchScalarGridSpec(
        num_scalar_prefetch=4,
        grid=(n_blocks // per_step,),
        in_specs=[pl.BlockSpec((per_step * rb * pc, PACK_LANES), x_index), hbm, hbm, hbm],
        out_specs=pl.BlockSpec((per_step * rb * pc, PACK_LANES), lambda i, be, nu, od, nx: (i, 0)),
        scratch_shapes=[pltpu.VMEM((2, d, ff), F32), pltpu.VMEM((2, d, ff), F32),
                        pltpu.VMEM((2, ff, d), F32),
                        pltpu.VMEM((d, ff), BF16), pltpu.VMEM((d, ff), BF16),
                        pltpu.VMEM((ff, d), BF16), pltpu.SemaphoreType.DMA((2,))],
    )
    return pl.pallas_call(
        _experts_kernel,
        grid_spec=grid_spec,
        out_shape=jax.ShapeDtypeStruct(x_sorted.shape, U32),
        compiler_params=_params(("arbitrary",)),
        name="routed_experts",
    )(block_e, nused, block_ord, block_next, x_sorted, w_gate, w_up, w_down)


COMBINE_TILE = 128


def _combine_kernel(d_cur_ref, d_next_ref, y_ref, w_ref, base_ref, g2_ref, lng_ref, lnb_ref,
                    o_ref, ybuf_a, ybuf_b, sem):
    i = pl.program_id(0)
    n_steps = pl.num_programs(0)
    tc = COMBINE_TILE
    n_copies = TOP_K * tc
    pc = PACK_CHUNKS

    def row_copy(row, buf, r, s):
        src = y_ref.at[pl.ds(pl.multiple_of(row * pc, pc), pc), :]
        return pltpu.make_async_copy(src, buf.at[pl.ds(r * pc, pc), :], sem.at[s])

    def wait_rows(buf, s):
        pltpu.make_async_copy(y_ref.at[pl.ds(0, n_copies * pc), :], buf, sem.at[s]).wait()

    @pl.when(i == 0)
    def _():
        def body(r, carry):
            src = y_ref.at[pl.ds(pl.multiple_of(d_cur_ref[r] * pc, pc), pc), :]
            dst = ybuf_a.at[pl.ds(pl.multiple_of(r * pc, pc), pc), :]
            pltpu.make_async_copy(src, dst, sem.at[0]).start()
            return carry
        lax.fori_loop(0, n_copies, body, 0, unroll=8)

    def step(cur, cur_s, nxt, nxt_s):
        wait_rows(cur, cur_s)
        for r in range(n_copies):
            row_copy(d_next_ref[r], nxt, r, nxt_s).start(priority=r % 2)
        w = w_ref[...]
        lo, hi = _load_packed_rows(cur, 0, tc)
        routed_lo, routed_hi = w[:, 0:1] * lo, w[:, 0:1] * hi
        for k in range(1, TOP_K):
            lo, hi = _load_packed_rows(cur, k * tc, tc)
            routed_lo = routed_lo + w[:, k:k + 1] * lo
            routed_hi = routed_hi + w[:, k:k + 1] * hi
        routed = jnp.concatenate([routed_lo, routed_hi], axis=1)
        y = base_ref[...] + (1.0 + g2_ref[0]) * routed
        o_ref[...] = _layer_norm(y, lng_ref[...], lnb_ref[...])

    @pl.when(i % 2 == 0)
    def _():
        step(ybuf_a, 0, ybuf_b, 1)

    @pl.when(i % 2 == 1)
    def _():
        step(ybuf_b, 1, ybuf_a, 0)

    @pl.when((i == n_steps - 1) & (i % 2 == 0))
    def _():
        wait_rows(ybuf_b, 1)

    @pl.when((i == n_steps - 1) & (i % 2 == 1))
    def _():
        wait_rows(ybuf_a, 0)


def combine(y_sorted, dest_tiles, w_tok, base, g2, ln_g, ln_b, seq):
    t_tokens, d = base.shape
    tc = COMBINE_TILE
    n_steps = t_tokens // tc
    tiles_per_seq = seq // tc
    n_copies = TOP_K * tc

    def const(shape):
        return pl.BlockSpec(shape, lambda i: (0,) * len(shape))

    return pl.pallas_call(
        _combine_kernel,
        grid=(n_steps,),
        in_specs=[pl.BlockSpec((n_copies,), lambda i: (i,), memory_space=pltpu.SMEM),
                  pl.BlockSpec((n_copies,), lambda i: (jnp.minimum(i + 1, n_steps - 1),),
                               memory_space=pltpu.SMEM),
                  pl.BlockSpec(memory_space=pl.ANY),
                  pl.BlockSpec((tc, TOP_K), lambda i: (i, 0)),
                  pl.BlockSpec((tc, d), lambda i: (i, 0)),
                  pl.BlockSpec((1, 1, d), lambda i: (i // tiles_per_seq, 0, 0)),
                  const((1, d)), const((1, d))],
        out_specs=pl.BlockSpec((tc, d), lambda i: (i, 0)),
        out_shape=jax.ShapeDtypeStruct((t_tokens, d), F32),
        scratch_shapes=[pltpu.VMEM((n_copies * PACK_CHUNKS, PACK_LANES), U32),
                        pltpu.VMEM((n_copies * PACK_CHUNKS, PACK_LANES), U32),
                        pltpu.SemaphoreType.DMA((2,))],
        compiler_params=_params(("arbitrary",)),
        name="combine",
    )(dest_tiles, dest_tiles, y_sorted, w_tok, base, g2, ln_g.reshape(1, d), ln_b.reshape(1, d))


def _dispatch_tables(e_t, pos_t, counts, t_tokens):
    rb = MOE_ROW_BLOCK
    n_blocks = t_tokens * TOP_K // rb + N_EXPERTS
    padded = (counts + rb - 1) // rb * rb
    pends = jnp.cumsum(padded)
    pstarts = pends - padded
    experts = jnp.arange(N_EXPERTS, dtype=I32)
    pstart_of = jnp.sum(jnp.where(e_t[..., None] == experts, pstarts.astype(I32), 0), axis=-1)
    dest = (pstart_of + pos_t).astype(I32)
    nused = (pends[-1] // rb).astype(I32)
    blk = jnp.arange(n_blocks, dtype=I32)
    last_blk = jnp.minimum(blk, jnp.maximum(nused - 1, 0))
    block_e = jnp.sum((pends[None, :] <= (last_blk * rb)[:, None]).astype(I32), axis=1)
    block_e = jnp.minimum(block_e, N_EXPERTS - 1).astype(I32)
    first = (blk < nused) & ((blk == 0) | (block_e != jnp.roll(block_e, 1)))
    block_ord = (jnp.cumsum(first.astype(I32)) - 1).astype(I32)
    later = (experts[None, :] > experts[:, None]) & (padded[None, :] > 0)
    next_of = jnp.min(jnp.where(later, experts[None, :], N_EXPERTS), axis=1)
    block_next = jnp.sum(jnp.where(block_e[:, None] == experts[None, :], next_of[None, :], 0),
                         axis=1).astype(I32)
    pad_start = (pstarts + counts).astype(I32)
    pad_count = (padded - counts).astype(I32)
    return (dest, block_e, nused.reshape(1), block_ord, block_next, pad_start, pad_count,
            n_blocks * rb)


def _layer(x, c, w_ada, b_ada, w_in, g_cq, w_uq, g_ckv, w_ukv, w_o, ln1_g, ln1_b,
           w_router, router_bias, w_gate, w_up, w_down, ws_gate, ws_up, ws_down, ln2_g, ln2_b):
    bsz, seq, d = x.shape
    t_tokens = bsz * seq
    x2d = x.reshape(t_tokens, d)

    mod = ada_modulation(c, w_ada, b_ada)
    sh1, sc1, g1, sh2, sc2, g2 = [m.reshape(bsz, 1, d) for m in jnp.split(mod, 6, axis=-1)]

    qa, ka, vat, qm, kn, vmt, kr = input_projection(x2d, sc1, sh1, w_in, g_cq, w_uq, g_ckv, w_ukv, seq)
    attn_a = moba_attention(qa, ka, vat, bsz, seq)
    attn_m = mla_attention(qm, kn, kr, vmt, bsz, seq)
    h2, h2p, base = output_projection(attn_a, attn_m, x2d, w_o, g1, sc2, sh2, g2, ln1_g, ln1_b,
                                 ws_gate, ws_up, ws_down, seq)

    e_t, pos_t, w_t, counts = router(h2, w_router, router_bias)
    (dest, block_e, nused, block_ord, block_next, pad_start, pad_count,
     n_rows) = _dispatch_tables(e_t, pos_t, counts, t_tokens)
    x_sorted = dispatch(h2p, dest.T.reshape(-1), pad_start, pad_count, nused, n_rows)
    y_sorted = routed_experts(x_sorted, block_e, nused, block_ord, block_next, w_gate, w_up, w_down)

    tc = COMBINE_TILE
    dest_tiles = dest.reshape(TOP_K, t_tokens // tc, tc).transpose(1, 0, 2).reshape(-1)
    out = combine(y_sorted, dest_tiles, w_t.T, base, g2, ln2_g, ln2_b, seq)
    return out.reshape(bsz, seq, d)


def kernel(x, c, w_ada, b_ada, w_in, g_cq, w_uq, g_ckv, w_ukv, w_o, ln1_g, ln1_b, w_router,
           router_bias, w_gate, w_up, w_down, ws_gate, ws_up, ws_down, ln2_g, ln2_b):
    depth = w_ada.shape[0]
    assert depth == 1, "DeepNorm constants are baked for a single layer"
    return _layer(x, c, w_ada[0], b_ada[0], w_in[0], g_cq[0], w_uq[0], g_ckv[0], w_ukv[0], w_o[0],
                  ln1_g[0], ln1_b[0], w_router[0], router_bias[0], w_gate[0], w_up[0], w_down[0],
                  ws_gate[0], ws_up[0], ws_down[0], ln2_g[0], ln2_b[0])
```
